```python
import math
import jax, jax.numpy as jnp
from jax import lax
import numpy as np

D_MODEL = 2048
BATCH = 1
SEQ = 8192
DEPTH = 1

BLK = 128
WINDOW = 128
HA = 16
KV_A = 2
G_A = HA // KV_A
DH_A = 64
HB = 8
DH_B = 64
N_BUCKETS = 32
MAX_DISTANCE = 128
H_BIAS = HA + HB
MEM_LEN = 256
HC = 4
DH_C = 128
D_FF = ((8 * D_MODEL // 3 + 255) // 256) * 256
QA_W = HA * DH_A
KVA_W = KV_A * DH_A
QB_W = HB * 2 * DH_B
VB_W = HB * 2 * DH_B
SPLITS = [QA_W, KVA_W, KVA_W, QB_W, QB_W, VB_W, D_MODEL, D_MODEL]
IN_WIDTH = sum(SPLITS)
SPLIT_IDX = [int(v) for v in np.cumsum(SPLITS)[:-1]]
LN_EPS = 1e-5

kernel_name = "hybrid_swa_sink_diffattn_deepnorm_layer"


def rel_bucket(dist):
    n = jnp.maximum(dist, 0)
    exact = N_BUCKETS // 2
    logv = jnp.log(jnp.maximum(n, 1).astype(jnp.float32) / exact) / math.log(MAX_DISTANCE / exact)
    large = exact + (logv * (N_BUCKETS - exact)).astype(jnp.int32)
    large = jnp.minimum(large, N_BUCKETS - 1)
    return jnp.where(n < exact, n, large)


def layer_norm(x, g, b):
    xf = x.astype(jnp.float32)
    mu = xf.mean(-1, keepdims=True)
    var = jnp.square(xf - mu).mean(-1, keepdims=True)
    y = (xf - mu) * lax.rsqrt(var + LN_EPS) * g.astype(jnp.float32) + b.astype(jnp.float32)
    return y.astype(x.dtype)


def swa_sink_attention(q, k, v, sinks, table):
    b, s = q.shape[0], q.shape[1]
    nb = s // BLK
    qb = q.reshape(b, nb, BLK, KV_A, G_A, DH_A)
    kb = k.reshape(b, nb, BLK, KV_A, DH_A)
    vb = v.reshape(b, nb, BLK, KV_A, DH_A)
    pad = ((0, 0), (1, 0), (0, 0), (0, 0), (0, 0))
    kw = jnp.concatenate([jnp.pad(kb, pad)[:, :-1], kb], axis=2)
    vw = jnp.concatenate([jnp.pad(vb, pad)[:, :-1], vb], axis=2)
    sc = jnp.einsum("bnqgrd,bnkgd->bngrqk", qb, kw).astype(jnp.float32) * (DH_A ** -0.5)
    i = jnp.arange(BLK)[:, None]
    j = jnp.arange(2 * BLK)[None, :]
    dist = BLK + i - j
    bias = table[:, :HA].T.astype(jnp.float32)[:, rel_bucket(dist)]
    sc = sc + bias.reshape(KV_A, G_A, BLK, 2 * BLK)
    band = (dist >= 0) & (dist < WINDOW)
    blk_ok = (jnp.arange(nb)[:, None] > 0) | (jnp.arange(2 * BLK)[None, :] >= BLK)
    mask = band[None, :, :] & blk_ok[:, None, :]
    sc = jnp.where(mask[None, :, None, None], sc, -jnp.inf)
    sink = sinks.astype(jnp.float32).reshape(KV_A, G_A)[None, None, :, :, None, None]
    m = jnp.maximum(sc.max(-1, keepdims=True), sink)
    p = jnp.exp(sc - m)
    p = p / (p.sum(-1, keepdims=True) + jnp.exp(sink - m))
    o = jnp.einsum("bngrqk,bnkgd->bnqgrd", p.astype(v.dtype), vw)
    return o.reshape(b, s, HA * DH_A)


def diff_attention(q, k, v, lam, lambda_init, subln_w, table):
    b, s = q.shape[0], q.shape[1]
    nb = s // BLK
    qblocks = q.reshape(b, nb, BLK, HB, 2, DH_B).transpose(1, 0, 2, 3, 4, 5)
    kpos = jnp.arange(s)
    tb = table[:, HA:].T.astype(jnp.float32)

    def one_block(args):
        n, qb = args
        sc = jnp.einsum("bqhcd,bkhcd->bchqk", qb, k).astype(jnp.float32) * (DH_B ** -0.5)
        qpos = n * BLK + jnp.arange(BLK)
        dist = qpos[:, None] - kpos[None, :]
        sc = jnp.where(dist >= 0, sc + tb[:, rel_bucket(dist)][None, None], -jnp.inf)
        p = jax.nn.softmax(sc, axis=-1)
        a = p[:, 0] - lam * p[:, 1]
        return jnp.einsum("bhqk,bkhe->bqhe", a.astype(v.dtype), v)

    o = lax.map(one_block, (jnp.arange(nb), qblocks))
    o = o.transpose(1, 0, 2, 3, 4).reshape(b, s, HB, 2 * DH_B).astype(jnp.float32)
    o = o * lax.rsqrt(jnp.square(o).mean(-1, keepdims=True) + LN_EPS) * subln_w.astype(jnp.float32)
    o = o * (1.0 - lambda_init)
    return o.reshape(b, s, HB * 2 * DH_B).astype(v.dtype)


def memory_cross_attention(h, mem, w_cq, w_mem_kv, w_co):
    b, s = h.shape[0], h.shape[1]
    q = (h @ w_cq).reshape(b, s, HC, DH_C)
    kv = (mem @ w_mem_kv).reshape(b, mem.shape[1], 2, HC, DH_C)
    sc = jnp.einsum("bqhd,bkhd->bhqk", q, kv[:, :, 0]).astype(jnp.float32) * (DH_C ** -0.5)
    p = jax.nn.softmax(sc, axis=-1)
    o = jnp.einsum("bhqk,bkhd->bqhd", p.astype(h.dtype), kv[:, :, 1])
    return o.reshape(b, s, HC * DH_C) @ w_co


def swiglu(h, w_gate_up, w_down):
    gu = h @ w_gate_up
    g, u = gu[..., :D_FF], gu[..., D_FF:]
    return (jax.nn.silu(g) * u) @ w_down


def setup_inputs(seed: int = 0) -> dict:
    key = jax.random.key(seed)
    ks = jax.random.split(key, 28)
    beta = (8 * DEPTH) ** -0.25

    def nrm(k, shape, scale):
        return jax.random.normal(k, shape, jnp.float32) * scale

    L = DEPTH
    return {
        "x": nrm(ks[0], (BATCH, SEQ, D_MODEL), 1.0),
        "mem": nrm(ks[1], (BATCH, MEM_LEN, D_MODEL), 1.0),
        "rel_bias_table": nrm(ks[2], (N_BUCKETS, H_BIAS), 0.5),
        "w_in": nrm(ks[3], (L, D_MODEL, IN_WIDTH), D_MODEL ** -0.5),
        "sinks": nrm(ks[4], (L, HA), 0.5),
        "lambda_q1": nrm(ks[5], (L, DH_B), 0.1),
        "lambda_k1": nrm(ks[6], (L, DH_B), 0.1),
        "lambda_q2": nrm(ks[7], (L, DH_B), 0.1),
        "lambda_k2": nrm(ks[8], (L, DH_B), 0.1),
        "subln_w": 1.0 + nrm(ks[9], (L, 2 * DH_B), 0.02),
        "w_branch_a": nrm(ks[10], (L, QA_W, D_MODEL), QA_W ** -0.5),
        "w_branch_b": nrm(ks[11], (L, VB_W, D_MODEL), VB_W ** -0.5),
        "w_o": nrm(ks[12], (L, D_MODEL, D_MODEL), beta * D_MODEL ** -0.5),
        "ln1_g": 1.0 + nrm(ks[13], (L, D_MODEL), 0.02),
        "ln1_b": nrm(ks[14], (L, D_MODEL), 0.02),
        "w_cq": nrm(ks[15], (L, D_MODEL, HC * DH_C), D_MODEL ** -0.5),
        "w_mem_kv": nrm(ks[16], (L, D_MODEL, 2 * HC * DH_C), D_MODEL ** -0.5),
        "w_co": nrm(ks[17], (L, HC * DH_C, D_MODEL), beta * (HC * DH_C) ** -0.5),
        "ln2_g": 1.0 + nrm(ks[18], (L, D_MODEL), 0.02),
        "ln2_b": nrm(ks[19], (L, D_MODEL), 0.02),
        "w_gate_up": nrm(ks[20], (L, D_MODEL, 2 * D_FF), D_MODEL ** -0.5),
        "w_down": nrm(ks[21], (L, D_FF, D_MODEL), beta * D_FF ** -0.5),
        "ln3_g": 1.0 + nrm(ks[22], (L, D_MODEL), 0.02),
        "ln3_b": nrm(ks[23], (L, D_MODEL), 0.02),
    }


def reference(x, mem, rel_bias_table, w_in, sinks, lambda_q1, lambda_k1, lambda_q2, lambda_k2,
              subln_w, w_branch_a, w_branch_b, w_o, ln1_g, ln1_b, w_cq, w_mem_kv, w_co,
              ln2_g, ln2_b, w_gate_up, w_down, ln3_g, ln3_b):
    alpha = (2 * DEPTH) ** 0.25
    b, s = x.shape[0], x.shape[1]
    h = x
    for l in range(DEPTH):
        lambda_init = 0.8 - 0.6 * math.exp(-0.3 * l)
        proj = h @ w_in[l]
        qa, ka, va, qb, kb, vb, ga, gb = jnp.split(proj, SPLIT_IDX, axis=-1)
        o_a = swa_sink_attention(qa.reshape(b, s, HA, DH_A), ka.reshape(b, s, KV_A, DH_A),
                                 va.reshape(b, s, KV_A, DH_A), sinks[l], rel_bias_table)
        f32 = jnp.float32
        lam = (jnp.exp(jnp.sum(lambda_q1[l].astype(f32) * lambda_k1[l].astype(f32)))
               - jnp.exp(jnp.sum(lambda_q2[l].astype(f32) * lambda_k2[l].astype(f32)))
               + lambda_init)
        o_b = diff_attention(qb.reshape(b, s, HB, 2, DH_B), kb.reshape(b, s, HB, 2, DH_B),
                             vb.reshape(b, s, HB, 2 * DH_B), lam, lambda_init, subln_w[l],
                             rel_bias_table)
        mix = jax.nn.sigmoid(ga) * (o_a @ w_branch_a[l]) + jax.nn.sigmoid(gb) * (o_b @ w_branch_b[l])
        h = layer_norm(alpha * h + mix @ w_o[l], ln1_g[l], ln1_b[l])
        c = memory_cross_attention(h, mem, w_cq[l], w_mem_kv[l], w_co[l])
        h = layer_norm(alpha * h + c, ln2_g[l], ln2_b[l])
        f = swiglu(h, w_gate_up[l], w_down[l])
        h = layer_norm(alpha * h + f, ln3_g[l], ln3_b[l])
    return h
```

```python
import functools
import math

import numpy as np
import jax
import jax.numpy as jnp
from jax import lax
from jax.experimental import pallas as pl
from jax.experimental.pallas import tpu as pltpu

F32 = jnp.float32
BF16 = jnp.bfloat16

BLK = 128
WINDOW = 128
HA, KV_A, DH_A = 16, 2, 64
G_A = HA // KV_A
HB, DH_B = 8, 64
N_BUCKETS, MAX_DISTANCE = 32, 128
HC, DH_C = 4, 128
LN_EPS = 1e-5
DEPTH = 1
ALPHA = (2 * DEPTH) ** 0.25
LAMBDA_INIT = 0.8 - 0.6 * math.exp(-0.3 * 0)

V7X_LANES = 128
V7X_VMEM_BYTES = 64 * 1024 * 1024
V7X_VMEM_TEMP_BYTES = 12 * 1024 * 1024

NEG = -1e30


def _nbytes(shape, dtype):
    return int(np.prod(shape)) * jnp.dtype(dtype).itemsize


def _vmem_limit(pipelined, scratch=()):
    need = 2 * sum(_nbytes(s, d) for s, d in pipelined) + sum(_nbytes(s, d) for s, d in scratch)
    need += V7X_VMEM_TEMP_BYTES
    assert need <= V7X_VMEM_BYTES - 4 * 1024 * 1024, need
    return need


def _params(semantics, vmem):
    return pltpu.CompilerParams(dimension_semantics=semantics, vmem_limit_bytes=vmem)


def _act_tile(a_ref, abf_ref):
    if abf_ref is None:
        return a_ref[...]

    @pl.when(pl.program_id(1) == 0)
    def _():
        abf_ref[...] = a_ref[...].astype(BF16)

    return abf_ref[...]


def _mm_body(a_ref, w_ref, o_ref, *scratch):
    a = _act_tile(a_ref, scratch[0] if scratch else None)
    o_ref[...] = jnp.dot(a, w_ref[...].astype(BF16), preferred_element_type=F32).astype(o_ref.dtype)


def _matmul(a, w, *, tm, tn, out_dtype, name):
    m, k = a.shape
    n = w.shape[1]
    cast = a.dtype != BF16
    scratch = [((tm, k), BF16)] if cast else []
    blocks = [((tm, k), a.dtype), ((k, tn), w.dtype), ((tm, tn), out_dtype)]
    return pl.pallas_call(
        _mm_body,
        grid=(m // tm, n // tn),
        in_specs=[pl.BlockSpec((tm, k), lambda i, j: (i, 0)),
                  pl.BlockSpec((k, tn), lambda i, j: (0, j))],
        out_specs=pl.BlockSpec((tm, tn), lambda i, j: (i, j)),
        out_shape=jax.ShapeDtypeStruct((m, n), out_dtype),
        scratch_shapes=[pltpu.VMEM(s, d) for s, d in scratch],
        compiler_params=_params(("arbitrary", "arbitrary"), _vmem_limit(blocks, scratch)),
        name=name,
    )(a, w)


def _branch_body(oa_ref, ob_ref, wa_ref, wb_ref, ga_ref, gb_ref, o_ref):
    ya = jnp.dot(oa_ref[...], wa_ref[...].astype(BF16), preferred_element_type=F32)
    yb = jnp.dot(ob_ref[...], wb_ref[...].astype(BF16), preferred_element_type=F32)
    ga = jax.nn.sigmoid(ga_ref[...].astype(F32))
    gb = jax.nn.sigmoid(gb_ref[...].astype(F32))
    o_ref[...] = (ga * ya + gb * yb).astype(o_ref.dtype)


def _gated_branches(o_a, o_b, w_a, w_b, proj, ga_col, gb_col, *, tm, tn):
    m, ka = o_a.shape
    kb = o_b.shape[1]
    n = w_a.shape[1]
    ga_blk, gb_blk = ga_col // tn, gb_col // tn
    blocks = [((tm, ka), BF16), ((tm, kb), BF16), ((ka, tn), F32), ((kb, tn), F32),
              ((tm, tn), proj.dtype), ((tm, tn), proj.dtype), ((tm, tn), BF16)]
    return pl.pallas_call(
        _branch_body,
        grid=(m // tm, n // tn),
        in_specs=[pl.BlockSpec((tm, ka), lambda i, j: (i, 0)),
                  pl.BlockSpec((tm, kb), lambda i, j: (i, 0)),
                  pl.BlockSpec((ka, tn), lambda i, j: (0, j)),
                  pl.BlockSpec((kb, tn), lambda i, j: (0, j)),
                  pl.BlockSpec((tm, tn), lambda i, j: (i, ga_blk + j)),
                  pl.BlockSpec((tm, tn), lambda i, j: (i, gb_blk + j))],
        out_specs=pl.BlockSpec((tm, tn), lambda i, j: (i, j)),
        out_shape=jax.ShapeDtypeStruct((m, n), BF16),
        compiler_params=_params(("arbitrary", "arbitrary"), _vmem_limit(blocks)),
        name="gated_branches",
    )(o_a, o_b, w_a, w_b, proj, proj)


def _swiglu_body(a_ref, wg_ref, wu_ref, o_ref):
    a = a_ref[...]
    g = jnp.dot(a, wg_ref[...].astype(BF16), preferred_element_type=F32)
    u = jnp.dot(a, wu_ref[...].astype(BF16), preferred_element_type=F32)
    o_ref[...] = (g * jax.nn.sigmoid(g) * u).astype(o_ref.dtype)


def _swiglu_up(a, w_gate_up, *, tm, tn):
    m, k = a.shape
    d_ff = w_gate_up.shape[1] // 2
    up_blk = d_ff // tn
    blocks = [((tm, k), BF16), ((k, tn), F32), ((k, tn), F32), ((tm, tn), BF16)]
    return pl.pallas_call(
        _swiglu_body,
        grid=(m // tm, d_ff // tn),
        in_specs=[pl.BlockSpec((tm, k), lambda i, j: (i, 0)),
                  pl.BlockSpec((k, tn), lambda i, j: (0, j)),
                  pl.BlockSpec((k, tn), lambda i, j: (0, up_blk + j))],
        out_specs=pl.BlockSpec((tm, tn), lambda i, j: (i, j)),
        out_shape=jax.ShapeDtypeStruct((m, d_ff), BF16),
        compiler_params=_params(("arbitrary", "arbitrary"), _vmem_limit(blocks)),
        name="swiglu_up",
    )(a, w_gate_up, w_gate_up)


def _mm_ln_body(a_ref, w_ref, r_ref, g_ref, b_ref, *rest, nk, emit_bf16):
    if emit_bf16:
        of_ref, ob_ref, acc_ref = rest
    else:
        of_ref, acc_ref = rest
    kk = pl.program_id(1)

    @pl.when(kk == 0)
    def _():
        acc_ref[...] = jnp.zeros_like(acc_ref)

    acc_ref[...] += jnp.dot(a_ref[...], w_ref[...].astype(BF16), preferred_element_type=F32)

    @pl.when(kk == nk - 1)
    def _():
        y = ALPHA * r_ref[...] + acc_ref[...]
        mu = jnp.mean(y, axis=-1, keepdims=True)
        yc = y - mu
        var = jnp.mean(yc * yc, axis=-1, keepdims=True)
        out = yc * lax.rsqrt(var + LN_EPS) * g_ref[...] + b_ref[...]
        of_ref[...] = out
        if emit_bf16:
            ob_ref[...] = out.astype(BF16)


def _matmul_residual_ln(a, w, resid, gain, bias, *, tm, tk, emit_bf16):
    m, k = a.shape
    n = w.shape[1]
    nk = k // tk
    out_shape = [jax.ShapeDtypeStruct((m, n), F32)]
    out_specs = [pl.BlockSpec((tm, n), lambda i, kk: (i, 0))]
    blocks = [((tm, tk), BF16), ((tk, n), w.dtype), ((tm, n), F32), ((tm, n), F32)]
    if emit_bf16:
        out_shape.append(jax.ShapeDtypeStruct((m, n), BF16))
        out_specs.append(pl.BlockSpec((tm, n), lambda i, kk: (i, 0)))
        blocks.append(((tm, n), BF16))
    scratch = [((tm, n), F32)]
    return pl.pallas_call(
        functools.partial(_mm_ln_body, nk=nk, emit_bf16=emit_bf16),
        grid=(m // tm, nk),
        in_specs=[pl.BlockSpec((tm, tk), lambda i, kk: (i, kk)),
                  pl.BlockSpec((tk, n), lambda i, kk: (kk, 0)),
                  pl.BlockSpec((tm, n), lambda i, kk: (i, 0)),
                  pl.BlockSpec((1, n), lambda i, kk: (0, 0)),
                  pl.BlockSpec((1, n), lambda i, kk: (0, 0))],
        out_specs=out_specs,
        out_shape=out_shape,
        scratch_shapes=[pltpu.VMEM(s, d) for s, d in scratch],
        compiler_params=_params(("arbitrary", "arbitrary"), _vmem_limit(blocks, scratch)),
        name="matmul_residual_ln",
    )(a, w, resid, gain.reshape(1, n), bias.reshape(1, n))


def _rel_bucket_np(dist):
    n = np.maximum(dist, 0)
    exact = N_BUCKETS // 2
    logv = (np.log(np.maximum(n, 1).astype(np.float32) / exact) / math.log(MAX_DISTANCE / exact))
    large = exact + (logv.astype(np.float32) * (N_BUCKETS - exact)).astype(np.int32)
    large = np.minimum(large, N_BUCKETS - 1)
    return np.where(n < exact, n, large).astype(np.int32)


def _bias_tiles(table_cols, dist, valid, shift_last_bucket):
    bucket = _rel_bucket_np(dist)
    tab = table_cols.astype(F32)
    if shift_last_bucket:
        tab = tab - tab[N_BUCKETS - 1:N_BUCKETS, :]
    b = jnp.take(tab, jnp.asarray(bucket.reshape(-1)), axis=0)
    b = b.T.reshape((tab.shape[1],) + dist.shape)
    return jnp.where(jnp.asarray(valid)[None], b, NEG)


def _swa_body(sink_ref, q_ref, kp_ref, kc_ref, vp_ref, vc_ref, bias_ref, o_ref):
    n = pl.program_id(0)
    half = DH_A
    lane_k = lax.broadcasted_iota(jnp.int32, (2 * BLK, 2 * half), 1)
    col = lax.broadcasted_iota(jnp.int32, (BLK, 2 * BLK), 1)
    lane_o = lax.broadcasted_iota(jnp.int32, (BLK, 2 * half), 1)
    no_prev = jnp.logical_and(n == 0, col < BLK)
    scale = DH_A ** -0.5

    kf = jnp.concatenate([kp_ref[...], kc_ref[...]], axis=0).astype(F32)
    vf = jnp.concatenate([vp_ref[...], vc_ref[...]], axis=0).astype(F32)
    kr = pltpu.roll(kf, half, 1)
    vr = pltpu.roll(vf, half, 1)

    outs = []
    for g in range(KV_A):
        k_own, k_other = (kf, kr) if g == 0 else (kr, kf)
        k_lo = jnp.where(lane_k < half, k_own, 0.0).astype(BF16)
        k_hi = jnp.where(lane_k >= half, k_other, 0.0).astype(BF16)
        v_dup = (jnp.where(lane_k < half, vf, vr) if g == 0 else jnp.where(lane_k < half, vr, vf)).astype(BF16)
        for t in range(G_A // 2):
            pair = g * (G_A // 2) + t
            q_pair = q_ref[:, pair * 2 * half:(pair + 1) * 2 * half]
            o_pair = []
            for par, k_sel in ((0, k_lo), (1, k_hi)):
                h = 2 * pair + par
                s = lax.dot_general(q_pair, k_sel, (((1,), (1,)), ((), ())),
                                    preferred_element_type=F32) * scale + bias_ref[h]
                s = jnp.where(no_prev, NEG, s)
                sink = sink_ref[h]
                mx = jnp.maximum(jnp.max(s, axis=1, keepdims=True), sink)
                p = jnp.exp(s - mx)
                den = jnp.sum(p, axis=1, keepdims=True) + jnp.exp(sink - mx)
                o_pair.append(jnp.dot(p.astype(BF16), v_dup, preferred_element_type=F32) / den)
            outs.append(jnp.where(lane_o < half, o_pair[0], o_pair[1]))
    o_ref[...] = jnp.concatenate(outs, axis=1).astype(o_ref.dtype)


def _swa_attention(proj, sinks, bias, *, q_col, k_col, v_col):
    s = proj.shape[0]
    nb = s // BLK
    qw = HA * DH_A
    kvw = KV_A * DH_A
    assert kvw == V7X_LANES and q_col % qw == 0 and k_col % kvw == 0 and v_col % kvw == 0
    kb, vb = k_col // kvw, v_col // kvw
    blocks = [((BLK, qw), BF16)] + [((BLK, kvw), BF16)] * 4 + [((HA, BLK, 2 * BLK), F32), ((BLK, qw), BF16)]
    prev = lambda n: jnp.maximum(n - 1, 0)
    return pl.pallas_call(
        _swa_body,
        grid=(nb,),
        in_specs=[pl.BlockSpec(memory_space=pltpu.SMEM),
                  pl.BlockSpec((BLK, qw), lambda n: (n, q_col // qw)),
                  pl.BlockSpec((BLK, kvw), lambda n: (prev(n), kb)),
                  pl.BlockSpec((BLK, kvw), lambda n: (n, kb)),
                  pl.BlockSpec((BLK, kvw), lambda n: (prev(n), vb)),
                  pl.BlockSpec((BLK, kvw), lambda n: (n, vb)),
                  pl.BlockSpec((HA, BLK, 2 * BLK), lambda n: (0, 0, 0))],
        out_specs=pl.BlockSpec((BLK, qw), lambda n: (n, 0)),
        out_shape=jax.ShapeDtypeStruct((s, qw), BF16),
        compiler_params=_params(("arbitrary",), _vmem_limit(blocks)),
        name="swa_sink_attention",
    )(sinks.astype(F32), proj, proj, proj, proj, proj, bias)


def _diff_body(q_ref, k_ref, vt_ref, bd_ref, bp_ref, lq1_ref, lk1_ref, lq2_ref, lk2_ref, sw_ref,
               o_ref, qz_ref, acc_ref, m_ref, l_ref, *, tq, tk):
    i = pl.program_id(1)
    width = 2 * DH_B

    qt = (q_ref[...].astype(F32) * (DH_B ** -0.5)).T
    row = lax.broadcasted_iota(jnp.int32, (width, tq), 0)
    qz_ref[0] = jnp.where(row < DH_B, qt, 0.0).astype(BF16)
    qz_ref[1] = jnp.where(row >= DH_B, qt, 0.0).astype(BF16)
    acc_ref[...] = jnp.zeros_like(acc_ref)
    m_ref[...] = jnp.full_like(m_ref, NEG)
    l_ref[...] = jnp.zeros_like(l_ref)

    def key_tile(j, bias):
        kblk = k_ref[pl.ds(pl.multiple_of(j * tk, tk), tk), :]
        vt = vt_ref[j]
        for c in range(2):
            s = jnp.dot(kblk, qz_ref[c], preferred_element_type=F32)
            if bias is not None:
                s = s + bias
            m_old = m_ref[c]
            m_new = jnp.maximum(m_old, jnp.max(s, axis=0, keepdims=True))
            rescale = jnp.exp(m_old - m_new)
            p = jnp.exp(s - m_new)
            l_ref[c] = rescale * l_ref[c] + jnp.sum(p, axis=0, keepdims=True)
            acc_ref[c] = rescale * acc_ref[c] + jnp.dot(vt, p.astype(BF16), preferred_element_type=F32)
            m_ref[c] = m_new

    def far_tile(j, carry):
        key_tile(j, None)
        return carry

    lax.fori_loop(0, jnp.maximum(i - 1, 0), far_tile, 0)

    @pl.when(i > 0)
    def _():
        key_tile(i - 1, bp_ref[...])

    key_tile(i, bd_ref[...])

    lam = (jnp.exp(jnp.sum(lq1_ref[...] * lk1_ref[...], axis=1, keepdims=True))
           - jnp.exp(jnp.sum(lq2_ref[...] * lk2_ref[...], axis=1, keepdims=True)) + LAMBDA_INIT)
    o = acc_ref[0] / l_ref[0] - lam * (acc_ref[1] / l_ref[1])
    ms = jnp.mean(o * o, axis=0, keepdims=True)
    o = o * lax.rsqrt(ms + LN_EPS) * sw_ref[...] * (1.0 - LAMBDA_INIT)
    o_ref[...] = o.T.astype(o_ref.dtype)


def _diff_attention(proj, vt, bias_diag, bias_prev, lq1, lk1, lq2, lk2, subln_w, *, q_col, k_col, tq, tk):
    s = proj.shape[0]
    width = 2 * DH_B
    assert width == V7X_LANES and tq == tk and q_col % width == 0 and k_col % width == 0
    qb, kb = q_col // width, k_col // width
    nkt = s // tk
    blocks = [((tq, width), BF16), ((s, width), BF16), ((nkt, width, tk), BF16),
              ((tk, tq), F32), ((tk, tq), F32), ((tq, width), BF16)]
    scratch = [((2, width, tq), BF16), ((2, width, tq), F32), ((2, 8, tq), F32), ((2, 8, tq), F32)]
    vec = lambda v: v.astype(F32).reshape(1, DH_B)
    small = pl.BlockSpec((1, DH_B), lambda h, i: (0, 0))
    return pl.pallas_call(
        functools.partial(_diff_body, tq=tq, tk=tk),
        grid=(HB, s // tq),
        in_specs=[pl.BlockSpec((tq, width), lambda h, i: (i, qb + h)),
                  pl.BlockSpec((s, width), lambda h, i: (0, kb + h)),
                  pl.BlockSpec((None, nkt, width, tk), lambda h, i: (h, 0, 0, 0)),
                  pl.BlockSpec((None, tk, tq), lambda h, i: (h, 0, 0)),
                  pl.BlockSpec((None, tk, tq), lambda h, i: (h, 0, 0)),
                  small, small, small, small,
                  pl.BlockSpec((width, 1), lambda h, i: (0, 0))],
        out_specs=pl.BlockSpec((tq, width), lambda h, i: (i, h)),
        out_shape=jax.ShapeDtypeStruct((s, HB * width), BF16),
        scratch_shapes=[pltpu.VMEM((2, width, tq), BF16), pltpu.VMEM((2, width, tq), F32),
                        pltpu.VMEM((2, 1, tq), F32), pltpu.VMEM((2, 1, tq), F32)],
        compiler_params=_params(("arbitrary", "arbitrary"), _vmem_limit(blocks, scratch)),
        name="diff_attention",
    )(proj, proj, vt, bias_diag, bias_prev, vec(lq1), vec(lk1), vec(lq2), vec(lk2),
      subln_w.astype(F32).reshape(width, 1))


def _xattn_body(q_ref, kv_ref, o_ref):
    scale = DH_C ** -0.5
    outs = []
    for h in range(HC):
        qh = q_ref[:, h * DH_C:(h + 1) * DH_C]
        kh = kv_ref[:, h * DH_C:(h + 1) * DH_C]
        vh = kv_ref[:, (HC + h) * DH_C:(HC + h + 1) * DH_C]
        s = lax.dot_general(qh, kh, (((1,), (1,)), ((), ())), preferred_element_type=F32) * scale
        mx = jnp.max(s, axis=1, keepdims=True)
        p = jnp.exp(s - mx)
        den = jnp.sum(p, axis=1, keepdims=True)
        outs.append(jnp.dot(p.astype(BF16), vh, preferred_element_type=F32) / den)
    o_ref[...] = jnp.concatenate(outs, axis=1).astype(o_ref.dtype)


def _cross_attention(q, kv, *, tq):
    s, w = q.shape
    mlen = kv.shape[0]
    blocks = [((tq, w), BF16), ((mlen, 2 * w), BF16), ((tq, w), BF16)]
    return pl.pallas_call(
        _xattn_body,
        grid=(s // tq,),
        in_specs=[pl.BlockSpec((tq, w), lambda i: (i, 0)),
                  pl.BlockSpec((mlen, 2 * w), lambda i: (0, 0))],
        out_specs=pl.BlockSpec((tq, w), lambda i: (i, 0)),
        out_shape=jax.ShapeDtypeStruct((s, w), BF16),
        compiler_params=_params(("arbitrary",), _vmem_limit(blocks)),
        name="memory_cross_attention",
    )(q, kv)


def kernel(x, mem, rel_bias_table, w_in, sinks, lambda_q1, lambda_k1, lambda_q2, lambda_k2, subln_w,
           w_branch_a, w_branch_b, w_o, ln1_g, ln1_b, w_cq, w_mem_kv, w_co, ln2_g, ln2_b,
           w_gate_up, w_down, ln3_g, ln3_b):
    b, s, d = x.shape
    assert b == 1 and w_in.shape[0] == DEPTH == 1
    l = 0
    qa_w, kva_w, qb_w = HA * DH_A, KV_A * DH_A, HB * 2 * DH_B
    col_qa, col_ka, col_va = 0, qa_w, qa_w + kva_w
    col_qb = col_va + kva_w
    col_kb, col_vb = col_qb + qb_w, col_qb + 2 * qb_w
    col_ga = col_vb + qb_w
    col_gb = col_ga + d
    tq = tk = 256

    h0 = x.reshape(s, d)

    proj = _matmul(h0, w_in[l], tm=1024, tn=768, out_dtype=BF16, name="in_proj")

    i_a = np.arange(BLK)[:, None]
    j_a = np.arange(2 * BLK)[None, :]
    dist_a = BLK + i_a - j_a
    bias_a = _bias_tiles(rel_bias_table[:, :HA], dist_a, (dist_a >= 0) & (dist_a < WINDOW), False)
    o_a = _swa_attention(proj, sinks[l], bias_a, q_col=col_qa, k_col=col_ka, v_col=col_va)

    r_b = np.arange(tk)[:, None]
    c_b = np.arange(tq)[None, :]
    dist_diag = c_b - r_b
    dist_prev = dist_diag + tk
    bias_diag = _bias_tiles(rel_bias_table[:, HA:], dist_diag, dist_diag >= 0, True)
    bias_prev = _bias_tiles(rel_bias_table[:, HA:], dist_prev, dist_prev >= 0, True)
    vt = proj[:, col_vb:col_vb + qb_w].reshape(s // tk, tk, HB, 2 * DH_B).transpose(2, 0, 3, 1)
    o_b = _diff_attention(proj, vt, bias_diag, bias_prev, lambda_q1[l], lambda_k1[l], lambda_q2[l],
                          lambda_k2[l], subln_w[l], q_col=col_qb, k_col=col_kb, tq=tq, tk=tk)

    mix = _gated_branches(o_a, o_b, w_branch_a[l], w_branch_b[l], proj, col_ga, col_gb, tm=1024, tn=256)
    h1, h1b = _matmul_residual_ln(mix, w_o[l], h0, ln1_g[l], ln1_b[l], tm=512, tk=512, emit_bf16=True)

    qc = _matmul(h1b, w_cq[l], tm=1024, tn=HC * DH_C, out_dtype=BF16, name="cross_q")
    kvm = _matmul(mem.reshape(mem.shape[1], d), w_mem_kv[l], tm=mem.shape[1], tn=HC * DH_C,
                  out_dtype=BF16, name="mem_kv")
    oc = _cross_attention(qc, kvm, tq=1024)
    h2, h2b = _matmul_residual_ln(oc, w_co[l], h1, ln2_g[l], ln2_b[l], tm=512, tk=HC * DH_C, emit_bf16=True)

    act = _swiglu_up(h2b, w_gate_up[l], tm=1024, tn=512)
    (h3,) = _matmul_residual_ln(act, w_down[l], h2, ln3_g[l], ln3_b[l], tm=512, tk=512, emit_bf16=False)
    return h3.reshape(b, s, d)
```

```python
import functools
import math

import numpy as np
import jax
import jax.numpy as jnp
from jax import lax
from jax.experimental import pallas as pl
from jax.experimental.pallas import tpu as pltpu

F32 = jnp.float32
BF16 = jnp.bfloat16

BLK = 128
WINDOW = 128
HA, KV_A, DH_A = 16, 2, 64
G_A = HA // KV_A
HB, DH_B = 8, 64
N_BUCKETS, MAX_DISTANCE = 32, 128
HC, DH_C = 4, 128
LN_EPS = 1e-5
DEPTH = 1
ALPHA = (2 * DEPTH) ** 0.25
LAMBDA_INIT = 0.8 - 0.6 * math.exp(-0.3 * 0)

V7X_LANES = 128
V7X_VMEM_BYTES = 64 * 1024 * 1024
V7X_VMEM_TEMP_BYTES = 12 * 1024 * 1024

NEG = -1e30


def _nbytes(shape, dtype):
    return int(np.prod(shape)) * jnp.dtype(dtype).itemsize


def _vmem_limit(pipelined, scratch=()):
    need = 2 * sum(_nbytes(s, d) for s, d in pipelined) + sum(_nbytes(s, d) for s, d in scratch)
    need += V7X_VMEM_TEMP_BYTES
    assert need <= V7X_VMEM_BYTES - 4 * 1024 * 1024, need
    return need


def _params(semantics, vmem):
    return pltpu.CompilerParams(dimension_semantics=semantics, vmem_limit_bytes=vmem)


def _act_tile(a_ref, abf_ref):
    if abf_ref is None:
        return a_ref[...]

    @pl.when(pl.program_id(1) == 0)
    def _():
        abf_ref[...] = a_ref[...].astype(BF16)

    return abf_ref[...]


def _mm_body(a_ref, w_ref, o_ref, *scratch):
    a = _act_tile(a_ref, scratch[0] if scratch else None)
    o_ref[...] = jnp.dot(a, w_ref[...].astype(BF16), preferred_element_type=F32).astype(o_ref.dtype)


def _matmul(a, w, *, tm, tn, out_dtype, name):
    m, k = a.shape
    n = w.shape[1]
    cast = a.dtype != BF16
    scratch = [((tm, k), BF16)] if cast else []
    blocks = [((tm, k), a.dtype), ((k, tn), w.dtype), ((tm, tn), out_dtype)]
    return pl.pallas_call(
        _mm_body,
        grid=(m // tm, n // tn),
        in_specs=[pl.BlockSpec((tm, k), lambda i, j: (i, 0)),
                  pl.BlockSpec((k, tn), lambda i, j: (0, j))],
        out_specs=pl.BlockSpec((tm, tn), lambda i, j: (i, j)),
        out_shape=jax.ShapeDtypeStruct((m, n), out_dtype),
        scratch_shapes=[pltpu.VMEM(s, d) for s, d in scratch],
        compiler_params=_params(("arbitrary", "arbitrary"), _vmem_limit(blocks, scratch)),
        name=name,
    )(a, w)


def _branch_body(oa_ref, ob_ref, wa_ref, wb_ref, ga_ref, gb_ref, o_ref):
    ya = jnp.dot(oa_ref[...], wa_ref[...].astype(BF16), preferred_element_type=F32)
    yb = jnp.dot(ob_ref[...], wb_ref[...].astype(BF16), preferred_element_type=F32)
    ga = jax.nn.sigmoid(ga_ref[...].astype(F32))
    gb = jax.nn.sigmoid(gb_ref[...].astype(F32))
    o_ref[...] = (ga * ya + gb * yb).astype(o_ref.dtype)


def _gated_branches(o_a, o_b, w_a, w_b, proj, ga_col, gb_col, *, tm, tn):
    m, ka = o_a.shape
    kb = o_b.shape[1]
    n = w_a.shape[1]
    ga_blk, gb_blk = ga_col // tn, gb_col // tn
    blocks = [((tm, ka), BF16), ((tm, kb), BF16), ((ka, tn), F32), ((kb, tn), F32),
              ((tm, tn), proj.dtype), ((tm, tn), proj.dtype), ((tm, tn), BF16)]
    return pl.pallas_call(
        _branch_body,
        grid=(m // tm, n // tn),
        in_specs=[pl.BlockSpec((tm, ka), lambda i, j: (i, 0)),
                  pl.BlockSpec((tm, kb), lambda i, j: (i, 0)),
                  pl.BlockSpec((ka, tn), lambda i, j: (0, j)),
                  pl.BlockSpec((kb, tn), lambda i, j: (0, j)),
                  pl.BlockSpec((tm, tn), lambda i, j: (i, ga_blk + j)),
                  pl.BlockSpec((tm, tn), lambda i, j: (i, gb_blk + j))],
        out_specs=pl.BlockSpec((tm, tn), lambda i, j: (i, j)),
        out_shape=jax.ShapeDtypeStruct((m, n), BF16),
        compiler_params=_params(("arbitrary", "arbitrary"), _vmem_limit(blocks)),
        name="gated_branches",
    )(o_a, o_b, w_a, w_b, proj, proj)


def _swiglu_body(a_ref, wg_ref, wu_ref, o_ref):
    a = a_ref[...]
    g = jnp.dot(a, wg_ref[...].astype(BF16), preferred_element_type=F32)
    u = jnp.dot(a, wu_ref[...].astype(BF16), preferred_element_type=F32)
    o_ref[...] = (g * jax.nn.sigmoid(g) * u).astype(o_ref.dtype)


def _swiglu_up(a, w_gate_up, *, tm, tn):
    m, k = a.shape
    d_ff = w_gate_up.shape[1] // 2
    up_blk = d_ff // tn
    blocks = [((tm, k), BF16), ((k, tn), F32), ((k, tn), F32), ((tm, tn), BF16)]
    return pl.pallas_call(
        _swiglu_body,
        grid=(m // tm, d_ff // tn),
        in_specs=[pl.BlockSpec((tm, k), lambda i, j: (i, 0)),
                  pl.BlockSpec((k, tn), lambda i, j: (0, j)),
                  pl.BlockSpec((k, tn), lambda i, j: (0, up_blk + j))],
        out_specs=pl.BlockSpec((tm, tn), lambda i, j: (i, j)),
        out_shape=jax.ShapeDtypeStruct((m, d_ff), BF16),
        compiler_params=_params(("arbitrary", "arbitrary"), _vmem_limit(blocks)),
        name="swiglu_up",
    )(a, w_gate_up, w_gate_up)


def _mm_ln_body(a_ref, w_ref, r_ref, g_ref, b_ref, *rest, nk, emit_bf16):
    if emit_bf16:
        of_ref, ob_ref, acc_ref = rest
    else:
        of_ref, acc_ref = rest
    kk = pl.program_id(1)

    @pl.when(kk == 0)
    def _():
        acc_ref[...] = jnp.zeros_like(acc_ref)

    acc_ref[...] += jnp.dot(a_ref[...], w_ref[...].astype(BF16), preferred_element_type=F32)

    @pl.when(kk == nk - 1)
    def _():
        y = ALPHA * r_ref[...] + acc_ref[...]
        mu = jnp.mean(y, axis=-1, keepdims=True)
        yc = y - mu
        var = jnp.mean(yc * yc, axis=-1, keepdims=True)
        out = yc * lax.rsqrt(var + LN_EPS) * g_ref[...] + b_ref[...]
        of_ref[...] = out
        if emit_bf16:
            ob_ref[...] = out.astype(BF16)


def _matmul_residual_ln(a, w, resid, gain, bias, *, tm, tk, emit_bf16):
    m, k = a.shape
    n = w.shape[1]
    nk = k // tk
    out_shape = [jax.ShapeDtypeStruct((m, n), F32)]
    out_specs = [pl.BlockSpec((tm, n), lambda i, kk: (i, 0))]
    blocks = [((tm, tk), BF16), ((tk, n), w.dtype), ((tm, n), F32), ((tm, n), F32)]
    if emit_bf16:
        out_shape.append(jax.ShapeDtypeStruct((m, n), BF16))
        out_specs.append(pl.BlockSpec((tm, n), lambda i, kk: (i, 0)))
        blocks.append(((tm, n), BF16))
    scratch = [((tm, n), F32)]
    return pl.pallas_call(
        functools.partial(_mm_ln_body, nk=nk, emit_bf16=emit_bf16),
        grid=(m // tm, nk),
        in_specs=[pl.BlockSpec((tm, tk), lambda i, kk: (i, kk)),
                  pl.BlockSpec((tk, n), lambda i, kk: (kk, 0)),
                  pl.BlockSpec((tm, n), lambda i, kk: (i, 0)),
                  pl.BlockSpec((1, n), lambda i, kk: (0, 0)),
                  pl.BlockSpec((1, n), lambda i, kk: (0, 0))],
        out_specs=out_specs,
        out_shape=out_shape,
        scratch_shapes=[pltpu.VMEM(s, d) for s, d in scratch],
        compiler_params=_params(("arbitrary", "arbitrary"), _vmem_limit(blocks, scratch)),
        name="matmul_residual_ln",
    )(a, w, resid, gain.reshape(1, n), bias.reshape(1, n))


def _rel_bucket_np(dist):
    n = np.maximum(dist, 0)
    exact = N_BUCKETS // 2
    logv = (np.log(np.maximum(n, 1).astype(np.float32) / exact) / math.log(MAX_DISTANCE / exact))
    large = exact + (logv.astype(np.float32) * (N_BUCKETS - exact)).astype(np.int32)
    large = np.minimum(large, N_BUCKETS - 1)
    return np.where(n < exact, n, large).astype(np.int32)


def _bias_expand_body(tab_ref, bucket_ref, o_ref, *, head0, shift_last_bucket, zero_tiles):
    h = head0 + pl.program_id(0)
    last = tab_ref[N_BUCKETS - 1, h] if shift_last_bucket else 0.0
    for z in range(zero_tiles):
        o_ref[z] = jnp.zeros(o_ref.shape[1:], F32)
    for t in range(bucket_ref.shape[0]):
        bucket = bucket_ref[t]
        acc = jnp.full(bucket.shape, NEG, F32)
        for bkt in range(N_BUCKETS):
            acc = jnp.where(bucket == bkt, tab_ref[bkt, h] - last, acc)
        o_ref[zero_tiles + t] = acc


def _bias_expand(table, dists, valids, *, head0, heads, shift_last_bucket, zero_tiles):
    bucket = np.stack([np.where(v, _rel_bucket_np(d), -1) for d, v in zip(dists, valids)]).astype(np.int32)
    nt, r, c = bucket.shape
    blocks = [((nt, r, c), jnp.int32), ((zero_tiles + nt, r, c), F32)]
    return pl.pallas_call(
        functools.partial(_bias_expand_body, head0=head0, shift_last_bucket=shift_last_bucket,
                          zero_tiles=zero_tiles),
        grid=(heads,),
        in_specs=[pl.BlockSpec(memory_space=pltpu.SMEM),
                  pl.BlockSpec((nt, r, c), lambda h: (0, 0, 0))],
        out_specs=pl.BlockSpec((None, zero_tiles + nt, r, c), lambda h: (h, 0, 0, 0)),
        out_shape=jax.ShapeDtypeStruct((heads, zero_tiles + nt, r, c), F32),
        compiler_params=_params(("arbitrary",), _vmem_limit(blocks)),
        name="bias_expand",
    )(table.astype(F32), jnp.asarray(bucket))


def _swa_body(sink_ref, q_ref, kp_ref, kc_ref, vp_ref, vc_ref, bias_ref, o_ref):
    n = pl.program_id(0)
    half = DH_A
    lane_k = lax.broadcasted_iota(jnp.int32, (2 * BLK, 2 * half), 1)
    col = lax.broadcasted_iota(jnp.int32, (BLK, 2 * BLK), 1)
    lane_o = lax.broadcasted_iota(jnp.int32, (BLK, 2 * half), 1)
    no_prev = jnp.logical_and(n == 0, col < BLK)
    scale = DH_A ** -0.5

    kf = jnp.concatenate([kp_ref[...], kc_ref[...]], axis=0).astype(F32)
    vf = jnp.concatenate([vp_ref[...], vc_ref[...]], axis=0).astype(F32)
    kr = pltpu.roll(kf, half, 1)
    vr = pltpu.roll(vf, half, 1)

    outs = []
    for g in range(KV_A):
        k_own, k_other = (kf, kr) if g == 0 else (kr, kf)
        k_lo = jnp.where(lane_k < half, k_own, 0.0).astype(BF16)
        k_hi = jnp.where(lane_k >= half, k_other, 0.0).astype(BF16)
        v_dup = (jnp.where(lane_k < half, vf, vr) if g == 0 else jnp.where(lane_k < half, vr, vf)).astype(BF16)
        for t in range(G_A // 2):
            pair = g * (G_A // 2) + t
            q_pair = q_ref[:, pair * 2 * half:(pair + 1) * 2 * half]
            o_pair = []
            for par, k_sel in ((0, k_lo), (1, k_hi)):
                h = 2 * pair + par
                s = lax.dot_general(q_pair, k_sel, (((1,), (1,)), ((), ())),
                                    preferred_element_type=F32) * scale + bias_ref[h]
                s = jnp.where(no_prev, NEG, s)
                sink = sink_ref[h]
                mx = jnp.maximum(jnp.max(s, axis=1, keepdims=True), sink)
                p = jnp.exp(s - mx)
                den = jnp.sum(p, axis=1, keepdims=True) + jnp.exp(sink - mx)
                o_pair.append(jnp.dot(p.astype(BF16), v_dup, preferred_element_type=F32) / den)
            outs.append(jnp.where(lane_o < half, o_pair[0], o_pair[1]))
    o_ref[...] = jnp.concatenate(outs, axis=1).astype(o_ref.dtype)


def _swa_attention(proj, sinks, bias, *, q_col, k_col, v_col):
    s = proj.shape[0]
    nb = s // BLK
    qw = HA * DH_A
    kvw = KV_A * DH_A
    assert kvw == V7X_LANES and q_col % qw == 0 and k_col % kvw == 0 and v_col % kvw == 0
    kb, vb = k_col // kvw, v_col // kvw
    blocks = [((BLK, qw), BF16)] + [((BLK, kvw), BF16)] * 4 + [((HA, BLK, 2 * BLK), F32), ((BLK, qw), BF16)]
    prev = lambda n: jnp.maximum(n - 1, 0)
    return pl.pallas_call(
        _swa_body,
        grid=(nb,),
        in_specs=[pl.BlockSpec(memory_space=pltpu.SMEM),
                  pl.BlockSpec((BLK, qw), lambda n: (n, q_col // qw)),
                  pl.BlockSpec((BLK, kvw), lambda n: (prev(n), kb)),
                  pl.BlockSpec((BLK, kvw), lambda n: (n, kb)),
                  pl.BlockSpec((BLK, kvw), lambda n: (prev(n), vb)),
                  pl.BlockSpec((BLK, kvw), lambda n: (n, vb)),
                  pl.BlockSpec((HA, BLK, 2 * BLK), lambda n: (0, 0, 0))],
        out_specs=pl.BlockSpec((BLK, qw), lambda n: (n, 0)),
        out_shape=jax.ShapeDtypeStruct((s, qw), BF16),
        compiler_params=_params(("arbitrary",), _vmem_limit(blocks)),
        name="swa_sink_attention",
    )(sinks.astype(F32), proj, proj, proj, proj, proj, bias)


def _diff_body(q_ref, k_ref, vt_ref, b3_ref, lq1_ref, lk1_ref, lq2_ref, lk2_ref, sw_ref,
               o_ref, qz_ref, sa_ref, sb_ref, cma_ref, cmb_ref, acc_ref, m_ref, l_ref, *, tq, tk):
    i = pl.program_id(1)
    width = 2 * DH_B

    qt = (q_ref[...].astype(F32) * (DH_B ** -0.5)).T
    row = lax.broadcasted_iota(jnp.int32, (width, tq), 0)
    qz_ref[0] = jnp.where(row < DH_B, qt, 0.0).astype(BF16)
    qz_ref[1] = jnp.where(row >= DH_B, qt, 0.0).astype(BF16)
    acc_ref[...] = jnp.zeros_like(acc_ref)
    m_ref[...] = jnp.full_like(m_ref, NEG)
    l_ref[...] = jnp.zeros_like(l_ref)

    def scores(j, s_ref, cm_ref):
        kblk = k_ref[pl.ds(pl.multiple_of(j * tk, tk), tk), :]
        bias = b3_ref[jnp.clip(j - (i - 2), 0, 2)]
        for c in range(2):
            s = jnp.dot(kblk, qz_ref[c], preferred_element_type=F32) + bias
            s_ref[c] = s
            cm_ref[c] = jnp.max(s, axis=0, keepdims=True)

    def accumulate(j, s_ref, cm_ref):
        vt = vt_ref[j]
        for c in range(2):
            m_old = m_ref[c]
            m_new = jnp.maximum(m_old, cm_ref[c])
            rescale = jnp.exp(m_old - m_new)
            p = jnp.exp(s_ref[c] - m_new)
            l_ref[c] = rescale * l_ref[c] + jnp.sum(p, axis=0, keepdims=True)
            acc_ref[c] = rescale * acc_ref[c] + jnp.dot(vt, p.astype(BF16), preferred_element_type=F32)
            m_ref[c] = m_new

    n_tiles = i + 1
    scores(0, sa_ref, cma_ref)

    def pair(t, carry):
        scores(2 * t + 1, sb_ref, cmb_ref)
        accumulate(2 * t, sa_ref, cma_ref)
        scores(jnp.minimum(2 * t + 2, i), sa_ref, cma_ref)
        accumulate(2 * t + 1, sb_ref, cmb_ref)
        return carry

    lax.fori_loop(0, n_tiles // 2, pair, 0)

    @pl.when(n_tiles % 2 == 1)
    def _():
        accumulate(i, sa_ref, cma_ref)

    lam = (jnp.exp(jnp.sum(lq1_ref[...] * lk1_ref[...], axis=1, keepdims=True))
           - jnp.exp(jnp.sum(lq2_ref[...] * lk2_ref[...], axis=1, keepdims=True)) + LAMBDA_INIT)
    o = acc_ref[0] / l_ref[0] - lam * (acc_ref[1] / l_ref[1])
    ms = jnp.mean(o * o, axis=0, keepdims=True)
    o = o * lax.rsqrt(ms + LN_EPS) * sw_ref[...] * (1.0 - LAMBDA_INIT)
    o_ref[...] = o.T.astype(o_ref.dtype)


def _diff_attention(proj, vt, bias3, lq1, lk1, lq2, lk2, subln_w, *, q_col, k_col, tq, tk):
    s = proj.shape[0]
    width = 2 * DH_B
    assert width == V7X_LANES and tq == tk and q_col % width == 0 and k_col % width == 0
    qb, kb = q_col // width, k_col // width
    nkt = s // tk
    blocks = [((tq, width), BF16), ((s, width), BF16), ((nkt, width, tk), BF16),
              ((3, tk, tq), F32), ((tq, width), BF16)]
    scratch = [((2, width, tq), BF16), ((2, tk, tq), F32), ((2, tk, tq), F32),
               ((2, 8, tq), F32), ((2, 8, tq), F32),
               ((2, width, tq), F32), ((2, 8, tq), F32), ((2, 8, tq), F32)]
    stat = pltpu.VMEM((2, 1, tq), F32)
    vec = lambda v: v.astype(F32).reshape(1, DH_B)
    small = pl.BlockSpec((1, DH_B), lambda h, i: (0, 0))
    return pl.pallas_call(
        functools.partial(_diff_body, tq=tq, tk=tk),
        grid=(HB, s // tq),
        in_specs=[pl.BlockSpec((tq, width), lambda h, i: (i, qb + h)),
                  pl.BlockSpec((s, width), lambda h, i: (0, kb + h)),
                  pl.BlockSpec((None, nkt, width, tk), lambda h, i: (h, 0, 0, 0)),
                  pl.BlockSpec((None, 3, tk, tq), lambda h, i: (h, 0, 0, 0)),
                  small, small, small, small,
                  pl.BlockSpec((width, 1), lambda h, i: (0, 0))],
        out_specs=pl.BlockSpec((tq, width), lambda h, i: (i, h)),
        out_shape=jax.ShapeDtypeStruct((s, HB * width), BF16),
        scratch_shapes=[pltpu.VMEM((2, width, tq), BF16),
                        pltpu.VMEM((2, tk, tq), F32), pltpu.VMEM((2, tk, tq), F32), stat, stat,
                        pltpu.VMEM((2, width, tq), F32), stat, stat],
        compiler_params=_params(("arbitrary", "arbitrary"), _vmem_limit(blocks, scratch)),
        name="diff_attention",
    )(proj, proj, vt, bias3, vec(lq1), vec(lk1), vec(lq2), vec(lk2),
      subln_w.astype(F32).reshape(width, 1))


def _xattn_body(q_ref, kv_ref, o_ref):
    scale = DH_C ** -0.5
    outs = []
    for h in range(HC):
        qh = q_ref[:, h * DH_C:(h + 1) * DH_C]
        kh = kv_ref[:, h * DH_C:(h + 1) * DH_C]
        vh = kv_ref[:, (HC + h) * DH_C:(HC + h + 1) * DH_C]
        s = lax.dot_general(qh, kh, (((1,), (1,)), ((), ())), preferred_element_type=F32) * scale
        mx = jnp.max(s, axis=1, keepdims=True)
        p = jnp.exp(s - mx)
        den = jnp.sum(p, axis=1, keepdims=True)
        outs.append(jnp.dot(p.astype(BF16), vh, preferred_element_type=F32) / den)
    o_ref[...] = jnp.concatenate(outs, axis=1).astype(o_ref.dtype)


def _cross_attention(q, kv, *, tq):
    s, w = q.shape
    mlen = kv.shape[0]
    blocks = [((tq, w), BF16), ((mlen, 2 * w), BF16), ((tq, w), BF16)]
    return pl.pallas_call(
        _xattn_body,
        grid=(s // tq,),
        in_specs=[pl.BlockSpec((tq, w), lambda i: (i, 0)),
                  pl.BlockSpec((mlen, 2 * w), lambda i: (0, 0))],
        out_specs=pl.BlockSpec((tq, w), lambda i: (i, 0)),
        out_shape=jax.ShapeDtypeStruct((s, w), BF16),
        compiler_params=_params(("arbitrary",), _vmem_limit(blocks)),
        name="memory_cross_attention",
    )(q, kv)


def kernel(x, mem, rel_bias_table, w_in, sinks, lambda_q1, lambda_k1, lambda_q2, lambda_k2, subln_w,
           w_branch_a, w_branch_b, w_o, ln1_g, ln1_b, w_cq, w_mem_kv, w_co, ln2_g, ln2_b,
           w_gate_up, w_down, ln3_g, ln3_b):
    b, s, d = x.shape
    assert b == 1 and w_in.shape[0] == DEPTH == 1
    l = 0
    qa_w, kva_w, qb_w = HA * DH_A, KV_A * DH_A, HB * 2 * DH_B
    col_qa, col_ka, col_va = 0, qa_w, qa_w + kva_w
    col_qb = col_va + kva_w
    col_kb, col_vb = col_qb + qb_w, col_qb + 2 * qb_w
    col_ga = col_vb + qb_w
    col_gb = col_ga + d
    tq = tk = 512

    h0 = x.reshape(s, d)

    proj = _matmul(h0, w_in[l], tm=1024, tn=768, out_dtype=BF16, name="in_proj")

    i_a = np.arange(BLK)[:, None]
    j_a = np.arange(2 * BLK)[None, :]
    dist_a = BLK + i_a - j_a
    bias_a = _bias_expand(rel_bias_table, [dist_a], [(dist_a >= 0) & (dist_a < WINDOW)], head0=0, heads=HA,
                          shift_last_bucket=False, zero_tiles=0).reshape(HA, BLK, 2 * BLK)
    o_a = _swa_attention(proj, sinks[l], bias_a, q_col=col_qa, k_col=col_ka, v_col=col_va)

    r_b = np.arange(tk)[:, None]
    c_b = np.arange(tq)[None, :]
    dist_diag = c_b - r_b
    dist_prev = dist_diag + tk
    bias3 = _bias_expand(rel_bias_table, [dist_prev, dist_diag], [dist_prev >= 0, dist_diag >= 0], head0=HA,
                         heads=HB, shift_last_bucket=True, zero_tiles=1)
    vt = proj[:, col_vb:col_vb + qb_w].reshape(s // tk, tk, HB, 2 * DH_B).transpose(2, 0, 3, 1)
    o_b = _diff_attention(proj, vt, bias3, lambda_q1[l], lambda_k1[l], lambda_q2[l],
                          lambda_k2[l], subln_w[l], q_col=col_qb, k_col=col_kb, tq=tq, tk=tk)

    mix = _gated_branches(o_a, o_b, w_branch_a[l], w_branch_b[l], proj, col_ga, col_gb, tm=1024, tn=256)
    h1, h1b = _matmul_residual_ln(mix, w_o[l], h0, ln1_g[l], ln1_b[l], tm=512, tk=512, emit_bf16=True)

    qc = _matmul(h1b, w_cq[l], tm=1024, tn=HC * DH_C, out_dtype=BF16, name="cross_q")
    kvm = _matmul(mem.reshape(mem.shape[1], d), w_mem_kv[l], tm=mem.shape[1], tn=HC * DH_C,
                  out_dtype=BF16, name="mem_kv")
    oc = _cross_attention(qc, kvm, tq=1024)
    h2, h2b = _matmul_residual_ln(oc, w_co[l], h1, ln2_g[l], ln2_b[l], tm=512, tk=HC * DH_C, emit_bf16=True)

    act = _swiglu_up(h2b, w_gate_up[l], tm=1024, tn=512)
    (h3,) = _matmul_residual_ln(act, w_down[l], h2, ln3_g[l], ln3_b[l], tm=512, tk=512, emit_bf16=False)
    return h3.reshape(b, s, d)
```

```python
import functools
import math

import numpy as np
import jax
import jax.numpy as jnp
from jax import lax
from jax.experimental import pallas as pl
from jax.experimental.pallas import tpu as pltpu

F32 = jnp.float32
BF16 = jnp.bfloat16

BLK = 128
WINDOW = 128
HA, KV_A, DH_A = 16, 2, 64
G_A = HA // KV_A
HB, DH_B = 8, 64
N_BUCKETS, MAX_DISTANCE = 32, 128
HC, DH_C = 4, 128
LN_EPS = 1e-5
DEPTH = 1
ALPHA = (2 * DEPTH) ** 0.25
LAMBDA_INIT = 0.8 - 0.6 * math.exp(-0.3 * 0)
LOG2E = math.log2(math.e)

V7X_LANES = 128
V7X_VMEM_BYTES = 64 * 1024 * 1024
V7X_VMEM_TEMP_BYTES = 12 * 1024 * 1024

NEG = -1e30


def _nbytes(shape, dtype):
    return int(np.prod(shape)) * jnp.dtype(dtype).itemsize


def _vmem_limit(pipelined, scratch=()):
    need = 2 * sum(_nbytes(s, d) for s, d in pipelined) + sum(_nbytes(s, d) for s, d in scratch)
    need += V7X_VMEM_TEMP_BYTES
    assert need <= V7X_VMEM_BYTES - 4 * 1024 * 1024, need
    return need


def _params(semantics, vmem):
    return pltpu.CompilerParams(dimension_semantics=semantics, vmem_limit_bytes=vmem)


def _act_tile(a_ref, abf_ref):
    if abf_ref is None:
        return a_ref[...]

    @pl.when(pl.program_id(1) == 0)
    def _():
        abf_ref[...] = a_ref[...].astype(BF16)

    return abf_ref[...]


def _mm_body(a_ref, w_ref, o_ref, *scratch):
    a = _act_tile(a_ref, scratch[0] if scratch else None)
    o_ref[...] = jnp.dot(a, w_ref[...].astype(BF16), preferred_element_type=F32).astype(o_ref.dtype)


def _matmul(a, w, *, tm, tn, out_dtype, name):
    m, k = a.shape
    n = w.shape[1]
    cast = a.dtype != BF16
    scratch = [((tm, k), BF16)] if cast else []
    blocks = [((tm, k), a.dtype), ((k, tn), w.dtype), ((tm, tn), out_dtype)]
    return pl.pallas_call(
        _mm_body,
        grid=(m // tm, n // tn),
        in_specs=[pl.BlockSpec((tm, k), lambda i, j: (i, 0)),
                  pl.BlockSpec((k, tn), lambda i, j: (0, j))],
        out_specs=pl.BlockSpec((tm, tn), lambda i, j: (i, j)),
        out_shape=jax.ShapeDtypeStruct((m, n), out_dtype),
        scratch_shapes=[pltpu.VMEM(s, d) for s, d in scratch],
        compiler_params=_params(("arbitrary", "arbitrary"), _vmem_limit(blocks, scratch)),
        name=name,
    )(a, w)


def _branch_body(oa_ref, ob_ref, wa_ref, wb_ref, ga_ref, gb_ref, o_ref):
    ya = jnp.dot(oa_ref[...], wa_ref[...].astype(BF16), preferred_element_type=F32)
    yb = jnp.dot(ob_ref[...], wb_ref[...].astype(BF16), preferred_element_type=F32)
    ga = jax.nn.sigmoid(ga_ref[...].astype(F32))
    gb = jax.nn.sigmoid(gb_ref[...].astype(F32))
    o_ref[...] = (ga * ya + gb * yb).astype(o_ref.dtype)


def _gated_branches(o_a, o_b, w_a, w_b, proj, ga_col, gb_col, *, tm, tn):
    m, ka = o_a.shape
    kb = o_b.shape[1]
    n = w_a.shape[1]
    ga_blk, gb_blk = ga_col // tn, gb_col // tn
    blocks = [((tm, ka), BF16), ((tm, kb), BF16), ((ka, tn), F32), ((kb, tn), F32),
              ((tm, tn), proj.dtype), ((tm, tn), proj.dtype), ((tm, tn), BF16)]
    return pl.pallas_call(
        _branch_body,
        grid=(m // tm, n // tn),
        in_specs=[pl.BlockSpec((tm, ka), lambda i, j: (i, 0)),
                  pl.BlockSpec((tm, kb), lambda i, j: (i, 0)),
                  pl.BlockSpec((ka, tn), lambda i, j: (0, j)),
                  pl.BlockSpec((kb, tn), lambda i, j: (0, j)),
                  pl.BlockSpec((tm, tn), lambda i, j: (i, ga_blk + j)),
                  pl.BlockSpec((tm, tn), lambda i, j: (i, gb_blk + j))],
        out_specs=pl.BlockSpec((tm, tn), lambda i, j: (i, j)),
        out_shape=jax.ShapeDtypeStruct((m, n), BF16),
        compiler_params=_params(("arbitrary", "arbitrary"), _vmem_limit(blocks)),
        name="gated_branches",
    )(o_a, o_b, w_a, w_b, proj, proj)


def _swiglu_body(a_ref, wg_ref, wu_ref, o_ref):
    a = a_ref[...]
    g = jnp.dot(a, wg_ref[...].astype(BF16), preferred_element_type=F32)
    u = jnp.dot(a, wu_ref[...].astype(BF16), preferred_element_type=F32)
    o_ref[...] = (g * jax.nn.sigmoid(g) * u).astype(o_ref.dtype)


def _swiglu_up(a, w_gate_up, *, tm, tn):
    m, k = a.shape
    d_ff = w_gate_up.shape[1] // 2
    up_blk = d_ff // tn
    blocks = [((tm, k), BF16), ((k, tn), F32), ((k, tn), F32), ((tm, tn), BF16)]
    return pl.pallas_call(
        _swiglu_body,
        grid=(m // tm, d_ff // tn),
        in_specs=[pl.BlockSpec((tm, k), lambda i, j: (i, 0)),
                  pl.BlockSpec((k, tn), lambda i, j: (0, j)),
                  pl.BlockSpec((k, tn), lambda i, j: (0, up_blk + j))],
        out_specs=pl.BlockSpec((tm, tn), lambda i, j: (i, j)),
        out_shape=jax.ShapeDtypeStruct((m, d_ff), BF16),
        compiler_params=_params(("arbitrary", "arbitrary"), _vmem_limit(blocks)),
        name="swiglu_up",
    )(a, w_gate_up, w_gate_up)


def _mm_ln_body(a_ref, w_ref, r_ref, g_ref, b_ref, *rest, nk, emit_bf16):
    if emit_bf16:
        of_ref, ob_ref, acc_ref = rest
    else:
        of_ref, acc_ref = rest
    kk = pl.program_id(1)

    @pl.when(kk == 0)
    def _():
        acc_ref[...] = jnp.zeros_like(acc_ref)

    acc_ref[...] += jnp.dot(a_ref[...], w_ref[...].astype(BF16), preferred_element_type=F32)

    @pl.when(kk == nk - 1)
    def _():
        y = ALPHA * r_ref[...] + acc_ref[...]
        mu = jnp.mean(y, axis=-1, keepdims=True)
        yc = y - mu
        var = jnp.mean(yc * yc, axis=-1, keepdims=True)
        out = yc * lax.rsqrt(var + LN_EPS) * g_ref[...] + b_ref[...]
        of_ref[...] = out
        if emit_bf16:
            ob_ref[...] = out.astype(BF16)


def _matmul_residual_ln(a, w, resid, gain, bias, *, tm, tk, emit_bf16):
    m, k = a.shape
    n = w.shape[1]
    nk = k // tk
    out_shape = [jax.ShapeDtypeStruct((m, n), F32)]
    out_specs = [pl.BlockSpec((tm, n), lambda i, kk: (i, 0))]
    blocks = [((tm, tk), BF16), ((tk, n), w.dtype), ((tm, n), F32), ((tm, n), F32)]
    if emit_bf16:
        out_shape.append(jax.ShapeDtypeStruct((m, n), BF16))
        out_specs.append(pl.BlockSpec((tm, n), lambda i, kk: (i, 0)))
        blocks.append(((tm, n), BF16))
    scratch = [((tm, n), F32)]
    return pl.pallas_call(
        functools.partial(_mm_ln_body, nk=nk, emit_bf16=emit_bf16),
        grid=(m // tm, nk),
        in_specs=[pl.BlockSpec((tm, tk), lambda i, kk: (i, kk)),
                  pl.BlockSpec((tk, n), lambda i, kk: (kk, 0)),
                  pl.BlockSpec((tm, n), lambda i, kk: (i, 0)),
                  pl.BlockSpec((1, n), lambda i, kk: (0, 0)),
                  pl.BlockSpec((1, n), lambda i, kk: (0, 0))],
        out_specs=out_specs,
        out_shape=out_shape,
        scratch_shapes=[pltpu.VMEM(s, d) for s, d in scratch],
        compiler_params=_params(("arbitrary", "arbitrary"), _vmem_limit(blocks, scratch)),
        name="matmul_residual_ln",
    )(a, w, resid, gain.reshape(1, n), bias.reshape(1, n))


def _rel_bucket_np(dist):
    n = np.maximum(dist, 0)
    exact = N_BUCKETS // 2
    logv = (np.log(np.maximum(n, 1).astype(np.float32) / exact) / math.log(MAX_DISTANCE / exact))
    large = exact + (logv.astype(np.float32) * (N_BUCKETS - exact)).astype(np.int32)
    large = np.minimum(large, N_BUCKETS - 1)
    return np.where(n < exact, n, large).astype(np.int32)


def _bias_expand_body(tab_ref, bucket_ref, o_ref, *, head0, shift_last_bucket, scale):
    h = head0 + pl.program_id(0)
    last = tab_ref[N_BUCKETS - 1, h] if shift_last_bucket else 0.0
    for t in range(bucket_ref.shape[0]):
        bucket = bucket_ref[t]
        acc = jnp.full(bucket.shape, NEG, F32)
        for bkt in range(N_BUCKETS):
            acc = jnp.where(bucket == bkt, (tab_ref[bkt, h] - last) * scale, acc)
        o_ref[t] = acc


def _bias_expand(table, dists, valids, *, head0, heads, shift_last_bucket, scale):
    bucket = np.stack([np.where(v, _rel_bucket_np(d), -1) for d, v in zip(dists, valids)]).astype(np.int32)
    nt, r, c = bucket.shape
    blocks = [((nt, r, c), jnp.int32), ((nt, r, c), F32)]
    return pl.pallas_call(
        functools.partial(_bias_expand_body, head0=head0, shift_last_bucket=shift_last_bucket,
                          scale=scale),
        grid=(heads,),
        in_specs=[pl.BlockSpec(memory_space=pltpu.SMEM),
                  pl.BlockSpec((nt, r, c), lambda h: (0, 0, 0))],
        out_specs=pl.BlockSpec((None, nt, r, c), lambda h: (h, 0, 0, 0)),
        out_shape=jax.ShapeDtypeStruct((heads, nt, r, c), F32),
        compiler_params=_params(("arbitrary",), _vmem_limit(blocks)),
        name="bias_expand",
    )(table.astype(F32), jnp.asarray(bucket))


def _swa_body(sink_ref, q_ref, kp_ref, kc_ref, vp_ref, vc_ref, bias_ref, o_ref):
    n = pl.program_id(0)
    half = DH_A
    lane_k = lax.broadcasted_iota(jnp.int32, (2 * BLK, 2 * half), 1)
    col = lax.broadcasted_iota(jnp.int32, (BLK, 2 * BLK), 1)
    lane_o = lax.broadcasted_iota(jnp.int32, (BLK, 2 * half), 1)
    no_prev = jnp.logical_and(n == 0, col < BLK)
    scale = DH_A ** -0.5

    kf = jnp.concatenate([kp_ref[...], kc_ref[...]], axis=0).astype(F32)
    vf = jnp.concatenate([vp_ref[...], vc_ref[...]], axis=0).astype(F32)
    kr = pltpu.roll(kf, half, 1)
    vr = pltpu.roll(vf, half, 1)

    outs = []
    for g in range(KV_A):
        k_own, k_other = (kf, kr) if g == 0 else (kr, kf)
        k_lo = jnp.where(lane_k < half, k_own, 0.0).astype(BF16)
        k_hi = jnp.where(lane_k >= half, k_other, 0.0).astype(BF16)
        v_dup = (jnp.where(lane_k < half, vf, vr) if g == 0 else jnp.where(lane_k < half, vr, vf)).astype(BF16)
        for t in range(G_A // 2):
            pair = g * (G_A // 2) + t
            q_pair = q_ref[:, pair * 2 * half:(pair + 1) * 2 * half]
            o_pair = []
            for par, k_sel in ((0, k_lo), (1, k_hi)):
                h = 2 * pair + par
                s = lax.dot_general(q_pair, k_sel, (((1,), (1,)), ((), ())),
                                    preferred_element_type=F32) * scale + bias_ref[h]
                s = jnp.where(no_prev, NEG, s)
                sink = sink_ref[h]
                mx = jnp.maximum(jnp.max(s, axis=1, keepdims=True), sink)
                p = jnp.exp(s - mx)
                den = jnp.sum(p, axis=1, keepdims=True) + jnp.exp(sink - mx)
                o_pair.append(jnp.dot(p.astype(BF16), v_dup, preferred_element_type=F32) / den)
            outs.append(jnp.where(lane_o < half, o_pair[0], o_pair[1]))
    o_ref[...] = jnp.concatenate(outs, axis=1).astype(o_ref.dtype)


def _swa_attention(proj, sinks, bias, *, q_col, k_col, v_col):
    s = proj.shape[0]
    nb = s // BLK
    qw = HA * DH_A
    kvw = KV_A * DH_A
    assert kvw == V7X_LANES and q_col % qw == 0 and k_col % kvw == 0 and v_col % kvw == 0
    kb, vb = k_col // kvw, v_col // kvw
    blocks = [((BLK, qw), BF16)] + [((BLK, kvw), BF16)] * 4 + [((HA, BLK, 2 * BLK), F32), ((BLK, qw), BF16)]
    prev = lambda n: jnp.maximum(n - 1, 0)
    return pl.pallas_call(
        _swa_body,
        grid=(nb,),
        in_specs=[pl.BlockSpec(memory_space=pltpu.SMEM),
                  pl.BlockSpec((BLK, qw), lambda n: (n, q_col // qw)),
                  pl.BlockSpec((BLK, kvw), lambda n: (prev(n), kb)),
                  pl.BlockSpec((BLK, kvw), lambda n: (n, kb)),
                  pl.BlockSpec((BLK, kvw), lambda n: (prev(n), vb)),
                  pl.BlockSpec((BLK, kvw), lambda n: (n, vb)),
                  pl.BlockSpec((HA, BLK, 2 * BLK), lambda n: (0, 0, 0))],
        out_specs=pl.BlockSpec((BLK, qw), lambda n: (n, 0)),
        out_shape=jax.ShapeDtypeStruct((s, qw), BF16),
        compiler_params=_params(("arbitrary",), _vmem_limit(blocks)),
        name="swa_sink_attention",
    )(sinks.astype(F32), proj, proj, proj, proj, proj, bias)


def _diff_body(q_ref, k_ref, vt_ref, b2_ref, lq1_ref, lk1_ref, lq2_ref, lk2_ref, sw_ref,
               o_ref, qz_ref, sa_ref, sb_ref, cma_ref, cmb_ref, acc_ref, m_ref, l_ref, *, tq, tk):
    i = pl.program_id(1)
    width = 2 * DH_B

    qt = (q_ref[...].astype(F32) * (DH_B ** -0.5 * LOG2E)).T
    row = lax.broadcasted_iota(jnp.int32, (width, tq), 0)
    qz_ref[0] = jnp.where(row < DH_B, qt, 0.0).astype(BF16)
    qz_ref[1] = jnp.where(row >= DH_B, qt, 0.0).astype(BF16)
    acc_ref[...] = jnp.zeros_like(acc_ref)
    m_ref[...] = jnp.full_like(m_ref, NEG)
    l_ref[...] = jnp.zeros_like(l_ref)

    def scores(tile, bias, s_ref, cm_ref):
        kblk = k_ref[pl.ds(pl.multiple_of(tile * tk, tk), tk), :]
        for c in range(2):
            s = jnp.dot(kblk, qz_ref[c], preferred_element_type=F32)
            if bias is not None:
                s = s + bias
            s_ref[c] = s
            cm_ref[c] = jnp.max(s, axis=0, keepdims=True)

    def accumulate(tile, s_ref, cm_ref):
        vt = vt_ref[tile]
        for c in range(2):
            m_old = m_ref[c]
            m_new = jnp.maximum(m_old, cm_ref[c])
            rescale = jnp.exp2(m_old - m_new)
            p = jnp.exp2(s_ref[c] - m_new)
            l_ref[c] = rescale * l_ref[c] + jnp.sum(p, axis=0, keepdims=True)
            acc_ref[c] = rescale * acc_ref[c] + jnp.dot(vt, p.astype(BF16), preferred_element_type=F32)
            m_ref[c] = m_new

    n_items = i + 1
    tile_of = lambda item: jnp.maximum(i - item, 0)
    scores(i, b2_ref[1], sa_ref, cma_ref)

    @pl.when(n_items >= 2)
    def _():
        scores(i - 1, b2_ref[0], sb_ref, cmb_ref)
        accumulate(i, sa_ref, cma_ref)
        scores(tile_of(2), None, sa_ref, cma_ref)
        accumulate(i - 1, sb_ref, cmb_ref)

    def pair(t, carry):
        scores(i - (2 * t + 1), None, sb_ref, cmb_ref)
        accumulate(i - 2 * t, sa_ref, cma_ref)
        scores(tile_of(2 * t + 2), None, sa_ref, cma_ref)
        accumulate(i - (2 * t + 1), sb_ref, cmb_ref)
        return carry

    lax.fori_loop(1, n_items // 2, pair, 0)

    @pl.when(n_items % 2 == 1)
    def _():
        accumulate(0, sa_ref, cma_ref)

    lam = (jnp.exp(jnp.sum(lq1_ref[...] * lk1_ref[...], axis=1, keepdims=True))
           - jnp.exp(jnp.sum(lq2_ref[...] * lk2_ref[...], axis=1, keepdims=True)) + LAMBDA_INIT)
    o = acc_ref[0] / l_ref[0] - lam * (acc_ref[1] / l_ref[1])
    ms = jnp.mean(o * o, axis=0, keepdims=True)
    o = o * lax.rsqrt(ms + LN_EPS) * sw_ref[...] * (1.0 - LAMBDA_INIT)
    o_ref[...] = o.T.astype(o_ref.dtype)


def _diff_attention(proj, vt, bias2, lq1, lk1, lq2, lk2, subln_w, *, q_col, k_col, tq, tk):
    s = proj.shape[0]
    width = 2 * DH_B
    assert width == V7X_LANES and tq == tk and q_col % width == 0 and k_col % width == 0
    qb, kb = q_col // width, k_col // width
    nkt = s // tk
    blocks = [((tq, width), BF16), ((s, width), BF16), ((nkt, width, tk), BF16),
              ((2, tk, tq), F32), ((tq, width), BF16)]
    scratch = [((2, width, tq), BF16), ((2, tk, tq), F32), ((2, tk, tq), F32),
               ((2, 8, tq), F32), ((2, 8, tq), F32),
               ((2, width, tq), F32), ((2, 8, tq), F32), ((2, 8, tq), F32)]
    stat = pltpu.VMEM((2, 1, tq), F32)
    vec = lambda v: v.astype(F32).reshape(1, DH_B)
    small = pl.BlockSpec((1, DH_B), lambda h, i: (0, 0))
    return pl.pallas_call(
        functools.partial(_diff_body, tq=tq, tk=tk),
        grid=(HB, s // tq),
        in_specs=[pl.BlockSpec((tq, width), lambda h, i: (i, qb + h)),
                  pl.BlockSpec((s, width), lambda h, i: (0, kb + h)),
                  pl.BlockSpec((None, nkt, width, tk), lambda h, i: (h, 0, 0, 0)),
                  pl.BlockSpec((None, 2, tk, tq), lambda h, i: (h, 0, 0, 0)),
                  small, small, small, small,
                  pl.BlockSpec((width, 1), lambda h, i: (0, 0))],
        out_specs=pl.BlockSpec((tq, width), lambda h, i: (i, h)),
        out_shape=jax.ShapeDtypeStruct((s, HB * width), BF16),
        scratch_shapes=[pltpu.VMEM((2, width, tq), BF16),
                        pltpu.VMEM((2, tk, tq), F32), pltpu.VMEM((2, tk, tq), F32), stat, stat,
                        pltpu.VMEM((2, width, tq), F32), stat, stat],
        compiler_params=_params(("arbitrary", "arbitrary"), _vmem_limit(blocks, scratch)),
        name="diff_attention",
    )(proj, proj, vt, bias2, vec(lq1), vec(lk1), vec(lq2), vec(lk2),
      subln_w.astype(F32).reshape(width, 1))


def _xattn_body(q_ref, kv_ref, o_ref):
    scale = DH_C ** -0.5
    outs = []
    for h in range(HC):
        qh = q_ref[:, h * DH_C:(h + 1) * DH_C]
        kh = kv_ref[:, h * DH_C:(h + 1) * DH_C]
        vh = kv_ref[:, (HC + h) * DH_C:(HC + h + 1) * DH_C]
        s = lax.dot_general(qh, kh, (((1,), (1,)), ((), ())), preferred_element_type=F32) * scale
        mx = jnp.max(s, axis=1, keepdims=True)
        p = jnp.exp(s - mx)
        den = jnp.sum(p, axis=1, keepdims=True)
        outs.append(jnp.dot(p.astype(BF16), vh, preferred_element_type=F32) / den)
    o_ref[...] = jnp.concatenate(outs, axis=1).astype(o_ref.dtype)


def _cross_attention(q, kv, *, tq):
    s, w = q.shape
    mlen = kv.shape[0]
    blocks = [((tq, w), BF16), ((mlen, 2 * w), BF16), ((tq, w), BF16)]
    return pl.pallas_call(
        _xattn_body,
        grid=(s // tq,),
        in_specs=[pl.BlockSpec((tq, w), lambda i: (i, 0)),
                  pl.BlockSpec((mlen, 2 * w), lambda i: (0, 0))],
        out_specs=pl.BlockSpec((tq, w), lambda i: (i, 0)),
        out_shape=jax.ShapeDtypeStruct((s, w), BF16),
        compiler_params=_params(("arbitrary",), _vmem_limit(blocks)),
        name="memory_cross_attention",
    )(q, kv)


def kernel(x, mem, rel_bias_table, w_in, sinks, lambda_q1, lambda_k1, lambda_q2, lambda_k2, subln_w,
           w_branch_a, w_branch_b, w_o, ln1_g, ln1_b, w_cq, w_mem_kv, w_co, ln2_g, ln2_b,
           w_gate_up, w_down, ln3_g, ln3_b):
    b, s, d = x.shape
    assert b == 1 and w_in.shape[0] == DEPTH == 1
    l = 0
    qa_w, kva_w, qb_w = HA * DH_A, KV_A * DH_A, HB * 2 * DH_B
    col_qa, col_ka, col_va = 0, qa_w, qa_w + kva_w
    col_qb = col_va + kva_w
    col_kb, col_vb = col_qb + qb_w, col_qb + 2 * qb_w
    col_ga = col_vb + qb_w
    col_gb = col_ga + d
    tq = tk = 512

    h0 = x.reshape(s, d)

    proj = _matmul(h0, w_in[l], tm=1024, tn=768, out_dtype=BF16, name="in_proj")

    i_a = np.arange(BLK)[:, None]
    j_a = np.arange(2 * BLK)[None, :]
    dist_a = BLK + i_a - j_a
    bias_a = _bias_expand(rel_bias_table, [dist_a], [(dist_a >= 0) & (dist_a < WINDOW)], head0=0, heads=HA,
                          shift_last_bucket=False, scale=1.0).reshape(HA, BLK, 2 * BLK)
    o_a = _swa_attention(proj, sinks[l], bias_a, q_col=col_qa, k_col=col_ka, v_col=col_va)

    r_b = np.arange(tk)[:, None]
    c_b = np.arange(tq)[None, :]
    dist_diag = c_b - r_b
    dist_prev = dist_diag + tk
    bias2 = _bias_expand(rel_bias_table, [dist_prev, dist_diag], [dist_prev >= 0, dist_diag >= 0], head0=HA,
                         heads=HB, shift_last_bucket=True, scale=LOG2E)
    vt = proj[:, col_vb:col_vb + qb_w].reshape(s // tk, tk, HB, 2 * DH_B).transpose(2, 0, 3, 1)
    o_b = _diff_attention(proj, vt, bias2, lambda_q1[l], lambda_k1[l], lambda_q2[l],
                          lambda_k2[l], subln_w[l], q_col=col_qb, k_col=col_kb, tq=tq, tk=tk)

    mix = _gated_branches(o_a, o_b, w_branch_a[l], w_branch_b[l], proj, col_ga, col_gb, tm=1024, tn=256)
    h1, h1b = _matmul_residual_ln(mix, w_o[l], h0, ln1_g[l], ln1_b[l], tm=512, tk=512, emit_bf16=True)

    qc = _matmul(h1b, w_cq[l], tm=1024, tn=HC * DH_C, out_dtype=BF16, name="cross_q")
    kvm = _matmul(mem.reshape(mem.shape[1], d), w_mem_kv[l], tm=mem.shape[1], tn=HC * DH_C,
                  out_dtype=BF16, name="mem_kv")
    oc = _cross_attention(qc, kvm, tq=1024)
    h2, h2b = _matmul_residual_ln(oc, w_co[l], h1, ln2_g[l], ln2_b[l], tm=512, tk=HC * DH_C, emit_bf16=True)

    act = _swiglu_up(h2b, w_gate_up[l], tm=1024, tn=512)
    (h3,) = _matmul_residual_ln(act, w_down[l], h2, ln3_g[l], ln3_b[l], tm=512, tk=512, emit_bf16=False)
    return h3.reshape(b, s, d)
```

```python
import functools
import math

import numpy as np
import jax
import jax.numpy as jnp
from jax import lax
from jax.experimental import pallas as pl
from jax.experimental.pallas import tpu as pltpu

F32 = jnp.float32
BF16 = jnp.bfloat16

BLK = 128
WINDOW = 128
HA, KV_A, DH_A = 16, 2, 64
G_A = HA // KV_A
HB, DH_B = 8, 64
N_BUCKETS, MAX_DISTANCE = 32, 128
HC, DH_C = 4, 128
LN_EPS = 1e-5
DEPTH = 1
ALPHA = (2 * DEPTH) ** 0.25
LAMBDA_INIT = 0.8 - 0.6 * math.exp(-0.3 * 0)
LOG2E = math.log2(math.e)

V7X_LANES = 128
V7X_VMEM_BYTES = 64 * 1024 * 1024
V7X_VMEM_TEMP_BYTES = 12 * 1024 * 1024

NEG = -1e30


def _nbytes(shape, dtype):
    return int(np.prod(shape)) * jnp.dtype(dtype).itemsize


def _vmem_limit(pipelined, scratch=()):
    need = 2 * sum(_nbytes(s, d) for s, d in pipelined) + sum(_nbytes(s, d) for s, d in scratch)
    need += V7X_VMEM_TEMP_BYTES
    assert need <= V7X_VMEM_BYTES - 4 * 1024 * 1024, need
    return need


def _params(semantics, vmem):
    return pltpu.CompilerParams(dimension_semantics=semantics, vmem_limit_bytes=vmem)


def _act_tile(a_ref, abf_ref):
    if abf_ref is None:
        return a_ref[...]

    @pl.when(pl.program_id(1) == 0)
    def _():
        abf_ref[...] = a_ref[...].astype(BF16)

    return abf_ref[...]


def _mm_body(a_ref, w_ref, o_ref, *scratch):
    a = _act_tile(a_ref, scratch[0] if scratch else None)
    o_ref[...] = jnp.dot(a, w_ref[...].astype(BF16), preferred_element_type=F32).astype(o_ref.dtype)


def _matmul(a, w, *, tm, tn, out_dtype, name):
    m, k = a.shape
    n = w.shape[1]
    cast = a.dtype != BF16
    scratch = [((tm, k), BF16)] if cast else []
    blocks = [((tm, k), a.dtype), ((k, tn), w.dtype), ((tm, tn), out_dtype)]
    return pl.pallas_call(
        _mm_body,
        grid=(m // tm, n // tn),
        in_specs=[pl.BlockSpec((tm, k), lambda i, j: (i, 0)),
                  pl.BlockSpec((k, tn), lambda i, j: (0, j))],
        out_specs=pl.BlockSpec((tm, tn), lambda i, j: (i, j)),
        out_shape=jax.ShapeDtypeStruct((m, n), out_dtype),
        scratch_shapes=[pltpu.VMEM(s, d) for s, d in scratch],
        compiler_params=_params(("arbitrary", "arbitrary"), _vmem_limit(blocks, scratch)),
        name=name,
    )(a, w)


def _branch_body(oa_ref, ob_ref, wa_ref, wb_ref, ga_ref, gb_ref, o_ref):
    ya = jnp.dot(oa_ref[...], wa_ref[...].astype(BF16), preferred_element_type=F32)
    yb = jnp.dot(ob_ref[...], wb_ref[...].astype(BF16), preferred_element_type=F32)
    ga = jax.nn.sigmoid(ga_ref[...].astype(F32))
    gb = jax.nn.sigmoid(gb_ref[...].astype(F32))
    o_ref[...] = (ga * ya + gb * yb).astype(o_ref.dtype)


def _gated_branches(o_a, o_b, w_a, w_b, proj, ga_col, gb_col, *, tm, tn):
    m, ka = o_a.shape
    kb = o_b.shape[1]
    n = w_a.shape[1]
    ga_blk, gb_blk = ga_col // tn, gb_col // tn
    blocks = [((tm, ka), BF16), ((tm, kb), BF16), ((ka, tn), F32), ((kb, tn), F32),
              ((tm, tn), proj.dtype), ((tm, tn), proj.dtype), ((tm, tn), BF16)]
    return pl.pallas_call(
        _branch_body,
        grid=(m // tm, n // tn),
        in_specs=[pl.BlockSpec((tm, ka), lambda i, j: (i, 0)),
                  pl.BlockSpec((tm, kb), lambda i, j: (i, 0)),
                  pl.BlockSpec((ka, tn), lambda i, j: (0, j)),
                  pl.BlockSpec((kb, tn), lambda i, j: (0, j)),
                  pl.BlockSpec((tm, tn), lambda i, j: (i, ga_blk + j)),
                  pl.BlockSpec((tm, tn), lambda i, j: (i, gb_blk + j))],
        out_specs=pl.BlockSpec((tm, tn), lambda i, j: (i, j)),
        out_shape=jax.ShapeDtypeStruct((m, n), BF16),
        compiler_params=_params(("arbitrary", "arbitrary"), _vmem_limit(blocks)),
        name="gated_branches",
    )(o_a, o_b, w_a, w_b, proj, proj)


def _swiglu_body(a_ref, wg_ref, wu_ref, o_ref, *scratch):
    a = _act_tile(a_ref, scratch[0] if scratch else None)
    g = jnp.dot(a, wg_ref[...].astype(BF16), preferred_element_type=F32)
    u = jnp.dot(a, wu_ref[...].astype(BF16), preferred_element_type=F32)
    o_ref[...] = (g * jax.nn.sigmoid(g) * u).astype(o_ref.dtype)


def _swiglu_up(a, w_gate_up, *, tm, tn):
    m, k = a.shape
    d_ff = w_gate_up.shape[1] // 2
    up_blk = d_ff // tn
    scratch = [((tm, k), BF16)] if a.dtype != BF16 else []
    blocks = [((tm, k), a.dtype), ((k, tn), F32), ((k, tn), F32), ((tm, tn), BF16)]
    return pl.pallas_call(
        _swiglu_body,
        grid=(m // tm, d_ff // tn),
        in_specs=[pl.BlockSpec((tm, k), lambda i, j: (i, 0)),
                  pl.BlockSpec((k, tn), lambda i, j: (0, j)),
                  pl.BlockSpec((k, tn), lambda i, j: (0, up_blk + j))],
        out_specs=pl.BlockSpec((tm, tn), lambda i, j: (i, j)),
        out_shape=jax.ShapeDtypeStruct((m, d_ff), BF16),
        scratch_shapes=[pltpu.VMEM(s, d) for s, d in scratch],
        compiler_params=_params(("arbitrary", "arbitrary"), _vmem_limit(blocks, scratch)),
        name="swiglu_up",
    )(a, w_gate_up, w_gate_up)


def _mm_ln_body(a_ref, w_ref, r_ref, g_ref, b_ref, o_ref, *, nk):
    kk = pl.program_id(1)

    def partial_product():
        return jnp.dot(a_ref[...], w_ref[...].astype(BF16), preferred_element_type=F32)

    @pl.when(kk == 0)
    def _():
        o_ref[...] = ALPHA * r_ref[...] + partial_product()

    @pl.when(kk > 0)
    def _():
        o_ref[...] += partial_product()

    @pl.when(kk == nk - 1)
    def _():
        y = o_ref[...]
        mu = jnp.mean(y, axis=-1, keepdims=True)
        yc = y - mu
        var = jnp.mean(yc * yc, axis=-1, keepdims=True)
        o_ref[...] = yc * lax.rsqrt(var + LN_EPS) * g_ref[...] + b_ref[...]


def _matmul_residual_ln(a, w, resid, gain, bias, *, tm, tk):
    m, k = a.shape
    n = w.shape[1]
    nk = k // tk
    blocks = [((tm, tk), BF16), ((tk, n), w.dtype), ((tm, n), F32), ((tm, n), F32)]
    return pl.pallas_call(
        functools.partial(_mm_ln_body, nk=nk),
        grid=(m // tm, nk),
        in_specs=[pl.BlockSpec((tm, tk), lambda i, kk: (i, kk)),
                  pl.BlockSpec((tk, n), lambda i, kk: (kk, 0)),
                  pl.BlockSpec((tm, n), lambda i, kk: (i, 0)),
                  pl.BlockSpec((1, n), lambda i, kk: (0, 0)),
                  pl.BlockSpec((1, n), lambda i, kk: (0, 0))],
        out_specs=pl.BlockSpec((tm, n), lambda i, kk: (i, 0)),
        out_shape=jax.ShapeDtypeStruct((m, n), F32),
        compiler_params=_params(("arbitrary", "arbitrary"), _vmem_limit(blocks)),
        name="matmul_residual_ln",
    )(a, w, resid, gain.reshape(1, n), bias.reshape(1, n))


def _rel_bucket_np(dist):
    n = np.maximum(dist, 0)
    exact = N_BUCKETS // 2
    logv = (np.log(np.maximum(n, 1).astype(np.float32) / exact) / math.log(MAX_DISTANCE / exact))
    large = exact + (logv.astype(np.float32) * (N_BUCKETS - exact)).astype(np.int32)
    large = np.minimum(large, N_BUCKETS - 1)
    return np.where(n < exact, n, large).astype(np.int32)


def _bias_expand_body(tab_ref, bucket_ref, o_ref, *, head0, shift_last_bucket, scale):
    h = head0 + pl.program_id(0)
    last = tab_ref[N_BUCKETS - 1, h] if shift_last_bucket else 0.0
    for t in range(bucket_ref.shape[0]):
        bucket = bucket_ref[t]
        acc = jnp.full(bucket.shape, NEG, F32)
        for bkt in range(N_BUCKETS):
            acc = jnp.where(bucket == bkt, (tab_ref[bkt, h] - last) * scale, acc)
        o_ref[t] = acc


def _bias_expand(table, dists, valids, *, head0, heads, shift_last_bucket, scale):
    bucket = np.stack([np.where(v, _rel_bucket_np(d), -1) for d, v in zip(dists, valids)]).astype(np.int32)
    nt, r, c = bucket.shape
    blocks = [((nt, r, c), jnp.int32), ((nt, r, c), F32)]
    return pl.pallas_call(
        functools.partial(_bias_expand_body, head0=head0, shift_last_bucket=shift_last_bucket,
                          scale=scale),
        grid=(heads,),
        in_specs=[pl.BlockSpec(memory_space=pltpu.SMEM),
                  pl.BlockSpec((nt, r, c), lambda h: (0, 0, 0))],
        out_specs=pl.BlockSpec((None, nt, r, c), lambda h: (h, 0, 0, 0)),
        out_shape=jax.ShapeDtypeStruct((heads, nt, r, c), F32),
        compiler_params=_params(("arbitrary",), _vmem_limit(blocks)),
        name="bias_expand",
    )(table.astype(F32), jnp.asarray(bucket))


def _swa_body(sink_ref, q_ref, kp_ref, kc_ref, vp_ref, vc_ref, bias_ref, o_ref):
    n = pl.program_id(0)
    half = DH_A
    lane_k = lax.broadcasted_iota(jnp.int32, (2 * BLK, 2 * half), 1)
    col = lax.broadcasted_iota(jnp.int32, (BLK, 2 * BLK), 1)
    lane_o = lax.broadcasted_iota(jnp.int32, (BLK, 2 * half), 1)
    no_prev = jnp.logical_and(n == 0, col < BLK)
    scale = DH_A ** -0.5

    kf = jnp.concatenate([kp_ref[...], kc_ref[...]], axis=0).astype(F32)
    vf = jnp.concatenate([vp_ref[...], vc_ref[...]], axis=0).astype(F32)
    kr = pltpu.roll(kf, half, 1)
    vr = pltpu.roll(vf, half, 1)

    outs = []
    for g in range(KV_A):
        k_own, k_other = (kf, kr) if g == 0 else (kr, kf)
        k_lo = jnp.where(lane_k < half, k_own, 0.0).astype(BF16)
        k_hi = jnp.where(lane_k >= half, k_other, 0.0).astype(BF16)
        v_dup = (jnp.where(lane_k < half, vf, vr) if g == 0 else jnp.where(lane_k < half, vr, vf)).astype(BF16)
        for t in range(G_A // 2):
            pair = g * (G_A // 2) + t
            q_pair = q_ref[:, pair * 2 * half:(pair + 1) * 2 * half]
            o_pair = []
            for par, k_sel in ((0, k_lo), (1, k_hi)):
                h = 2 * pair + par
                s = lax.dot_general(q_pair, k_sel, (((1,), (1,)), ((), ())),
                                    preferred_element_type=F32) * scale + bias_ref[h]
                s = jnp.where(no_prev, NEG, s)
                sink = sink_ref[h]
                mx = jnp.maximum(jnp.max(s, axis=1, keepdims=True), sink)
                p = jnp.exp(s - mx)
                den = jnp.sum(p, axis=1, keepdims=True) + jnp.exp(sink - mx)
                o_pair.append(jnp.dot(p.astype(BF16), v_dup, preferred_element_type=F32) / den)
            outs.append(jnp.where(lane_o < half, o_pair[0], o_pair[1]))
    o_ref[...] = jnp.concatenate(outs, axis=1).astype(o_ref.dtype)


def _swa_attention(proj, sinks, bias, *, q_col, k_col, v_col):
    s = proj.shape[0]
    nb = s // BLK
    qw = HA * DH_A
    kvw = KV_A * DH_A
    assert kvw == V7X_LANES and q_col % qw == 0 and k_col % kvw == 0 and v_col % kvw == 0
    kb, vb = k_col // kvw, v_col // kvw
    blocks = [((BLK, qw), BF16)] + [((BLK, kvw), BF16)] * 4 + [((HA, BLK, 2 * BLK), F32), ((BLK, qw), BF16)]
    prev = lambda n: jnp.maximum(n - 1, 0)
    return pl.pallas_call(
        _swa_body,
        grid=(nb,),
        in_specs=[pl.BlockSpec(memory_space=pltpu.SMEM),
                  pl.BlockSpec((BLK, qw), lambda n: (n, q_col // qw)),
                  pl.BlockSpec((BLK, kvw), lambda n: (prev(n), kb)),
                  pl.BlockSpec((BLK, kvw), lambda n: (n, kb)),
                  pl.BlockSpec((BLK, kvw), lambda n: (prev(n), vb)),
                  pl.BlockSpec((BLK, kvw), lambda n: (n, vb)),
                  pl.BlockSpec((HA, BLK, 2 * BLK), lambda n: (0, 0, 0))],
        out_specs=pl.BlockSpec((BLK, qw), lambda n: (n, 0)),
        out_shape=jax.ShapeDtypeStruct((s, qw), BF16),
        compiler_params=_params(("arbitrary",), _vmem_limit(blocks)),
        name="swa_sink_attention",
    )(sinks.astype(F32), proj, proj, proj, proj, proj, bias)


def _diff_body(q_ref, k_ref, vt_ref, b2_ref, lq1_ref, lk1_ref, lq2_ref, lk2_ref, sw_ref,
               o_ref, qz_ref, sa_ref, sb_ref, cma_ref, cmb_ref, acc_ref, m_ref, l_ref, *, tq, tk):
    i = pl.program_id(1)
    width = 2 * DH_B

    qt = (q_ref[...].astype(F32) * (DH_B ** -0.5 * LOG2E)).T
    row = lax.broadcasted_iota(jnp.int32, (width, tq), 0)
    qz_ref[0] = jnp.where(row < DH_B, qt, 0.0).astype(BF16)
    qz_ref[1] = jnp.where(row >= DH_B, qt, 0.0).astype(BF16)
    acc_ref[...] = jnp.zeros_like(acc_ref)
    m_ref[...] = jnp.full_like(m_ref, NEG)
    l_ref[...] = jnp.zeros_like(l_ref)

    def scores(tile, bias, s_ref, cm_ref):
        kblk = k_ref[pl.ds(pl.multiple_of(tile * tk, tk), tk), :]
        for c in range(2):
            s = jnp.dot(kblk, qz_ref[c], preferred_element_type=F32)
            if bias is not None:
                s = s + bias
            s_ref[c] = s
            cm_ref[c] = jnp.max(s, axis=0, keepdims=True)

    def accumulate(tile, s_ref, cm_ref):
        vt = vt_ref[tile]
        for c in range(2):
            m_old = m_ref[c]
            m_new = jnp.maximum(m_old, cm_ref[c])
            rescale = jnp.exp2(m_old - m_new)
            p = jnp.exp2(s_ref[c] - m_new)
            l_ref[c] = rescale * l_ref[c] + jnp.sum(p, axis=0, keepdims=True)
            acc_ref[c] = rescale * acc_ref[c] + jnp.dot(vt, p.astype(BF16), preferred_element_type=F32)
            m_ref[c] = m_new

    n_items = i + 1
    tile_of = lambda item: jnp.maximum(i - item, 0)
    scores(i, b2_ref[1], sa_ref, cma_ref)

    @pl.when(n_items >= 2)
    def _():
        scores(i - 1, b2_ref[0], sb_ref, cmb_ref)
        accumulate(i, sa_ref, cma_ref)
        scores(tile_of(2), None, sa_ref, cma_ref)
        accumulate(i - 1, sb_ref, cmb_ref)

    def pair(t, carry):
        scores(i - (2 * t + 1), None, sb_ref, cmb_ref)
        accumulate(i - 2 * t, sa_ref, cma_ref)
        scores(tile_of(2 * t + 2), None, sa_ref, cma_ref)
        accumulate(i - (2 * t + 1), sb_ref, cmb_ref)
        return carry

    lax.fori_loop(1, n_items // 2, pair, 0)

    @pl.when(n_items % 2 == 1)
    def _():
        accumulate(0, sa_ref, cma_ref)

    lam = (jnp.exp(jnp.sum(lq1_ref[...] * lk1_ref[...], axis=1, keepdims=True))
           - jnp.exp(jnp.sum(lq2_ref[...] * lk2_ref[...], axis=1, keepdims=True)) + LAMBDA_INIT)
    o = acc_ref[0] / l_ref[0] - lam * (acc_ref[1] / l_ref[1])
    ms = jnp.mean(o * o, axis=0, keepdims=True)
    o = o * lax.rsqrt(ms + LN_EPS) * sw_ref[...] * (1.0 - LAMBDA_INIT)
    o_ref[...] = o.T.astype(o_ref.dtype)


def _diff_attention(proj, vt, bias2, lq1, lk1, lq2, lk2, subln_w, *, q_col, k_col, tq, tk):
    s = proj.shape[0]
    width = 2 * DH_B
    assert width == V7X_LANES and tq == tk and q_col % width == 0 and k_col % width == 0
    qb, kb = q_col // width, k_col // width
    nkt = s // tk
    blocks = [((tq, width), BF16), ((s, width), BF16), ((nkt, width, tk), BF16),
              ((2, tk, tq), F32), ((tq, width), BF16)]
    scratch = [((2, width, tq), BF16), ((2, tk, tq), F32), ((2, tk, tq), F32),
               ((2, 8, tq), F32), ((2, 8, tq), F32),
               ((2, width, tq), F32), ((2, 8, tq), F32), ((2, 8, tq), F32)]
    stat = pltpu.VMEM((2, 1, tq), F32)
    vec = lambda v: v.astype(F32).reshape(1, DH_B)
    small = pl.BlockSpec((1, DH_B), lambda h, i: (0, 0))
    return pl.pallas_call(
        functools.partial(_diff_body, tq=tq, tk=tk),
        grid=(HB, s // tq),
        in_specs=[pl.BlockSpec((tq, width), lambda h, i: (i, qb + h)),
                  pl.BlockSpec((s, width), lambda h, i: (0, kb + h)),
                  pl.BlockSpec((None, nkt, width, tk), lambda h, i: (h, 0, 0, 0)),
                  pl.BlockSpec((None, 2, tk, tq), lambda h, i: (h, 0, 0, 0)),
                  small, small, small, small,
                  pl.BlockSpec((width, 1), lambda h, i: (0, 0))],
        out_specs=pl.BlockSpec((tq, width), lambda h, i: (i, h)),
        out_shape=jax.ShapeDtypeStruct((s, HB * width), BF16),
        scratch_shapes=[pltpu.VMEM((2, width, tq), BF16),
                        pltpu.VMEM((2, tk, tq), F32), pltpu.VMEM((2, tk, tq), F32), stat, stat,
                        pltpu.VMEM((2, width, tq), F32), stat, stat],
        compiler_params=_params(("arbitrary", "arbitrary"), _vmem_limit(blocks, scratch)),
        name="diff_attention",
    )(proj, proj, vt, bias2, vec(lq1), vec(lk1), vec(lq2), vec(lk2),
      subln_w.astype(F32).reshape(width, 1))


def _xattn_body(q_ref, kv_ref, o_ref):
    scale = DH_C ** -0.5
    outs = []
    for h in range(HC):
        qh = q_ref[:, h * DH_C:(h + 1) * DH_C]
        kh = kv_ref[:, h * DH_C:(h + 1) * DH_C]
        vh = kv_ref[:, (HC + h) * DH_C:(HC + h + 1) * DH_C]
        s = lax.dot_general(qh, kh, (((1,), (1,)), ((), ())), preferred_element_type=F32) * scale
        mx = jnp.max(s, axis=1, keepdims=True)
        p = jnp.exp(s - mx)
        den = jnp.sum(p, axis=1, keepdims=True)
        outs.append(jnp.dot(p.astype(BF16), vh, preferred_element_type=F32) / den)
    o_ref[...] = jnp.concatenate(outs, axis=1).astype(o_ref.dtype)


def _cross_attention(q, kv, *, tq):
    s, w = q.shape
    mlen = kv.shape[0]
    blocks = [((tq, w), BF16), ((mlen, 2 * w), BF16), ((tq, w), BF16)]
    return pl.pallas_call(
        _xattn_body,
        grid=(s // tq,),
        in_specs=[pl.BlockSpec((tq, w), lambda i: (i, 0)),
                  pl.BlockSpec((mlen, 2 * w), lambda i: (0, 0))],
        out_specs=pl.BlockSpec((tq, w), lambda i: (i, 0)),
        out_shape=jax.ShapeDtypeStruct((s, w), BF16),
        compiler_params=_params(("arbitrary",), _vmem_limit(blocks)),
        name="memory_cross_attention",
    )(q, kv)


def kernel(x, mem, rel_bias_table, w_in, sinks, lambda_q1, lambda_k1, lambda_q2, lambda_k2, subln_w,
           w_branch_a, w_branch_b, w_o, ln1_g, ln1_b, w_cq, w_mem_kv, w_co, ln2_g, ln2_b,
           w_gate_up, w_down, ln3_g, ln3_b):
    b, s, d = x.shape
    assert b == 1 and w_in.shape[0] == DEPTH == 1
    l = 0
    qa_w, kva_w, qb_w = HA * DH_A, KV_A * DH_A, HB * 2 * DH_B
    col_qa, col_ka, col_va = 0, qa_w, qa_w + kva_w
    col_qb = col_va + kva_w
    col_kb, col_vb = col_qb + qb_w, col_qb + 2 * qb_w
    col_ga = col_vb + qb_w
    col_gb = col_ga + d
    tq = tk = 512

    h0 = x.reshape(s, d)

    proj = _matmul(h0, w_in[l], tm=1024, tn=768, out_dtype=BF16, name="in_proj")

    i_a = np.arange(BLK)[:, None]
    j_a = np.arange(2 * BLK)[None, :]
    dist_a = BLK + i_a - j_a
    bias_a = _bias_expand(rel_bias_table, [dist_a], [(dist_a >= 0) & (dist_a < WINDOW)], head0=0, heads=HA,
                          shift_last_bucket=False, scale=1.0).reshape(HA, BLK, 2 * BLK)
    o_a = _swa_attention(proj, sinks[l], bias_a, q_col=col_qa, k_col=col_ka, v_col=col_va)

    r_b = np.arange(tk)[:, None]
    c_b = np.arange(tq)[None, :]
    dist_diag = c_b - r_b
    dist_prev = dist_diag + tk
    bias2 = _bias_expand(rel_bias_table, [dist_prev, dist_diag], [dist_prev >= 0, dist_diag >= 0], head0=HA,
                         heads=HB, shift_last_bucket=True, scale=LOG2E)
    vt = proj[:, col_vb:col_vb + qb_w].reshape(s // tk, tk, HB, 2 * DH_B).transpose(2, 0, 3, 1)
    o_b = _diff_attention(proj, vt, bias2, lambda_q1[l], lambda_k1[l], lambda_q2[l],
                          lambda_k2[l], subln_w[l], q_col=col_qb, k_col=col_kb, tq=tq, tk=tk)

    mix = _gated_branches(o_a, o_b, w_branch_a[l], w_branch_b[l], proj, col_ga, col_gb, tm=2048, tn=256)
    h1 = _matmul_residual_ln(mix, w_o[l], h0, ln1_g[l], ln1_b[l], tm=1024, tk=512)

    qc = _matmul(h1, w_cq[l], tm=1024, tn=HC * DH_C, out_dtype=BF16, name="cross_q")
    kvm = _matmul(mem.reshape(mem.shape[1], d), w_mem_kv[l], tm=mem.shape[1], tn=HC * DH_C,
                  out_dtype=BF16, name="mem_kv")
    oc = _cross_attention(qc, kvm, tq=1024)
    h2 = _matmul_residual_ln(oc, w_co[l], h1, ln2_g[l], ln2_b[l], tm=1024, tk=HC * DH_C)

    act = _swiglu_up(h2, w_gate_up[l], tm=1024, tn=512)
    h3 = _matmul_residual_ln(act, w_down[l], h2, ln3_g[l], ln3_b[l], tm=1024, tk=512)
    return h3.reshape(b, s, d)
```

```python
import functools
import math

import numpy as np
import jax
import jax.numpy as jnp
from jax import lax
from jax.experimental import pallas as pl
from jax.experimental.pallas import tpu as pltpu

F32 = jnp.float32
BF16 = jnp.bfloat16

BLK = 128
WINDOW = 128
HA, KV_A, DH_A = 16, 2, 64
G_A = HA // KV_A
HB, DH_B = 8, 64
N_BUCKETS, MAX_DISTANCE = 32, 128
HC, DH_C = 4, 128
LN_EPS = 1e-5
DEPTH = 1
ALPHA = (2 * DEPTH) ** 0.25
LAMBDA_INIT = 0.8 - 0.6 * math.exp(-0.3 * 0)
LOG2E = math.log2(math.e)

V7X_LANES = 128
V7X_VMEM_BYTES = 64 * 1024 * 1024
V7X_VMEM_TEMP_BYTES = 12 * 1024 * 1024

NEG = -1e30


def _nbytes(shape, dtype):
    return int(np.prod(shape)) * jnp.dtype(dtype).itemsize


def _vmem_limit(pipelined, scratch=()):
    need = 2 * sum(_nbytes(s, d) for s, d in pipelined) + sum(_nbytes(s, d) for s, d in scratch)
    need += V7X_VMEM_TEMP_BYTES
    assert need <= V7X_VMEM_BYTES - 4 * 1024 * 1024, need
    return need


def _params(semantics, vmem):
    return pltpu.CompilerParams(dimension_semantics=semantics, vmem_limit_bytes=vmem)


def _act_tile(a_ref, abf_ref):
    if abf_ref is None:
        return a_ref[...]

    @pl.when(pl.program_id(1) == 0)
    def _():
        abf_ref[...] = a_ref[...].astype(BF16)

    return abf_ref[...]


def _mm_body(a_ref, w_ref, o_ref, *scratch):
    a = _act_tile(a_ref, scratch[0] if scratch else None)
    o_ref[...] = jnp.dot(a, w_ref[...].astype(BF16), preferred_element_type=F32).astype(o_ref.dtype)


def _matmul(a, w, *, tm, tn, out_dtype, name):
    m, k = a.shape
    n = w.shape[1]
    cast = a.dtype != BF16
    scratch = [((tm, k), BF16)] if cast else []
    blocks = [((tm, k), a.dtype), ((k, tn), w.dtype), ((tm, tn), out_dtype)]
    return pl.pallas_call(
        _mm_body,
        grid=(m // tm, n // tn),
        in_specs=[pl.BlockSpec((tm, k), lambda i, j: (i, 0)),
                  pl.BlockSpec((k, tn), lambda i, j: (0, j))],
        out_specs=pl.BlockSpec((tm, tn), lambda i, j: (i, j)),
        out_shape=jax.ShapeDtypeStruct((m, n), out_dtype),
        scratch_shapes=[pltpu.VMEM(s, d) for s, d in scratch],
        compiler_params=_params(("arbitrary", "arbitrary"), _vmem_limit(blocks, scratch)),
        name=name,
    )(a, w)


def _branch_body(oa_ref, ob_ref, wa_ref, wb_ref, ga_ref, gb_ref, o_ref):
    ya = jnp.dot(oa_ref[...], wa_ref[...].astype(BF16), preferred_element_type=F32)
    yb = jnp.dot(ob_ref[...], wb_ref[...].astype(BF16), preferred_element_type=F32)
    ga = jax.nn.sigmoid(ga_ref[...].astype(F32))
    gb = jax.nn.sigmoid(gb_ref[...].astype(F32))
    o_ref[...] = (ga * ya + gb * yb).astype(o_ref.dtype)


def _gated_branches(o_a, o_b, w_a, w_b, proj, ga_col, gb_col, *, tm, tn):
    m, ka = o_a.shape
    kb = o_b.shape[1]
    n = w_a.shape[1]
    ga_blk, gb_blk = ga_col // tn, gb_col // tn
    blocks = [((tm, ka), BF16), ((tm, kb), BF16), ((ka, tn), F32), ((kb, tn), F32),
              ((tm, tn), proj.dtype), ((tm, tn), proj.dtype), ((tm, tn), BF16)]
    return pl.pallas_call(
        _branch_body,
        grid=(m // tm, n // tn),
        in_specs=[pl.BlockSpec((tm, ka), lambda i, j: (i, 0)),
                  pl.BlockSpec((tm, kb), lambda i, j: (i, 0)),
                  pl.BlockSpec((ka, tn), lambda i, j: (0, j)),
                  pl.BlockSpec((kb, tn), lambda i, j: (0, j)),
                  pl.BlockSpec((tm, tn), lambda i, j: (i, ga_blk + j)),
                  pl.BlockSpec((tm, tn), lambda i, j: (i, gb_blk + j))],
        out_specs=pl.BlockSpec((tm, tn), lambda i, j: (i, j)),
        out_shape=jax.ShapeDtypeStruct((m, n), BF16),
        compiler_params=_params(("arbitrary", "arbitrary"), _vmem_limit(blocks)),
        name="gated_branches",
    )(o_a, o_b, w_a, w_b, proj, proj)


def _swiglu_body(a_ref, wg_ref, wu_ref, o_ref, *scratch):
    a = _act_tile(a_ref, scratch[0] if scratch else None)
    g = jnp.dot(a, wg_ref[...].astype(BF16), preferred_element_type=F32)
    u = jnp.dot(a, wu_ref[...].astype(BF16), preferred_element_type=F32)
    o_ref[...] = (g * jax.nn.sigmoid(g) * u).astype(o_ref.dtype)


def _swiglu_up(a, w_gate_up, *, tm, tn):
    m, k = a.shape
    d_ff = w_gate_up.shape[1] // 2
    up_blk = d_ff // tn
    scratch = [((tm, k), BF16)] if a.dtype != BF16 else []
    blocks = [((tm, k), a.dtype), ((k, tn), F32), ((k, tn), F32), ((tm, tn), BF16)]
    return pl.pallas_call(
        _swiglu_body,
        grid=(m // tm, d_ff // tn),
        in_specs=[pl.BlockSpec((tm, k), lambda i, j: (i, 0)),
                  pl.BlockSpec((k, tn), lambda i, j: (0, j)),
                  pl.BlockSpec((k, tn), lambda i, j: (0, up_blk + j))],
        out_specs=pl.BlockSpec((tm, tn), lambda i, j: (i, j)),
        out_shape=jax.ShapeDtypeStruct((m, d_ff), BF16),
        scratch_shapes=[pltpu.VMEM(s, d) for s, d in scratch],
        compiler_params=_params(("arbitrary", "arbitrary"), _vmem_limit(blocks, scratch)),
        name="swiglu_up",
    )(a, w_gate_up, w_gate_up)


def _mm_ln_body(a_ref, w_ref, r_ref, g_ref, b_ref, o_ref, *maybe_bf16_ref, nk):
    kk = pl.program_id(1)

    def partial_product():
        return jnp.dot(a_ref[...], w_ref[...].astype(BF16), preferred_element_type=F32)

    @pl.when(kk == 0)
    def _():
        o_ref[...] = ALPHA * r_ref[...] + partial_product()

    @pl.when(kk > 0)
    def _():
        o_ref[...] += partial_product()

    @pl.when(kk == nk - 1)
    def _():
        y = o_ref[...]
        mu = jnp.mean(y, axis=-1, keepdims=True)
        yc = y - mu
        var = jnp.mean(yc * yc, axis=-1, keepdims=True)
        out = yc * lax.rsqrt(var + LN_EPS) * g_ref[...] + b_ref[...]
        o_ref[...] = out
        for ob_ref in maybe_bf16_ref:
            ob_ref[...] = out.astype(BF16)


def _matmul_residual_ln(a, w, resid, gain, bias, *, tm, tk, emit_bf16):
    m, k = a.shape
    n = w.shape[1]
    nk = k // tk
    row_block = pl.BlockSpec((tm, n), lambda i, kk: (i, 0))
    blocks = [((tm, tk), BF16), ((tk, n), w.dtype), ((tm, n), F32), ((tm, n), F32)]
    out_specs, out_shape = [row_block], [jax.ShapeDtypeStruct((m, n), F32)]
    if emit_bf16:
        blocks.append(((tm, n), BF16))
        out_specs.append(row_block)
        out_shape.append(jax.ShapeDtypeStruct((m, n), BF16))
    outs = pl.pallas_call(
        functools.partial(_mm_ln_body, nk=nk),
        grid=(m // tm, nk),
        in_specs=[pl.BlockSpec((tm, tk), lambda i, kk: (i, kk)),
                  pl.BlockSpec((tk, n), lambda i, kk: (kk, 0)),
                  row_block,
                  pl.BlockSpec((1, n), lambda i, kk: (0, 0)),
                  pl.BlockSpec((1, n), lambda i, kk: (0, 0))],
        out_specs=out_specs,
        out_shape=out_shape,
        compiler_params=_params(("arbitrary", "arbitrary"), _vmem_limit(blocks)),
        name="matmul_residual_ln",
    )(a, w, resid, gain.reshape(1, n), bias.reshape(1, n))
    return outs if emit_bf16 else outs[0]


def _rel_bucket_np(dist):
    n = np.maximum(dist, 0)
    exact = N_BUCKETS // 2
    logv = (np.log(np.maximum(n, 1).astype(np.float32) / exact) / math.log(MAX_DISTANCE / exact))
    large = exact + (logv.astype(np.float32) * (N_BUCKETS - exact)).astype(np.int32)
    large = np.minimum(large, N_BUCKETS - 1)
    return np.where(n < exact, n, large).astype(np.int32)


def _bias_expand_body(tab_ref, bucket_ref, o_ref, *, head0, shift_last_bucket, scale):
    h = head0 + pl.program_id(0)
    last = tab_ref[N_BUCKETS - 1, h] if shift_last_bucket else 0.0
    for t in range(bucket_ref.shape[0]):
        bucket = bucket_ref[t]
        acc = jnp.full(bucket.shape, NEG, F32)
        for bkt in range(N_BUCKETS):
            acc = jnp.where(bucket == bkt, (tab_ref[bkt, h] - last) * scale, acc)
        o_ref[t] = acc


def _bias_expand(table, dists, valids, *, head0, heads, shift_last_bucket, scale):
    bucket = np.stack([np.where(v, _rel_bucket_np(d), -1) for d, v in zip(dists, valids)]).astype(np.int32)
    nt, r, c = bucket.shape
    blocks = [((nt, r, c), jnp.int32), ((nt, r, c), F32)]
    return pl.pallas_call(
        functools.partial(_bias_expand_body, head0=head0, shift_last_bucket=shift_last_bucket,
                          scale=scale),
        grid=(heads,),
        in_specs=[pl.BlockSpec(memory_space=pltpu.SMEM),
                  pl.BlockSpec((nt, r, c), lambda h: (0, 0, 0))],
        out_specs=pl.BlockSpec((None, nt, r, c), lambda h: (h, 0, 0, 0)),
        out_shape=jax.ShapeDtypeStruct((heads, nt, r, c), F32),
        compiler_params=_params(("arbitrary",), _vmem_limit(blocks)),
        name="bias_expand",
    )(table.astype(F32), jnp.asarray(bucket))


def _swa_body(sink_ref, q_ref, kp_ref, kc_ref, vp_ref, vc_ref, bias_ref, o_ref):
    n = pl.program_id(0)
    half = DH_A
    pairs = G_A // 2
    rows = pairs * BLK
    lane_k = lax.broadcasted_iota(jnp.int32, (2 * BLK, 2 * half), 1)
    row = lax.broadcasted_iota(jnp.int32, (rows, 1), 0)
    lane_o = lax.broadcasted_iota(jnp.int32, (BLK, 2 * half), 1)
    variant = jnp.minimum(n, 1)

    kf = jnp.concatenate([kp_ref[...], kc_ref[...]], axis=0).astype(F32)
    vf = jnp.concatenate([vp_ref[...], vc_ref[...]], axis=0).astype(F32)
    kr = pltpu.roll(kf, half, 1)
    vr = pltpu.roll(vf, half, 1)

    outs = []
    for g in range(KV_A):
        k_own, k_other = (kf, kr) if g == 0 else (kr, kf)
        k_lo = jnp.where(lane_k < half, k_own, 0.0).astype(BF16)
        k_hi = jnp.where(lane_k >= half, k_other, 0.0).astype(BF16)
        v_dup = (jnp.where(lane_k < half, vf, vr) if g == 0 else jnp.where(lane_k < half, vr, vf)).astype(BF16)
        q_stack = jnp.concatenate(
            [q_ref[:, (g * pairs + t) * 2 * half:(g * pairs + t + 1) * 2 * half] for t in range(pairs)], axis=0)
        q_stack = (q_stack.astype(F32) * (DH_A ** -0.5 * LOG2E)).astype(BF16)
        o_par = []
        for par, k_sel in ((0, k_lo), (1, k_hi)):
            sink = jnp.zeros((rows, 1), F32)
            for t in range(pairs):
                sink = jnp.where(row >= t * BLK, sink_ref[2 * (g * pairs + t) + par] * LOG2E, sink)
            s = lax.dot_general(q_stack, k_sel, (((1,), (1,)), ((), ())),
                                preferred_element_type=F32) + bias_ref[variant, g, par]
            mx = jnp.maximum(jnp.max(s, axis=1, keepdims=True), sink)
            p = jnp.exp2(s - mx)
            den = jnp.sum(p, axis=1, keepdims=True) + jnp.exp2(sink - mx)
            o_par.append(jnp.dot(p.astype(BF16), v_dup, preferred_element_type=F32) / den)
        for t in range(pairs):
            outs.append(jnp.where(lane_o < half, o_par[0][t * BLK:(t + 1) * BLK], o_par[1][t * BLK:(t + 1) * BLK]))
    o_ref[...] = jnp.concatenate(outs, axis=1).astype(o_ref.dtype)


def _swa_attention(proj, sinks, bias, *, q_col, k_col, v_col):
    s = proj.shape[0]
    nb = s // BLK
    qw = HA * DH_A
    kvw = KV_A * DH_A
    assert kvw == V7X_LANES and q_col % qw == 0 and k_col % kvw == 0 and v_col % kvw == 0
    kb, vb = k_col // kvw, v_col // kvw
    pairs = G_A // 2
    bias = bias.reshape(KV_A, pairs, 2, 2, BLK, 2 * BLK).transpose(3, 0, 2, 1, 4, 5)
    bias = bias.reshape(2, KV_A, 2, pairs * BLK, 2 * BLK)
    blocks = [((BLK, qw), BF16)] + [((BLK, kvw), BF16)] * 4 + [(bias.shape, F32), ((BLK, qw), BF16)]
    prev = lambda n: jnp.maximum(n - 1, 0)
    return pl.pallas_call(
        _swa_body,
        grid=(nb,),
        in_specs=[pl.BlockSpec(memory_space=pltpu.SMEM),
                  pl.BlockSpec((BLK, qw), lambda n: (n, q_col // qw)),
                  pl.BlockSpec((BLK, kvw), lambda n: (prev(n), kb)),
                  pl.BlockSpec((BLK, kvw), lambda n: (n, kb)),
                  pl.BlockSpec((BLK, kvw), lambda n: (prev(n), vb)),
                  pl.BlockSpec((BLK, kvw), lambda n: (n, vb)),
                  pl.BlockSpec(bias.shape, lambda n: (0, 0, 0, 0, 0))],
        out_specs=pl.BlockSpec((BLK, qw), lambda n: (n, 0)),
        out_shape=jax.ShapeDtypeStruct((s, qw), BF16),
        compiler_params=_params(("arbitrary",), _vmem_limit(blocks)),
        name="swa_sink_attention",
    )(sinks.astype(F32), proj, proj, proj, proj, proj, bias)


def _diff_body(q_ref, k_ref, v_ref, b2_ref, lq1_ref, lk1_ref, lq2_ref, lk2_ref, sw_ref,
               o_ref, qz_ref, sa_ref, sb_ref, cma_ref, cmb_ref, acc_ref, m_ref, l_ref, *, tq, tk):
    i = pl.program_id(1)
    width = 2 * DH_B

    qt = (q_ref[...].astype(F32) * (DH_B ** -0.5 * LOG2E)).T
    row = lax.broadcasted_iota(jnp.int32, (width, tq), 0)
    qz_ref[0] = jnp.where(row < DH_B, qt, 0.0).astype(BF16)
    qz_ref[1] = jnp.where(row >= DH_B, qt, 0.0).astype(BF16)
    acc_ref[...] = jnp.zeros_like(acc_ref)
    m_ref[...] = jnp.full_like(m_ref, NEG)
    l_ref[...] = jnp.zeros_like(l_ref)

    def scores(tile, bias, s_ref, cm_ref):
        kblk = k_ref[pl.ds(pl.multiple_of(tile * tk, tk), tk), :]
        for c in range(2):
            s = jnp.dot(kblk, qz_ref[c], preferred_element_type=F32)
            if bias is not None:
                s = s + bias
            s_ref[c] = s
            cm_ref[c] = jnp.max(s, axis=0, keepdims=True)

    def accumulate(tile, s_ref, cm_ref):
        vblk = v_ref[pl.ds(pl.multiple_of(tile * tk, tk), tk), :]
        for c in range(2):
            m_old = m_ref[c]
            m_new = jnp.maximum(m_old, cm_ref[c])
            rescale = jnp.exp2(m_old - m_new)
            p = jnp.exp2(s_ref[c] - m_new)
            l_ref[c] = rescale * l_ref[c] + jnp.sum(p, axis=0, keepdims=True)
            pv = lax.dot_general(vblk, p.astype(BF16), (((0,), (0,)), ((), ())), preferred_element_type=F32)
            acc_ref[c] = rescale * acc_ref[c] + pv
            m_ref[c] = m_new

    n_items = i + 1
    tile_of = lambda item: jnp.maximum(i - item, 0)
    scores(i, b2_ref[1], sa_ref, cma_ref)

    @pl.when(n_items >= 2)
    def _():
        scores(i - 1, b2_ref[0], sb_ref, cmb_ref)
        accumulate(i, sa_ref, cma_ref)
        scores(tile_of(2), None, sa_ref, cma_ref)
        accumulate(i - 1, sb_ref, cmb_ref)

    def pair(t, carry):
        scores(i - (2 * t + 1), None, sb_ref, cmb_ref)
        accumulate(i - 2 * t, sa_ref, cma_ref)
        scores(tile_of(2 * t + 2), None, sa_ref, cma_ref)
        accumulate(i - (2 * t + 1), sb_ref, cmb_ref)
        return carry

    lax.fori_loop(1, n_items // 2, pair, 0)

    @pl.when(n_items % 2 == 1)
    def _():
        accumulate(0, sa_ref, cma_ref)

    lam = (jnp.exp(jnp.sum(lq1_ref[...] * lk1_ref[...], axis=1, keepdims=True))
           - jnp.exp(jnp.sum(lq2_ref[...] * lk2_ref[...], axis=1, keepdims=True)) + LAMBDA_INIT)
    o = acc_ref[0] / l_ref[0] - lam * (acc_ref[1] / l_ref[1])
    ms = jnp.mean(o * o, axis=0, keepdims=True)
    o = o * lax.rsqrt(ms + LN_EPS) * sw_ref[...] * (1.0 - LAMBDA_INIT)
    o_ref[...] = o.T.astype(o_ref.dtype)


def _diff_attention(proj, bias2, lq1, lk1, lq2, lk2, subln_w, *, q_col, k_col, v_col, tq, tk):
    s = proj.shape[0]
    width = 2 * DH_B
    assert width == V7X_LANES and tq == tk and q_col % width == 0 and k_col % width == 0 and v_col % width == 0
    qb, kb, vb = q_col // width, k_col // width, v_col // width
    blocks = [((tq, width), BF16), ((s, width), BF16), ((s, width), BF16),
              ((2, tk, tq), F32), ((tq, width), BF16)]
    scratch = [((2, width, tq), BF16), ((2, tk, tq), F32), ((2, tk, tq), F32),
               ((2, 8, tq), F32), ((2, 8, tq), F32),
               ((2, width, tq), F32), ((2, 8, tq), F32), ((2, 8, tq), F32)]
    stat = pltpu.VMEM((2, 1, tq), F32)
    vec = lambda v: v.astype(F32).reshape(1, DH_B)
    small = pl.BlockSpec((1, DH_B), lambda h, i: (0, 0))
    return pl.pallas_call(
        functools.partial(_diff_body, tq=tq, tk=tk),
        grid=(HB, s // tq),
        in_specs=[pl.BlockSpec((tq, width), lambda h, i: (i, qb + h)),
                  pl.BlockSpec((s, width), lambda h, i: (0, kb + h)),
                  pl.BlockSpec((s, width), lambda h, i: (0, vb + h)),
                  pl.BlockSpec((None, 2, tk, tq), lambda h, i: (h, 0, 0, 0)),
                  small, small, small, small,
                  pl.BlockSpec((width, 1), lambda h, i: (0, 0))],
        out_specs=pl.BlockSpec((tq, width), lambda h, i: (i, h)),
        out_shape=jax.ShapeDtypeStruct((s, HB * width), BF16),
        scratch_shapes=[pltpu.VMEM((2, width, tq), BF16),
                        pltpu.VMEM((2, tk, tq), F32), pltpu.VMEM((2, tk, tq), F32), stat, stat,
                        pltpu.VMEM((2, width, tq), F32), stat, stat],
        compiler_params=_params(("arbitrary", "arbitrary"), _vmem_limit(blocks, scratch)),
        name="diff_attention",
    )(proj, proj, proj, bias2, vec(lq1), vec(lk1), vec(lq2), vec(lk2),
      subln_w.astype(F32).reshape(width, 1))


def _xattn_body(q_ref, kv_ref, o_ref):
    scale = DH_C ** -0.5
    outs = []
    for h in range(HC):
        qh = q_ref[:, h * DH_C:(h + 1) * DH_C]
        kh = kv_ref[:, h * DH_C:(h + 1) * DH_C]
        vh = kv_ref[:, (HC + h) * DH_C:(HC + h + 1) * DH_C]
        s = lax.dot_general(qh, kh, (((1,), (1,)), ((), ())), preferred_element_type=F32) * scale
        mx = jnp.max(s, axis=1, keepdims=True)
        p = jnp.exp(s - mx)
        den = jnp.sum(p, axis=1, keepdims=True)
        outs.append(jnp.dot(p.astype(BF16), vh, preferred_element_type=F32) / den)
    o_ref[...] = jnp.concatenate(outs, axis=1).astype(o_ref.dtype)


def _cross_attention(q, kv, *, tq):
    s, w = q.shape
    mlen = kv.shape[0]
    blocks = [((tq, w), BF16), ((mlen, 2 * w), BF16), ((tq, w), BF16)]
    return pl.pallas_call(
        _xattn_body,
        grid=(s // tq,),
        in_specs=[pl.BlockSpec((tq, w), lambda i: (i, 0)),
                  pl.BlockSpec((mlen, 2 * w), lambda i: (0, 0))],
        out_specs=pl.BlockSpec((tq, w), lambda i: (i, 0)),
        out_shape=jax.ShapeDtypeStruct((s, w), BF16),
        compiler_params=_params(("arbitrary",), _vmem_limit(blocks)),
        name="memory_cross_attention",
    )(q, kv)


def kernel(x, mem, rel_bias_table, w_in, sinks, lambda_q1, lambda_k1, lambda_q2, lambda_k2, subln_w,
           w_branch_a, w_branch_b, w_o, ln1_g, ln1_b, w_cq, w_mem_kv, w_co, ln2_g, ln2_b,
           w_gate_up, w_down, ln3_g, ln3_b):
    b, s, d = x.shape
    assert b == 1 and w_in.shape[0] == DEPTH == 1
    l = 0
    qa_w, kva_w, qb_w = HA * DH_A, KV_A * DH_A, HB * 2 * DH_B
    col_qa, col_ka, col_va = 0, qa_w, qa_w + kva_w
    col_qb = col_va + kva_w
    col_kb, col_vb = col_qb + qb_w, col_qb + 2 * qb_w
    col_ga = col_vb + qb_w
    col_gb = col_ga + d
    tq = tk = 512

    h0 = x.reshape(s, d)

    proj = _matmul(h0, w_in[l], tm=1024, tn=768, out_dtype=BF16, name="in_proj")

    i_a = np.arange(BLK)[:, None]
    j_a = np.arange(2 * BLK)[None, :]
    dist_a = BLK + i_a - j_a
    band_a = (dist_a >= 0) & (dist_a < WINDOW)
    bias_a = _bias_expand(rel_bias_table, [dist_a, dist_a], [band_a & (j_a >= BLK), band_a], head0=0, heads=HA,
                          shift_last_bucket=False, scale=LOG2E)
    o_a = _swa_attention(proj, sinks[l], bias_a, q_col=col_qa, k_col=col_ka, v_col=col_va)

    r_b = np.arange(tk)[:, None]
    c_b = np.arange(tq)[None, :]
    dist_diag = c_b - r_b
    dist_prev = dist_diag + tk
    bias2 = _bias_expand(rel_bias_table, [dist_prev, dist_diag], [dist_prev >= 0, dist_diag >= 0], head0=HA,
                         heads=HB, shift_last_bucket=True, scale=LOG2E)
    o_b = _diff_attention(proj, bias2, lambda_q1[l], lambda_k1[l], lambda_q2[l],
                          lambda_k2[l], subln_w[l], q_col=col_qb, k_col=col_kb, v_col=col_vb, tq=tq, tk=tk)

    mix = _gated_branches(o_a, o_b, w_branch_a[l], w_branch_b[l], proj, col_ga, col_gb, tm=2048, tn=256)
    h1 = _matmul_residual_ln(mix, w_o[l], h0, ln1_g[l], ln1_b[l], tm=1024, tk=512, emit_bf16=False)

    qc = _matmul(h1, w_cq[l], tm=1024, tn=HC * DH_C, out_dtype=BF16, name="cross_q")
    kvm = _matmul(mem.reshape(mem.shape[1], d), w_mem_kv[l], tm=mem.shape[1], tn=HC * DH_C,
                  out_dtype=BF16, name="mem_kv")
    oc = _cross_attention(qc, kvm, tq=1024)
    h2, h2b = _matmul_residual_ln(oc, w_co[l], h1, ln2_g[l], ln2_b[l], tm=512, tk=HC * DH_C, emit_bf16=True)

    act = _swiglu_up(h2b, w_gate_up[l], tm=2048, tn=256)
    h3 = _matmul_residual_ln(act, w_down[l], h2, ln3_g[l], ln3_b[l], tm=1024, tk=512, emit_bf16=False)
    return h3.reshape(b, s, d)
```

```python
import functools
import math

import numpy as np
import jax
import jax.numpy as jnp
from jax import lax
from jax.experimental import pallas as pl
from jax.experimental.pallas import tpu as pltpu

F32 = jnp.float32
BF16 = jnp.bfloat16

BLK = 128
WINDOW = 128
HA, KV_A, DH_A = 16, 2, 64
G_A = HA // KV_A
HB, DH_B = 8, 64
N_BUCKETS, MAX_DISTANCE = 32, 128
HC, DH_C = 4, 128
LN_EPS = 1e-5
DEPTH = 1
ALPHA = (2 * DEPTH) ** 0.25
LAMBDA_INIT = 0.8 - 0.6 * math.exp(-0.3 * 0)
LOG2E = math.log2(math.e)

V7X_LANES = 128
V7X_VMEM_BYTES = 64 * 1024 * 1024
V7X_VMEM_TEMP_BYTES = 12 * 1024 * 1024

NEG = -1e30


def _nbytes(shape, dtype):
    return int(np.prod(shape)) * jnp.dtype(dtype).itemsize


def _vmem_limit(pipelined, scratch=()):
    need = 2 * sum(_nbytes(s, d) for s, d in pipelined) + sum(_nbytes(s, d) for s, d in scratch)
    need += V7X_VMEM_TEMP_BYTES
    assert need <= V7X_VMEM_BYTES - 4 * 1024 * 1024, need
    return need


def _params(semantics, vmem):
    return pltpu.CompilerParams(dimension_semantics=semantics, vmem_limit_bytes=vmem)


def _act_tile(a_ref, abf_ref):
    if abf_ref is None:
        return a_ref[...]

    @pl.when(pl.program_id(1) == 0)
    def _():
        abf_ref[...] = a_ref[...].astype(BF16)

    return abf_ref[...]


def _mm_body(a_ref, w_ref, o_ref, *scratch):
    a = _act_tile(a_ref, scratch[0] if scratch else None)
    o_ref[...] = jnp.dot(a, w_ref[...].astype(BF16), preferred_element_type=F32).astype(o_ref.dtype)


def _matmul(a, w, *, tm, tn, out_dtype, name):
    m, k = a.shape
    n = w.shape[1]
    cast = a.dtype != BF16
    scratch = [((tm, k), BF16)] if cast else []
    blocks = [((tm, k), a.dtype), ((k, tn), w.dtype), ((tm, tn), out_dtype)]
    return pl.pallas_call(
        _mm_body,
        grid=(m // tm, n // tn),
        in_specs=[pl.BlockSpec((tm, k), lambda i, j: (i, 0)),
                  pl.BlockSpec((k, tn), lambda i, j: (0, j))],
        out_specs=pl.BlockSpec((tm, tn), lambda i, j: (i, j)),
        out_shape=jax.ShapeDtypeStruct((m, n), out_dtype),
        scratch_shapes=[pltpu.VMEM(s, d) for s, d in scratch],
        compiler_params=_params(("arbitrary", "arbitrary"), _vmem_limit(blocks, scratch)),
        name=name,
    )(a, w)


def _branch_body(oa_ref, ob_ref, wa_ref, wb_ref, ga_ref, gb_ref, o_ref):
    ya = jnp.dot(oa_ref[...], wa_ref[...].astype(BF16), preferred_element_type=F32)
    yb = jnp.dot(ob_ref[...], wb_ref[...].astype(BF16), preferred_element_type=F32)
    ga = jax.nn.sigmoid(ga_ref[...].astype(F32))
    gb = jax.nn.sigmoid(gb_ref[...].astype(F32))
    o_ref[...] = (ga * ya + gb * yb).astype(o_ref.dtype)


def _gated_branches(o_a, o_b, w_a, w_b, proj, ga_col, gb_col, *, tm, tn):
    m, ka = o_a.shape
    kb = o_b.shape[1]
    n = w_a.shape[1]
    ga_blk, gb_blk = ga_col // tn, gb_col // tn
    blocks = [((tm, ka), BF16), ((tm, kb), BF16), ((ka, tn), F32), ((kb, tn), F32),
              ((tm, tn), proj.dtype), ((tm, tn), proj.dtype), ((tm, tn), BF16)]
    return pl.pallas_call(
        _branch_body,
        grid=(m // tm, n // tn),
        in_specs=[pl.BlockSpec((tm, ka), lambda i, j: (i, 0)),
                  pl.BlockSpec((tm, kb), lambda i, j: (i, 0)),
                  pl.BlockSpec((ka, tn), lambda i, j: (0, j)),
                  pl.BlockSpec((kb, tn), lambda i, j: (0, j)),
                  pl.BlockSpec((tm, tn), lambda i, j: (i, ga_blk + j)),
                  pl.BlockSpec((tm, tn), lambda i, j: (i, gb_blk + j))],
        out_specs=pl.BlockSpec((tm, tn), lambda i, j: (i, j)),
        out_shape=jax.ShapeDtypeStruct((m, n), BF16),
        compiler_params=_params(("arbitrary", "arbitrary"), _vmem_limit(blocks)),
        name="gated_branches",
    )(o_a, o_b, w_a, w_b, proj, proj)


def _swiglu_body(a_ref, wg_ref, wu_ref, o_ref, *scratch):
    a = _act_tile(a_ref, scratch[0] if scratch else None)
    g = jnp.dot(a, wg_ref[...].astype(BF16), preferred_element_type=F32)
    u = jnp.dot(a, wu_ref[...].astype(BF16), preferred_element_type=F32)
    o_ref[...] = (g * jax.nn.sigmoid(g) * u).astype(o_ref.dtype)


def _swiglu_up(a, w_gate_up, *, tm, tn):
    m, k = a.shape
    d_ff = w_gate_up.shape[1] // 2
    up_blk = d_ff // tn
    scratch = [((tm, k), BF16)] if a.dtype != BF16 else []
    blocks = [((tm, k), a.dtype), ((k, tn), F32), ((k, tn), F32), ((tm, tn), BF16)]
    return pl.pallas_call(
        _swiglu_body,
        grid=(m // tm, d_ff // tn),
        in_specs=[pl.BlockSpec((tm, k), lambda i, j: (i, 0)),
                  pl.BlockSpec((k, tn), lambda i, j: (0, j)),
                  pl.BlockSpec((k, tn), lambda i, j: (0, up_blk + j))],
        out_specs=pl.BlockSpec((tm, tn), lambda i, j: (i, j)),
        out_shape=jax.ShapeDtypeStruct((m, d_ff), BF16),
        scratch_shapes=[pltpu.VMEM(s, d) for s, d in scratch],
        compiler_params=_params(("arbitrary", "arbitrary"), _vmem_limit(blocks, scratch)),
        name="swiglu_up",
    )(a, w_gate_up, w_gate_up)


def _mm_ln_body(a_ref, w_ref, r_ref, g_ref, b_ref, o_ref, *maybe_bf16_ref, nk):
    kk = pl.program_id(1)

    def partial_product():
        return jnp.dot(a_ref[...], w_ref[...].astype(BF16), preferred_element_type=F32)

    @pl.when(kk == 0)
    def _():
        o_ref[...] = ALPHA * r_ref[...] + partial_product()

    @pl.when(kk > 0)
    def _():
        o_ref[...] += partial_product()

    @pl.when(kk == nk - 1)
    def _():
        y = o_ref[...]
        mu = jnp.mean(y, axis=-1, keepdims=True)
        yc = y - mu
        var = jnp.mean(yc * yc, axis=-1, keepdims=True)
        out = yc * lax.rsqrt(var + LN_EPS) * g_ref[...] + b_ref[...]
        o_ref[...] = out
        for ob_ref in maybe_bf16_ref:
            ob_ref[...] = out.astype(BF16)


def _matmul_residual_ln(a, w, resid, gain, bias, *, tm, tk, emit_bf16):
    m, k = a.shape
    n = w.shape[1]
    nk = k // tk
    row_block = pl.BlockSpec((tm, n), lambda i, kk: (i, 0))
    blocks = [((tm, tk), BF16), ((tk, n), w.dtype), ((tm, n), F32), ((tm, n), F32)]
    out_specs, out_shape = [row_block], [jax.ShapeDtypeStruct((m, n), F32)]
    if emit_bf16:
        blocks.append(((tm, n), BF16))
        out_specs.append(row_block)
        out_shape.append(jax.ShapeDtypeStruct((m, n), BF16))
    outs = pl.pallas_call(
        functools.partial(_mm_ln_body, nk=nk),
        grid=(m // tm, nk),
        in_specs=[pl.BlockSpec((tm, tk), lambda i, kk: (i, kk)),
                  pl.BlockSpec((tk, n), lambda i, kk: (kk, 0)),
                  row_block,
                  pl.BlockSpec((1, n), lambda i, kk: (0, 0)),
                  pl.BlockSpec((1, n), lambda i, kk: (0, 0))],
        out_specs=out_specs,
        out_shape=out_shape,
        compiler_params=_params(("arbitrary", "arbitrary"), _vmem_limit(blocks)),
        name="matmul_residual_ln",
    )(a, w, resid, gain.reshape(1, n), bias.reshape(1, n))
    return outs if emit_bf16 else outs[0]


def _rel_bucket_np(dist):
    n = np.maximum(dist, 0)
    exact = N_BUCKETS // 2
    logv = (np.log(np.maximum(n, 1).astype(np.float32) / exact) / math.log(MAX_DISTANCE / exact))
    large = exact + (logv.astype(np.float32) * (N_BUCKETS - exact)).astype(np.int32)
    large = np.minimum(large, N_BUCKETS - 1)
    return np.where(n < exact, n, large).astype(np.int32)


def _bias_expand_body(tab_ref, bucket_ref, o_ref, *, head0, shift_last_bucket, scale):
    h = head0 + pl.program_id(0)
    last = tab_ref[N_BUCKETS - 1, h] if shift_last_bucket else 0.0
    for t in range(bucket_ref.shape[0]):
        bucket = bucket_ref[t]
        acc = jnp.full(bucket.shape, NEG, F32)
        for bkt in range(N_BUCKETS):
            acc = jnp.where(bucket == bkt, (tab_ref[bkt, h] - last) * scale, acc)
        o_ref[t] = acc


def _bias_expand(table, dists, valids, *, head0, heads, shift_last_bucket, scale):
    bucket = np.stack([np.where(v, _rel_bucket_np(d), -1) for d, v in zip(dists, valids)]).astype(np.int32)
    nt, r, c = bucket.shape
    blocks = [((nt, r, c), jnp.int32), ((nt, r, c), F32)]
    return pl.pallas_call(
        functools.partial(_bias_expand_body, head0=head0, shift_last_bucket=shift_last_bucket,
                          scale=scale),
        grid=(heads,),
        in_specs=[pl.BlockSpec(memory_space=pltpu.SMEM),
                  pl.BlockSpec((nt, r, c), lambda h: (0, 0, 0))],
        out_specs=pl.BlockSpec((None, nt, r, c), lambda h: (h, 0, 0, 0)),
        out_shape=jax.ShapeDtypeStruct((heads, nt, r, c), F32),
        compiler_params=_params(("arbitrary",), _vmem_limit(blocks)),
        name="bias_expand",
    )(table.astype(F32), jnp.asarray(bucket))


def _swa_body(sink_ref, q_ref, kp_ref, kc_ref, vp_ref, vc_ref, bias_ref, o_ref):
    n = pl.program_id(0)
    half = DH_A
    pairs = G_A // 2
    rows = pairs * BLK
    lane_k = lax.broadcasted_iota(jnp.int32, (2 * BLK, 2 * half), 1)
    row = lax.broadcasted_iota(jnp.int32, (rows, 1), 0)
    lane_o = lax.broadcasted_iota(jnp.int32, (BLK, 2 * half), 1)
    variant = jnp.minimum(n, 1)

    kf = jnp.concatenate([kp_ref[...], kc_ref[...]], axis=0).astype(F32)
    vf = jnp.concatenate([vp_ref[...], vc_ref[...]], axis=0).astype(F32)
    kr = pltpu.roll(kf, half, 1)
    vr = pltpu.roll(vf, half, 1)

    outs = []
    for g in range(KV_A):
        k_own, k_other = (kf, kr) if g == 0 else (kr, kf)
        k_lo = jnp.where(lane_k < half, k_own, 0.0).astype(BF16)
        k_hi = jnp.where(lane_k >= half, k_other, 0.0).astype(BF16)
        v_dup = (jnp.where(lane_k < half, vf, vr) if g == 0 else jnp.where(lane_k < half, vr, vf)).astype(BF16)
        q_stack = jnp.concatenate(
            [q_ref[:, (g * pairs + t) * 2 * half:(g * pairs + t + 1) * 2 * half] for t in range(pairs)], axis=0)
        q_stack = (q_stack.astype(F32) * (DH_A ** -0.5 * LOG2E)).astype(BF16)
        o_par = []
        for par, k_sel in ((0, k_lo), (1, k_hi)):
            sink = jnp.zeros((rows, 1), F32)
            for t in range(pairs):
                sink = jnp.where(row >= t * BLK, sink_ref[2 * (g * pairs + t) + par] * LOG2E, sink)
            s = lax.dot_general(q_stack, k_sel, (((1,), (1,)), ((), ())),
                                preferred_element_type=F32) + bias_ref[variant, g, par]
            mx = jnp.maximum(jnp.max(s, axis=1, keepdims=True), sink)
            p = jnp.exp2(s - mx)
            den = jnp.sum(p, axis=1, keepdims=True) + jnp.exp2(sink - mx)
            o_par.append(jnp.dot(p.astype(BF16), v_dup, preferred_element_type=F32) / den)
        for t in range(pairs):
            outs.append(jnp.where(lane_o < half, o_par[0][t * BLK:(t + 1) * BLK], o_par[1][t * BLK:(t + 1) * BLK]))
    o_ref[...] = jnp.concatenate(outs, axis=1).astype(o_ref.dtype)


def _swa_attention(proj, sinks, bias, *, q_col, k_col, v_col):
    s = proj.shape[0]
    nb = s // BLK
    qw = HA * DH_A
    kvw = KV_A * DH_A
    assert kvw == V7X_LANES and q_col % qw == 0 and k_col % kvw == 0 and v_col % kvw == 0
    kb, vb = k_col // kvw, v_col // kvw
    pairs = G_A // 2
    bias = bias.reshape(KV_A, pairs, 2, 2, BLK, 2 * BLK).transpose(3, 0, 2, 1, 4, 5)
    bias = bias.reshape(2, KV_A, 2, pairs * BLK, 2 * BLK)
    blocks = [((BLK, qw), BF16)] + [((BLK, kvw), BF16)] * 4 + [(bias.shape, F32), ((BLK, qw), BF16)]
    prev = lambda n: jnp.maximum(n - 1, 0)
    return pl.pallas_call(
        _swa_body,
        grid=(nb,),
        in_specs=[pl.BlockSpec(memory_space=pltpu.SMEM),
                  pl.BlockSpec((BLK, qw), lambda n: (n, q_col // qw)),
                  pl.BlockSpec((BLK, kvw), lambda n: (prev(n), kb)),
                  pl.BlockSpec((BLK, kvw), lambda n: (n, kb)),
                  pl.BlockSpec((BLK, kvw), lambda n: (prev(n), vb)),
                  pl.BlockSpec((BLK, kvw), lambda n: (n, vb)),
                  pl.BlockSpec(bias.shape, lambda n: (0, 0, 0, 0, 0))],
        out_specs=pl.BlockSpec((BLK, qw), lambda n: (n, 0)),
        out_shape=jax.ShapeDtypeStruct((s, qw), BF16),
        compiler_params=_params(("arbitrary",), _vmem_limit(blocks)),
        name="swa_sink_attention",
    )(sinks.astype(F32), proj, proj, proj, proj, proj, bias)


def _band_bias_body(tab_ref, bucket_ref, o_ref, *, head0, scale, tile_offsets):
    h = head0 + pl.program_id(0)
    last = tab_ref[N_BUCKETS - 1, h]
    bands = []
    for t in range(2):
        bucket = bucket_ref[t]
        acc = jnp.full(bucket.shape, NEG, F32)
        for bkt in range(N_BUCKETS):
            acc = jnp.where(bucket == bkt, (tab_ref[bkt, h] - last) * scale, acc)
        bands.append(acc)
    sub = MAX_DISTANCE
    nb = o_ref.shape[1] // sub
    for t, base in enumerate(tile_offsets):
        for rb in range(nb):
            for cb in range(nb):
                off = base + sub * (cb - rb)
                if off < 0:
                    blk = jnp.full((sub, sub), NEG, F32)
                elif off < 2 * sub:
                    blk = bands[off // sub]
                else:
                    blk = jnp.zeros((sub, sub), F32)
                o_ref[t, rb * sub:(rb + 1) * sub, cb * sub:(cb + 1) * sub] = blk


def _band_bias_tiles(table, *, head0, heads, tile, tile_offsets, scale):
    sub = MAX_DISTANCE
    assert tile % sub == 0 and all(o % sub == 0 for o in tile_offsets)
    r = np.arange(sub)[:, None]
    c = np.arange(sub)[None, :]
    bucket = np.stack([np.where(c - r + o >= 0, _rel_bucket_np(c - r + o), -1) for o in (0, sub)]).astype(np.int32)
    nt = len(tile_offsets)
    blocks = [((2, sub, sub), jnp.int32), ((nt, tile, tile), F32)]
    return pl.pallas_call(
        functools.partial(_band_bias_body, head0=head0, scale=scale, tile_offsets=tuple(tile_offsets)),
        grid=(heads,),
        in_specs=[pl.BlockSpec(memory_space=pltpu.SMEM),
                  pl.BlockSpec((2, sub, sub), lambda h: (0, 0, 0))],
        out_specs=pl.BlockSpec((None, nt, tile, tile), lambda h: (h, 0, 0, 0)),
        out_shape=jax.ShapeDtypeStruct((heads, nt, tile, tile), F32),
        compiler_params=_params(("arbitrary",), _vmem_limit(blocks)),
        name="band_bias_tiles",
    )(table.astype(F32), jnp.asarray(bucket))


HEADS_PER_STEP = 2


def _diff_body(q_ref, k_ref, v_ref, b2_ref, lq1_ref, lk1_ref, lq2_ref, lk2_ref, sw_ref,
               o_ref, qz_ref, sa_ref, sb_ref, cma_ref, cmb_ref, acc_ref, m_ref, l_ref, *, tq, tk):
    i = pl.program_id(1)
    width = 2 * DH_B
    heads = range(HEADS_PER_STEP)
    lanes = lambda hh: slice(hh * width, (hh + 1) * width)

    row = lax.broadcasted_iota(jnp.int32, (width, tq), 0)
    for hh in heads:
        qt = (q_ref[:, lanes(hh)].astype(F32) * (DH_B ** -0.5 * LOG2E)).T
        qz_ref[hh, 0] = jnp.where(row < DH_B, qt, 0.0).astype(BF16)
        qz_ref[hh, 1] = jnp.where(row >= DH_B, qt, 0.0).astype(BF16)
    acc_ref[...] = jnp.zeros_like(acc_ref)
    m_ref[...] = jnp.full_like(m_ref, NEG)
    l_ref[...] = jnp.zeros_like(l_ref)

    def scores(tile, bias_tile, s_ref, cm_ref):
        rows = pl.ds(pl.multiple_of(tile * tk, tk), tk)
        for hh in heads:
            kblk = k_ref[rows, lanes(hh)]
            for c in range(2):
                s = jnp.dot(kblk, qz_ref[hh, c], preferred_element_type=F32)
                if bias_tile is not None:
                    s = s + b2_ref[hh, bias_tile]
                s_ref[hh, c] = s
                cm_ref[hh, c] = jnp.max(s, axis=0, keepdims=True)

    def accumulate(tile, s_ref, cm_ref):
        rows = pl.ds(pl.multiple_of(tile * tk, tk), tk)
        for hh in heads:
            vblk = v_ref[rows, lanes(hh)]
            for c in range(2):
                m_old = m_ref[hh, c]
                m_new = jnp.maximum(m_old, cm_ref[hh, c])
                rescale = jnp.exp2(m_old - m_new)
                p = jnp.exp2(s_ref[hh, c] - m_new)
                l_ref[hh, c] = rescale * l_ref[hh, c] + jnp.sum(p, axis=0, keepdims=True)
                pv = lax.dot_general(vblk, p.astype(BF16), (((0,), (0,)), ((), ())),
                                     preferred_element_type=F32)
                acc_ref[hh, c] = rescale * acc_ref[hh, c] + pv
                m_ref[hh, c] = m_new

    n_items = i + 1
    tile_of = lambda item: jnp.maximum(i - item, 0)
    scores(i, 1, sa_ref, cma_ref)

    @pl.when(n_items >= 2)
    def _():
        scores(i - 1, 0, sb_ref, cmb_ref)
        accumulate(i, sa_ref, cma_ref)
        scores(tile_of(2), None, sa_ref, cma_ref)
        accumulate(i - 1, sb_ref, cmb_ref)

    def pair(t, carry):
        scores(i - (2 * t + 1), None, sb_ref, cmb_ref)
        accumulate(i - 2 * t, sa_ref, cma_ref)
        scores(tile_of(2 * t + 2), None, sa_ref, cma_ref)
        accumulate(i - (2 * t + 1), sb_ref, cmb_ref)
        return carry

    lax.fori_loop(1, n_items // 2, pair, 0)

    @pl.when(n_items % 2 == 1)
    def _():
        accumulate(0, sa_ref, cma_ref)

    lam = (jnp.exp(jnp.sum(lq1_ref[...] * lk1_ref[...], axis=1, keepdims=True))
           - jnp.exp(jnp.sum(lq2_ref[...] * lk2_ref[...], axis=1, keepdims=True)) + LAMBDA_INIT)
    for hh in heads:
        o = acc_ref[hh, 0] / l_ref[hh, 0] - lam * (acc_ref[hh, 1] / l_ref[hh, 1])
        ms = jnp.mean(o * o, axis=0, keepdims=True)
        o = o * lax.rsqrt(ms + LN_EPS) * sw_ref[...] * (1.0 - LAMBDA_INIT)
        o_ref[:, lanes(hh)] = o.T.astype(o_ref.dtype)


def _diff_attention(proj, bias2, lq1, lk1, lq2, lk2, subln_w, *, q_col, k_col, v_col, tq, tk):
    s = proj.shape[0]
    width = 2 * DH_B
    hps = HEADS_PER_STEP
    wide = hps * width
    assert width == V7X_LANES and tq == tk and HB % hps == 0
    assert q_col % wide == 0 and k_col % wide == 0 and v_col % wide == 0
    qb, kb, vb = q_col // wide, k_col // wide, v_col // wide
    blocks = [((tq, wide), BF16), ((s, wide), BF16), ((s, wide), BF16),
              ((hps, 2, tk, tq), F32), ((tq, wide), BF16)]
    scores_buf = ((hps, 2, tk, tq), F32)
    stats_buf = ((hps, 2, 1, tq), F32)
    scratch = [((hps, 2, width, tq), BF16), scores_buf, scores_buf, stats_buf, stats_buf,
               ((hps, 2, width, tq), F32), stats_buf, stats_buf]
    padded = [((hps, 2, 8, tq), F32) if sd == stats_buf else sd for sd in scratch]
    vec = lambda v: v.astype(F32).reshape(1, DH_B)
    small = pl.BlockSpec((1, DH_B), lambda h, i: (0, 0))
    return pl.pallas_call(
        functools.partial(_diff_body, tq=tq, tk=tk),
        grid=(HB // hps, s // tq),
        in_specs=[pl.BlockSpec((tq, wide), lambda h, i: (i, qb + h)),
                  pl.BlockSpec((s, wide), lambda h, i: (0, kb + h)),
                  pl.BlockSpec((s, wide), lambda h, i: (0, vb + h)),
                  pl.BlockSpec((hps, 2, tk, tq), lambda h, i: (h, 0, 0, 0)),
                  small, small, small, small,
                  pl.BlockSpec((width, 1), lambda h, i: (0, 0))],
        out_specs=pl.BlockSpec((tq, wide), lambda h, i: (i, h)),
        out_shape=jax.ShapeDtypeStruct((s, HB * width), BF16),
        scratch_shapes=[pltpu.VMEM(sh, dt) for sh, dt in scratch],
        compiler_params=_params(("arbitrary", "arbitrary"), _vmem_limit(blocks, padded)),
        name="diff_attention",
    )(proj, proj, proj, bias2, vec(lq1), vec(lk1), vec(lq2), vec(lk2),
      subln_w.astype(F32).reshape(width, 1))


def _xattn_body(q_ref, kv_ref, o_ref):
    scale = DH_C ** -0.5
    outs = []
    for h in range(HC):
        qh = q_ref[:, h * DH_C:(h + 1) * DH_C]
        kh = kv_ref[:, h * DH_C:(h + 1) * DH_C]
        vh = kv_ref[:, (HC + h) * DH_C:(HC + h + 1) * DH_C]
        s = lax.dot_general(qh, kh, (((1,), (1,)), ((), ())), preferred_element_type=F32) * scale
        mx = jnp.max(s, axis=1, keepdims=True)
        p = jnp.exp(s - mx)
        den = jnp.sum(p, axis=1, keepdims=True)
        outs.append(jnp.dot(p.astype(BF16), vh, preferred_element_type=F32) / den)
    o_ref[...] = jnp.concatenate(outs, axis=1).astype(o_ref.dtype)


def _cross_attention(q, kv, *, tq):
    s, w = q.shape
    mlen = kv.shape[0]
    blocks = [((tq, w), BF16), ((mlen, 2 * w), BF16), ((tq, w), BF16)]
    return pl.pallas_call(
        _xattn_body,
        grid=(s // tq,),
        in_specs=[pl.BlockSpec((tq, w), lambda i: (i, 0)),
                  pl.BlockSpec((mlen, 2 * w), lambda i: (0, 0))],
        out_specs=pl.BlockSpec((tq, w), lambda i: (i, 0)),
        out_shape=jax.ShapeDtypeStruct((s, w), BF16),
        compiler_params=_params(("arbitrary",), _vmem_limit(blocks)),
        name="memory_cross_attention",
    )(q, kv)


def kernel(x, mem, rel_bias_table, w_in, sinks, lambda_q1, lambda_k1, lambda_q2, lambda_k2, subln_w,
           w_branch_a, w_branch_b, w_o, ln1_g, ln1_b, w_cq, w_mem_kv, w_co, ln2_g, ln2_b,
           w_gate_up, w_down, ln3_g, ln3_b):
    b, s, d = x.shape
    assert b == 1 and w_in.shape[0] == DEPTH == 1
    l = 0
    qa_w, kva_w, qb_w = HA * DH_A, KV_A * DH_A, HB * 2 * DH_B
    col_qa, col_ka, col_va = 0, qa_w, qa_w + kva_w
    col_qb = col_va + kva_w
    col_kb, col_vb = col_qb + qb_w, col_qb + 2 * qb_w
    col_ga = col_vb + qb_w
    col_gb = col_ga + d
    tq = tk = 512

    h0 = x.reshape(s, d)

    proj = _matmul(h0, w_in[l], tm=2048, tn=256, out_dtype=BF16, name="in_proj")

    i_a = np.arange(BLK)[:, None]
    j_a = np.arange(2 * BLK)[None, :]
    dist_a = BLK + i_a - j_a
    band_a = (dist_a >= 0) & (dist_a < WINDOW)
    bias_a = _bias_expand(rel_bias_table, [dist_a, dist_a], [band_a & (j_a >= BLK), band_a], head0=0, heads=HA,
                          shift_last_bucket=False, scale=LOG2E)
    o_a = _swa_attention(proj, sinks[l], bias_a, q_col=col_qa, k_col=col_ka, v_col=col_va)

    bias2 = _band_bias_tiles(rel_bias_table, head0=HA, heads=HB, tile=tq, tile_offsets=(tk, 0), scale=LOG2E)
    o_b = _diff_attention(proj, bias2, lambda_q1[l], lambda_k1[l], lambda_q2[l],
                          lambda_k2[l], subln_w[l], q_col=col_qb, k_col=col_kb, v_col=col_vb, tq=tq, tk=tk)

    mix = _gated_branches(o_a, o_b, w_branch_a[l], w_branch_b[l], proj, col_ga, col_gb, tm=2048, tn=256)
    h1 = _matmul_residual_ln(mix, w_o[l], h0, ln1_g[l], ln1_b[l], tm=1024, tk=512, emit_bf16=False)

    qc = _matmul(h1, w_cq[l], tm=1024, tn=HC * DH_C, out_dtype=BF16, name="cross_q")
    kvm = _matmul(mem.reshape(mem.shape[1], d), w_mem_kv[l], tm=mem.shape[1], tn=HC * DH_C,
                  out_dtype=BF16, name="mem_kv")
    oc = _cross_attention(qc, kvm, tq=1024)
    h2, h2b = _matmul_residual_ln(oc, w_co[l], h1, ln2_g[l], ln2_b[l], tm=512, tk=HC * DH_C, emit_bf16=True)

    act = _swiglu_up(h2b, w_gate_up[l], tm=2048, tn=256)
    h3 = _matmul_residual_ln(act, w_down[l], h2, ln3_g[l], ln3_b[l], tm=1024, tk=512, emit_bf16=False)
    return h3.reshape(b, s, d)
```

```python
import functools
import math

import numpy as np
import jax
import jax.numpy as jnp
from jax import lax
from jax.experimental import pallas as pl
from jax.experimental.pallas import tpu as pltpu

F32 = jnp.float32
BF16 = jnp.bfloat16

BLK = 128
WINDOW = 128
HA, KV_A, DH_A = 16, 2, 64
G_A = HA // KV_A
HB, DH_B = 8, 64
N_BUCKETS, MAX_DISTANCE = 32, 128
HC, DH_C = 4, 128
LN_EPS = 1e-5
DEPTH = 1
ALPHA = (2 * DEPTH) ** 0.25
LAMBDA_INIT = 0.8 - 0.6 * math.exp(-0.3 * 0)
LOG2E = math.log2(math.e)

V7X_LANES = 128
V7X_VMEM_BYTES = 64 * 1024 * 1024
V7X_VMEM_TEMP_BYTES = 12 * 1024 * 1024

EPILOGUE_CHUNKS = 4

NEG = -1e30


def _nbytes(shape, dtype):
    return int(np.prod(shape)) * jnp.dtype(dtype).itemsize


def _vmem_limit(pipelined, scratch=()):
    need = 2 * sum(_nbytes(s, d) for s, d in pipelined) + sum(_nbytes(s, d) for s, d in scratch)
    need += V7X_VMEM_TEMP_BYTES
    assert need <= V7X_VMEM_BYTES - 4 * 1024 * 1024, need
    return need


def _params(semantics, vmem):
    return pltpu.CompilerParams(dimension_semantics=semantics, vmem_limit_bytes=vmem)


def _act_tile(a_ref, abf_ref):
    if abf_ref is None:
        return a_ref[...]

    @pl.when(pl.program_id(1) == 0)
    def _():
        abf_ref[...] = a_ref[...].astype(BF16)

    return abf_ref[...]


def _mm_body(a_ref, w_ref, o_ref, *scratch):
    a = _act_tile(a_ref, scratch[0] if scratch else None)
    o_ref[...] = jnp.dot(a, w_ref[...].astype(BF16), preferred_element_type=F32).astype(o_ref.dtype)


def _matmul(a, w, *, tm, tn, out_dtype, name):
    m, k = a.shape
    n = w.shape[1]
    cast = a.dtype != BF16
    scratch = [((tm, k), BF16)] if cast else []
    blocks = [((tm, k), a.dtype), ((k, tn), w.dtype), ((tm, tn), out_dtype)]
    return pl.pallas_call(
        _mm_body,
        grid=(m // tm, n // tn),
        in_specs=[pl.BlockSpec((tm, k), lambda i, j: (i, 0)),
                  pl.BlockSpec((k, tn), lambda i, j: (0, j))],
        out_specs=pl.BlockSpec((tm, tn), lambda i, j: (i, j)),
        out_shape=jax.ShapeDtypeStruct((m, n), out_dtype),
        scratch_shapes=[pltpu.VMEM(s, d) for s, d in scratch],
        compiler_params=_params(("arbitrary", "arbitrary"), _vmem_limit(blocks, scratch)),
        name=name,
    )(a, w)


def _row_chunks(rows, chunks):
    size = rows // chunks
    return [slice(c * size, (c + 1) * size) for c in range(chunks)]


def _branch_body(oa_ref, ob_ref, wa_ref, wb_ref, ga_ref, gb_ref, o_ref):
    wa = wa_ref[...].astype(BF16)
    wb = wb_ref[...].astype(BF16)
    for rows in _row_chunks(o_ref.shape[0], EPILOGUE_CHUNKS):
        ya = jnp.dot(oa_ref[rows, :], wa, preferred_element_type=F32)
        yb = jnp.dot(ob_ref[rows, :], wb, preferred_element_type=F32)
        ga = jax.nn.sigmoid(ga_ref[rows, :].astype(F32))
        gb = jax.nn.sigmoid(gb_ref[rows, :].astype(F32))
        o_ref[rows, :] = (ga * ya + gb * yb).astype(o_ref.dtype)


def _gated_branches(o_a, o_b, w_a, w_b, proj, ga_col, gb_col, *, tm, tn):
    m, ka = o_a.shape
    kb = o_b.shape[1]
    n = w_a.shape[1]
    ga_blk, gb_blk = ga_col // tn, gb_col // tn
    blocks = [((tm, ka), BF16), ((tm, kb), BF16), ((ka, tn), F32), ((kb, tn), F32),
              ((tm, tn), proj.dtype), ((tm, tn), proj.dtype), ((tm, tn), BF16)]
    return pl.pallas_call(
        _branch_body,
        grid=(m // tm, n // tn),
        in_specs=[pl.BlockSpec((tm, ka), lambda i, j: (i, 0)),
                  pl.BlockSpec((tm, kb), lambda i, j: (i, 0)),
                  pl.BlockSpec((ka, tn), lambda i, j: (0, j)),
                  pl.BlockSpec((kb, tn), lambda i, j: (0, j)),
                  pl.BlockSpec((tm, tn), lambda i, j: (i, ga_blk + j)),
                  pl.BlockSpec((tm, tn), lambda i, j: (i, gb_blk + j))],
        out_specs=pl.BlockSpec((tm, tn), lambda i, j: (i, j)),
        out_shape=jax.ShapeDtypeStruct((m, n), BF16),
        compiler_params=_params(("arbitrary", "arbitrary"), _vmem_limit(blocks)),
        name="gated_branches",
    )(o_a, o_b, w_a, w_b, proj, proj)


def _swiglu_body(a_ref, wg_ref, wu_ref, o_ref, *scratch):
    a = _act_tile(a_ref, scratch[0] if scratch else None)
    g = jnp.dot(a, wg_ref[...].astype(BF16), preferred_element_type=F32)
    u = jnp.dot(a, wu_ref[...].astype(BF16), preferred_element_type=F32)
    o_ref[...] = (g * jax.nn.sigmoid(g) * u).astype(o_ref.dtype)


def _swiglu_up(a, w_gate_up, *, tm, tn):
    m, k = a.shape
    d_ff = w_gate_up.shape[1] // 2
    up_blk = d_ff // tn
    scratch = [((tm, k), BF16)] if a.dtype != BF16 else []
    blocks = [((tm, k), a.dtype), ((k, tn), F32), ((k, tn), F32), ((tm, tn), BF16)]
    return pl.pallas_call(
        _swiglu_body,
        grid=(m // tm, d_ff // tn),
        in_specs=[pl.BlockSpec((tm, k), lambda i, j: (i, 0)),
                  pl.BlockSpec((k, tn), lambda i, j: (0, j)),
                  pl.BlockSpec((k, tn), lambda i, j: (0, up_blk + j))],
        out_specs=pl.BlockSpec((tm, tn), lambda i, j: (i, j)),
        out_shape=jax.ShapeDtypeStruct((m, d_ff), BF16),
        scratch_shapes=[pltpu.VMEM(s, d) for s, d in scratch],
        compiler_params=_params(("arbitrary", "arbitrary"), _vmem_limit(blocks, scratch)),
        name="swiglu_up",
    )(a, w_gate_up, w_gate_up)


def _mm_ln_body(a_ref, w_ref, r_ref, g_ref, b_ref, o_ref, *maybe_bf16_ref, nk):
    kk = pl.program_id(1)
    last = kk == nk - 1

    def partial_product(rows=slice(None)):
        return jnp.dot(a_ref[rows, :], w_ref[...].astype(BF16), preferred_element_type=F32)

    if nk > 1:
        @pl.when(kk == 0)
        def _():
            o_ref[...] = ALPHA * r_ref[...] + partial_product()

        @pl.when(jnp.logical_and(kk > 0, jnp.logical_not(last)))
        def _():
            o_ref[...] += partial_product()

    @pl.when(last)
    def _():
        for rows in _row_chunks(o_ref.shape[0], EPILOGUE_CHUNKS):
            seed = o_ref[rows, :] if nk > 1 else ALPHA * r_ref[rows, :]
            y = seed + partial_product(rows)
            mu = jnp.mean(y, axis=-1, keepdims=True)
            yc = y - mu
            var = jnp.mean(yc * yc, axis=-1, keepdims=True)
            out = yc * lax.rsqrt(var + LN_EPS) * g_ref[...] + b_ref[...]
            o_ref[rows, :] = out
            for ob_ref in maybe_bf16_ref:
                ob_ref[rows, :] = out.astype(BF16)


def _matmul_residual_ln(a, w, resid, gain, bias, *, tm, tk, emit_bf16):
    m, k = a.shape
    n = w.shape[1]
    nk = k // tk
    row_block = pl.BlockSpec((tm, n), lambda i, kk: (i, 0))
    blocks = [((tm, tk), BF16), ((tk, n), w.dtype), ((tm, n), F32), ((tm, n), F32)]
    out_specs, out_shape = [row_block], [jax.ShapeDtypeStruct((m, n), F32)]
    if emit_bf16:
        blocks.append(((tm, n), BF16))
        out_specs.append(row_block)
        out_shape.append(jax.ShapeDtypeStruct((m, n), BF16))
    outs = pl.pallas_call(
        functools.partial(_mm_ln_body, nk=nk),
        grid=(m // tm, nk),
        in_specs=[pl.BlockSpec((tm, tk), lambda i, kk: (i, kk)),
                  pl.BlockSpec((tk, n), lambda i, kk: (kk, 0)),
                  row_block,
                  pl.BlockSpec((1, n), lambda i, kk: (0, 0)),
                  pl.BlockSpec((1, n), lambda i, kk: (0, 0))],
        out_specs=out_specs,
        out_shape=out_shape,
        compiler_params=_params(("arbitrary", "arbitrary"), _vmem_limit(blocks)),
        name="matmul_residual_ln",
    )(a, w, resid, gain.reshape(1, n), bias.reshape(1, n))
    return outs if emit_bf16 else outs[0]


def _rel_bucket_np(dist):
    n = np.maximum(dist, 0)
    exact = N_BUCKETS // 2
    logv = (np.log(np.maximum(n, 1).astype(np.float32) / exact) / math.log(MAX_DISTANCE / exact))
    large = exact + (logv.astype(np.float32) * (N_BUCKETS - exact)).astype(np.int32)
    large = np.minimum(large, N_BUCKETS - 1)
    return np.where(n < exact, n, large).astype(np.int32)


def _bias_expand_body(tab_ref, bucket_ref, o_ref, *, head0, shift_last_bucket, scale):
    h = head0 + pl.program_id(0)
    last = tab_ref[N_BUCKETS - 1, h] if shift_last_bucket else 0.0
    for t in range(bucket_ref.shape[0]):
        bucket = bucket_ref[t]
        acc = jnp.full(bucket.shape, NEG, F32)
        for bkt in range(N_BUCKETS):
            acc = jnp.where(bucket == bkt, (tab_ref[bkt, h] - last) * scale, acc)
        o_ref[t] = acc


def _bias_expand(table, dists, valids, *, head0, heads, shift_last_bucket, scale):
    bucket = np.stack([np.where(v, _rel_bucket_np(d), -1) for d, v in zip(dists, valids)]).astype(np.int32)
    nt, r, c = bucket.shape
    blocks = [((nt, r, c), jnp.int32), ((nt, r, c), F32)]
    return pl.pallas_call(
        functools.partial(_bias_expand_body, head0=head0, shift_last_bucket=shift_last_bucket,
                          scale=scale),
        grid=(heads,),
        in_specs=[pl.BlockSpec(memory_space=pltpu.SMEM),
                  pl.BlockSpec((nt, r, c), lambda h: (0, 0, 0))],
        out_specs=pl.BlockSpec((None, nt, r, c), lambda h: (h, 0, 0, 0)),
        out_shape=jax.ShapeDtypeStruct((heads, nt, r, c), F32),
        compiler_params=_params(("arbitrary",), _vmem_limit(blocks)),
        name="bias_expand",
    )(table.astype(F32), jnp.asarray(bucket))


def _swa_body(sink_ref, q_ref, kp_ref, kc_ref, vp_ref, vc_ref, bias_ref, o_ref):
    n = pl.program_id(0)
    half = DH_A
    pairs = G_A // 2
    rows = pairs * BLK
    lane_k = lax.broadcasted_iota(jnp.int32, (2 * BLK, 2 * half), 1)
    row = lax.broadcasted_iota(jnp.int32, (rows, 1), 0)
    lane_o = lax.broadcasted_iota(jnp.int32, (BLK, 2 * half), 1)
    variant = jnp.minimum(n, 1)

    kf = jnp.concatenate([kp_ref[...], kc_ref[...]], axis=0).astype(F32)
    vf = jnp.concatenate([vp_ref[...], vc_ref[...]], axis=0).astype(F32)
    kr = pltpu.roll(kf, half, 1)
    vr = pltpu.roll(vf, half, 1)

    outs = []
    for g in range(KV_A):
        k_own, k_other = (kf, kr) if g == 0 else (kr, kf)
        k_lo = jnp.where(lane_k < half, k_own, 0.0).astype(BF16)
        k_hi = jnp.where(lane_k >= half, k_other, 0.0).astype(BF16)
        v_dup = (jnp.where(lane_k < half, vf, vr) if g == 0 else jnp.where(lane_k < half, vr, vf)).astype(BF16)
        q_stack = jnp.concatenate(
            [q_ref[:, (g * pairs + t) * 2 * half:(g * pairs + t + 1) * 2 * half] for t in range(pairs)], axis=0)
        q_stack = (q_stack.astype(F32) * (DH_A ** -0.5 * LOG2E)).astype(BF16)
        o_par = []
        for par, k_sel in ((0, k_lo), (1, k_hi)):
            sink = jnp.zeros((rows, 1), F32)
            for t in range(pairs):
                sink = jnp.where(row >= t * BLK, sink_ref[2 * (g * pairs + t) + par] * LOG2E, sink)
            s = lax.dot_general(q_stack, k_sel, (((1,), (1,)), ((), ())),
                                preferred_element_type=F32) + bias_ref[variant, g, par]
            mx = jnp.maximum(jnp.max(s, axis=1, keepdims=True), sink)
            p = jnp.exp2(s - mx)
            den = jnp.sum(p, axis=1, keepdims=True) + jnp.exp2(sink - mx)
            o_par.append(jnp.dot(p.astype(BF16), v_dup, preferred_element_type=F32) / den)
        for t in range(pairs):
            outs.append(jnp.where(lane_o < half, o_par[0][t * BLK:(t + 1) * BLK], o_par[1][t * BLK:(t + 1) * BLK]))
    o_ref[...] = jnp.concatenate(outs, axis=1).astype(o_ref.dtype)


def _swa_attention(proj, sinks, bias, *, q_col, k_col, v_col):
    s = proj.shape[0]
    nb = s // BLK
    qw = HA * DH_A
    kvw = KV_A * DH_A
    assert kvw == V7X_LANES and q_col % qw == 0 and k_col % kvw == 0 and v_col % kvw == 0
    kb, vb = k_col // kvw, v_col // kvw
    pairs = G_A // 2
    bias = bias.reshape(KV_A, pairs, 2, 2, BLK, 2 * BLK).transpose(3, 0, 2, 1, 4, 5)
    bias = bias.reshape(2, KV_A, 2, pairs * BLK, 2 * BLK)
    blocks = [((BLK, qw), BF16)] + [((BLK, kvw), BF16)] * 4 + [(bias.shape, F32), ((BLK, qw), BF16)]
    prev = lambda n: jnp.maximum(n - 1, 0)
    return pl.pallas_call(
        _swa_body,
        grid=(nb,),
        in_specs=[pl.BlockSpec(memory_space=pltpu.SMEM),
                  pl.BlockSpec((BLK, qw), lambda n: (n, q_col // qw)),
                  pl.BlockSpec((BLK, kvw), lambda n: (prev(n), kb)),
                  pl.BlockSpec((BLK, kvw), lambda n: (n, kb)),
                  pl.BlockSpec((BLK, kvw), lambda n: (prev(n), vb)),
                  pl.BlockSpec((BLK, kvw), lambda n: (n, vb)),
                  pl.BlockSpec(bias.shape, lambda n: (0, 0, 0, 0, 0))],
        out_specs=pl.BlockSpec((BLK, qw), lambda n: (n, 0)),
        out_shape=jax.ShapeDtypeStruct((s, qw), BF16),
        compiler_params=_params(("arbitrary",), _vmem_limit(blocks)),
        name="swa_sink_attention",
    )(sinks.astype(F32), proj, proj, proj, proj, proj, bias)


def _band_bias_body(tab_ref, bucket_ref, o_ref, *, head0, scale, tile_offsets):
    h = head0 + pl.program_id(0)
    last = tab_ref[N_BUCKETS - 1, h]
    bands = []
    for t in range(2):
        bucket = bucket_ref[t]
        acc = jnp.full(bucket.shape, NEG, F32)
        for bkt in range(N_BUCKETS):
            acc = jnp.where(bucket == bkt, (tab_ref[bkt, h] - last) * scale, acc)
        bands.append(acc)
    sub = MAX_DISTANCE
    nb = o_ref.shape[1] // sub
    for t, base in enumerate(tile_offsets):
        for rb in range(nb):
            for cb in range(nb):
                off = base + sub * (cb - rb)
                if off < 0:
                    blk = jnp.full((sub, sub), NEG, F32)
                elif off < 2 * sub:
                    blk = bands[off // sub]
                else:
                    blk = jnp.zeros((sub, sub), F32)
                o_ref[t, rb * sub:(rb + 1) * sub, cb * sub:(cb + 1) * sub] = blk


def _band_bias_tiles(table, *, head0, heads, tile, tile_offsets, scale):
    sub = MAX_DISTANCE
    assert tile % sub == 0 and all(o % sub == 0 for o in tile_offsets)
    r = np.arange(sub)[:, None]
    c = np.arange(sub)[None, :]
    bucket = np.stack([np.where(c - r + o >= 0, _rel_bucket_np(c - r + o), -1) for o in (0, sub)]).astype(np.int32)
    nt = len(tile_offsets)
    blocks = [((2, sub, sub), jnp.int32), ((nt, tile, tile), F32)]
    return pl.pallas_call(
        functools.partial(_band_bias_body, head0=head0, scale=scale, tile_offsets=tuple(tile_offsets)),
        grid=(heads,),
        in_specs=[pl.BlockSpec(memory_space=pltpu.SMEM),
                  pl.BlockSpec((2, sub, sub), lambda h: (0, 0, 0))],
        out_specs=pl.BlockSpec((None, nt, tile, tile), lambda h: (h, 0, 0, 0)),
        out_shape=jax.ShapeDtypeStruct((heads, nt, tile, tile), F32),
        compiler_params=_params(("arbitrary",), _vmem_limit(blocks)),
        name="band_bias_tiles",
    )(table.astype(F32), jnp.asarray(bucket))


HEADS_PER_STEP = 2


def _diff_body(q_ref, k_ref, v_ref, b2_ref, lq1_ref, lk1_ref, lq2_ref, lk2_ref, sw_ref,
               o_ref, qz_ref, sa_ref, sb_ref, cma_ref, cmb_ref, acc_ref, m_ref, l_ref, *, tq, tk):
    i = pl.program_id(1)
    width = 2 * DH_B
    heads = range(HEADS_PER_STEP)
    lanes = lambda hh: slice(hh * width, (hh + 1) * width)

    row = lax.broadcasted_iota(jnp.int32, (width, tq), 0)
    for hh in heads:
        qt = (q_ref[:, lanes(hh)].astype(F32) * (DH_B ** -0.5 * LOG2E)).T
        qz_ref[hh, 0] = jnp.where(row < DH_B, qt, 0.0).astype(BF16)
        qz_ref[hh, 1] = jnp.where(row >= DH_B, qt, 0.0).astype(BF16)
    acc_ref[...] = jnp.zeros_like(acc_ref)
    m_ref[...] = jnp.full_like(m_ref, NEG)
    l_ref[...] = jnp.zeros_like(l_ref)

    def scores(tile, bias_tile, s_ref, cm_ref):
        rows = pl.ds(pl.multiple_of(tile * tk, tk), tk)
        for hh in heads:
            kblk = k_ref[rows, lanes(hh)]
            for c in range(2):
                s = jnp.dot(kblk, qz_ref[hh, c], preferred_element_type=F32)
                if bias_tile is not None:
                    s = s + b2_ref[hh, bias_tile]
                s_ref[hh, c] = s
                cm_ref[hh, c] = jnp.max(s, axis=0, keepdims=True)

    def accumulate(tile, s_ref, cm_ref):
        rows = pl.ds(pl.multiple_of(tile * tk, tk), tk)
        for hh in heads:
            vblk = v_ref[rows, lanes(hh)]
            for c in range(2):
                m_old = m_ref[hh, c]
                m_new = jnp.maximum(m_old, cm_ref[hh, c])
                rescale = jnp.exp2(m_old - m_new)
                p = jnp.exp2(s_ref[hh, c] - m_new)
                l_ref[hh, c] = rescale * l_ref[hh, c] + jnp.sum(p, axis=0, keepdims=True)
                pv = lax.dot_general(vblk, p.astype(BF16), (((0,), (0,)), ((), ())),
                                     preferred_element_type=F32)
                acc_ref[hh, c] = rescale * acc_ref[hh, c] + pv
                m_ref[hh, c] = m_new

    n_items = i + 1
    tile_of = lambda item: jnp.maximum(i - item, 0)
    scores(i, 1, sa_ref, cma_ref)

    @pl.when(n_items >= 2)
    def _():
        scores(i - 1, 0, sb_ref, cmb_ref)
        accumulate(i, sa_ref, cma_ref)
        scores(tile_of(2), None, sa_ref, cma_ref)
        accumulate(i - 1, sb_ref, cmb_ref)

    def pair(t, carry):
        scores(i - (2 * t + 1), None, sb_ref, cmb_ref)
        accumulate(i - 2 * t, sa_ref, cma_ref)
        scores(tile_of(2 * t + 2), None, sa_ref, cma_ref)
        accumulate(i - (2 * t + 1), sb_ref, cmb_ref)
        return carry

    lax.fori_loop(1, n_items // 2, pair, 0)

    @pl.when(n_items % 2 == 1)
    def _():
        accumulate(0, sa_ref, cma_ref)

    lam = (jnp.exp(jnp.sum(lq1_ref[...] * lk1_ref[...], axis=1, keepdims=True))
           - jnp.exp(jnp.sum(lq2_ref[...] * lk2_ref[...], axis=1, keepdims=True)) + LAMBDA_INIT)
    for hh in heads:
        o = acc_ref[hh, 0] / l_ref[hh, 0] - lam * (acc_ref[hh, 1] / l_ref[hh, 1])
        ms = jnp.mean(o * o, axis=0, keepdims=True)
        o = o * lax.rsqrt(ms + LN_EPS) * sw_ref[...] * (1.0 - LAMBDA_INIT)
        o_ref[:, lanes(hh)] = o.T.astype(o_ref.dtype)


def _diff_attention(proj, bias2, lq1, lk1, lq2, lk2, subln_w, *, q_col, k_col, v_col, tq, tk):
    s = proj.shape[0]
    width = 2 * DH_B
    hps = HEADS_PER_STEP
    wide = hps * width
    assert width == V7X_LANES and tq == tk and HB % hps == 0
    assert q_col % wide == 0 and k_col % wide == 0 and v_col % wide == 0
    qb, kb, vb = q_col // wide, k_col // wide, v_col // wide
    blocks = [((tq, wide), BF16), ((s, wide), BF16), ((s, wide), BF16),
              ((hps, 2, tk, tq), F32), ((tq, wide), BF16)]
    scores_buf = ((hps, 2, tk, tq), F32)
    stats_buf = ((hps, 2, 1, tq), F32)
    scratch = [((hps, 2, width, tq), BF16), scores_buf, scores_buf, stats_buf, stats_buf,
               ((hps, 2, width, tq), F32), stats_buf, stats_buf]
    padded = [((hps, 2, 8, tq), F32) if sd == stats_buf else sd for sd in scratch]
    vec = lambda v: v.astype(F32).reshape(1, DH_B)
    small = pl.BlockSpec((1, DH_B), lambda h, i: (0, 0))
    return pl.pallas_call(
        functools.partial(_diff_body, tq=tq, tk=tk),
        grid=(HB // hps, s // tq),
        in_specs=[pl.BlockSpec((tq, wide), lambda h, i: (i, qb + h)),
                  pl.BlockSpec((s, wide), lambda h, i: (0, kb + h)),
                  pl.BlockSpec((s, wide), lambda h, i: (0, vb + h)),
                  pl.BlockSpec((hps, 2, tk, tq), lambda h, i: (h, 0, 0, 0)),
                  small, small, small, small,
                  pl.BlockSpec((width, 1), lambda h, i: (0, 0))],
        out_specs=pl.BlockSpec((tq, wide), lambda h, i: (i, h)),
        out_shape=jax.ShapeDtypeStruct((s, HB * width), BF16),
        scratch_shapes=[pltpu.VMEM(sh, dt) for sh, dt in scratch],
        compiler_params=_params(("arbitrary", "arbitrary"), _vmem_limit(blocks, padded)),
        name="diff_attention",
    )(proj, proj, proj, bias2, vec(lq1), vec(lk1), vec(lq2), vec(lk2),
      subln_w.astype(F32).reshape(width, 1))


def _cross_block_body(h_ref, wq_ref, kv_ref, wo_ref, g_ref, b_ref, o_ref, ob_ref, wq_bf_ref, wo_bf_ref):
    @pl.when(pl.program_id(0) == 0)
    def _():
        wq_bf_ref[...] = wq_ref[...].astype(BF16)
        wo_bf_ref[...] = wo_ref[...].astype(BF16)

    scale = DH_C ** -0.5
    for rows in _row_chunks(h_ref.shape[0], 1):
        h = h_ref[rows, :]
        q = jnp.dot(h.astype(BF16), wq_bf_ref[...], preferred_element_type=F32).astype(BF16)
        outs = []
        for hd in range(HC):
            qh = q[:, hd * DH_C:(hd + 1) * DH_C]
            kh = kv_ref[:, hd * DH_C:(hd + 1) * DH_C]
            vh = kv_ref[:, (HC + hd) * DH_C:(HC + hd + 1) * DH_C]
            s = lax.dot_general(qh, kh, (((1,), (1,)), ((), ())), preferred_element_type=F32) * scale
            mx = jnp.max(s, axis=1, keepdims=True)
            p = jnp.exp(s - mx)
            den = jnp.sum(p, axis=1, keepdims=True)
            outs.append((jnp.dot(p.astype(BF16), vh, preferred_element_type=F32) / den).astype(BF16))
        oc = jnp.concatenate(outs, axis=1)
        y = ALPHA * h + jnp.dot(oc, wo_bf_ref[...], preferred_element_type=F32)
        mu = jnp.mean(y, axis=-1, keepdims=True)
        yc = y - mu
        var = jnp.mean(yc * yc, axis=-1, keepdims=True)
        out = yc * lax.rsqrt(var + LN_EPS) * g_ref[...] + b_ref[...]
        o_ref[rows, :] = out
        ob_ref[rows, :] = out.astype(BF16)


def _cross_attention_block(h, kv, w_cq, w_co, gain, bias, *, tm):
    s, d = h.shape
    mlen, w = kv.shape[0], HC * DH_C
    row_block = pl.BlockSpec((tm, d), lambda i: (i, 0))
    whole = lambda shape: pl.BlockSpec(shape, lambda i: (0,) * len(shape))
    blocks = [((tm, d), F32), ((d, w), F32), ((mlen, 2 * w), BF16), ((w, d), F32), ((tm, d), F32), ((tm, d), BF16)]
    scratch = [((d, w), BF16), ((w, d), BF16)]
    return pl.pallas_call(
        _cross_block_body,
        grid=(s // tm,),
        in_specs=[row_block, whole((d, w)), whole((mlen, 2 * w)), whole((w, d)), whole((1, d)), whole((1, d))],
        out_specs=[row_block, row_block],
        out_shape=[jax.ShapeDtypeStruct((s, d), F32), jax.ShapeDtypeStruct((s, d), BF16)],
        scratch_shapes=[pltpu.VMEM(sh, dt) for sh, dt in scratch],
        compiler_params=_params(("arbitrary",), _vmem_limit(blocks, scratch)),
        name="memory_cross_attention_block",
    )(h, w_cq, kv, w_co, gain.reshape(1, d), bias.reshape(1, d))


def kernel(x, mem, rel_bias_table, w_in, sinks, lambda_q1, lambda_k1, lambda_q2, lambda_k2, subln_w,
           w_branch_a, w_branch_b, w_o, ln1_g, ln1_b, w_cq, w_mem_kv, w_co, ln2_g, ln2_b,
           w_gate_up, w_down, ln3_g, ln3_b):
    b, s, d = x.shape
    assert b == 1 and w_in.shape[0] == DEPTH == 1
    l = 0
    qa_w, kva_w, qb_w = HA * DH_A, KV_A * DH_A, HB * 2 * DH_B
    col_qa, col_ka, col_va = 0, qa_w, qa_w + kva_w
    col_qb = col_va + kva_w
    col_kb, col_vb = col_qb + qb_w, col_qb + 2 * qb_w
    col_ga = col_vb + qb_w
    col_gb = col_ga + d
    tq = tk = 512

    h0 = x.reshape(s, d)

    proj = _matmul(h0, w_in[l], tm=1024, tn=768, out_dtype=BF16, name="in_proj")

    i_a = np.arange(BLK)[:, None]
    j_a = np.arange(2 * BLK)[None, :]
    dist_a = BLK + i_a - j_a
    band_a = (dist_a >= 0) & (dist_a < WINDOW)
    bias_a = _bias_expand(rel_bias_table, [dist_a, dist_a], [band_a & (j_a >= BLK), band_a], head0=0, heads=HA,
                          shift_last_bucket=False, scale=LOG2E)
    o_a = _swa_attention(proj, sinks[l], bias_a, q_col=col_qa, k_col=col_ka, v_col=col_va)

    bias2 = _band_bias_tiles(rel_bias_table, head0=HA, heads=HB, tile=tq, tile_offsets=(tk, 0), scale=LOG2E)
    o_b = _diff_attention(proj, bias2, lambda_q1[l], lambda_k1[l], lambda_q2[l],
                          lambda_k2[l], subln_w[l], q_col=col_qb, k_col=col_kb, v_col=col_vb, tq=tq, tk=tk)

    mix = _gated_branches(o_a, o_b, w_branch_a[l], w_branch_b[l], proj, col_ga, col_gb, tm=2048, tn=256)
    h1 = _matmul_residual_ln(mix, w_o[l], h0, ln1_g[l], ln1_b[l], tm=1024, tk=512, emit_bf16=False)

    kvm = _matmul(mem.reshape(mem.shape[1], d), w_mem_kv[l], tm=mem.shape[1], tn=HC * DH_C,
                  out_dtype=BF16, name="mem_kv")
    h2, h2b = _cross_attention_block(h1, kvm, w_cq[l], w_co[l], ln2_g[l], ln2_b[l], tm=512)

    act = _swiglu_up(h2b, w_gate_up[l], tm=2048, tn=256)
    h3 = _matmul_residual_ln(act, w_down[l], h2, ln3_g[l], ln3_b[l], tm=1024, tk=512, emit_bf16=False)
    return h3.reshape(b, s, d)
```

```python
import functools
import math

import numpy as np
import jax
import jax.numpy as jnp
from jax import lax
from jax.experimental import pallas as pl
from jax.experimental.pallas import tpu as pltpu

F32 = jnp.float32
BF16 = jnp.bfloat16

BLK = 128
WINDOW = 128
HA, KV_A, DH_A = 16, 2, 64
G_A = HA // KV_A
HB, DH_B = 8, 64
N_BUCKETS, MAX_DISTANCE = 32, 128
HC, DH_C = 4, 128
LN_EPS = 1e-5
DEPTH = 1
ALPHA = (2 * DEPTH) ** 0.25
LAMBDA_INIT = 0.8 - 0.6 * math.exp(-0.3 * 0)
LOG2E = math.log2(math.e)

V7X_LANES = 128
V7X_VMEM_BYTES = 64 * 1024 * 1024
V7X_VMEM_TEMP_BYTES = 12 * 1024 * 1024

EPILOGUE_CHUNKS = 4

NEG = -1e30


def _nbytes(shape, dtype):
    return int(np.prod(shape)) * jnp.dtype(dtype).itemsize


def _vmem_limit(pipelined, scratch=()):
    need = 2 * sum(_nbytes(s, d) for s, d in pipelined) + sum(_nbytes(s, d) for s, d in scratch)
    need += V7X_VMEM_TEMP_BYTES
    assert need <= V7X_VMEM_BYTES - 4 * 1024 * 1024, need
    return need


def _params(semantics, vmem):
    return pltpu.CompilerParams(dimension_semantics=semantics, vmem_limit_bytes=vmem)


def _act_tile(a_ref, abf_ref):
    if abf_ref is None:
        return a_ref[...]

    @pl.when(pl.program_id(1) == 0)
    def _():
        abf_ref[...] = a_ref[...].astype(BF16)

    return abf_ref[...]


def _mm_body(a_ref, w_ref, o_ref, *scratch):
    a = _act_tile(a_ref, scratch[0] if scratch else None)
    o_ref[...] = jnp.dot(a, w_ref[...].astype(BF16), preferred_element_type=F32).astype(o_ref.dtype)


def _matmul(a, w, *, tm, tn, out_dtype, name):
    m, k = a.shape
    n = w.shape[1]
    cast = a.dtype != BF16
    scratch = [((tm, k), BF16)] if cast else []
    blocks = [((tm, k), a.dtype), ((k, tn), w.dtype), ((tm, tn), out_dtype)]
    return pl.pallas_call(
        _mm_body,
        grid=(m // tm, n // tn),
        in_specs=[pl.BlockSpec((tm, k), lambda i, j: (i, 0)),
                  pl.BlockSpec((k, tn), lambda i, j: (0, j))],
        out_specs=pl.BlockSpec((tm, tn), lambda i, j: (i, j)),
        out_shape=jax.ShapeDtypeStruct((m, n), out_dtype),
        scratch_shapes=[pltpu.VMEM(s, d) for s, d in scratch],
        compiler_params=_params(("arbitrary", "arbitrary"), _vmem_limit(blocks, scratch)),
        name=name,
    )(a, w)


def _row_chunks(rows, chunks):
    size = rows // chunks
    return [slice(c * size, (c + 1) * size) for c in range(chunks)]


def _branch_body(oa_ref, ob_ref, wa_ref, wb_ref, ga_ref, gb_ref, o_ref):
    wa = wa_ref[...].astype(BF16)
    wb = wb_ref[...].astype(BF16)
    for rows in _row_chunks(o_ref.shape[0], EPILOGUE_CHUNKS):
        ya = jnp.dot(oa_ref[rows, :], wa, preferred_element_type=F32)
        yb = jnp.dot(ob_ref[rows, :], wb, preferred_element_type=F32)
        ga = jax.nn.sigmoid(ga_ref[rows, :].astype(F32))
        gb = jax.nn.sigmoid(gb_ref[rows, :].astype(F32))
        o_ref[rows, :] = (ga * ya + gb * yb).astype(o_ref.dtype)


def _gated_branches(o_a, o_b, w_a, w_b, proj, ga_col, gb_col, *, tm, tn):
    m, ka = o_a.shape
    kb = o_b.shape[1]
    n = w_a.shape[1]
    ga_blk, gb_blk = ga_col // tn, gb_col // tn
    blocks = [((tm, ka), BF16), ((tm, kb), BF16), ((ka, tn), F32), ((kb, tn), F32),
              ((tm, tn), proj.dtype), ((tm, tn), proj.dtype), ((tm, tn), BF16)]
    return pl.pallas_call(
        _branch_body,
        grid=(m // tm, n // tn),
        in_specs=[pl.BlockSpec((tm, ka), lambda i, j: (i, 0)),
                  pl.BlockSpec((tm, kb), lambda i, j: (i, 0)),
                  pl.BlockSpec((ka, tn), lambda i, j: (0, j)),
                  pl.BlockSpec((kb, tn), lambda i, j: (0, j)),
                  pl.BlockSpec((tm, tn), lambda i, j: (i, ga_blk + j)),
                  pl.BlockSpec((tm, tn), lambda i, j: (i, gb_blk + j))],
        out_specs=pl.BlockSpec((tm, tn), lambda i, j: (i, j)),
        out_shape=jax.ShapeDtypeStruct((m, n), BF16),
        compiler_params=_params(("arbitrary", "arbitrary"), _vmem_limit(blocks)),
        name="gated_branches",
    )(o_a, o_b, w_a, w_b, proj, proj)


def _swiglu_body(a_ref, wg_ref, wu_ref, o_ref, *scratch):
    a = _act_tile(a_ref, scratch[0] if scratch else None)
    g = jnp.dot(a, wg_ref[...].astype(BF16), preferred_element_type=F32)
    u = jnp.dot(a, wu_ref[...].astype(BF16), preferred_element_type=F32)
    o_ref[...] = (g * jax.nn.sigmoid(g) * u).astype(o_ref.dtype)


def _swiglu_up(a, w_gate_up, *, tm, tn):
    m, k = a.shape
    d_ff = w_gate_up.shape[1] // 2
    up_blk = d_ff // tn
    scratch = [((tm, k), BF16)] if a.dtype != BF16 else []
    blocks = [((tm, k), a.dtype), ((k, tn), F32), ((k, tn), F32), ((tm, tn), BF16)]
    return pl.pallas_call(
        _swiglu_body,
        grid=(m // tm, d_ff // tn),
        in_specs=[pl.BlockSpec((tm, k), lambda i, j: (i, 0)),
                  pl.BlockSpec((k, tn), lambda i, j: (0, j)),
                  pl.BlockSpec((k, tn), lambda i, j: (0, up_blk + j))],
        out_specs=pl.BlockSpec((tm, tn), lambda i, j: (i, j)),
        out_shape=jax.ShapeDtypeStruct((m, d_ff), BF16),
        scratch_shapes=[pltpu.VMEM(s, d) for s, d in scratch],
        compiler_params=_params(("arbitrary", "arbitrary"), _vmem_limit(blocks, scratch)),
        name="swiglu_up",
    )(a, w_gate_up, w_gate_up)


def _layer_norm_rows(y, g_ref, b_ref):
    mu = jnp.mean(y, axis=-1, keepdims=True)
    yc = y - mu
    var = jnp.mean(yc * yc, axis=-1, keepdims=True)
    return yc * lax.rsqrt(var + LN_EPS) * g_ref[...] + b_ref[...]


def _resident_ln_body(a_ref, w_ref, r_ref, g_ref, b_ref, o_ref, wbf_ref):
    @pl.when(pl.program_id(0) == 0)
    def _():
        wbf_ref[...] = w_ref[...].astype(BF16)

    for rows in _row_chunks(o_ref.shape[0], 2):
        y = ALPHA * r_ref[rows, :] + jnp.dot(a_ref[rows, :], wbf_ref[...], preferred_element_type=F32)
        o_ref[rows, :] = _layer_norm_rows(y, g_ref, b_ref)


def _matmul_resident_ln(a, w, resid, gain, bias, *, tm):
    m, k = a.shape
    n = w.shape[1]
    row_block = pl.BlockSpec((tm, n), lambda i: (i, 0))
    blocks = [((tm, k), BF16), ((tm, n), F32), ((tm, n), F32)]
    single = [((k, n), F32), ((k, n), BF16)]
    return pl.pallas_call(
        _resident_ln_body,
        grid=(m // tm,),
        in_specs=[pl.BlockSpec((tm, k), lambda i: (i, 0)),
                  pl.BlockSpec((k, n), lambda i: (0, 0), pipeline_mode=pl.Buffered(1)),
                  row_block,
                  pl.BlockSpec((1, n), lambda i: (0, 0)),
                  pl.BlockSpec((1, n), lambda i: (0, 0))],
        out_specs=row_block,
        out_shape=jax.ShapeDtypeStruct((m, n), F32),
        scratch_shapes=[pltpu.VMEM((k, n), BF16)],
        compiler_params=_params(("arbitrary",), _vmem_limit(blocks, single)),
        name="matmul_resident_ln",
    )(a, w, resid, gain.reshape(1, n), bias.reshape(1, n))


def _mm_ln_body(a_ref, w_ref, r_ref, g_ref, b_ref, o_ref, *maybe_bf16_ref, nk):
    kk = pl.program_id(1)
    last = kk == nk - 1

    def partial_product(rows=slice(None)):
        return jnp.dot(a_ref[rows, :], w_ref[...].astype(BF16), preferred_element_type=F32)

    if nk > 1:
        @pl.when(kk == 0)
        def _():
            o_ref[...] = ALPHA * r_ref[...] + partial_product()

        @pl.when(jnp.logical_and(kk > 0, jnp.logical_not(last)))
        def _():
            o_ref[...] += partial_product()

    @pl.when(last)
    def _():
        for rows in _row_chunks(o_ref.shape[0], EPILOGUE_CHUNKS):
            seed = o_ref[rows, :] if nk > 1 else ALPHA * r_ref[rows, :]
            out = _layer_norm_rows(seed + partial_product(rows), g_ref, b_ref)
            o_ref[rows, :] = out
            for ob_ref in maybe_bf16_ref:
                ob_ref[rows, :] = out.astype(BF16)


def _matmul_residual_ln(a, w, resid, gain, bias, *, tm, tk, emit_bf16):
    m, k = a.shape
    n = w.shape[1]
    nk = k // tk
    row_block = pl.BlockSpec((tm, n), lambda i, kk: (i, 0))
    blocks = [((tm, tk), BF16), ((tk, n), w.dtype), ((tm, n), F32), ((tm, n), F32)]
    out_specs, out_shape = [row_block], [jax.ShapeDtypeStruct((m, n), F32)]
    if emit_bf16:
        blocks.append(((tm, n), BF16))
        out_specs.append(row_block)
        out_shape.append(jax.ShapeDtypeStruct((m, n), BF16))
    outs = pl.pallas_call(
        functools.partial(_mm_ln_body, nk=nk),
        grid=(m // tm, nk),
        in_specs=[pl.BlockSpec((tm, tk), lambda i, kk: (i, kk)),
                  pl.BlockSpec((tk, n), lambda i, kk: (kk, 0)),
                  row_block,
                  pl.BlockSpec((1, n), lambda i, kk: (0, 0)),
                  pl.BlockSpec((1, n), lambda i, kk: (0, 0))],
        out_specs=out_specs,
        out_shape=out_shape,
        compiler_params=_params(("arbitrary", "arbitrary"), _vmem_limit(blocks)),
        name="matmul_residual_ln",
    )(a, w, resid, gain.reshape(1, n), bias.reshape(1, n))
    return outs if emit_bf16 else outs[0]


def _rel_bucket_np(dist):
    n = np.maximum(dist, 0)
    exact = N_BUCKETS // 2
    logv = (np.log(np.maximum(n, 1).astype(np.float32) / exact) / math.log(MAX_DISTANCE / exact))
    large = exact + (logv.astype(np.float32) * (N_BUCKETS - exact)).astype(np.int32)
    large = np.minimum(large, N_BUCKETS - 1)
    return np.where(n < exact, n, large).astype(np.int32)


def _bias_expand_body(tab_ref, bucket_ref, o_ref, *, head0, shift_last_bucket, scale):
    h = head0 + pl.program_id(0)
    last = tab_ref[N_BUCKETS - 1, h] if shift_last_bucket else 0.0
    for t in range(bucket_ref.shape[0]):
        bucket = bucket_ref[t]
        acc = jnp.full(bucket.shape, NEG, F32)
        for bkt in range(N_BUCKETS):
            acc = jnp.where(bucket == bkt, (tab_ref[bkt, h] - last) * scale, acc)
        o_ref[t] = acc


def _bias_expand(table, dists, valids, *, head0, heads, shift_last_bucket, scale):
    bucket = np.stack([np.where(v, _rel_bucket_np(d), -1) for d, v in zip(dists, valids)]).astype(np.int32)
    nt, r, c = bucket.shape
    blocks = [((nt, r, c), jnp.int32), ((nt, r, c), F32)]
    return pl.pallas_call(
        functools.partial(_bias_expand_body, head0=head0, shift_last_bucket=shift_last_bucket,
                          scale=scale),
        grid=(heads,),
        in_specs=[pl.BlockSpec(memory_space=pltpu.SMEM),
                  pl.BlockSpec((nt, r, c), lambda h: (0, 0, 0))],
        out_specs=pl.BlockSpec((None, nt, r, c), lambda h: (h, 0, 0, 0)),
        out_shape=jax.ShapeDtypeStruct((heads, nt, r, c), F32),
        compiler_params=_params(("arbitrary",), _vmem_limit(blocks)),
        name="bias_expand",
    )(table.astype(F32), jnp.asarray(bucket))


def _swa_body(sink_ref, q_ref, kp_ref, kc_ref, vp_ref, vc_ref, bias_ref, o_ref):
    n = pl.program_id(0)
    half = DH_A
    pairs = G_A // 2
    rows = pairs * BLK
    lane_k = lax.broadcasted_iota(jnp.int32, (2 * BLK, 2 * half), 1)
    row = lax.broadcasted_iota(jnp.int32, (rows, 1), 0)
    lane_o = lax.broadcasted_iota(jnp.int32, (BLK, 2 * half), 1)
    variant = jnp.minimum(n, 1)

    kf = jnp.concatenate([kp_ref[...], kc_ref[...]], axis=0).astype(F32)
    vf = jnp.concatenate([vp_ref[...], vc_ref[...]], axis=0).astype(F32)
    kr = pltpu.roll(kf, half, 1)
    vr = pltpu.roll(vf, half, 1)

    outs = []
    for g in range(KV_A):
        k_own, k_other = (kf, kr) if g == 0 else (kr, kf)
        k_lo = jnp.where(lane_k < half, k_own, 0.0).astype(BF16)
        k_hi = jnp.where(lane_k >= half, k_other, 0.0).astype(BF16)
        v_dup = (jnp.where(lane_k < half, vf, vr) if g == 0 else jnp.where(lane_k < half, vr, vf)).astype(BF16)
        q_stack = jnp.concatenate(
            [q_ref[:, (g * pairs + t) * 2 * half:(g * pairs + t + 1) * 2 * half] for t in range(pairs)], axis=0)
        q_stack = (q_stack.astype(F32) * (DH_A ** -0.5 * LOG2E)).astype(BF16)
        o_par = []
        for par, k_sel in ((0, k_lo), (1, k_hi)):
            sink = jnp.zeros((rows, 1), F32)
            for t in range(pairs):
                sink = jnp.where(row >= t * BLK, sink_ref[2 * (g * pairs + t) + par] * LOG2E, sink)
            s = lax.dot_general(q_stack, k_sel, (((1,), (1,)), ((), ())),
                                preferred_element_type=F32) + bias_ref[variant, g, par]
            mx = jnp.maximum(jnp.max(s, axis=1, keepdims=True), sink)
            p = jnp.exp2(s - mx)
            den = jnp.sum(p, axis=1, keepdims=True) + jnp.exp2(sink - mx)
            o_par.append(jnp.dot(p.astype(BF16), v_dup, preferred_element_type=F32) / den)
        for t in range(pairs):
            outs.append(jnp.where(lane_o < half, o_par[0][t * BLK:(t + 1) * BLK], o_par[1][t * BLK:(t + 1) * BLK]))
    o_ref[...] = jnp.concatenate(outs, axis=1).astype(o_ref.dtype)


def _swa_attention(proj, sinks, bias, *, q_col, k_col, v_col):
    s = proj.shape[0]
    nb = s // BLK
    qw = HA * DH_A
    kvw = KV_A * DH_A
    assert kvw == V7X_LANES and q_col % qw == 0 and k_col % kvw == 0 and v_col % kvw == 0
    kb, vb = k_col // kvw, v_col // kvw
    pairs = G_A // 2
    bias = bias.reshape(KV_A, pairs, 2, 2, BLK, 2 * BLK).transpose(3, 0, 2, 1, 4, 5)
    bias = bias.reshape(2, KV_A, 2, pairs * BLK, 2 * BLK)
    blocks = [((BLK, qw), BF16)] + [((BLK, kvw), BF16)] * 4 + [(bias.shape, F32), ((BLK, qw), BF16)]
    prev = lambda n: jnp.maximum(n - 1, 0)
    return pl.pallas_call(
        _swa_body,
        grid=(nb,),
        in_specs=[pl.BlockSpec(memory_space=pltpu.SMEM),
                  pl.BlockSpec((BLK, qw), lambda n: (n, q_col // qw)),
                  pl.BlockSpec((BLK, kvw), lambda n: (prev(n), kb)),
                  pl.BlockSpec((BLK, kvw), lambda n: (n, kb)),
                  pl.BlockSpec((BLK, kvw), lambda n: (prev(n), vb)),
                  pl.BlockSpec((BLK, kvw), lambda n: (n, vb)),
                  pl.BlockSpec(bias.shape, lambda n: (0, 0, 0, 0, 0))],
        out_specs=pl.BlockSpec((BLK, qw), lambda n: (n, 0)),
        out_shape=jax.ShapeDtypeStruct((s, qw), BF16),
        compiler_params=_params(("arbitrary",), _vmem_limit(blocks)),
        name="swa_sink_attention",
    )(sinks.astype(F32), proj, proj, proj, proj, proj, bias)


def _band_bias_body(tab_ref, bucket_ref, o_ref, *, head0, scale, tile_offsets):
    h = head0 + pl.program_id(0)
    last = tab_ref[N_BUCKETS - 1, h]
    bands = []
    for t in range(2):
        bucket = bucket_ref[t]
        acc = jnp.full(bucket.shape, NEG, F32)
        for bkt in range(N_BUCKETS):
            acc = jnp.where(bucket == bkt, (tab_ref[bkt, h] - last) * scale, acc)
        bands.append(acc)
    sub = MAX_DISTANCE
    nb = o_ref.shape[1] // sub
    for t, base in enumerate(tile_offsets):
        for rb in range(nb):
            for cb in range(nb):
                off = base + sub * (cb - rb)
                if off < 0:
                    blk = jnp.full((sub, sub), NEG, F32)
                elif off < 2 * sub:
                    blk = bands[off // sub]
                else:
                    blk = jnp.zeros((sub, sub), F32)
                o_ref[t, rb * sub:(rb + 1) * sub, cb * sub:(cb + 1) * sub] = blk


def _band_bias_tiles(table, *, head0, heads, tile, tile_offsets, scale):
    sub = MAX_DISTANCE
    assert tile % sub == 0 and all(o % sub == 0 for o in tile_offsets)
    r = np.arange(sub)[:, None]
    c = np.arange(sub)[None, :]
    bucket = np.stack([np.where(c - r + o >= 0, _rel_bucket_np(c - r + o), -1) for o in (0, sub)]).astype(np.int32)
    nt = len(tile_offsets)
    blocks = [((2, sub, sub), jnp.int32), ((nt, tile, tile), F32)]
    return pl.pallas_call(
        functools.partial(_band_bias_body, head0=head0, scale=scale, tile_offsets=tuple(tile_offsets)),
        grid=(heads,),
        in_specs=[pl.BlockSpec(memory_space=pltpu.SMEM),
                  pl.BlockSpec((2, sub, sub), lambda h: (0, 0, 0))],
        out_specs=pl.BlockSpec((None, nt, tile, tile), lambda h: (h, 0, 0, 0)),
        out_shape=jax.ShapeDtypeStruct((heads, nt, tile, tile), F32),
        compiler_params=_params(("arbitrary",), _vmem_limit(blocks)),
        name="band_bias_tiles",
    )(table.astype(F32), jnp.asarray(bucket))


HEADS_PER_STEP = 2


def _diff_body(q_ref, k_ref, v_ref, b2_ref, lq1_ref, lk1_ref, lq2_ref, lk2_ref, sw_ref,
               o_ref, qz_ref, sa_ref, sb_ref, cma_ref, cmb_ref, acc_ref, m_ref, l_ref, *, tq, tk):
    i = pl.program_id(1)
    width = 2 * DH_B
    heads = range(HEADS_PER_STEP)
    lanes = lambda hh: slice(hh * width, (hh + 1) * width)

    row = lax.broadcasted_iota(jnp.int32, (width, tq), 0)
    for hh in heads:
        qt = (q_ref[:, lanes(hh)].astype(F32) * (DH_B ** -0.5 * LOG2E)).T
        qz_ref[hh, 0] = jnp.where(row < DH_B, qt, 0.0).astype(BF16)
        qz_ref[hh, 1] = jnp.where(row >= DH_B, qt, 0.0).astype(BF16)
    acc_ref[...] = jnp.zeros_like(acc_ref)
    m_ref[...] = jnp.full_like(m_ref, NEG)
    l_ref[...] = jnp.zeros_like(l_ref)

    def scores(tile, bias_tile, s_ref, cm_ref):
        rows = pl.ds(pl.multiple_of(tile * tk, tk), tk)
        for hh in heads:
            kblk = k_ref[rows, lanes(hh)]
            for c in range(2):
                s = jnp.dot(kblk, qz_ref[hh, c], preferred_element_type=F32)
                if bias_tile is not None:
                    s = s + b2_ref[hh, bias_tile]
                s_ref[hh, c] = s
                cm_ref[hh, c] = jnp.max(s, axis=0, keepdims=True)

    def accumulate(tile, s_ref, cm_ref):
        rows = pl.ds(pl.multiple_of(tile * tk, tk), tk)
        for hh in heads:
            vblk = v_ref[rows, lanes(hh)]
            for c in range(2):
                m_old = m_ref[hh, c]
                m_new = jnp.maximum(m_old, cm_ref[hh, c])
                rescale = jnp.exp2(m_old - m_new)
                p = jnp.exp2(s_ref[hh, c] - m_new)
                l_ref[hh, c] = rescale * l_ref[hh, c] + jnp.sum(p, axis=0, keepdims=True)
                pv = lax.dot_general(vblk, p.astype(BF16), (((0,), (0,)), ((), ())),
                                     preferred_element_type=F32)
                acc_ref[hh, c] = rescale * acc_ref[hh, c] + pv
                m_ref[hh, c] = m_new

    n_items = i + 1
    tile_of = lambda item: jnp.maximum(i - item, 0)
    scores(i, 1, sa_ref, cma_ref)

    @pl.when(n_items >= 2)
    def _():
        scores(i - 1, 0, sb_ref, cmb_ref)
        accumulate(i, sa_ref, cma_ref)
        scores(tile_of(2), None, sa_ref, cma_ref)
        accumulate(i - 1, sb_ref, cmb_ref)

    def pair(t, carry):
        scores(i - (2 * t + 1), None, sb_ref, cmb_ref)
        accumulate(i - 2 * t, sa_ref, cma_ref)
        scores(tile_of(2 * t + 2), None, sa_ref, cma_ref)
        accumulate(i - (2 * t + 1), sb_ref, cmb_ref)
        return carry

    lax.fori_loop(1, n_items // 2, pair, 0)

    @pl.when(n_items % 2 == 1)
    def _():
        accumulate(0, sa_ref, cma_ref)

    lam = (jnp.exp(jnp.sum(lq1_ref[...] * lk1_ref[...], axis=1, keepdims=True))
           - jnp.exp(jnp.sum(lq2_ref[...] * lk2_ref[...], axis=1, keepdims=True)) + LAMBDA_INIT)
    for hh in heads:
        o = acc_ref[hh, 0] / l_ref[hh, 0] - lam * (acc_ref[hh, 1] / l_ref[hh, 1])
        ms = jnp.mean(o * o, axis=0, keepdims=True)
        o = o * lax.rsqrt(ms + LN_EPS) * sw_ref[...] * (1.0 - LAMBDA_INIT)
        o_ref[:, lanes(hh)] = o.T.astype(o_ref.dtype)


def _diff_attention(proj, bias2, lq1, lk1, lq2, lk2, subln_w, *, q_col, k_col, v_col, tq, tk):
    s = proj.shape[0]
    width = 2 * DH_B
    hps = HEADS_PER_STEP
    wide = hps * width
    assert width == V7X_LANES and tq == tk and HB % hps == 0
    assert q_col % wide == 0 and k_col % wide == 0 and v_col % wide == 0
    qb, kb, vb = q_col // wide, k_col // wide, v_col // wide
    blocks = [((tq, wide), BF16), ((s, wide), BF16), ((s, wide), BF16),
              ((hps, 2, tk, tq), F32), ((tq, wide), BF16)]
    scores_buf = ((hps, 2, tk, tq), F32)
    stats_buf = ((hps, 2, 1, tq), F32)
    scratch = [((hps, 2, width, tq), BF16), scores_buf, scores_buf, stats_buf, stats_buf,
               ((hps, 2, width, tq), F32), stats_buf, stats_buf]
    padded = [((hps, 2, 8, tq), F32) if sd == stats_buf else sd for sd in scratch]
    vec = lambda v: v.astype(F32).reshape(1, DH_B)
    small = pl.BlockSpec((1, DH_B), lambda h, i: (0, 0))
    return pl.pallas_call(
        functools.partial(_diff_body, tq=tq, tk=tk),
        grid=(HB // hps, s // tq),
        in_specs=[pl.BlockSpec((tq, wide), lambda h, i: (i, qb + h)),
                  pl.BlockSpec((s, wide), lambda h, i: (0, kb + h)),
                  pl.BlockSpec((s, wide), lambda h, i: (0, vb + h)),
                  pl.BlockSpec((hps, 2, tk, tq), lambda h, i: (h, 0, 0, 0)),
                  small, small, small, small,
                  pl.BlockSpec((width, 1), lambda h, i: (0, 0))],
        out_specs=pl.BlockSpec((tq, wide), lambda h, i: (i, h)),
        out_shape=jax.ShapeDtypeStruct((s, HB * width), BF16),
        scratch_shapes=[pltpu.VMEM(sh, dt) for sh, dt in scratch],
        compiler_params=_params(("arbitrary", "arbitrary"), _vmem_limit(blocks, padded)),
        name="diff_attention",
    )(proj, proj, proj, bias2, vec(lq1), vec(lk1), vec(lq2), vec(lk2),
      subln_w.astype(F32).reshape(width, 1))


def _cross_block_body(h_ref, wq_ref, kv_ref, wo_ref, g_ref, b_ref, o_ref, ob_ref, wq_bf_ref, wo_bf_ref):
    @pl.when(pl.program_id(0) == 0)
    def _():
        wq_bf_ref[...] = wq_ref[...].astype(BF16)
        wo_bf_ref[...] = wo_ref[...].astype(BF16)

    scale = DH_C ** -0.5
    h = h_ref[...]
    q = jnp.dot(h.astype(BF16), wq_bf_ref[...], preferred_element_type=F32).astype(BF16)
    outs = []
    for hd in range(HC):
        qh = q[:, hd * DH_C:(hd + 1) * DH_C]
        kh = kv_ref[:, hd * DH_C:(hd + 1) * DH_C]
        vh = kv_ref[:, (HC + hd) * DH_C:(HC + hd + 1) * DH_C]
        s = lax.dot_general(qh, kh, (((1,), (1,)), ((), ())), preferred_element_type=F32) * scale
        mx = jnp.max(s, axis=1, keepdims=True)
        p = jnp.exp(s - mx)
        den = jnp.sum(p, axis=1, keepdims=True)
        outs.append((jnp.dot(p.astype(BF16), vh, preferred_element_type=F32) / den).astype(BF16))
    oc = jnp.concatenate(outs, axis=1)
    y = ALPHA * h + jnp.dot(oc, wo_bf_ref[...], preferred_element_type=F32)
    out = _layer_norm_rows(y, g_ref, b_ref)
    o_ref[...] = out
    ob_ref[...] = out.astype(BF16)


def _cross_attention_block(h, kv, w_cq, w_co, gain, bias, *, tm):
    s, d = h.shape
    mlen, w = kv.shape[0], HC * DH_C
    row_block = pl.BlockSpec((tm, d), lambda i: (i, 0))
    whole = lambda shape: pl.BlockSpec(shape, lambda i: (0,) * len(shape))
    blocks = [((tm, d), F32), ((d, w), F32), ((mlen, 2 * w), BF16), ((w, d), F32), ((tm, d), F32), ((tm, d), BF16)]
    scratch = [((d, w), BF16), ((w, d), BF16)]
    return pl.pallas_call(
        _cross_block_body,
        grid=(s // tm,),
        in_specs=[row_block, whole((d, w)), whole((mlen, 2 * w)), whole((w, d)), whole((1, d)), whole((1, d))],
        out_specs=[row_block, row_block],
        out_shape=[jax.ShapeDtypeStruct((s, d), F32), jax.ShapeDtypeStruct((s, d), BF16)],
        scratch_shapes=[pltpu.VMEM(sh, dt) for sh, dt in scratch],
        compiler_params=_params(("arbitrary",), _vmem_limit(blocks, scratch)),
        name="memory_cross_attention_block",
    )(h, w_cq, kv, w_co, gain.reshape(1, d), bias.reshape(1, d))


def kernel(x, mem, rel_bias_table, w_in, sinks, lambda_q1, lambda_k1, lambda_q2, lambda_k2, subln_w,
           w_branch_a, w_branch_b, w_o, ln1_g, ln1_b, w_cq, w_mem_kv, w_co, ln2_g, ln2_b,
           w_gate_up, w_down, ln3_g, ln3_b):
    b, s, d = x.shape
    assert b == 1 and w_in.shape[0] == DEPTH == 1
    (w_in, sinks, lambda_q1, lambda_k1, lambda_q2, lambda_k2, subln_w, w_branch_a, w_branch_b, w_o, ln1_g, ln1_b,
     w_cq, w_mem_kv, w_co, ln2_g, ln2_b, w_gate_up, w_down, ln3_g, ln3_b) = [
        p.reshape(p.shape[1:]) for p in (
            w_in, sinks, lambda_q1, lambda_k1, lambda_q2, lambda_k2, subln_w, w_branch_a, w_branch_b, w_o, ln1_g,
            ln1_b, w_cq, w_mem_kv, w_co, ln2_g, ln2_b, w_gate_up, w_down, ln3_g, ln3_b)]
    qa_w, kva_w, qb_w = HA * DH_A, KV_A * DH_A, HB * 2 * DH_B
    col_qa, col_ka, col_va = 0, qa_w, qa_w + kva_w
    col_qb = col_va + kva_w
    col_kb, col_vb = col_qb + qb_w, col_qb + 2 * qb_w
    col_ga = col_vb + qb_w
    col_gb = col_ga + d
    tq = tk = 512

    h0 = x.reshape(s, d)

    proj = _matmul(h0, w_in, tm=1024, tn=768, out_dtype=BF16, name="in_proj")

    i_a = np.arange(BLK)[:, None]
    j_a = np.arange(2 * BLK)[None, :]
    dist_a = BLK + i_a - j_a
    band_a = (dist_a >= 0) & (dist_a < WINDOW)
    bias_a = _bias_expand(rel_bias_table, [dist_a, dist_a], [band_a & (j_a >= BLK), band_a], head0=0, heads=HA,
                          shift_last_bucket=False, scale=LOG2E)
    o_a = _swa_attention(proj, sinks, bias_a, q_col=col_qa, k_col=col_ka, v_col=col_va)

    bias2 = _band_bias_tiles(rel_bias_table, head0=HA, heads=HB, tile=tq, tile_offsets=(tk, 0), scale=LOG2E)
    o_b = _diff_attention(proj, bias2, lambda_q1, lambda_k1, lambda_q2,
                          lambda_k2, subln_w, q_col=col_qb, k_col=col_kb, v_col=col_vb, tq=tq, tk=tk)

    mix = _gated_branches(o_a, o_b, w_branch_a, w_branch_b, proj, col_ga, col_gb, tm=2048, tn=256)
    h1 = _matmul_resident_ln(mix, w_o, h0, ln1_g, ln1_b, tm=512)

    kvm = _matmul(mem.reshape(mem.shape[1], d), w_mem_kv, tm=mem.shape[1], tn=HC * DH_C,
                  out_dtype=BF16, name="mem_kv")
    h2, h2b = _cross_attention_block(h1, kvm, w_cq, w_co, ln2_g, ln2_b, tm=512)

    act = _swiglu_up(h2b, w_gate_up, tm=2048, tn=256)
    h3 = _matmul_residual_ln(act, w_down, h2, ln3_g, ln3_b, tm=1024, tk=512, emit_bf16=False)
    return h3.reshape(b, s, d)
```

```python
import functools
import math

import numpy as np
import jax
import jax.numpy as jnp
from jax import lax
from jax.experimental import pallas as pl
from jax.experimental.pallas import tpu as pltpu

F32 = jnp.float32
BF16 = jnp.bfloat16

BLK = 128
WINDOW = 128
HA, KV_A, DH_A = 16, 2, 64
G_A = HA // KV_A
HB, DH_B = 8, 64
N_BUCKETS, MAX_DISTANCE = 32, 128
HC, DH_C = 4, 128
LN_EPS = 1e-5
DEPTH = 1
ALPHA = (2 * DEPTH) ** 0.25
LAMBDA_INIT = 0.8 - 0.6 * math.exp(-0.3 * 0)
LOG2E = math.log2(math.e)

V7X_LANES = 128
V7X_VMEM_BYTES = 64 * 1024 * 1024
V7X_VMEM_TEMP_BYTES = 12 * 1024 * 1024

EPILOGUE_CHUNKS = 4

NEG = -1e30


def _nbytes(shape, dtype):
    return int(np.prod(shape)) * jnp.dtype(dtype).itemsize


def _vmem_limit(pipelined, scratch=()):
    need = 2 * sum(_nbytes(s, d) for s, d in pipelined) + sum(_nbytes(s, d) for s, d in scratch)
    need += V7X_VMEM_TEMP_BYTES
    assert need <= V7X_VMEM_BYTES - 4 * 1024 * 1024, need
    return need


def _params(semantics, vmem):
    return pltpu.CompilerParams(dimension_semantics=semantics, vmem_limit_bytes=vmem)


def _act_tile(a_ref, abf_ref):
    if abf_ref is None:
        return a_ref[...]

    @pl.when(pl.program_id(1) == 0)
    def _():
        abf_ref[...] = a_ref[...].astype(BF16)

    return abf_ref[...]


def _mm_body(a_ref, w_ref, o_ref, *scratch):
    a = _act_tile(a_ref, scratch[0] if scratch else None)
    o_ref[...] = jnp.dot(a, w_ref[...].astype(BF16), preferred_element_type=F32).astype(o_ref.dtype)


def _matmul(a, w, *, tm, tn, out_dtype, name):
    m, k = a.shape
    n = w.shape[1]
    cast = a.dtype != BF16
    scratch = [((tm, k), BF16)] if cast else []
    blocks = [((tm, k), a.dtype), ((k, tn), w.dtype), ((tm, tn), out_dtype)]
    return pl.pallas_call(
        _mm_body,
        grid=(m // tm, n // tn),
        in_specs=[pl.BlockSpec((tm, k), lambda i, j: (i, 0)),
                  pl.BlockSpec((k, tn), lambda i, j: (0, j))],
        out_specs=pl.BlockSpec((tm, tn), lambda i, j: (i, j)),
        out_shape=jax.ShapeDtypeStruct((m, n), out_dtype),
        scratch_shapes=[pltpu.VMEM(s, d) for s, d in scratch],
        compiler_params=_params(("arbitrary", "arbitrary"), _vmem_limit(blocks, scratch)),
        name=name,
    )(a, w)


def _row_chunks(rows, chunks):
    size = rows // chunks
    return [slice(c * size, (c + 1) * size) for c in range(chunks)]


def _branch_body(oa_ref, ob_ref, wa_ref, wb_ref, ga_ref, gb_ref, o_ref):
    wa = wa_ref[...].astype(BF16)
    wb = wb_ref[...].astype(BF16)
    for rows in _row_chunks(o_ref.shape[0], EPILOGUE_CHUNKS):
        ya = jnp.dot(oa_ref[rows, :], wa, preferred_element_type=F32)
        yb = jnp.dot(ob_ref[rows, :], wb, preferred_element_type=F32)
        ga = jax.nn.sigmoid(ga_ref[rows, :].astype(F32))
        gb = jax.nn.sigmoid(gb_ref[rows, :].astype(F32))
        o_ref[rows, :] = (ga * ya + gb * yb).astype(o_ref.dtype)


def _gated_branches(o_a, o_b, w_a, w_b, proj, ga_col, gb_col, *, tm, tn):
    m, ka = o_a.shape
    kb = o_b.shape[1]
    n = w_a.shape[1]
    ga_blk, gb_blk = ga_col // tn, gb_col // tn
    blocks = [((tm, ka), BF16), ((tm, kb), BF16), ((ka, tn), F32), ((kb, tn), F32),
              ((tm, tn), proj.dtype), ((tm, tn), proj.dtype), ((tm, tn), BF16)]
    return pl.pallas_call(
        _branch_body,
        grid=(m // tm, n // tn),
        in_specs=[pl.BlockSpec((tm, ka), lambda i, j: (i, 0)),
                  pl.BlockSpec((tm, kb), lambda i, j: (i, 0)),
                  pl.BlockSpec((ka, tn), lambda i, j: (0, j)),
                  pl.BlockSpec((kb, tn), lambda i, j: (0, j)),
                  pl.BlockSpec((tm, tn), lambda i, j: (i, ga_blk + j)),
                  pl.BlockSpec((tm, tn), lambda i, j: (i, gb_blk + j))],
        out_specs=pl.BlockSpec((tm, tn), lambda i, j: (i, j)),
        out_shape=jax.ShapeDtypeStruct((m, n), BF16),
        compiler_params=_params(("arbitrary", "arbitrary"), _vmem_limit(blocks)),
        name="gated_branches",
    )(o_a, o_b, w_a, w_b, proj, proj)


def _swiglu_body(a_ref, wg_ref, wu_ref, o_ref, *scratch):
    a = _act_tile(a_ref, scratch[0] if scratch else None)
    wg = wg_ref[...].astype(BF16)
    wu = wu_ref[...].astype(BF16)
    for rows in _row_chunks(o_ref.shape[0], 2):
        g = jnp.dot(a[rows, :], wg, preferred_element_type=F32)
        u = jnp.dot(a[rows, :], wu, preferred_element_type=F32)
        o_ref[rows, :] = (g * jax.nn.sigmoid(g) * u).astype(o_ref.dtype)


def _swiglu_up(a, w_gate_up, *, tm, tn):
    m, k = a.shape
    d_ff = w_gate_up.shape[1] // 2
    up_blk = d_ff // tn
    scratch = [((tm, k), BF16)] if a.dtype != BF16 else []
    blocks = [((tm, k), a.dtype), ((k, tn), F32), ((k, tn), F32), ((tm, tn), BF16)]
    return pl.pallas_call(
        _swiglu_body,
        grid=(m // tm, d_ff // tn),
        in_specs=[pl.BlockSpec((tm, k), lambda i, j: (i, 0)),
                  pl.BlockSpec((k, tn), lambda i, j: (0, j)),
                  pl.BlockSpec((k, tn), lambda i, j: (0, up_blk + j))],
        out_specs=pl.BlockSpec((tm, tn), lambda i, j: (i, j)),
        out_shape=jax.ShapeDtypeStruct((m, d_ff), BF16),
        scratch_shapes=[pltpu.VMEM(s, d) for s, d in scratch],
        compiler_params=_params(("arbitrary", "arbitrary"), _vmem_limit(blocks, scratch)),
        name="swiglu_up",
    )(a, w_gate_up, w_gate_up)


def _layer_norm_rows(y, g_ref, b_ref):
    mu = jnp.mean(y, axis=-1, keepdims=True)
    yc = y - mu
    var = jnp.mean(yc * yc, axis=-1, keepdims=True)
    return yc * lax.rsqrt(var + LN_EPS) * g_ref[...] + b_ref[...]


def _resident_ln_body(a_ref, w_ref, r_ref, g_ref, b_ref, o_ref, wbf_ref):
    @pl.when(pl.program_id(0) == 0)
    def _():
        wbf_ref[...] = w_ref[...].astype(BF16)

    for rows in _row_chunks(o_ref.shape[0], 2):
        y = ALPHA * r_ref[rows, :] + jnp.dot(a_ref[rows, :], wbf_ref[...], preferred_element_type=F32)
        o_ref[rows, :] = _layer_norm_rows(y, g_ref, b_ref)


def _matmul_resident_ln(a, w, resid, gain, bias, *, tm):
    m, k = a.shape
    n = w.shape[1]
    row_block = pl.BlockSpec((tm, n), lambda i: (i, 0))
    blocks = [((tm, k), BF16), ((tm, n), F32), ((tm, n), F32)]
    single = [((k, n), F32), ((k, n), BF16)]
    return pl.pallas_call(
        _resident_ln_body,
        grid=(m // tm,),
        in_specs=[pl.BlockSpec((tm, k), lambda i: (i, 0)),
                  pl.BlockSpec((k, n), lambda i: (0, 0), pipeline_mode=pl.Buffered(1)),
                  row_block,
                  pl.BlockSpec((1, n), lambda i: (0, 0)),
                  pl.BlockSpec((1, n), lambda i: (0, 0))],
        out_specs=row_block,
        out_shape=jax.ShapeDtypeStruct((m, n), F32),
        scratch_shapes=[pltpu.VMEM((k, n), BF16)],
        compiler_params=_params(("arbitrary",), _vmem_limit(blocks, single)),
        name="matmul_resident_ln",
    )(a, w, resid, gain.reshape(1, n), bias.reshape(1, n))


def _mm_ln_body(a_ref, w_ref, r_ref, g_ref, b_ref, o_ref, *maybe_bf16_ref, nk):
    kk = pl.program_id(1)
    last = kk == nk - 1

    def partial_product(rows=slice(None)):
        return jnp.dot(a_ref[rows, :], w_ref[...].astype(BF16), preferred_element_type=F32)

    if nk > 1:
        @pl.when(kk == 0)
        def _():
            o_ref[...] = ALPHA * r_ref[...] + partial_product()

        @pl.when(jnp.logical_and(kk > 0, jnp.logical_not(last)))
        def _():
            o_ref[...] += partial_product()

    @pl.when(last)
    def _():
        for rows in _row_chunks(o_ref.shape[0], EPILOGUE_CHUNKS):
            seed = o_ref[rows, :] if nk > 1 else ALPHA * r_ref[rows, :]
            out = _layer_norm_rows(seed + partial_product(rows), g_ref, b_ref)
            o_ref[rows, :] = out
            for ob_ref in maybe_bf16_ref:
                ob_ref[rows, :] = out.astype(BF16)


def _matmul_residual_ln(a, w, resid, gain, bias, *, tm, tk, emit_bf16):
    m, k = a.shape
    n = w.shape[1]
    nk = k // tk
    row_block = pl.BlockSpec((tm, n), lambda i, kk: (i, 0))
    blocks = [((tm, tk), BF16), ((tk, n), w.dtype), ((tm, n), F32), ((tm, n), F32)]
    out_specs, out_shape = [row_block], [jax.ShapeDtypeStruct((m, n), F32)]
    if emit_bf16:
        blocks.append(((tm, n), BF16))
        out_specs.append(row_block)
        out_shape.append(jax.ShapeDtypeStruct((m, n), BF16))
    outs = pl.pallas_call(
        functools.partial(_mm_ln_body, nk=nk),
        grid=(m // tm, nk),
        in_specs=[pl.BlockSpec((tm, tk), lambda i, kk: (i, kk)),
                  pl.BlockSpec((tk, n), lambda i, kk: (kk, 0)),
                  row_block,
                  pl.BlockSpec((1, n), lambda i, kk: (0, 0)),
                  pl.BlockSpec((1, n), lambda i, kk: (0, 0))],
        out_specs=out_specs,
        out_shape=out_shape,
        compiler_params=_params(("arbitrary", "arbitrary"), _vmem_limit(blocks)),
        name="matmul_residual_ln",
    )(a, w, resid, gain.reshape(1, n), bias.reshape(1, n))
    return outs if emit_bf16 else outs[0]


def _rel_bucket_np(dist):
    n = np.maximum(dist, 0)
    exact = N_BUCKETS // 2
    logv = (np.log(np.maximum(n, 1).astype(np.float32) / exact) / math.log(MAX_DISTANCE / exact))
    large = exact + (logv.astype(np.float32) * (N_BUCKETS - exact)).astype(np.int32)
    large = np.minimum(large, N_BUCKETS - 1)
    return np.where(n < exact, n, large).astype(np.int32)


def _bias_expand_body(tab_ref, bucket_ref, o_ref, *, head0, shift_last_bucket, scale):
    h = head0 + pl.program_id(0)
    last = tab_ref[N_BUCKETS - 1, h] if shift_last_bucket else 0.0
    for t in range(bucket_ref.shape[0]):
        bucket = bucket_ref[t]
        acc = jnp.full(bucket.shape, NEG, F32)
        for bkt in range(N_BUCKETS):
            acc = jnp.where(bucket == bkt, (tab_ref[bkt, h] - last) * scale, acc)
        o_ref[t] = acc


def _bias_expand(table, dists, valids, *, head0, heads, shift_last_bucket, scale):
    bucket = np.stack([np.where(v, _rel_bucket_np(d), -1) for d, v in zip(dists, valids)]).astype(np.int32)
    nt, r, c = bucket.shape
    blocks = [((nt, r, c), jnp.int32), ((nt, r, c), F32)]
    return pl.pallas_call(
        functools.partial(_bias_expand_body, head0=head0, shift_last_bucket=shift_last_bucket,
                          scale=scale),
        grid=(heads,),
        in_specs=[pl.BlockSpec(memory_space=pltpu.SMEM),
                  pl.BlockSpec((nt, r, c), lambda h: (0, 0, 0))],
        out_specs=pl.BlockSpec((None, nt, r, c), lambda h: (h, 0, 0, 0)),
        out_shape=jax.ShapeDtypeStruct((heads, nt, r, c), F32),
        compiler_params=_params(("arbitrary",), _vmem_limit(blocks)),
        name="bias_expand",
    )(table.astype(F32), jnp.asarray(bucket))


def _swa_body(sink_ref, q_ref, kp_ref, kc_ref, vp_ref, vc_ref, bias_ref, o_ref):
    n = pl.program_id(0)
    half = DH_A
    pairs = G_A // 2
    rows = pairs * BLK
    lane_k = lax.broadcasted_iota(jnp.int32, (2 * BLK, 2 * half), 1)
    row = lax.broadcasted_iota(jnp.int32, (rows, 1), 0)
    lane_o = lax.broadcasted_iota(jnp.int32, (BLK, 2 * half), 1)
    variant = jnp.minimum(n, 1)

    kf = jnp.concatenate([kp_ref[...], kc_ref[...]], axis=0).astype(F32)
    vf = jnp.concatenate([vp_ref[...], vc_ref[...]], axis=0).astype(F32)
    kr = pltpu.roll(kf, half, 1)
    vr = pltpu.roll(vf, half, 1)

    outs = []
    for g in range(KV_A):
        k_own, k_other = (kf, kr) if g == 0 else (kr, kf)
        k_lo = jnp.where(lane_k < half, k_own, 0.0).astype(BF16)
        k_hi = jnp.where(lane_k >= half, k_other, 0.0).astype(BF16)
        v_dup = (jnp.where(lane_k < half, vf, vr) if g == 0 else jnp.where(lane_k < half, vr, vf)).astype(BF16)
        q_stack = jnp.concatenate(
            [q_ref[:, (g * pairs + t) * 2 * half:(g * pairs + t + 1) * 2 * half] for t in range(pairs)], axis=0)
        q_stack = (q_stack.astype(F32) * (DH_A ** -0.5 * LOG2E)).astype(BF16)
        o_par = []
        for par, k_sel in ((0, k_lo), (1, k_hi)):
            sink = jnp.zeros((rows, 1), F32)
            for t in range(pairs):
                sink = jnp.where(row >= t * BLK, sink_ref[2 * (g * pairs + t) + par] * LOG2E, sink)
            s = lax.dot_general(q_stack, k_sel, (((1,), (1,)), ((), ())),
                                preferred_element_type=F32) + bias_ref[variant, g, par]
            mx = jnp.maximum(jnp.max(s, axis=1, keepdims=True), sink)
            p = jnp.exp2(s - mx)
            den = jnp.sum(p, axis=1, keepdims=True) + jnp.exp2(sink - mx)
            o_par.append(jnp.dot(p.astype(BF16), v_dup, preferred_element_type=F32) / den)
        for t in range(pairs):
            outs.append(jnp.where(lane_o < half, o_par[0][t * BLK:(t + 1) * BLK], o_par[1][t * BLK:(t + 1) * BLK]))
    o_ref[...] = jnp.concatenate(outs, axis=1).astype(o_ref.dtype)


def _swa_attention(proj, sinks, bias, *, q_col, k_col, v_col):
    s = proj.shape[0]
    nb = s // BLK
    qw = HA * DH_A
    kvw = KV_A * DH_A
    assert kvw == V7X_LANES and q_col % qw == 0 and k_col % kvw == 0 and v_col % kvw == 0
    kb, vb = k_col // kvw, v_col // kvw
    pairs = G_A // 2
    bias = bias.reshape(KV_A, pairs, 2, 2, BLK, 2 * BLK).transpose(3, 0, 2, 1, 4, 5)
    bias = bias.reshape(2, KV_A, 2, pairs * BLK, 2 * BLK)
    blocks = [((BLK, qw), BF16)] + [((BLK, kvw), BF16)] * 4 + [(bias.shape, F32), ((BLK, qw), BF16)]
    prev = lambda n: jnp.maximum(n - 1, 0)
    return pl.pallas_call(
        _swa_body,
        grid=(nb,),
        in_specs=[pl.BlockSpec(memory_space=pltpu.SMEM),
                  pl.BlockSpec((BLK, qw), lambda n: (n, q_col // qw)),
                  pl.BlockSpec((BLK, kvw), lambda n: (prev(n), kb)),
                  pl.BlockSpec((BLK, kvw), lambda n: (n, kb)),
                  pl.BlockSpec((BLK, kvw), lambda n: (prev(n), vb)),
                  pl.BlockSpec((BLK, kvw), lambda n: (n, vb)),
                  pl.BlockSpec(bias.shape, lambda n: (0, 0, 0, 0, 0))],
        out_specs=pl.BlockSpec((BLK, qw), lambda n: (n, 0)),
        out_shape=jax.ShapeDtypeStruct((s, qw), BF16),
        compiler_params=_params(("arbitrary",), _vmem_limit(blocks)),
        name="swa_sink_attention",
    )(sinks.astype(F32), proj, proj, proj, proj, proj, bias)


def _band_bias_body(tab_ref, bucket_ref, o_ref, *, head0, scale, tile_offsets):
    h = head0 + pl.program_id(0)
    last = tab_ref[N_BUCKETS - 1, h]
    bands = []
    for t in range(2):
        bucket = bucket_ref[t]
        acc = jnp.full(bucket.shape, NEG, F32)
        for bkt in range(N_BUCKETS):
            acc = jnp.where(bucket == bkt, (tab_ref[bkt, h] - last) * scale, acc)
        bands.append(acc)
    sub = MAX_DISTANCE
    nb = o_ref.shape[1] // sub
    for t, base in enumerate(tile_offsets):
        for rb in range(nb):
            for cb in range(nb):
                off = base + sub * (cb - rb)
                if off < 0:
                    blk = jnp.full((sub, sub), NEG, F32)
                elif off < 2 * sub:
                    blk = bands[off // sub]
                else:
                    blk = jnp.zeros((sub, sub), F32)
                o_ref[t, rb * sub:(rb + 1) * sub, cb * sub:(cb + 1) * sub] = blk


def _band_bias_tiles(table, *, head0, heads, tile, tile_offsets, scale):
    sub = MAX_DISTANCE
    assert tile % sub == 0 and all(o % sub == 0 for o in tile_offsets)
    r = np.arange(sub)[:, None]
    c = np.arange(sub)[None, :]
    bucket = np.stack([np.where(c - r + o >= 0, _rel_bucket_np(c - r + o), -1) for o in (0, sub)]).astype(np.int32)
    nt = len(tile_offsets)
    blocks = [((2, sub, sub), jnp.int32), ((nt, tile, tile), F32)]
    return pl.pallas_call(
        functools.partial(_band_bias_body, head0=head0, scale=scale, tile_offsets=tuple(tile_offsets)),
        grid=(heads,),
        in_specs=[pl.BlockSpec(memory_space=pltpu.SMEM),
                  pl.BlockSpec((2, sub, sub), lambda h: (0, 0, 0))],
        out_specs=pl.BlockSpec((None, nt, tile, tile), lambda h: (h, 0, 0, 0)),
        out_shape=jax.ShapeDtypeStruct((heads, nt, tile, tile), F32),
        compiler_params=_params(("arbitrary",), _vmem_limit(blocks)),
        name="band_bias_tiles",
    )(table.astype(F32), jnp.asarray(bucket))


HEADS_PER_STEP = 2


def _diff_body(q_ref, k_ref, v_ref, b2_ref, lq1_ref, lk1_ref, lq2_ref, lk2_ref, sw_ref,
               o_ref, qz_ref, sa_ref, sb_ref, cma_ref, cmb_ref, acc_ref, m_ref, l_ref, *, tq, tk):
    i = pl.program_id(1)
    width = 2 * DH_B
    heads = range(HEADS_PER_STEP)
    lanes = lambda hh: slice(hh * width, (hh + 1) * width)

    row = lax.broadcasted_iota(jnp.int32, (width, tq), 0)
    for hh in heads:
        qt = (q_ref[:, lanes(hh)].astype(F32) * (DH_B ** -0.5 * LOG2E)).T
        qz_ref[hh, 0] = jnp.where(row < DH_B, qt, 0.0).astype(BF16)
        qz_ref[hh, 1] = jnp.where(row >= DH_B, qt, 0.0).astype(BF16)
    acc_ref[...] = jnp.zeros_like(acc_ref)
    m_ref[...] = jnp.full_like(m_ref, NEG)
    l_ref[...] = jnp.zeros_like(l_ref)

    def score_chain(hh, c, tile, bias_tile, s_ref, cm_ref):
        kblk = k_ref[pl.ds(pl.multiple_of(tile * tk, tk), tk), lanes(hh)]
        s = jnp.dot(kblk, qz_ref[hh, c], preferred_element_type=F32)
        if bias_tile is not None:
            s = s + b2_ref[hh, bias_tile]
        s_ref[hh, c] = s
        cm_ref[hh, c] = jnp.max(s, axis=0, keepdims=True)

    def accumulate_chain(hh, c, tile, s_ref, cm_ref):
        vblk = v_ref[pl.ds(pl.multiple_of(tile * tk, tk), tk), lanes(hh)]
        m_old = m_ref[hh, c]
        m_new = jnp.maximum(m_old, cm_ref[hh, c])
        rescale = jnp.exp2(m_old - m_new)
        p = jnp.exp2(s_ref[hh, c] - m_new)
        l_ref[hh, c] = rescale * l_ref[hh, c] + jnp.sum(p, axis=0, keepdims=True)
        pv = lax.dot_general(vblk, p.astype(BF16), (((0,), (0,)), ((), ())), preferred_element_type=F32)
        acc_ref[hh, c] = rescale * acc_ref[hh, c] + pv
        m_ref[hh, c] = m_new

    def stage(score_args=None, acc_args=None):
        for hh in heads:
            for c in range(2):
                if score_args is not None:
                    score_chain(hh, c, *score_args)
                if acc_args is not None:
                    accumulate_chain(hh, c, *acc_args)

    n_items = i + 1
    tile_of = lambda item: jnp.maximum(i - item, 0)
    slot_a, slot_b = (sa_ref, cma_ref), (sb_ref, cmb_ref)
    stage(score_args=(i, 1, *slot_a))

    @pl.when(n_items >= 2)
    def _():
        stage((i - 1, 0, *slot_b), (i, *slot_a))
        stage((tile_of(2), None, *slot_a), (i - 1, *slot_b))

    def pair(t, carry):
        stage((i - (2 * t + 1), None, *slot_b), (i - 2 * t, *slot_a))
        stage((tile_of(2 * t + 2), None, *slot_a), (i - (2 * t + 1), *slot_b))
        return carry

    lax.fori_loop(1, n_items // 2, pair, 0)

    @pl.when(n_items % 2 == 1)
    def _():
        stage(acc_args=(0, *slot_a))

    lam = (jnp.exp(jnp.sum(lq1_ref[...] * lk1_ref[...], axis=1, keepdims=True))
           - jnp.exp(jnp.sum(lq2_ref[...] * lk2_ref[...], axis=1, keepdims=True)) + LAMBDA_INIT)
    for hh in heads:
        o = acc_ref[hh, 0] / l_ref[hh, 0] - lam * (acc_ref[hh, 1] / l_ref[hh, 1])
        ms = jnp.mean(o * o, axis=0, keepdims=True)
        o = o * lax.rsqrt(ms + LN_EPS) * sw_ref[...] * (1.0 - LAMBDA_INIT)
        o_ref[:, lanes(hh)] = o.T.astype(o_ref.dtype)


def _diff_attention(proj, bias2, lq1, lk1, lq2, lk2, subln_w, *, q_col, k_col, v_col, tq, tk):
    s = proj.shape[0]
    width = 2 * DH_B
    hps = HEADS_PER_STEP
    wide = hps * width
    assert width == V7X_LANES and tq == tk and HB % hps == 0
    assert q_col % wide == 0 and k_col % wide == 0 and v_col % wide == 0
    qb, kb, vb = q_col // wide, k_col // wide, v_col // wide
    blocks = [((tq, wide), BF16), ((s, wide), BF16), ((s, wide), BF16),
              ((hps, 2, tk, tq), F32), ((tq, wide), BF16)]
    scores_buf = ((hps, 2, tk, tq), F32)
    stats_buf = ((hps, 2, 1, tq), F32)
    scratch = [((hps, 2, width, tq), BF16), scores_buf, scores_buf, stats_buf, stats_buf,
               ((hps, 2, width, tq), F32), stats_buf, stats_buf]
    padded = [((hps, 2, 8, tq), F32) if sd == stats_buf else sd for sd in scratch]
    vec = lambda v: v.astype(F32).reshape(1, DH_B)
    small = pl.BlockSpec((1, DH_B), lambda h, i: (0, 0))
    return pl.pallas_call(
        functools.partial(_diff_body, tq=tq, tk=tk),
        grid=(HB // hps, s // tq),
        in_specs=[pl.BlockSpec((tq, wide), lambda h, i: (i, qb + h)),
                  pl.BlockSpec((s, wide), lambda h, i: (0, kb + h)),
                  pl.BlockSpec((s, wide), lambda h, i: (0, vb + h)),
                  pl.BlockSpec((hps, 2, tk, tq), lambda h, i: (h, 0, 0, 0)),
                  small, small, small, small,
                  pl.BlockSpec((width, 1), lambda h, i: (0, 0))],
        out_specs=pl.BlockSpec((tq, wide), lambda h, i: (i, h)),
        out_shape=jax.ShapeDtypeStruct((s, HB * width), BF16),
        scratch_shapes=[pltpu.VMEM(sh, dt) for sh, dt in scratch],
        compiler_params=_params(("arbitrary", "arbitrary"), _vmem_limit(blocks, padded)),
        name="diff_attention",
    )(proj, proj, proj, bias2, vec(lq1), vec(lk1), vec(lq2), vec(lk2),
      subln_w.astype(F32).reshape(width, 1))


def _cross_block_body(h_ref, wq_ref, kv_ref, wo_ref, g_ref, b_ref, o_ref, ob_ref, wq_bf_ref, wo_bf_ref):
    @pl.when(pl.program_id(0) == 0)
    def _():
        wq_bf_ref[...] = wq_ref[...].astype(BF16)
        wo_bf_ref[...] = wo_ref[...].astype(BF16)

    scale = DH_C ** -0.5
    h = h_ref[...]
    q = jnp.dot(h.astype(BF16), wq_bf_ref[...], preferred_element_type=F32).astype(BF16)
    outs = []
    for hd in range(HC):
        qh = q[:, hd * DH_C:(hd + 1) * DH_C]
        kh = kv_ref[:, hd * DH_C:(hd + 1) * DH_C]
        vh = kv_ref[:, (HC + hd) * DH_C:(HC + hd + 1) * DH_C]
        s = lax.dot_general(qh, kh, (((1,), (1,)), ((), ())), preferred_element_type=F32) * scale
        mx = jnp.max(s, axis=1, keepdims=True)
        p = jnp.exp(s - mx)
        den = jnp.sum(p, axis=1, keepdims=True)
        outs.append((jnp.dot(p.astype(BF16), vh, preferred_element_type=F32) / den).astype(BF16))
    oc = jnp.concatenate(outs, axis=1)
    y = ALPHA * h + jnp.dot(oc, wo_bf_ref[...], preferred_element_type=F32)
    out = _layer_norm_rows(y, g_ref, b_ref)
    o_ref[...] = out
    ob_ref[...] = out.astype(BF16)


def _cross_attention_block(h, kv, w_cq, w_co, gain, bias, *, tm):
    s, d = h.shape
    mlen, w = kv.shape[0], HC * DH_C
    row_block = pl.BlockSpec((tm, d), lambda i: (i, 0))
    whole = lambda shape: pl.BlockSpec(shape, lambda i: (0,) * len(shape))
    blocks = [((tm, d), F32), ((d, w), F32), ((mlen, 2 * w), BF16), ((w, d), F32), ((tm, d), F32), ((tm, d), BF16)]
    scratch = [((d, w), BF16), ((w, d), BF16)]
    return pl.pallas_call(
        _cross_block_body,
        grid=(s // tm,),
        in_specs=[row_block, whole((d, w)), whole((mlen, 2 * w)), whole((w, d)), whole((1, d)), whole((1, d))],
        out_specs=[row_block, row_block],
        out_shape=[jax.ShapeDtypeStruct((s, d), F32), jax.ShapeDtypeStruct((s, d), BF16)],
        scratch_shapes=[pltpu.VMEM(sh, dt) for sh, dt in scratch],
        compiler_params=_params(("arbitrary",), _vmem_limit(blocks, scratch)),
        name="memory_cross_attention_block",
    )(h, w_cq, kv, w_co, gain.reshape(1, d), bias.reshape(1, d))


def kernel(x, mem, rel_bias_table, w_in, sinks, lambda_q1, lambda_k1, lambda_q2, lambda_k2, subln_w,
           w_branch_a, w_branch_b, w_o, ln1_g, ln1_b, w_cq, w_mem_kv, w_co, ln2_g, ln2_b,
           w_gate_up, w_down, ln3_g, ln3_b):
    b, s, d = x.shape
    assert b == 1 and w_in.shape[0] == DEPTH == 1
    (w_in, sinks, lambda_q1, lambda_k1, lambda_q2, lambda_k2, subln_w, w_branch_a, w_branch_b, w_o, ln1_g, ln1_b,
     w_cq, w_mem_kv, w_co, ln2_g, ln2_b, w_gate_up, w_down, ln3_g, ln3_b) = [
        p.reshape(p.shape[1:]) for p in (
            w_in, sinks, lambda_q1, lambda_k1, lambda_q2, lambda_k2, subln_w, w_branch_a, w_branch_b, w_o, ln1_g,
            ln1_b, w_cq, w_mem_kv, w_co, ln2_g, ln2_b, w_gate_up, w_down, ln3_g, ln3_b)]
    qa_w, kva_w, qb_w = HA * DH_A, KV_A * DH_A, HB * 2 * DH_B
    col_qa, col_ka, col_va = 0, qa_w, qa_w + kva_w
    col_qb = col_va + kva_w
    col_kb, col_vb = col_qb + qb_w, col_qb + 2 * qb_w
    col_ga = col_vb + qb_w
    col_gb = col_ga + d
    tq = tk = 512

    h0 = x.reshape(s, d)

    proj = _matmul(h0, w_in, tm=1024, tn=768, out_dtype=BF16, name="in_proj")

    i_a = np.arange(BLK)[:, None]
    j_a = np.arange(2 * BLK)[None, :]
    dist_a = BLK + i_a - j_a
    band_a = (dist_a >= 0) & (dist_a < WINDOW)
    bias_a = _bias_expand(rel_bias_table, [dist_a, dist_a], [band_a & (j_a >= BLK), band_a], head0=0, heads=HA,
                          shift_last_bucket=False, scale=LOG2E)
    o_a = _swa_attention(proj, sinks, bias_a, q_col=col_qa, k_col=col_ka, v_col=col_va)

    bias2 = _band_bias_tiles(rel_bias_table, head0=HA, heads=HB, tile=tq, tile_offsets=(tk, 0), scale=LOG2E)
    o_b = _diff_attention(proj, bias2, lambda_q1, lambda_k1, lambda_q2,
                          lambda_k2, subln_w, q_col=col_qb, k_col=col_kb, v_col=col_vb, tq=tq, tk=tk)

    mix = _gated_branches(o_a, o_b, w_branch_a, w_branch_b, proj, col_ga, col_gb, tm=2048, tn=256)
    h1 = _matmul_resident_ln(mix, w_o, h0, ln1_g, ln1_b, tm=512)

    kvm = _matmul(mem.reshape(mem.shape[1], d), w_mem_kv, tm=mem.shape[1], tn=HC * DH_C,
                  out_dtype=BF16, name="mem_kv")
    h2, h2b = _cross_attention_block(h1, kvm, w_cq, w_co, ln2_g, ln2_b, tm=512)

    act = _swiglu_up(h2b, w_gate_up, tm=2048, tn=256)
    h3 = _matmul_residual_ln(act, w_down, h2, ln3_g, ln3_b, tm=1024, tk=512, emit_bf16=False)
    return h3.reshape(b, s, d)
```

```python
import functools
import math

import numpy as np
import jax
import jax.numpy as jnp
from jax import lax
from jax.experimental import pallas as pl
from jax.experimental.pallas import tpu as pltpu

F32 = jnp.float32
BF16 = jnp.bfloat16

BLK = 128
WINDOW = 128
HA, KV_A, DH_A = 16, 2, 64
G_A = HA // KV_A
HB, DH_B = 8, 64
N_BUCKETS, MAX_DISTANCE = 32, 128
HC, DH_C = 4, 128
LN_EPS = 1e-5
DEPTH = 1
ALPHA = (2 * DEPTH) ** 0.25
LAMBDA_INIT = 0.8 - 0.6 * math.exp(-0.3 * 0)
LOG2E = math.log2(math.e)

V7X_LANES = 128
V7X_VMEM_BYTES = 64 * 1024 * 1024
V7X_VMEM_TEMP_BYTES = 12 * 1024 * 1024

EPILOGUE_CHUNKS = 4

NEG = -1e30


def _nbytes(shape, dtype):
    return int(np.prod(shape)) * jnp.dtype(dtype).itemsize


def _vmem_limit(pipelined, scratch=()):
    need = 2 * sum(_nbytes(s, d) for s, d in pipelined) + sum(_nbytes(s, d) for s, d in scratch)
    need += V7X_VMEM_TEMP_BYTES
    assert need <= V7X_VMEM_BYTES - 4 * 1024 * 1024, need
    return need


def _params(semantics, vmem):
    return pltpu.CompilerParams(dimension_semantics=semantics, vmem_limit_bytes=vmem)


def _act_tile(a_ref, abf_ref):
    if abf_ref is None:
        return a_ref[...]

    @pl.when(pl.program_id(1) == 0)
    def _():
        abf_ref[...] = a_ref[...].astype(BF16)

    return abf_ref[...]


def _mm_body(a_ref, w_ref, o_ref, *scratch):
    a = _act_tile(a_ref, scratch[0] if scratch else None)
    o_ref[...] = jnp.dot(a, w_ref[...].astype(BF16), preferred_element_type=F32).astype(o_ref.dtype)


def _matmul(a, w, *, tm, tn, out_dtype, name):
    m, k = a.shape
    n = w.shape[1]
    cast = a.dtype != BF16
    scratch = [((tm, k), BF16)] if cast else []
    blocks = [((tm, k), a.dtype), ((k, tn), w.dtype), ((tm, tn), out_dtype)]
    return pl.pallas_call(
        _mm_body,
        grid=(m // tm, n // tn),
        in_specs=[pl.BlockSpec((tm, k), lambda i, j: (i, 0)),
                  pl.BlockSpec((k, tn), lambda i, j: (0, j))],
        out_specs=pl.BlockSpec((tm, tn), lambda i, j: (i, j)),
        out_shape=jax.ShapeDtypeStruct((m, n), out_dtype),
        scratch_shapes=[pltpu.VMEM(s, d) for s, d in scratch],
        compiler_params=_params(("arbitrary", "arbitrary"), _vmem_limit(blocks, scratch)),
        name=name,
    )(a, w)


def _row_chunks(rows, chunks):
    size = rows // chunks
    return [slice(c * size, (c + 1) * size) for c in range(chunks)]


def _branch_body(oa_ref, ob_ref, wa_ref, wb_ref, ga_ref, gb_ref, o_ref):
    wa = wa_ref[...].astype(BF16)
    wb = wb_ref[...].astype(BF16)
    for rows in _row_chunks(o_ref.shape[0], EPILOGUE_CHUNKS):
        ya = jnp.dot(oa_ref[rows, :], wa, preferred_element_type=F32)
        yb = jnp.dot(ob_ref[rows, :], wb, preferred_element_type=F32)
        ga = jax.nn.sigmoid(ga_ref[rows, :].astype(F32))
        gb = jax.nn.sigmoid(gb_ref[rows, :].astype(F32))
        o_ref[rows, :] = (ga * ya + gb * yb).astype(o_ref.dtype)


def _gated_branches(o_a, o_b, w_a, w_b, proj, ga_col, gb_col, *, tm, tn):
    m, ka = o_a.shape
    kb = o_b.shape[1]
    n = w_a.shape[1]
    ga_blk, gb_blk = ga_col // tn, gb_col // tn
    blocks = [((tm, ka), BF16), ((tm, kb), BF16), ((ka, tn), F32), ((kb, tn), F32),
              ((tm, tn), proj.dtype), ((tm, tn), proj.dtype), ((tm, tn), BF16)]
    return pl.pallas_call(
        _branch_body,
        grid=(m // tm, n // tn),
        in_specs=[pl.BlockSpec((tm, ka), lambda i, j: (i, 0)),
                  pl.BlockSpec((tm, kb), lambda i, j: (i, 0)),
                  pl.BlockSpec((ka, tn), lambda i, j: (0, j)),
                  pl.BlockSpec((kb, tn), lambda i, j: (0, j)),
                  pl.BlockSpec((tm, tn), lambda i, j: (i, ga_blk + j)),
                  pl.BlockSpec((tm, tn), lambda i, j: (i, gb_blk + j))],
        out_specs=pl.BlockSpec((tm, tn), lambda i, j: (i, j)),
        out_shape=jax.ShapeDtypeStruct((m, n), BF16),
        compiler_params=_params(("arbitrary", "arbitrary"), _vmem_limit(blocks)),
        name="gated_branches",
    )(o_a, o_b, w_a, w_b, proj, proj)


def _swiglu_body(a_ref, wg_ref, wu_ref, wd_ref, o_ref, wd_bf_ref):
    a = a_ref[...]
    wg = wg_ref[...].astype(BF16)
    wu = wu_ref[...].astype(BF16)
    for rows in _row_chunks(o_ref.shape[0], 2):
        g = jnp.dot(a[rows, :], wg, preferred_element_type=F32)
        u = jnp.dot(a[rows, :], wu, preferred_element_type=F32)
        o_ref[rows, :] = (g * jax.nn.sigmoid(g) * u).astype(o_ref.dtype)
    wd_bf_ref[...] = wd_ref[...].astype(BF16)


def _swiglu_up(a, w_gate_up, w_down, *, tm, tn):
    m, k = a.shape
    d_ff = w_gate_up.shape[1] // 2
    row_tiles, col_tiles = m // tm, d_ff // tn
    up_blk = d_ff // tn
    kd, nd = w_down.shape
    slab = kd // (row_tiles * col_tiles)
    assert slab * row_tiles * col_tiles == kd and slab % 16 == 0 and a.dtype == BF16
    slab_block = pl.BlockSpec((slab, nd), lambda i, j: (i * col_tiles + j, 0))
    blocks = [((tm, k), BF16), ((k, tn), F32), ((k, tn), F32), ((tm, tn), BF16), ((slab, nd), F32), ((slab, nd), BF16)]
    return pl.pallas_call(
        _swiglu_body,
        grid=(row_tiles, col_tiles),
        in_specs=[pl.BlockSpec((tm, k), lambda i, j: (i, 0)),
                  pl.BlockSpec((k, tn), lambda i, j: (0, j)),
                  pl.BlockSpec((k, tn), lambda i, j: (0, up_blk + j)),
                  slab_block],
        out_specs=[pl.BlockSpec((tm, tn), lambda i, j: (i, j)), slab_block],
        out_shape=[jax.ShapeDtypeStruct((m, d_ff), BF16), jax.ShapeDtypeStruct((kd, nd), BF16)],
        compiler_params=_params(("arbitrary", "arbitrary"), _vmem_limit(blocks)),
        name="swiglu_up",
    )(a, w_gate_up, w_gate_up, w_down)


def _layer_norm_rows(y, g_ref, b_ref):
    mu = jnp.mean(y, axis=-1, keepdims=True)
    yc = y - mu
    var = jnp.mean(yc * yc, axis=-1, keepdims=True)
    return yc * lax.rsqrt(var + LN_EPS) * g_ref[...] + b_ref[...]


def _resident_ln_body(a_ref, w_ref, r_ref, g_ref, b_ref, o_ref, wbf_ref):
    @pl.when(pl.program_id(0) == 0)
    def _():
        wbf_ref[...] = w_ref[...].astype(BF16)

    for rows in _row_chunks(o_ref.shape[0], 2):
        y = ALPHA * r_ref[rows, :] + jnp.dot(a_ref[rows, :], wbf_ref[...], preferred_element_type=F32)
        o_ref[rows, :] = _layer_norm_rows(y, g_ref, b_ref)


def _matmul_resident_ln(a, w, resid, gain, bias, *, tm):
    m, k = a.shape
    n = w.shape[1]
    row_block = pl.BlockSpec((tm, n), lambda i: (i, 0))
    blocks = [((tm, k), BF16), ((tm, n), F32), ((tm, n), F32)]
    single = [((k, n), F32), ((k, n), BF16)]
    return pl.pallas_call(
        _resident_ln_body,
        grid=(m // tm,),
        in_specs=[pl.BlockSpec((tm, k), lambda i: (i, 0)),
                  pl.BlockSpec((k, n), lambda i: (0, 0), pipeline_mode=pl.Buffered(1)),
                  row_block,
                  pl.BlockSpec((1, n), lambda i: (0, 0)),
                  pl.BlockSpec((1, n), lambda i: (0, 0))],
        out_specs=row_block,
        out_shape=jax.ShapeDtypeStruct((m, n), F32),
        scratch_shapes=[pltpu.VMEM((k, n), BF16)],
        compiler_params=_params(("arbitrary",), _vmem_limit(blocks, single)),
        name="matmul_resident_ln",
    )(a, w, resid, gain.reshape(1, n), bias.reshape(1, n))


def _mm_ln_body(a_ref, w_ref, r_ref, g_ref, b_ref, o_ref, *maybe_bf16_ref, nk):
    kk = pl.program_id(1)
    last = kk == nk - 1

    def partial_product(rows=slice(None)):
        return jnp.dot(a_ref[rows, :], w_ref[...].astype(BF16), preferred_element_type=F32)

    if nk > 1:
        @pl.when(kk == 0)
        def _():
            o_ref[...] = ALPHA * r_ref[...] + partial_product()

        @pl.when(jnp.logical_and(kk > 0, jnp.logical_not(last)))
        def _():
            o_ref[...] += partial_product()

    @pl.when(last)
    def _():
        for rows in _row_chunks(o_ref.shape[0], EPILOGUE_CHUNKS):
            seed = o_ref[rows, :] if nk > 1 else ALPHA * r_ref[rows, :]
            out = _layer_norm_rows(seed + partial_product(rows), g_ref, b_ref)
            o_ref[rows, :] = out
            for ob_ref in maybe_bf16_ref:
                ob_ref[rows, :] = out.astype(BF16)


def _matmul_residual_ln(a, w, resid, gain, bias, *, tm, tk, emit_bf16):
    m, k = a.shape
    n = w.shape[1]
    nk = k // tk
    row_block = pl.BlockSpec((tm, n), lambda i, kk: (i, 0))
    blocks = [((tm, tk), BF16), ((tk, n), w.dtype), ((tm, n), F32), ((tm, n), F32)]
    out_specs, out_shape = [row_block], [jax.ShapeDtypeStruct((m, n), F32)]
    if emit_bf16:
        blocks.append(((tm, n), BF16))
        out_specs.append(row_block)
        out_shape.append(jax.ShapeDtypeStruct((m, n), BF16))
    outs = pl.pallas_call(
        functools.partial(_mm_ln_body, nk=nk),
        grid=(m // tm, nk),
        in_specs=[pl.BlockSpec((tm, tk), lambda i, kk: (i, kk)),
                  pl.BlockSpec((tk, n), lambda i, kk: (kk, 0)),
                  row_block,
                  pl.BlockSpec((1, n), lambda i, kk: (0, 0)),
                  pl.BlockSpec((1, n), lambda i, kk: (0, 0))],
        out_specs=out_specs,
        out_shape=out_shape,
        compiler_params=_params(("arbitrary", "arbitrary"), _vmem_limit(blocks)),
        name="matmul_residual_ln",
    )(a, w, resid, gain.reshape(1, n), bias.reshape(1, n))
    return outs if emit_bf16 else outs[0]


def _rel_bucket_np(dist):
    n = np.maximum(dist, 0)
    exact = N_BUCKETS // 2
    logv = (np.log(np.maximum(n, 1).astype(np.float32) / exact) / math.log(MAX_DISTANCE / exact))
    large = exact + (logv.astype(np.float32) * (N_BUCKETS - exact)).astype(np.int32)
    large = np.minimum(large, N_BUCKETS - 1)
    return np.where(n < exact, n, large).astype(np.int32)


def _bias_expand_body(tab_ref, bucket_ref, o_ref, *, head0, shift_last_bucket, scale):
    h = head0 + pl.program_id(0)
    last = tab_ref[N_BUCKETS - 1, h] if shift_last_bucket else 0.0
    for t in range(bucket_ref.shape[0]):
        bucket = bucket_ref[t]
        acc = jnp.full(bucket.shape, NEG, F32)
        for bkt in range(N_BUCKETS):
            acc = jnp.where(bucket == bkt, (tab_ref[bkt, h] - last) * scale, acc)
        o_ref[t] = acc


def _bias_expand(table, dists, valids, *, head0, heads, shift_last_bucket, scale):
    bucket = np.stack([np.where(v, _rel_bucket_np(d), -1) for d, v in zip(dists, valids)]).astype(np.int32)
    nt, r, c = bucket.shape
    blocks = [((nt, r, c), jnp.int32), ((nt, r, c), F32)]
    return pl.pallas_call(
        functools.partial(_bias_expand_body, head0=head0, shift_last_bucket=shift_last_bucket,
                          scale=scale),
        grid=(heads,),
        in_specs=[pl.BlockSpec(memory_space=pltpu.SMEM),
                  pl.BlockSpec((nt, r, c), lambda h: (0, 0, 0))],
        out_specs=pl.BlockSpec((None, nt, r, c), lambda h: (h, 0, 0, 0)),
        out_shape=jax.ShapeDtypeStruct((heads, nt, r, c), F32),
        compiler_params=_params(("arbitrary",), _vmem_limit(blocks)),
        name="bias_expand",
    )(table.astype(F32), jnp.asarray(bucket))


def _swa_body(sink_ref, q_ref, kp_ref, kc_ref, vp_ref, vc_ref, bias_ref, o_ref):
    n = pl.program_id(0)
    half = DH_A
    pairs = G_A // 2
    rows = pairs * BLK
    lane_k = lax.broadcasted_iota(jnp.int32, (2 * BLK, 2 * half), 1)
    row = lax.broadcasted_iota(jnp.int32, (rows, 1), 0)
    lane_o = lax.broadcasted_iota(jnp.int32, (BLK, 2 * half), 1)
    variant = jnp.minimum(n, 1)

    kf = jnp.concatenate([kp_ref[...], kc_ref[...]], axis=0).astype(F32)
    vf = jnp.concatenate([vp_ref[...], vc_ref[...]], axis=0).astype(F32)
    kr = pltpu.roll(kf, half, 1)
    vr = pltpu.roll(vf, half, 1)

    outs = []
    for g in range(KV_A):
        k_own, k_other = (kf, kr) if g == 0 else (kr, kf)
        k_lo = jnp.where(lane_k < half, k_own, 0.0).astype(BF16)
        k_hi = jnp.where(lane_k >= half, k_other, 0.0).astype(BF16)
        v_dup = (jnp.where(lane_k < half, vf, vr) if g == 0 else jnp.where(lane_k < half, vr, vf)).astype(BF16)
        q_stack = jnp.concatenate(
            [q_ref[:, (g * pairs + t) * 2 * half:(g * pairs + t + 1) * 2 * half] for t in range(pairs)], axis=0)
        q_stack = (q_stack.astype(F32) * (DH_A ** -0.5 * LOG2E)).astype(BF16)
        o_par = []
        for par, k_sel in ((0, k_lo), (1, k_hi)):
            sink = jnp.zeros((rows, 1), F32)
            for t in range(pairs):
                sink = jnp.where(row >= t * BLK, sink_ref[2 * (g * pairs + t) + par] * LOG2E, sink)
            s = lax.dot_general(q_stack, k_sel, (((1,), (1,)), ((), ())),
                                preferred_element_type=F32) + bias_ref[variant, g, par]
            mx = jnp.maximum(jnp.max(s, axis=1, keepdims=True), sink)
            p = jnp.exp2(s - mx)
            den = jnp.sum(p, axis=1, keepdims=True) + jnp.exp2(sink - mx)
            o_par.append(jnp.dot(p.astype(BF16), v_dup, preferred_element_type=F32) / den)
        for t in range(pairs):
            outs.append(jnp.where(lane_o < half, o_par[0][t * BLK:(t + 1) * BLK], o_par[1][t * BLK:(t + 1) * BLK]))
    o_ref[...] = jnp.concatenate(outs, axis=1).astype(o_ref.dtype)


def _swa_attention(proj, sinks, bias, *, q_col, k_col, v_col):
    s = proj.shape[0]
    nb = s // BLK
    qw = HA * DH_A
    kvw = KV_A * DH_A
    assert kvw == V7X_LANES and q_col % qw == 0 and k_col % kvw == 0 and v_col % kvw == 0
    kb, vb = k_col // kvw, v_col // kvw
    pairs = G_A // 2
    bias = bias.reshape(KV_A, pairs, 2, 2, BLK, 2 * BLK).transpose(3, 0, 2, 1, 4, 5)
    bias = bias.reshape(2, KV_A, 2, pairs * BLK, 2 * BLK)
    blocks = [((BLK, qw), BF16)] + [((BLK, kvw), BF16)] * 4 + [(bias.shape, F32), ((BLK, qw), BF16)]
    prev = lambda n: jnp.maximum(n - 1, 0)
    return pl.pallas_call(
        _swa_body,
        grid=(nb,),
        in_specs=[pl.BlockSpec(memory_space=pltpu.SMEM),
                  pl.BlockSpec((BLK, qw), lambda n: (n, q_col // qw)),
                  pl.BlockSpec((BLK, kvw), lambda n: (prev(n), kb)),
                  pl.BlockSpec((BLK, kvw), lambda n: (n, kb)),
                  pl.BlockSpec((BLK, kvw), lambda n: (prev(n), vb)),
                  pl.BlockSpec((BLK, kvw), lambda n: (n, vb)),
                  pl.BlockSpec(bias.shape, lambda n: (0, 0, 0, 0, 0))],
        out_specs=pl.BlockSpec((BLK, qw), lambda n: (n, 0)),
        out_shape=jax.ShapeDtypeStruct((s, qw), BF16),
        compiler_params=_params(("arbitrary",), _vmem_limit(blocks)),
        name="swa_sink_attention",
    )(sinks.astype(F32), proj, proj, proj, proj, proj, bias)


def _band_bias_body(tab_ref, bucket_ref, o_ref, *, head0, scale, tile_offsets):
    h = head0 + pl.program_id(0)
    last = tab_ref[N_BUCKETS - 1, h]
    bands = []
    for t in range(2):
        bucket = bucket_ref[t]
        acc = jnp.full(bucket.shape, NEG, F32)
        for bkt in range(N_BUCKETS):
            acc = jnp.where(bucket == bkt, (tab_ref[bkt, h] - last) * scale, acc)
        bands.append(acc)
    sub = MAX_DISTANCE
    nb = o_ref.shape[1] // sub
    for t, base in enumerate(tile_offsets):
        for rb in range(nb):
            for cb in range(nb):
                off = base + sub * (cb - rb)
                if off < 0:
                    blk = jnp.full((sub, sub), NEG, F32)
                elif off < 2 * sub:
                    blk = bands[off // sub]
                else:
                    blk = jnp.zeros((sub, sub), F32)
                o_ref[t, rb * sub:(rb + 1) * sub, cb * sub:(cb + 1) * sub] = blk


def _band_bias_tiles(table, *, head0, heads, tile, tile_offsets, scale):
    sub = MAX_DISTANCE
    assert tile % sub == 0 and all(o % sub == 0 for o in tile_offsets)
    r = np.arange(sub)[:, None]
    c = np.arange(sub)[None, :]
    bucket = np.stack([np.where(c - r + o >= 0, _rel_bucket_np(c - r + o), -1) for o in (0, sub)]).astype(np.int32)
    nt = len(tile_offsets)
    blocks = [((2, sub, sub), jnp.int32), ((nt, tile, tile), F32)]
    return pl.pallas_call(
        functools.partial(_band_bias_body, head0=head0, scale=scale, tile_offsets=tuple(tile_offsets)),
        grid=(heads,),
        in_specs=[pl.BlockSpec(memory_space=pltpu.SMEM),
                  pl.BlockSpec((2, sub, sub), lambda h: (0, 0, 0))],
        out_specs=pl.BlockSpec((None, nt, tile, tile), lambda h: (h, 0, 0, 0)),
        out_shape=jax.ShapeDtypeStruct((heads, nt, tile, tile), F32),
        compiler_params=_params(("arbitrary",), _vmem_limit(blocks)),
        name="band_bias_tiles",
    )(table.astype(F32), jnp.asarray(bucket))


HEADS_PER_STEP = 2


def _diff_body(q_ref, k_ref, v_ref, b2_ref, lq1_ref, lk1_ref, lq2_ref, lk2_ref, sw_ref,
               o_ref, qz_ref, sa_ref, sb_ref, cma_ref, cmb_ref, acc_ref, m_ref, l_ref, *, tq, tk):
    i = pl.program_id(1)
    width = 2 * DH_B
    heads = range(HEADS_PER_STEP)
    lanes = lambda hh: slice(hh * width, (hh + 1) * width)

    row = lax.broadcasted_iota(jnp.int32, (width, tq), 0)
    for hh in heads:
        qt = (q_ref[:, lanes(hh)].astype(F32) * (DH_B ** -0.5 * LOG2E)).T
        qz_ref[hh, 0] = jnp.where(row < DH_B, qt, 0.0).astype(BF16)
        qz_ref[hh, 1] = jnp.where(row >= DH_B, qt, 0.0).astype(BF16)
    acc_ref[...] = jnp.zeros_like(acc_ref)
    m_ref[...] = jnp.full_like(m_ref, NEG)
    l_ref[...] = jnp.zeros_like(l_ref)

    def score_chain(hh, c, tile, bias_tile, s_ref, cm_ref):
        kblk = k_ref[pl.ds(pl.multiple_of(tile * tk, tk), tk), lanes(hh)]
        s = jnp.dot(kblk, qz_ref[hh, c], preferred_element_type=F32)
        if bias_tile is not None:
            s = s + b2_ref[hh, bias_tile]
        s_ref[hh, c] = s
        cm_ref[hh, c] = jnp.max(s, axis=0, keepdims=True)

    def accumulate_chain(hh, c, tile, s_ref, cm_ref):
        vblk = v_ref[pl.ds(pl.multiple_of(tile * tk, tk), tk), lanes(hh)]
        m_old = m_ref[hh, c]
        m_new = jnp.maximum(m_old, cm_ref[hh, c])
        rescale = jnp.exp2(m_old - m_new)
        p = jnp.exp2(s_ref[hh, c] - m_new)
        l_ref[hh, c] = rescale * l_ref[hh, c] + jnp.sum(p, axis=0, keepdims=True)
        pv = lax.dot_general(vblk, p.astype(BF16), (((0,), (0,)), ((), ())), preferred_element_type=F32)
        acc_ref[hh, c] = rescale * acc_ref[hh, c] + pv
        m_ref[hh, c] = m_new

    def stage(score_args=None, acc_args=None):
        for c in range(2):
            for hh in heads:
                if score_args is not None:
                    score_chain(hh, c, *score_args)
                if acc_args is not None:
                    accumulate_chain(hh, c, *acc_args)

    n_items = i + 1
    tile_of = lambda item: jnp.maximum(i - item, 0)
    slot_a, slot_b = (sa_ref, cma_ref), (sb_ref, cmb_ref)
    stage(score_args=(i, 1, *slot_a))

    @pl.when(n_items >= 2)
    def _():
        stage((i - 1, 0, *slot_b), (i, *slot_a))
        stage((tile_of(2), None, *slot_a), (i - 1, *slot_b))

    def pair(t, carry):
        stage((i - (2 * t + 1), None, *slot_b), (i - 2 * t, *slot_a))
        stage((tile_of(2 * t + 2), None, *slot_a), (i - (2 * t + 1), *slot_b))
        return carry

    lax.fori_loop(1, n_items // 2, pair, 0)

    @pl.when(n_items % 2 == 1)
    def _():
        stage(acc_args=(0, *slot_a))

    lam = (jnp.exp(jnp.sum(lq1_ref[...] * lk1_ref[...], axis=1, keepdims=True))
           - jnp.exp(jnp.sum(lq2_ref[...] * lk2_ref[...], axis=1, keepdims=True)) + LAMBDA_INIT)
    for hh in heads:
        o = acc_ref[hh, 0] / l_ref[hh, 0] - lam * (acc_ref[hh, 1] / l_ref[hh, 1])
        ms = jnp.mean(o * o, axis=0, keepdims=True)
        o = o * lax.rsqrt(ms + LN_EPS) * sw_ref[...] * (1.0 - LAMBDA_INIT)
        o_ref[:, lanes(hh)] = o.T.astype(o_ref.dtype)


def _diff_attention(proj, bias2, lq1, lk1, lq2, lk2, subln_w, *, q_col, k_col, v_col, tq, tk):
    s = proj.shape[0]
    width = 2 * DH_B
    hps = HEADS_PER_STEP
    wide = hps * width
    assert width == V7X_LANES and tq == tk and HB % hps == 0
    assert q_col % wide == 0 and k_col % wide == 0 and v_col % wide == 0
    qb, kb, vb = q_col // wide, k_col // wide, v_col // wide
    blocks = [((tq, wide), BF16), ((s, wide), BF16), ((s, wide), BF16),
              ((hps, 2, tk, tq), F32), ((tq, wide), BF16)]
    scores_buf = ((hps, 2, tk, tq), F32)
    stats_buf = ((hps, 2, 1, tq), F32)
    scratch = [((hps, 2, width, tq), BF16), scores_buf, scores_buf, stats_buf, stats_buf,
               ((hps, 2, width, tq), F32), stats_buf, stats_buf]
    padded = [((hps, 2, 8, tq), F32) if sd == stats_buf else sd for sd in scratch]
    vec = lambda v: v.astype(F32).reshape(1, DH_B)
    small = pl.BlockSpec((1, DH_B), lambda h, i: (0, 0))
    return pl.pallas_call(
        functools.partial(_diff_body, tq=tq, tk=tk),
        grid=(HB // hps, s // tq),
        in_specs=[pl.BlockSpec((tq, wide), lambda h, i: (i, qb + h)),
                  pl.BlockSpec((s, wide), lambda h, i: (0, kb + h)),
                  pl.BlockSpec((s, wide), lambda h, i: (0, vb + h)),
                  pl.BlockSpec((hps, 2, tk, tq), lambda h, i: (h, 0, 0, 0)),
                  small, small, small, small,
                  pl.BlockSpec((width, 1), lambda h, i: (0, 0))],
        out_specs=pl.BlockSpec((tq, wide), lambda h, i: (i, h)),
        out_shape=jax.ShapeDtypeStruct((s, HB * width), BF16),
        scratch_shapes=[pltpu.VMEM(sh, dt) for sh, dt in scratch],
        compiler_params=_params(("arbitrary", "arbitrary"), _vmem_limit(blocks, padded)),
        name="diff_attention",
    )(proj, proj, proj, bias2, vec(lq1), vec(lk1), vec(lq2), vec(lk2),
      subln_w.astype(F32).reshape(width, 1))


def _cross_block_body(h_ref, wq_ref, kv_ref, wo_ref, g_ref, b_ref, o_ref, ob_ref, wq_bf_ref, wo_bf_ref):
    @pl.when(pl.program_id(0) == 0)
    def _():
        wq_bf_ref[...] = wq_ref[...].astype(BF16)
        wo_bf_ref[...] = wo_ref[...].astype(BF16)

    scale = DH_C ** -0.5
    h = h_ref[...]
    q = jnp.dot(h.astype(BF16), wq_bf_ref[...], preferred_element_type=F32).astype(BF16)
    outs = []
    for hd in range(HC):
        qh = q[:, hd * DH_C:(hd + 1) * DH_C]
        kh = kv_ref[:, hd * DH_C:(hd + 1) * DH_C]
        vh = kv_ref[:, (HC + hd) * DH_C:(HC + hd + 1) * DH_C]
        s = lax.dot_general(qh, kh, (((1,), (1,)), ((), ())), preferred_element_type=F32) * scale
        mx = jnp.max(s, axis=1, keepdims=True)
        p = jnp.exp(s - mx)
        den = jnp.sum(p, axis=1, keepdims=True)
        outs.append((jnp.dot(p.astype(BF16), vh, preferred_element_type=F32) / den).astype(BF16))
    oc = jnp.concatenate(outs, axis=1)
    y = ALPHA * h + jnp.dot(oc, wo_bf_ref[...], preferred_element_type=F32)
    out = _layer_norm_rows(y, g_ref, b_ref)
    o_ref[...] = out
    ob_ref[...] = out.astype(BF16)


def _cross_attention_block(h, kv, w_cq, w_co, gain, bias, *, tm):
    s, d = h.shape
    mlen, w = kv.shape[0], HC * DH_C
    row_block = pl.BlockSpec((tm, d), lambda i: (i, 0))
    whole = lambda shape: pl.BlockSpec(shape, lambda i: (0,) * len(shape))
    blocks = [((tm, d), F32), ((d, w), F32), ((mlen, 2 * w), BF16), ((w, d), F32), ((tm, d), F32), ((tm, d), BF16)]
    scratch = [((d, w), BF16), ((w, d), BF16)]
    return pl.pallas_call(
        _cross_block_body,
        grid=(s // tm,),
        in_specs=[row_block, whole((d, w)), whole((mlen, 2 * w)), whole((w, d)), whole((1, d)), whole((1, d))],
        out_specs=[row_block, row_block],
        out_shape=[jax.ShapeDtypeStruct((s, d), F32), jax.ShapeDtypeStruct((s, d), BF16)],
        scratch_shapes=[pltpu.VMEM(sh, dt) for sh, dt in scratch],
        compiler_params=_params(("arbitrary",), _vmem_limit(blocks, scratch)),
        name="memory_cross_attention_block",
    )(h, w_cq, kv, w_co, gain.reshape(1, d), bias.reshape(1, d))


def kernel(x, mem, rel_bias_table, w_in, sinks, lambda_q1, lambda_k1, lambda_q2, lambda_k2, subln_w,
           w_branch_a, w_branch_b, w_o, ln1_g, ln1_b, w_cq, w_mem_kv, w_co, ln2_g, ln2_b,
           w_gate_up, w_down, ln3_g, ln3_b):
    b, s, d = x.shape
    assert b == 1 and w_in.shape[0] == DEPTH == 1
    (w_in, sinks, lambda_q1, lambda_k1, lambda_q2, lambda_k2, subln_w, w_branch_a, w_branch_b, w_o, ln1_g, ln1_b,
     w_cq, w_mem_kv, w_co, ln2_g, ln2_b, w_gate_up, w_down, ln3_g, ln3_b) = [
        p.reshape(p.shape[1:]) for p in (
            w_in, sinks, lambda_q1, lambda_k1, lambda_q2, lambda_k2, subln_w, w_branch_a, w_branch_b, w_o, ln1_g,
            ln1_b, w_cq, w_mem_kv, w_co, ln2_g, ln2_b, w_gate_up, w_down, ln3_g, ln3_b)]
    qa_w, kva_w, qb_w = HA * DH_A, KV_A * DH_A, HB * 2 * DH_B
    col_qa, col_ka, col_va = 0, qa_w, qa_w + kva_w
    col_qb = col_va + kva_w
    col_kb, col_vb = col_qb + qb_w, col_qb + 2 * qb_w
    col_ga = col_vb + qb_w
    col_gb = col_ga + d
    tq = tk = 512

    h0 = x.reshape(s, d)

    proj = _matmul(h0, w_in, tm=1024, tn=768, out_dtype=BF16, name="in_proj")

    i_a = np.arange(BLK)[:, None]
    j_a = np.arange(2 * BLK)[None, :]
    dist_a = BLK + i_a - j_a
    band_a = (dist_a >= 0) & (dist_a < WINDOW)
    bias_a = _bias_expand(rel_bias_table, [dist_a, dist_a], [band_a & (j_a >= BLK), band_a], head0=0, heads=HA,
                          shift_last_bucket=False, scale=LOG2E)
    o_a = _swa_attention(proj, sinks, bias_a, q_col=col_qa, k_col=col_ka, v_col=col_va)

    bias2 = _band_bias_tiles(rel_bias_table, head0=HA, heads=HB, tile=tq, tile_offsets=(tk, 0), scale=LOG2E)
    o_b = _diff_attention(proj, bias2, lambda_q1, lambda_k1, lambda_q2,
                          lambda_k2, subln_w, q_col=col_qb, k_col=col_kb, v_col=col_vb, tq=tq, tk=tk)

    mix = _gated_branches(o_a, o_b, w_branch_a, w_branch_b, proj, col_ga, col_gb, tm=2048, tn=256)
    h1 = _matmul_resident_ln(mix, w_o, h0, ln1_g, ln1_b, tm=512)

    kvm = _matmul(mem.reshape(mem.shape[1], d), w_mem_kv, tm=mem.shape[1], tn=HC * DH_C,
                  out_dtype=BF16, name="mem_kv")
    h2, h2b = _cross_attention_block(h1, kvm, w_cq, w_co, ln2_g, ln2_b, tm=512)

    act, w_down_bf = _swiglu_up(h2b, w_gate_up, w_down, tm=2048, tn=256)
    h3 = _matmul_residual_ln(act, w_down_bf, h2, ln3_g, ln3_b, tm=1024, tk=512, emit_bf16=False)
    return h3.reshape(b, s, d)
```

```python
import functools
import math

import numpy as np
import jax
import jax.numpy as jnp
from jax import lax
from jax.experimental import pallas as pl
from jax.experimental.pallas import tpu as pltpu

F32 = jnp.float32
BF16 = jnp.bfloat16

BLK = 128
WINDOW = 128
HA, KV_A, DH_A = 16, 2, 64
G_A = HA // KV_A
HB, DH_B = 8, 64
N_BUCKETS, MAX_DISTANCE = 32, 128
HC, DH_C = 4, 128
LN_EPS = 1e-5
DEPTH = 1
ALPHA = (2 * DEPTH) ** 0.25
LAMBDA_INIT = 0.8 - 0.6 * math.exp(-0.3 * 0)
LOG2E = math.log2(math.e)

V7X_LANES = 128
V7X_VMEM_BYTES = 64 * 1024 * 1024
V7X_VMEM_TEMP_BYTES = 12 * 1024 * 1024

EPILOGUE_CHUNKS = 4

TILES = dict(
    in_proj=dict(tm=1024, tn=768),
    diff_attention=dict(tq=512, tk=512),
    gated_branches=dict(tm=2048, tn=256),
    out_proj_ln=dict(tm=512),
    cross_block=dict(tm=512),
    swiglu_up=dict(tm=2048, tn=256),
    down_proj_ln=dict(tm=1024, tk=512),
)

NEG = -1e30


def _nbytes(shape, dtype):
    return int(np.prod(shape)) * jnp.dtype(dtype).itemsize


def _vmem_limit(pipelined, scratch=()):
    need = 2 * sum(_nbytes(s, d) for s, d in pipelined) + sum(_nbytes(s, d) for s, d in scratch)
    need += V7X_VMEM_TEMP_BYTES
    assert need <= V7X_VMEM_BYTES - 4 * 1024 * 1024, need
    return need


def _params(semantics, vmem):
    return pltpu.CompilerParams(dimension_semantics=semantics, vmem_limit_bytes=vmem)


def _act_tile(a_ref, abf_ref):
    if abf_ref is None:
        return a_ref[...]

    @pl.when(pl.program_id(1) == 0)
    def _():
        abf_ref[...] = a_ref[...].astype(BF16)

    return abf_ref[...]


def _mm_body(a_ref, w_ref, o_ref, *scratch):
    a = _act_tile(a_ref, scratch[0] if scratch else None)
    o_ref[...] = jnp.dot(a, w_ref[...].astype(BF16), preferred_element_type=F32).astype(o_ref.dtype)


def _matmul(a, w, *, tm, tn, out_dtype, name):
    m, k = a.shape
    n = w.shape[1]
    cast = a.dtype != BF16
    scratch = [((tm, k), BF16)] if cast else []
    blocks = [((tm, k), a.dtype), ((k, tn), w.dtype), ((tm, tn), out_dtype)]
    return pl.pallas_call(
        _mm_body,
        grid=(m // tm, n // tn),
        in_specs=[pl.BlockSpec((tm, k), lambda i, j: (i, 0)),
                  pl.BlockSpec((k, tn), lambda i, j: (0, j))],
        out_specs=pl.BlockSpec((tm, tn), lambda i, j: (i, j)),
        out_shape=jax.ShapeDtypeStruct((m, n), out_dtype),
        scratch_shapes=[pltpu.VMEM(s, d) for s, d in scratch],
        compiler_params=_params(("arbitrary", "arbitrary"), _vmem_limit(blocks, scratch)),
        name=name,
    )(a, w)


def _row_chunks(rows, chunks):
    size = rows // chunks
    return [slice(c * size, (c + 1) * size) for c in range(chunks)]


def _branch_body(oa_ref, ob_ref, wa_ref, wb_ref, ga_ref, gb_ref, o_ref):
    wa = wa_ref[...].astype(BF16)
    wb = wb_ref[...].astype(BF16)
    for rows in _row_chunks(o_ref.shape[0], EPILOGUE_CHUNKS):
        ya = jnp.dot(oa_ref[rows, :], wa, preferred_element_type=F32)
        yb = jnp.dot(ob_ref[rows, :], wb, preferred_element_type=F32)
        ga = jax.nn.sigmoid(ga_ref[rows, :].astype(F32))
        gb = jax.nn.sigmoid(gb_ref[rows, :].astype(F32))
        o_ref[rows, :] = (ga * ya + gb * yb).astype(o_ref.dtype)


def _gated_branches(o_a, o_b, w_a, w_b, proj, ga_col, gb_col, *, tm, tn):
    m, ka = o_a.shape
    kb = o_b.shape[1]
    n = w_a.shape[1]
    ga_blk, gb_blk = ga_col // tn, gb_col // tn
    blocks = [((tm, ka), BF16), ((tm, kb), BF16), ((ka, tn), F32), ((kb, tn), F32),
              ((tm, tn), proj.dtype), ((tm, tn), proj.dtype), ((tm, tn), BF16)]
    return pl.pallas_call(
        _branch_body,
        grid=(m // tm, n // tn),
        in_specs=[pl.BlockSpec((tm, ka), lambda i, j: (i, 0)),
                  pl.BlockSpec((tm, kb), lambda i, j: (i, 0)),
                  pl.BlockSpec((ka, tn), lambda i, j: (0, j)),
                  pl.BlockSpec((kb, tn), lambda i, j: (0, j)),
                  pl.BlockSpec((tm, tn), lambda i, j: (i, ga_blk + j)),
                  pl.BlockSpec((tm, tn), lambda i, j: (i, gb_blk + j))],
        out_specs=pl.BlockSpec((tm, tn), lambda i, j: (i, j)),
        out_shape=jax.ShapeDtypeStruct((m, n), BF16),
        compiler_params=_params(("arbitrary", "arbitrary"), _vmem_limit(blocks)),
        name="gated_branches",
    )(o_a, o_b, w_a, w_b, proj, proj)


def _swiglu_body(a_ref, wg_ref, wu_ref, wd_ref, o_ref, wd_bf_ref):
    a = a_ref[...]
    wg = wg_ref[...].astype(BF16)
    wu = wu_ref[...].astype(BF16)
    for rows in _row_chunks(o_ref.shape[0], 2):
        g = jnp.dot(a[rows, :], wg, preferred_element_type=F32)
        u = jnp.dot(a[rows, :], wu, preferred_element_type=F32)
        o_ref[rows, :] = (g * jax.nn.sigmoid(g) * u).astype(o_ref.dtype)
    wd_bf_ref[...] = wd_ref[...].astype(BF16)


def _swiglu_up(a, w_gate_up, w_down, *, tm, tn):
    m, k = a.shape
    d_ff = w_gate_up.shape[1] // 2
    row_tiles, col_tiles = m // tm, d_ff // tn
    up_blk = d_ff // tn
    kd, nd = w_down.shape
    slab = kd // (row_tiles * col_tiles)
    assert slab * row_tiles * col_tiles == kd and slab % 16 == 0 and a.dtype == BF16
    slab_block = pl.BlockSpec((slab, nd), lambda i, j: (i * col_tiles + j, 0))
    blocks = [((tm, k), BF16), ((k, tn), F32), ((k, tn), F32), ((tm, tn), BF16), ((slab, nd), F32), ((slab, nd), BF16)]
    return pl.pallas_call(
        _swiglu_body,
        grid=(row_tiles, col_tiles),
        in_specs=[pl.BlockSpec((tm, k), lambda i, j: (i, 0)),
                  pl.BlockSpec((k, tn), lambda i, j: (0, j)),
                  pl.BlockSpec((k, tn), lambda i, j: (0, up_blk + j)),
                  slab_block],
        out_specs=[pl.BlockSpec((tm, tn), lambda i, j: (i, j)), slab_block],
        out_shape=[jax.ShapeDtypeStruct((m, d_ff), BF16), jax.ShapeDtypeStruct((kd, nd), BF16)],
        compiler_params=_params(("arbitrary", "arbitrary"), _vmem_limit(blocks)),
        name="swiglu_up",
    )(a, w_gate_up, w_gate_up, w_down)


def _layer_norm_rows(y, g_ref, b_ref):
    mu = jnp.mean(y, axis=-1, keepdims=True)
    yc = y - mu
    var = jnp.mean(yc * yc, axis=-1, keepdims=True)
    return yc * lax.rsqrt(var + LN_EPS) * g_ref[...] + b_ref[...]


def _resident_ln_body(a_ref, w_ref, r_ref, g_ref, b_ref, o_ref, wbf_ref):
    @pl.when(pl.program_id(0) == 0)
    def _():
        wbf_ref[...] = w_ref[...].astype(BF16)

    halves = _row_chunks(o_ref.shape[0], 2)
    ys = [ALPHA * r_ref[rows, :] + jnp.dot(a_ref[rows, :], wbf_ref[...], preferred_element_type=F32)
          for rows in halves]
    for rows, y in zip(halves, ys):
        o_ref[rows, :] = _layer_norm_rows(y, g_ref, b_ref)


def _matmul_resident_ln(a, w, resid, gain, bias, *, tm):
    m, k = a.shape
    n = w.shape[1]
    row_block = pl.BlockSpec((tm, n), lambda i: (i, 0))
    blocks = [((tm, k), BF16), ((tm, n), F32), ((tm, n), F32)]
    single = [((k, n), F32), ((k, n), BF16)]
    return pl.pallas_call(
        _resident_ln_body,
        grid=(m // tm,),
        in_specs=[pl.BlockSpec((tm, k), lambda i: (i, 0)),
                  pl.BlockSpec((k, n), lambda i: (0, 0), pipeline_mode=pl.Buffered(1)),
                  row_block,
                  pl.BlockSpec((1, n), lambda i: (0, 0)),
                  pl.BlockSpec((1, n), lambda i: (0, 0))],
        out_specs=row_block,
        out_shape=jax.ShapeDtypeStruct((m, n), F32),
        scratch_shapes=[pltpu.VMEM((k, n), BF16)],
        compiler_params=_params(("arbitrary",), _vmem_limit(blocks, single)),
        name="matmul_resident_ln",
    )(a, w, resid, gain.reshape(1, n), bias.reshape(1, n))


def _mm_ln_body(a_ref, w_ref, r_ref, g_ref, b_ref, o_ref, *maybe_bf16_ref, nk):
    kk = pl.program_id(1)
    last = kk == nk - 1

    def partial_product(rows=slice(None)):
        return jnp.dot(a_ref[rows, :], w_ref[...].astype(BF16), preferred_element_type=F32)

    if nk > 1:
        @pl.when(kk == 0)
        def _():
            o_ref[...] = ALPHA * r_ref[...] + partial_product()

        @pl.when(jnp.logical_and(kk > 0, jnp.logical_not(last)))
        def _():
            o_ref[...] += partial_product()

    @pl.when(last)
    def _():
        for rows in _row_chunks(o_ref.shape[0], EPILOGUE_CHUNKS):
            seed = o_ref[rows, :] if nk > 1 else ALPHA * r_ref[rows, :]
            out = _layer_norm_rows(seed + partial_product(rows), g_ref, b_ref)
            o_ref[rows, :] = out
            for ob_ref in maybe_bf16_ref:
                ob_ref[rows, :] = out.astype(BF16)


def _matmul_residual_ln(a, w, resid, gain, bias, *, tm, tk, emit_bf16):
    m, k = a.shape
    n = w.shape[1]
    nk = k // tk
    row_block = pl.BlockSpec((tm, n), lambda i, kk: (i, 0))
    blocks = [((tm, tk), BF16), ((tk, n), w.dtype), ((tm, n), F32), ((tm, n), F32)]
    out_specs, out_shape = [row_block], [jax.ShapeDtypeStruct((m, n), F32)]
    if emit_bf16:
        blocks.append(((tm, n), BF16))
        out_specs.append(row_block)
        out_shape.append(jax.ShapeDtypeStruct((m, n), BF16))
    outs = pl.pallas_call(
        functools.partial(_mm_ln_body, nk=nk),
        grid=(m // tm, nk),
        in_specs=[pl.BlockSpec((tm, tk), lambda i, kk: (i, kk)),
                  pl.BlockSpec((tk, n), lambda i, kk: (kk, 0)),
                  row_block,
                  pl.BlockSpec((1, n), lambda i, kk: (0, 0)),
                  pl.BlockSpec((1, n), lambda i, kk: (0, 0))],
        out_specs=out_specs,
        out_shape=out_shape,
        compiler_params=_params(("arbitrary", "arbitrary"), _vmem_limit(blocks)),
        name="matmul_residual_ln",
    )(a, w, resid, gain.reshape(1, n), bias.reshape(1, n))
    return outs if emit_bf16 else outs[0]


def _rel_bucket_np(dist):
    n = np.maximum(dist, 0)
    exact = N_BUCKETS // 2
    logv = (np.log(np.maximum(n, 1).astype(np.float32) / exact) / math.log(MAX_DISTANCE / exact))
    large = exact + (logv.astype(np.float32) * (N_BUCKETS - exact)).astype(np.int32)
    large = np.minimum(large, N_BUCKETS - 1)
    return np.where(n < exact, n, large).astype(np.int32)


def _bias_expand_body(tab_ref, bucket_ref, o_ref, *, head0, shift_last_bucket, scale):
    h = head0 + pl.program_id(0)
    last = tab_ref[N_BUCKETS - 1, h] if shift_last_bucket else 0.0
    for t in range(bucket_ref.shape[0]):
        bucket = bucket_ref[t]
        acc = jnp.full(bucket.shape, NEG, F32)
        for bkt in range(N_BUCKETS):
            acc = jnp.where(bucket == bkt, (tab_ref[bkt, h] - last) * scale, acc)
        o_ref[t] = acc


def _bias_expand(table, dists, valids, *, head0, heads, shift_last_bucket, scale):
    bucket = np.stack([np.where(v, _rel_bucket_np(d), -1) for d, v in zip(dists, valids)]).astype(np.int32)
    nt, r, c = bucket.shape
    blocks = [((nt, r, c), jnp.int32), ((nt, r, c), F32)]
    return pl.pallas_call(
        functools.partial(_bias_expand_body, head0=head0, shift_last_bucket=shift_last_bucket,
                          scale=scale),
        grid=(heads,),
        in_specs=[pl.BlockSpec(memory_space=pltpu.SMEM),
                  pl.BlockSpec((nt, r, c), lambda h: (0, 0, 0))],
        out_specs=pl.BlockSpec((None, nt, r, c), lambda h: (h, 0, 0, 0)),
        out_shape=jax.ShapeDtypeStruct((heads, nt, r, c), F32),
        compiler_params=_params(("arbitrary",), _vmem_limit(blocks)),
        name="bias_expand",
    )(table.astype(F32), jnp.asarray(bucket))


def _swa_body(sink_ref, q_ref, kp_ref, kc_ref, vp_ref, vc_ref, bias_ref, o_ref):
    n = pl.program_id(0)
    half = DH_A
    pairs = G_A // 2
    rows = pairs * BLK
    lane_k = lax.broadcasted_iota(jnp.int32, (2 * BLK, 2 * half), 1)
    row = lax.broadcasted_iota(jnp.int32, (rows, 1), 0)
    lane_o = lax.broadcasted_iota(jnp.int32, (BLK, 2 * half), 1)
    variant = jnp.minimum(n, 1)

    kf = jnp.concatenate([kp_ref[...], kc_ref[...]], axis=0).astype(F32)
    vf = jnp.concatenate([vp_ref[...], vc_ref[...]], axis=0).astype(F32)
    kr = pltpu.roll(kf, half, 1)
    vr = pltpu.roll(vf, half, 1)

    chains = []
    for g in range(KV_A):
        k_own, k_other = (kf, kr) if g == 0 else (kr, kf)
        k_lo = jnp.where(lane_k < half, k_own, 0.0).astype(BF16)
        k_hi = jnp.where(lane_k >= half, k_other, 0.0).astype(BF16)
        v_dup = (jnp.where(lane_k < half, vf, vr) if g == 0 else jnp.where(lane_k < half, vr, vf)).astype(BF16)
        q_stack = jnp.concatenate(
            [q_ref[:, (g * pairs + t) * 2 * half:(g * pairs + t + 1) * 2 * half] for t in range(pairs)], axis=0)
        q_stack = (q_stack.astype(F32) * (DH_A ** -0.5 * LOG2E)).astype(BF16)
        for par, k_sel in ((0, k_lo), (1, k_hi)):
            s = lax.dot_general(q_stack, k_sel, (((1,), (1,)), ((), ())),
                                preferred_element_type=F32) + bias_ref[variant, g, par]
            chains.append((g, par, s, v_dup))

    o_par = {}
    for g, par, s, v_dup in chains:
        sink = jnp.zeros((rows, 1), F32)
        for t in range(pairs):
            sink = jnp.where(row >= t * BLK, sink_ref[2 * (g * pairs + t) + par] * LOG2E, sink)
        mx = jnp.maximum(jnp.max(s, axis=1, keepdims=True), sink)
        p = jnp.exp2(s - mx)
        den = jnp.sum(p, axis=1, keepdims=True) + jnp.exp2(sink - mx)
        o_par[g, par] = jnp.dot(p.astype(BF16), v_dup, preferred_element_type=F32) / den

    outs = []
    for g in range(KV_A):
        for t in range(pairs):
            qs = slice(t * BLK, (t + 1) * BLK)
            outs.append(jnp.where(lane_o < half, o_par[g, 0][qs], o_par[g, 1][qs]))
    o_ref[...] = jnp.concatenate(outs, axis=1).astype(o_ref.dtype)


def _swa_attention(proj, sinks, bias, *, q_col, k_col, v_col):
    s = proj.shape[0]
    nb = s // BLK
    qw = HA * DH_A
    kvw = KV_A * DH_A
    assert kvw == V7X_LANES and q_col % qw == 0 and k_col % kvw == 0 and v_col % kvw == 0
    kb, vb = k_col // kvw, v_col // kvw
    pairs = G_A // 2
    bias = bias.reshape(KV_A, pairs, 2, 2, BLK, 2 * BLK).transpose(3, 0, 2, 1, 4, 5)
    bias = bias.reshape(2, KV_A, 2, pairs * BLK, 2 * BLK)
    blocks = [((BLK, qw), BF16)] + [((BLK, kvw), BF16)] * 4 + [(bias.shape, F32), ((BLK, qw), BF16)]
    prev = lambda n: jnp.maximum(n - 1, 0)
    return pl.pallas_call(
        _swa_body,
        grid=(nb,),
        in_specs=[pl.BlockSpec(memory_space=pltpu.SMEM),
                  pl.BlockSpec((BLK, qw), lambda n: (n, q_col // qw)),
                  pl.BlockSpec((BLK, kvw), lambda n: (prev(n), kb)),
                  pl.BlockSpec((BLK, kvw), lambda n: (n, kb)),
                  pl.BlockSpec((BLK, kvw), lambda n: (prev(n), vb)),
                  pl.BlockSpec((BLK, kvw), lambda n: (n, vb)),
                  pl.BlockSpec(bias.shape, lambda n: (0, 0, 0, 0, 0))],
        out_specs=pl.BlockSpec((BLK, qw), lambda n: (n, 0)),
        out_shape=jax.ShapeDtypeStruct((s, qw), BF16),
        compiler_params=_params(("arbitrary",), _vmem_limit(blocks)),
        name="swa_sink_attention",
    )(sinks.astype(F32), proj, proj, proj, proj, proj, bias)


def _band_bias_body(tab_ref, bucket_ref, o_ref, *, head0, scale, tile_offsets):
    h = head0 + pl.program_id(0)
    last = tab_ref[N_BUCKETS - 1, h]
    bands = []
    for t in range(2):
        bucket = bucket_ref[t]
        acc = jnp.full(bucket.shape, NEG, F32)
        for bkt in range(N_BUCKETS):
            acc = jnp.where(bucket == bkt, (tab_ref[bkt, h] - last) * scale, acc)
        bands.append(acc)
    sub = MAX_DISTANCE
    nb = o_ref.shape[1] // sub
    for t, base in enumerate(tile_offsets):
        for rb in range(nb):
            for cb in range(nb):
                off = base + sub * (cb - rb)
                if off < 0:
                    blk = jnp.full((sub, sub), NEG, F32)
                elif off < 2 * sub:
                    blk = bands[off // sub]
                else:
                    blk = jnp.zeros((sub, sub), F32)
                o_ref[t, rb * sub:(rb + 1) * sub, cb * sub:(cb + 1) * sub] = blk


def _band_bias_tiles(table, *, head0, heads, tile, tile_offsets, scale):
    sub = MAX_DISTANCE
    assert tile % sub == 0 and all(o % sub == 0 for o in tile_offsets)
    r = np.arange(sub)[:, None]
    c = np.arange(sub)[None, :]
    bucket = np.stack([np.where(c - r + o >= 0, _rel_bucket_np(c - r + o), -1) for o in (0, sub)]).astype(np.int32)
    nt = len(tile_offsets)
    blocks = [((2, sub, sub), jnp.int32), ((nt, tile, tile), F32)]
    return pl.pallas_call(
        functools.partial(_band_bias_body, head0=head0, scale=scale, tile_offsets=tuple(tile_offsets)),
        grid=(heads,),
        in_specs=[pl.BlockSpec(memory_space=pltpu.SMEM),
                  pl.BlockSpec((2, sub, sub), lambda h: (0, 0, 0))],
        out_specs=pl.BlockSpec((None, nt, tile, tile), lambda h: (h, 0, 0, 0)),
        out_shape=jax.ShapeDtypeStruct((heads, nt, tile, tile), F32),
        compiler_params=_params(("arbitrary",), _vmem_limit(blocks)),
        name="band_bias_tiles",
    )(table.astype(F32), jnp.asarray(bucket))


HEADS_PER_STEP = 2


def _diff_body(q_ref, k_ref, v_ref, b2_ref, lq1_ref, lk1_ref, lq2_ref, lk2_ref, sw_ref,
               o_ref, qz_ref, sa_ref, sb_ref, cma_ref, cmb_ref, acc_ref, m_ref, l_ref, *, tq, tk):
    i = pl.program_id(1)
    width = 2 * DH_B
    heads = range(HEADS_PER_STEP)
    lanes = lambda hh: slice(hh * width, (hh + 1) * width)

    row = lax.broadcasted_iota(jnp.int32, (width, tq), 0)
    for hh in heads:
        qt = (q_ref[:, lanes(hh)].astype(F32) * (DH_B ** -0.5 * LOG2E)).T
        qz_ref[hh, 0] = jnp.where(row < DH_B, qt, 0.0).astype(BF16)
        qz_ref[hh, 1] = jnp.where(row >= DH_B, qt, 0.0).astype(BF16)
    acc_ref[...] = jnp.zeros_like(acc_ref)
    m_ref[...] = jnp.full_like(m_ref, NEG)
    l_ref[...] = jnp.zeros_like(l_ref)

    def score_chain(hh, c, tile, bias_tile, s_ref, cm_ref):
        kblk = k_ref[pl.ds(pl.multiple_of(tile * tk, tk), tk), lanes(hh)]
        s = jnp.dot(kblk, qz_ref[hh, c], preferred_element_type=F32)
        if bias_tile is not None:
            s = s + b2_ref[hh, bias_tile]
        s_ref[hh, c] = s
        cm_ref[hh, c] = jnp.max(s, axis=0, keepdims=True)

    def accumulate_chain(hh, c, tile, s_ref, cm_ref):
        vblk = v_ref[pl.ds(pl.multiple_of(tile * tk, tk), tk), lanes(hh)]
        m_old = m_ref[hh, c]
        m_new = jnp.maximum(m_old, cm_ref[hh, c])
        rescale = jnp.exp2(m_old - m_new)
        p = jnp.exp2(s_ref[hh, c] - m_new)
        l_ref[hh, c] = rescale * l_ref[hh, c] + jnp.sum(p, axis=0, keepdims=True)
        pv = lax.dot_general(vblk, p.astype(BF16), (((0,), (0,)), ((), ())), preferred_element_type=F32)
        acc_ref[hh, c] = rescale * acc_ref[hh, c] + pv
        m_ref[hh, c] = m_new

    def stage(score_args=None, acc_args=None):
        for c in range(2):
            for hh in heads:
                if score_args is not None:
                    score_chain(hh, c, *score_args)
                if acc_args is not None:
                    accumulate_chain(hh, c, *acc_args)

    n_items = i + 1
    tile_of = lambda item: jnp.maximum(i - item, 0)
    slot_a, slot_b = (sa_ref, cma_ref), (sb_ref, cmb_ref)
    stage(score_args=(i, 1, *slot_a))

    @pl.when(n_items >= 2)
    def _():
        stage((i - 1, 0, *slot_b), (i, *slot_a))
        stage((tile_of(2), None, *slot_a), (i - 1, *slot_b))

    def pair(t):
        stage((i - (2 * t + 1), None, *slot_b), (i - 2 * t, *slot_a))
        stage((tile_of(2 * t + 2), None, *slot_a), (i - (2 * t + 1), *slot_b))

    far_pairs = jnp.maximum(n_items // 2 - 1, 0)

    def two_pairs(u, carry):
        pair(2 * u + 1)
        pair(2 * u + 2)
        return carry

    lax.fori_loop(0, far_pairs // 2, two_pairs, 0)

    @pl.when(far_pairs % 2 == 1)
    def _():
        pair(far_pairs)

    @pl.when(n_items % 2 == 1)
    def _():
        stage(acc_args=(0, *slot_a))

    lam = (jnp.exp(jnp.sum(lq1_ref[...] * lk1_ref[...], axis=1, keepdims=True))
           - jnp.exp(jnp.sum(lq2_ref[...] * lk2_ref[...], axis=1, keepdims=True)) + LAMBDA_INIT)
    for hh in heads:
        o = acc_ref[hh, 0] / l_ref[hh, 0] - lam * (acc_ref[hh, 1] / l_ref[hh, 1])
        ms = jnp.mean(o * o, axis=0, keepdims=True)
        o = o * lax.rsqrt(ms + LN_EPS) * sw_ref[...] * (1.0 - LAMBDA_INIT)
        o_ref[:, lanes(hh)] = o.T.astype(o_ref.dtype)


def _diff_attention(proj, bias2, lq1, lk1, lq2, lk2, subln_w, *, q_col, k_col, v_col, tq, tk):
    s = proj.shape[0]
    width = 2 * DH_B
    hps = HEADS_PER_STEP
    wide = hps * width
    assert width == V7X_LANES and tq == tk and HB % hps == 0
    assert q_col % wide == 0 and k_col % wide == 0 and v_col % wide == 0
    qb, kb, vb = q_col // wide, k_col // wide, v_col // wide
    blocks = [((tq, wide), BF16), ((s, wide), BF16), ((s, wide), BF16),
              ((hps, 2, tk, tq), F32), ((tq, wide), BF16)]
    scores_buf = ((hps, 2, tk, tq), F32)
    stats_buf = ((hps, 2, 1, tq), F32)
    scratch = [((hps, 2, width, tq), BF16), scores_buf, scores_buf, stats_buf, stats_buf,
               ((hps, 2, width, tq), F32), stats_buf, stats_buf]
    padded = [((hps, 2, 8, tq), F32) if sd == stats_buf else sd for sd in scratch]
    vec = lambda v: v.astype(F32).reshape(1, DH_B)
    small = pl.BlockSpec((1, DH_B), lambda h, i: (0, 0))
    return pl.pallas_call(
        functools.partial(_diff_body, tq=tq, tk=tk),
        grid=(HB // hps, s // tq),
        in_specs=[pl.BlockSpec((tq, wide), lambda h, i: (i, qb + h)),
                  pl.BlockSpec((s, wide), lambda h, i: (0, kb + h)),
                  pl.BlockSpec((s, wide), lambda h, i: (0, vb + h)),
                  pl.BlockSpec((hps, 2, tk, tq), lambda h, i: (h, 0, 0, 0)),
                  small, small, small, small,
                  pl.BlockSpec((width, 1), lambda h, i: (0, 0))],
        out_specs=pl.BlockSpec((tq, wide), lambda h, i: (i, h)),
        out_shape=jax.ShapeDtypeStruct((s, HB * width), BF16),
        scratch_shapes=[pltpu.VMEM(sh, dt) for sh, dt in scratch],
        compiler_params=_params(("arbitrary", "arbitrary"), _vmem_limit(blocks, padded)),
        name="diff_attention",
    )(proj, proj, proj, bias2, vec(lq1), vec(lk1), vec(lq2), vec(lk2),
      subln_w.astype(F32).reshape(width, 1))


def _cross_block_body(h_ref, wq_ref, kv_ref, wo_ref, g_ref, b_ref, o_ref, ob_ref, wq_bf_ref, wo_bf_ref):
    @pl.when(pl.program_id(0) == 0)
    def _():
        wq_bf_ref[...] = wq_ref[...].astype(BF16)
        wo_bf_ref[...] = wo_ref[...].astype(BF16)

    scale = DH_C ** -0.5
    halves = _row_chunks(h_ref.shape[0], 2)
    hs = [h_ref[rows, :] for rows in halves]
    qs = [jnp.dot(h.astype(BF16), wq_bf_ref[...], preferred_element_type=F32).astype(BF16) for h in hs]
    ss = [[lax.dot_general(q[:, hd * DH_C:(hd + 1) * DH_C], kv_ref[:, hd * DH_C:(hd + 1) * DH_C],
                           (((1,), (1,)), ((), ())), preferred_element_type=F32) * scale
           for hd in range(HC)] for q in qs]
    ocs = []
    for s_heads in ss:
        outs = []
        for hd, s in enumerate(s_heads):
            vh = kv_ref[:, (HC + hd) * DH_C:(HC + hd + 1) * DH_C]
            mx = jnp.max(s, axis=1, keepdims=True)
            p = jnp.exp(s - mx)
            den = jnp.sum(p, axis=1, keepdims=True)
            outs.append((jnp.dot(p.astype(BF16), vh, preferred_element_type=F32) / den).astype(BF16))
        ocs.append(jnp.concatenate(outs, axis=1))
    ys = [ALPHA * h + jnp.dot(oc, wo_bf_ref[...], preferred_element_type=F32) for h, oc in zip(hs, ocs)]
    for rows, y in zip(halves, ys):
        out = _layer_norm_rows(y, g_ref, b_ref)
        o_ref[rows, :] = out
        ob_ref[rows, :] = out.astype(BF16)


def _cross_attention_block(h, kv, w_cq, w_co, gain, bias, *, tm):
    s, d = h.shape
    mlen, w = kv.shape[0], HC * DH_C
    row_block = pl.BlockSpec((tm, d), lambda i: (i, 0))
    whole = lambda shape: pl.BlockSpec(shape, lambda i: (0,) * len(shape))
    blocks = [((tm, d), F32), ((d, w), F32), ((mlen, 2 * w), BF16), ((w, d), F32), ((tm, d), F32), ((tm, d), BF16)]
    scratch = [((d, w), BF16), ((w, d), BF16)]
    return pl.pallas_call(
        _cross_block_body,
        grid=(s // tm,),
        in_specs=[row_block, whole((d, w)), whole((mlen, 2 * w)), whole((w, d)), whole((1, d)), whole((1, d))],
        out_specs=[row_block, row_block],
        out_shape=[jax.ShapeDtypeStruct((s, d), F32), jax.ShapeDtypeStruct((s, d), BF16)],
        scratch_shapes=[pltpu.VMEM(sh, dt) for sh, dt in scratch],
        compiler_params=_params(("arbitrary",), _vmem_limit(blocks, scratch)),
        name="memory_cross_attention_block",
    )(h, w_cq, kv, w_co, gain.reshape(1, d), bias.reshape(1, d))


def kernel(x, mem, rel_bias_table, w_in, sinks, lambda_q1, lambda_k1, lambda_q2, lambda_k2, subln_w,
           w_branch_a, w_branch_b, w_o, ln1_g, ln1_b, w_cq, w_mem_kv, w_co, ln2_g, ln2_b,
           w_gate_up, w_down, ln3_g, ln3_b):
    b, s, d = x.shape
    assert b == 1 and w_in.shape[0] == DEPTH == 1
    (w_in, sinks, lambda_q1, lambda_k1, lambda_q2, lambda_k2, subln_w, w_branch_a, w_branch_b, w_o, ln1_g, ln1_b,
     w_cq, w_mem_kv, w_co, ln2_g, ln2_b, w_gate_up, w_down, ln3_g, ln3_b) = [
        p.reshape(p.shape[1:]) for p in (
            w_in, sinks, lambda_q1, lambda_k1, lambda_q2, lambda_k2, subln_w, w_branch_a, w_branch_b, w_o, ln1_g,
            ln1_b, w_cq, w_mem_kv, w_co, ln2_g, ln2_b, w_gate_up, w_down, ln3_g, ln3_b)]
    qa_w, kva_w, qb_w = HA * DH_A, KV_A * DH_A, HB * 2 * DH_B
    col_qa, col_ka, col_va = 0, qa_w, qa_w + kva_w
    col_qb = col_va + kva_w
    col_kb, col_vb = col_qb + qb_w, col_qb + 2 * qb_w
    col_ga = col_vb + qb_w
    col_gb = col_ga + d
    tq, tk = TILES["diff_attention"]["tq"], TILES["diff_attention"]["tk"]

    h0 = x.reshape(s, d)

    proj = _matmul(h0, w_in, **TILES["in_proj"], out_dtype=BF16, name="in_proj")

    i_a = np.arange(BLK)[:, None]
    j_a = np.arange(2 * BLK)[None, :]
    dist_a = BLK + i_a - j_a
    band_a = (dist_a >= 0) & (dist_a < WINDOW)
    bias_a = _bias_expand(rel_bias_table, [dist_a, dist_a], [band_a & (j_a >= BLK), band_a], head0=0, heads=HA,
                          shift_last_bucket=False, scale=LOG2E)
    o_a = _swa_attention(proj, sinks, bias_a, q_col=col_qa, k_col=col_ka, v_col=col_va)

    bias2 = _band_bias_tiles(rel_bias_table, head0=HA, heads=HB, tile=tq, tile_offsets=(tk, 0), scale=LOG2E)
    o_b = _diff_attention(proj, bias2, lambda_q1, lambda_k1, lambda_q2, lambda_k2, subln_w,
                          q_col=col_qb, k_col=col_kb, v_col=col_vb, tq=tq, tk=tk)

    mix = _gated_branches(o_a, o_b, w_branch_a, w_branch_b, proj, col_ga, col_gb, **TILES["gated_branches"])
    h1 = _matmul_resident_ln(mix, w_o, h0, ln1_g, ln1_b, **TILES["out_proj_ln"])

    kvm = _matmul(mem.reshape(mem.shape[1], d), w_mem_kv, tm=mem.shape[1], tn=HC * DH_C,
                  out_dtype=BF16, name="mem_kv")
    h2, h2b = _cross_attention_block(h1, kvm, w_cq, w_co, ln2_g, ln2_b, **TILES["cross_block"])

    act, w_down_bf = _swiglu_up(h2b, w_gate_up, w_down, **TILES["swiglu_up"])
    h3 = _matmul_residual_ln(act, w_down_bf, h2, ln3_g, ln3_b, **TILES["down_proj_ln"], emit_bf16=False)
    return h3.reshape(b, s, d)
```

```python
import functools
import math

import numpy as np
import jax
import jax.numpy as jnp
from jax import lax
from jax.experimental import pallas as pl
from jax.experimental.pallas import tpu as pltpu

F32 = jnp.float32
BF16 = jnp.bfloat16

BLK = 128
WINDOW = 128
HA, KV_A, DH_A = 16, 2, 64
G_A = HA // KV_A
HB, DH_B = 8, 64
N_BUCKETS, MAX_DISTANCE = 32, 128
HC, DH_C = 4, 128
LN_EPS = 1e-5
DEPTH = 1
ALPHA = (2 * DEPTH) ** 0.25
LAMBDA_INIT = 0.8 - 0.6 * math.exp(-0.3 * 0)
LOG2E = math.log2(math.e)

V7X_LANES = 128
V7X_VMEM_BYTES = 64 * 1024 * 1024
V7X_VMEM_TEMP_BYTES = 12 * 1024 * 1024

EPILOGUE_CHUNKS = 4

TILES = dict(
    in_proj=dict(tm=1024, tn=768),
    diff_attention=dict(tq=512, tk=512),
    gated_branches=dict(tm=2048, tn=256),
    out_proj_ln=dict(tm=512),
    cross_block=dict(tm=512),
    swiglu_up=dict(tm=2048, tn=256),
    down_proj_ln=dict(tm=1024, tk=512),
)

NEG = -1e30


def _nbytes(shape, dtype):
    return int(np.prod(shape)) * jnp.dtype(dtype).itemsize


def _vmem_limit(pipelined, scratch=()):
    need = 2 * sum(_nbytes(s, d) for s, d in pipelined) + sum(_nbytes(s, d) for s, d in scratch)
    need += V7X_VMEM_TEMP_BYTES
    assert need <= V7X_VMEM_BYTES - 4 * 1024 * 1024, need
    return need


def _params(semantics, vmem):
    return pltpu.CompilerParams(dimension_semantics=semantics, vmem_limit_bytes=vmem)


def _act_tile(a_ref, abf_ref):
    if abf_ref is None:
        return a_ref[...]

    @pl.when(pl.program_id(1) == 0)
    def _():
        abf_ref[...] = a_ref[...].astype(BF16)

    return abf_ref[...]


def _mm_body(a_ref, w_ref, o_ref, *scratch):
    a = _act_tile(a_ref, scratch[0] if scratch else None)
    o_ref[...] = jnp.dot(a, w_ref[...].astype(BF16), preferred_element_type=F32).astype(o_ref.dtype)


def _matmul(a, w, *, tm, tn, out_dtype, name):
    m, k = a.shape
    n = w.shape[1]
    cast = a.dtype != BF16
    scratch = [((tm, k), BF16)] if cast else []
    blocks = [((tm, k), a.dtype), ((k, tn), w.dtype), ((tm, tn), out_dtype)]
    return pl.pallas_call(
        _mm_body,
        grid=(m // tm, n // tn),
        in_specs=[pl.BlockSpec((tm, k), lambda i, j: (i, 0)),
                  pl.BlockSpec((k, tn), lambda i, j: (0, j))],
        out_specs=pl.BlockSpec((tm, tn), lambda i, j: (i, j)),
        out_shape=jax.ShapeDtypeStruct((m, n), out_dtype),
        scratch_shapes=[pltpu.VMEM(s, d) for s, d in scratch],
        compiler_params=_params(("arbitrary", "arbitrary"), _vmem_limit(blocks, scratch)),
        name=name,
    )(a, w)


def _row_chunks(rows, chunks):
    size = rows // chunks
    return [slice(c * size, (c + 1) * size) for c in range(chunks)]


def _branch_body(oa_ref, ob_ref, wa_ref, wb_ref, ga_ref, gb_ref, o_ref):
    wa = wa_ref[...].astype(BF16)
    wb = wb_ref[...].astype(BF16)
    for rows in _row_chunks(o_ref.shape[0], EPILOGUE_CHUNKS):
        ya = jnp.dot(oa_ref[rows, :], wa, preferred_element_type=F32)
        yb = jnp.dot(ob_ref[rows, :], wb, preferred_element_type=F32)
        ga = jax.nn.sigmoid(ga_ref[rows, :].astype(F32))
        gb = jax.nn.sigmoid(gb_ref[rows, :].astype(F32))
        o_ref[rows, :] = (ga * ya + gb * yb).astype(o_ref.dtype)


def _gated_branches(o_a, o_b, w_a, w_b, proj, ga_col, gb_col, *, tm, tn):
    m, ka = o_a.shape
    kb = o_b.shape[1]
    n = w_a.shape[1]
    ga_blk, gb_blk = ga_col // tn, gb_col // tn
    blocks = [((tm, ka), BF16), ((tm, kb), BF16), ((ka, tn), F32), ((kb, tn), F32),
              ((tm, tn), proj.dtype), ((tm, tn), proj.dtype), ((tm, tn), BF16)]
    return pl.pallas_call(
        _branch_body,
        grid=(m // tm, n // tn),
        in_specs=[pl.BlockSpec((tm, ka), lambda i, j: (i, 0)),
                  pl.BlockSpec((tm, kb), lambda i, j: (i, 0)),
                  pl.BlockSpec((ka, tn), lambda i, j: (0, j)),
                  pl.BlockSpec((kb, tn), lambda i, j: (0, j)),
                  pl.BlockSpec((tm, tn), lambda i, j: (i, ga_blk + j)),
                  pl.BlockSpec((tm, tn), lambda i, j: (i, gb_blk + j))],
        out_specs=pl.BlockSpec((tm, tn), lambda i, j: (i, j)),
        out_shape=jax.ShapeDtypeStruct((m, n), BF16),
        compiler_params=_params(("arbitrary", "arbitrary"), _vmem_limit(blocks)),
        name="gated_branches",
    )(o_a, o_b, w_a, w_b, proj, proj)


def _swiglu_body(a_ref, wg_ref, wu_ref, wd_ref, o_ref, wd_bf_ref):
    a = a_ref[...]
    wg = wg_ref[...].astype(BF16)
    wu = wu_ref[...].astype(BF16)
    for rows in _row_chunks(o_ref.shape[0], 2):
        g = jnp.dot(a[rows, :], wg, preferred_element_type=F32)
        u = jnp.dot(a[rows, :], wu, preferred_element_type=F32)
        o_ref[rows, :] = (g * jax.nn.sigmoid(g) * u).astype(o_ref.dtype)
    wd_bf_ref[...] = wd_ref[...].astype(BF16)


def _swiglu_up(a, w_gate_up, w_down, *, tm, tn):
    m, k = a.shape
    d_ff = w_gate_up.shape[1] // 2
    row_tiles, col_tiles = m // tm, d_ff // tn
    up_blk = d_ff // tn
    kd, nd = w_down.shape
    slab = kd // (row_tiles * col_tiles)
    assert slab * row_tiles * col_tiles == kd and slab % 16 == 0 and a.dtype == BF16
    slab_block = pl.BlockSpec((slab, nd), lambda i, j: (i * col_tiles + j, 0))
    blocks = [((tm, k), BF16), ((k, tn), F32), ((k, tn), F32), ((tm, tn), BF16), ((slab, nd), F32), ((slab, nd), BF16)]
    return pl.pallas_call(
        _swiglu_body,
        grid=(row_tiles, col_tiles),
        in_specs=[pl.BlockSpec((tm, k), lambda i, j: (i, 0)),
                  pl.BlockSpec((k, tn), lambda i, j: (0, j)),
                  pl.BlockSpec((k, tn), lambda i, j: (0, up_blk + j)),
                  slab_block],
        out_specs=[pl.BlockSpec((tm, tn), lambda i, j: (i, j)), slab_block],
        out_shape=[jax.ShapeDtypeStruct((m, d_ff), BF16), jax.ShapeDtypeStruct((kd, nd), BF16)],
        compiler_params=_params(("arbitrary", "arbitrary"), _vmem_limit(blocks)),
        name="swiglu_up",
    )(a, w_gate_up, w_gate_up, w_down)


def _layer_norm_rows(y, g_ref, b_ref):
    mu = jnp.mean(y, axis=-1, keepdims=True)
    yc = y - mu
    var = jnp.mean(yc * yc, axis=-1, keepdims=True)
    return yc * lax.rsqrt(var + LN_EPS) * g_ref[...] + b_ref[...]


def _resident_ln_body(a_ref, w_ref, r_ref, g_ref, b_ref, o_ref, wbf_ref):
    @pl.when(pl.program_id(0) == 0)
    def _():
        wbf_ref[...] = w_ref[...].astype(BF16)

    halves = _row_chunks(o_ref.shape[0], 2)
    ys = [ALPHA * r_ref[rows, :] + jnp.dot(a_ref[rows, :], wbf_ref[...], preferred_element_type=F32)
          for rows in halves]
    for rows, y in zip(halves, ys):
        o_ref[rows, :] = _layer_norm_rows(y, g_ref, b_ref)


def _matmul_resident_ln(a, w, resid, gain, bias, *, tm):
    m, k = a.shape
    n = w.shape[1]
    row_block = pl.BlockSpec((tm, n), lambda i: (i, 0))
    blocks = [((tm, k), BF16), ((tm, n), F32), ((tm, n), F32)]
    single = [((k, n), F32), ((k, n), BF16)]
    return pl.pallas_call(
        _resident_ln_body,
        grid=(m // tm,),
        in_specs=[pl.BlockSpec((tm, k), lambda i: (i, 0)),
                  pl.BlockSpec((k, n), lambda i: (0, 0), pipeline_mode=pl.Buffered(1)),
                  row_block,
                  pl.BlockSpec((1, n), lambda i: (0, 0)),
                  pl.BlockSpec((1, n), lambda i: (0, 0))],
        out_specs=row_block,
        out_shape=jax.ShapeDtypeStruct((m, n), F32),
        scratch_shapes=[pltpu.VMEM((k, n), BF16)],
        compiler_params=_params(("arbitrary",), _vmem_limit(blocks, single)),
        name="matmul_resident_ln",
    )(a, w, resid, gain.reshape(1, n), bias.reshape(1, n))


def _mm_ln_body(a_ref, w_ref, r_ref, g_ref, b_ref, o_ref, *maybe_bf16_ref, nk):
    kk = pl.program_id(1)
    last = kk == nk - 1

    def partial_product(rows=slice(None)):
        return jnp.dot(a_ref[rows, :], w_ref[...].astype(BF16), preferred_element_type=F32)

    if nk > 1:
        @pl.when(kk == 0)
        def _():
            o_ref[...] = ALPHA * r_ref[...] + partial_product()

        @pl.when(jnp.logical_and(kk > 0, jnp.logical_not(last)))
        def _():
            o_ref[...] += partial_product()

    @pl.when(last)
    def _():
        for rows in _row_chunks(o_ref.shape[0], EPILOGUE_CHUNKS):
            seed = o_ref[rows, :] if nk > 1 else ALPHA * r_ref[rows, :]
            out = _layer_norm_rows(seed + partial_product(rows), g_ref, b_ref)
            o_ref[rows, :] = out
            for ob_ref in maybe_bf16_ref:
                ob_ref[rows, :] = out.astype(BF16)


def _matmul_residual_ln(a, w, resid, gain, bias, *, tm, tk, emit_bf16):
    m, k = a.shape
    n = w.shape[1]
    nk = k // tk
    row_block = pl.BlockSpec((tm, n), lambda i, kk: (i, 0))
    blocks = [((tm, tk), BF16), ((tk, n), w.dtype), ((tm, n), F32), ((tm, n), F32)]
    out_specs, out_shape = [row_block], [jax.ShapeDtypeStruct((m, n), F32)]
    if emit_bf16:
        blocks.append(((tm, n), BF16))
        out_specs.append(row_block)
        out_shape.append(jax.ShapeDtypeStruct((m, n), BF16))
    outs = pl.pallas_call(
        functools.partial(_mm_ln_body, nk=nk),
        grid=(m // tm, nk),
        in_specs=[pl.BlockSpec((tm, tk), lambda i, kk: (i, kk)),
                  pl.BlockSpec((tk, n), lambda i, kk: (kk, 0)),
                  row_block,
                  pl.BlockSpec((1, n), lambda i, kk: (0, 0)),
                  pl.BlockSpec((1, n), lambda i, kk: (0, 0))],
        out_specs=out_specs,
        out_shape=out_shape,
        compiler_params=_params(("arbitrary", "arbitrary"), _vmem_limit(blocks)),
        name="matmul_residual_ln",
    )(a, w, resid, gain.reshape(1, n), bias.reshape(1, n))
    return outs if emit_bf16 else outs[0]


def _rel_bucket_np(dist):
    n = np.maximum(dist, 0)
    exact = N_BUCKETS // 2
    logv = (np.log(np.maximum(n, 1).astype(np.float32) / exact) / math.log(MAX_DISTANCE / exact))
    large = exact + (logv.astype(np.float32) * (N_BUCKETS - exact)).astype(np.int32)
    large = np.minimum(large, N_BUCKETS - 1)
    return np.where(n < exact, n, large).astype(np.int32)


def _bias_expand_body(tab_ref, bucket_ref, o_ref, *, head0, shift_last_bucket, scale):
    h = head0 + pl.program_id(0)
    last = tab_ref[N_BUCKETS - 1, h] if shift_last_bucket else 0.0
    for t in range(bucket_ref.shape[0]):
        bucket = bucket_ref[t]
        acc = jnp.full(bucket.shape, NEG, F32)
        for bkt in range(N_BUCKETS):
            acc = jnp.where(bucket == bkt, (tab_ref[bkt, h] - last) * scale, acc)
        o_ref[t] = acc


def _bias_expand(table, dists, valids, *, head0, heads, shift_last_bucket, scale):
    bucket = np.stack([np.where(v, _rel_bucket_np(d), -1) for d, v in zip(dists, valids)]).astype(np.int32)
    nt, r, c = bucket.shape
    blocks = [((nt, r, c), jnp.int32), ((nt, r, c), F32)]
    return pl.pallas_call(
        functools.partial(_bias_expand_body, head0=head0, shift_last_bucket=shift_last_bucket,
                          scale=scale),
        grid=(heads,),
        in_specs=[pl.BlockSpec(memory_space=pltpu.SMEM),
                  pl.BlockSpec((nt, r, c), lambda h: (0, 0, 0))],
        out_specs=pl.BlockSpec((None, nt, r, c), lambda h: (h, 0, 0, 0)),
        out_shape=jax.ShapeDtypeStruct((heads, nt, r, c), F32),
        compiler_params=_params(("arbitrary",), _vmem_limit(blocks)),
        name="bias_expand",
    )(table.astype(F32), jnp.asarray(bucket))


def _swa_body(sink_ref, q_ref, kp_ref, kc_ref, vp_ref, vc_ref, bias_ref, o_ref):
    n = pl.program_id(0)
    half = DH_A
    pairs = G_A // 2
    rows = pairs * BLK
    lane_k = lax.broadcasted_iota(jnp.int32, (2 * BLK, 2 * half), 1)
    row = lax.broadcasted_iota(jnp.int32, (rows, 1), 0)
    lane_o = lax.broadcasted_iota(jnp.int32, (BLK, 2 * half), 1)
    variant = jnp.minimum(n, 1)

    kf = jnp.concatenate([kp_ref[...], kc_ref[...]], axis=0).astype(F32)
    vf = jnp.concatenate([vp_ref[...], vc_ref[...]], axis=0).astype(F32)
    kr = pltpu.roll(kf, half, 1)
    vr = pltpu.roll(vf, half, 1)

    outs = []
    for g in range(KV_A):
        k_own, k_other = (kf, kr) if g == 0 else (kr, kf)
        k_lo = jnp.where(lane_k < half, k_own, 0.0).astype(BF16)
        k_hi = jnp.where(lane_k >= half, k_other, 0.0).astype(BF16)
        v_dup = (jnp.where(lane_k < half, vf, vr) if g == 0 else jnp.where(lane_k < half, vr, vf)).astype(BF16)
        q_stack = jnp.concatenate(
            [q_ref[:, (g * pairs + t) * 2 * half:(g * pairs + t + 1) * 2 * half] for t in range(pairs)], axis=0)
        q_stack = (q_stack.astype(F32) * (DH_A ** -0.5 * LOG2E)).astype(BF16)
        o_par = []
        for par, k_sel in ((0, k_lo), (1, k_hi)):
            sink = jnp.zeros((rows, 1), F32)
            for t in range(pairs):
                sink = jnp.where(row >= t * BLK, sink_ref[2 * (g * pairs + t) + par] * LOG2E, sink)
            s = lax.dot_general(q_stack, k_sel, (((1,), (1,)), ((), ())),
                                preferred_element_type=F32) + bias_ref[variant, g, par]
            mx = jnp.maximum(jnp.max(s, axis=1, keepdims=True), sink)
            p = jnp.exp2(s - mx)
            den = jnp.sum(p, axis=1, keepdims=True) + jnp.exp2(sink - mx)
            o_par.append(jnp.dot(p.astype(BF16), v_dup, preferred_element_type=F32) / den)
        for t in range(pairs):
            outs.append(jnp.where(lane_o < half, o_par[0][t * BLK:(t + 1) * BLK], o_par[1][t * BLK:(t + 1) * BLK]))
    o_ref[...] = jnp.concatenate(outs, axis=1).astype(o_ref.dtype)


def _swa_attention(proj, sinks, bias, *, q_col, k_col, v_col):
    s = proj.shape[0]
    nb = s // BLK
    qw = HA * DH_A
    kvw = KV_A * DH_A
    assert kvw == V7X_LANES and q_col % qw == 0 and k_col % kvw == 0 and v_col % kvw == 0
    kb, vb = k_col // kvw, v_col // kvw
    pairs = G_A // 2
    bias = bias.reshape(KV_A, pairs, 2, 2, BLK, 2 * BLK).transpose(3, 0, 2, 1, 4, 5)
    bias = bias.reshape(2, KV_A, 2, pairs * BLK, 2 * BLK)
    blocks = [((BLK, qw), BF16)] + [((BLK, kvw), BF16)] * 4 + [(bias.shape, F32), ((BLK, qw), BF16)]
    prev = lambda n: jnp.maximum(n - 1, 0)
    return pl.pallas_call(
        _swa_body,
        grid=(nb,),
        in_specs=[pl.BlockSpec(memory_space=pltpu.SMEM),
                  pl.BlockSpec((BLK, qw), lambda n: (n, q_col // qw)),
                  pl.BlockSpec((BLK, kvw), lambda n: (prev(n), kb)),
                  pl.BlockSpec((BLK, kvw), lambda n: (n, kb)),
                  pl.BlockSpec((BLK, kvw), lambda n: (prev(n), vb)),
                  pl.BlockSpec((BLK, kvw), lambda n: (n, vb)),
                  pl.BlockSpec(bias.shape, lambda n: (0, 0, 0, 0, 0))],
        out_specs=pl.BlockSpec((BLK, qw), lambda n: (n, 0)),
        out_shape=jax.ShapeDtypeStruct((s, qw), BF16),
        compiler_params=_params(("arbitrary",), _vmem_limit(blocks)),
        name="swa_sink_attention",
    )(sinks.astype(F32), proj, proj, proj, proj, proj, bias)


def _band_bias_body(tab_ref, bucket_ref, o_ref, *, head0, scale, tile_offsets):
    h = head0 + pl.program_id(0)
    last = tab_ref[N_BUCKETS - 1, h]
    bands = []
    for t in range(2):
        bucket = bucket_ref[t]
        acc = jnp.full(bucket.shape, NEG, F32)
        for bkt in range(N_BUCKETS):
            acc = jnp.where(bucket == bkt, (tab_ref[bkt, h] - last) * scale, acc)
        bands.append(acc)
    sub = MAX_DISTANCE
    nb = o_ref.shape[1] // sub
    for t, base in enumerate(tile_offsets):
        for rb in range(nb):
            for cb in range(nb):
                off = base + sub * (cb - rb)
                if off < 0:
                    blk = jnp.full((sub, sub), NEG, F32)
                elif off < 2 * sub:
                    blk = bands[off // sub]
                else:
                    blk = jnp.zeros((sub, sub), F32)
                o_ref[t, rb * sub:(rb + 1) * sub, cb * sub:(cb + 1) * sub] = blk


def _band_bias_tiles(table, *, head0, heads, tile, tile_offsets, scale):
    sub = MAX_DISTANCE
    assert tile % sub == 0 and all(o % sub == 0 for o in tile_offsets)
    r = np.arange(sub)[:, None]
    c = np.arange(sub)[None, :]
    bucket = np.stack([np.where(c - r + o >= 0, _rel_bucket_np(c - r + o), -1) for o in (0, sub)]).astype(np.int32)
    nt = len(tile_offsets)
    blocks = [((2, sub, sub), jnp.int32), ((nt, tile, tile), F32)]
    return pl.pallas_call(
        functools.partial(_band_bias_body, head0=head0, scale=scale, tile_offsets=tuple(tile_offsets)),
        grid=(heads,),
        in_specs=[pl.BlockSpec(memory_space=pltpu.SMEM),
                  pl.BlockSpec((2, sub, sub), lambda h: (0, 0, 0))],
        out_specs=pl.BlockSpec((None, nt, tile, tile), lambda h: (h, 0, 0, 0)),
        out_shape=jax.ShapeDtypeStruct((heads, nt, tile, tile), F32),
        compiler_params=_params(("arbitrary",), _vmem_limit(blocks)),
        name="band_bias_tiles",
    )(table.astype(F32), jnp.asarray(bucket))


HEADS_PER_STEP = 2


def _diff_body(q_ref, k_ref, v_ref, b2_ref, lq1_ref, lk1_ref, lq2_ref, lk2_ref, sw_ref,
               o_ref, qz_ref, sa_ref, sb_ref, cma_ref, cmb_ref, acc_ref, m_ref, l_ref, *, tq, tk):
    i = pl.program_id(1)
    width = 2 * DH_B
    heads = range(HEADS_PER_STEP)
    lanes = lambda hh: slice(hh * width, (hh + 1) * width)

    row = lax.broadcasted_iota(jnp.int32, (width, tq), 0)
    for hh in heads:
        qt = (q_ref[:, lanes(hh)].astype(F32) * (DH_B ** -0.5 * LOG2E)).T
        qz_ref[hh, 0] = jnp.where(row < DH_B, qt, 0.0).astype(BF16)
        qz_ref[hh, 1] = jnp.where(row >= DH_B, qt, 0.0).astype(BF16)
    def score_chain(hh, c, tile, bias_tile, s_ref, cm_ref):
        kblk = k_ref[pl.ds(pl.multiple_of(tile * tk, tk), tk), lanes(hh)]
        s = jnp.dot(kblk, qz_ref[hh, c], preferred_element_type=F32)
        if bias_tile is not None:
            s = s + b2_ref[hh, bias_tile]
        s_ref[hh, c] = s
        cm_ref[hh, c] = jnp.max(s, axis=0, keepdims=True)

    def accumulate_chain(hh, c, tile, s_ref, cm_ref, first):
        vblk = v_ref[pl.ds(pl.multiple_of(tile * tk, tk), tk), lanes(hh)]
        m_new = cm_ref[hh, c] if first else jnp.maximum(m_ref[hh, c], cm_ref[hh, c])
        p = jnp.exp2(s_ref[hh, c] - m_new)
        l_new = jnp.sum(p, axis=0, keepdims=True)
        pv = lax.dot_general(vblk, p.astype(BF16), (((0,), (0,)), ((), ())), preferred_element_type=F32)
        if first:
            l_ref[hh, c] = l_new
            acc_ref[hh, c] = pv
        else:
            rescale = jnp.exp2(m_ref[hh, c] - m_new)
            l_ref[hh, c] = rescale * l_ref[hh, c] + l_new
            acc_ref[hh, c] = rescale * acc_ref[hh, c] + pv
        m_ref[hh, c] = m_new

    def stage(score_args=None, acc_args=None, first=False):
        for c in range(2):
            for hh in heads:
                if score_args is not None:
                    score_chain(hh, c, *score_args)
                if acc_args is not None:
                    accumulate_chain(hh, c, *acc_args, first)

    n_items = i + 1
    tile_of = lambda item: jnp.maximum(i - item, 0)
    slot_a, slot_b = (sa_ref, cma_ref), (sb_ref, cmb_ref)
    stage(score_args=(i, 1, *slot_a))

    @pl.when(n_items >= 2)
    def _():
        stage((i - 1, 0, *slot_b), (i, *slot_a), first=True)
        stage((tile_of(2), None, *slot_a), (i - 1, *slot_b))

    def pair(t):
        stage((i - (2 * t + 1), None, *slot_b), (i - 2 * t, *slot_a))
        stage((tile_of(2 * t + 2), None, *slot_a), (i - (2 * t + 1), *slot_b))

    far_pairs = jnp.maximum(n_items // 2 - 1, 0)

    def two_pairs(u, carry):
        pair(2 * u + 1)
        pair(2 * u + 2)
        return carry

    lax.fori_loop(0, far_pairs // 2, two_pairs, 0)

    @pl.when(far_pairs % 2 == 1)
    def _():
        pair(far_pairs)

    @pl.when(n_items == 1)
    def _():
        stage(acc_args=(0, *slot_a), first=True)

    @pl.when(jnp.logical_and(n_items % 2 == 1, n_items > 1))
    def _():
        stage(acc_args=(0, *slot_a))

    lam = (jnp.exp(jnp.sum(lq1_ref[...] * lk1_ref[...], axis=1, keepdims=True))
           - jnp.exp(jnp.sum(lq2_ref[...] * lk2_ref[...], axis=1, keepdims=True)) + LAMBDA_INIT)
    for hh in heads:
        o = acc_ref[hh, 0] / l_ref[hh, 0] - lam * (acc_ref[hh, 1] / l_ref[hh, 1])
        ms = jnp.mean(o * o, axis=0, keepdims=True)
        o = o * lax.rsqrt(ms + LN_EPS) * sw_ref[...] * (1.0 - LAMBDA_INIT)
        o_ref[:, lanes(hh)] = o.T.astype(o_ref.dtype)


def _diff_attention(proj, bias2, lq1, lk1, lq2, lk2, subln_w, *, q_col, k_col, v_col, tq, tk):
    s = proj.shape[0]
    width = 2 * DH_B
    hps = HEADS_PER_STEP
    wide = hps * width
    assert width == V7X_LANES and tq == tk and HB % hps == 0
    assert q_col % wide == 0 and k_col % wide == 0 and v_col % wide == 0
    qb, kb, vb = q_col // wide, k_col // wide, v_col // wide
    blocks = [((tq, wide), BF16), ((s, wide), BF16), ((s, wide), BF16),
              ((hps, 2, tk, tq), F32), ((tq, wide), BF16)]
    scores_buf = ((hps, 2, tk, tq), F32)
    stats_buf = ((hps, 2, 1, tq), F32)
    scratch = [((hps, 2, width, tq), BF16), scores_buf, scores_buf, stats_buf, stats_buf,
               ((hps, 2, width, tq), F32), stats_buf, stats_buf]
    padded = [((hps, 2, 8, tq), F32) if sd == stats_buf else sd for sd in scratch]
    vec = lambda v: v.astype(F32).reshape(1, DH_B)
    small = pl.BlockSpec((1, DH_B), lambda h, i: (0, 0))
    return pl.pallas_call(
        functools.partial(_diff_body, tq=tq, tk=tk),
        grid=(HB // hps, s // tq),
        in_specs=[pl.BlockSpec((tq, wide), lambda h, i: (i, qb + h)),
                  pl.BlockSpec((s, wide), lambda h, i: (0, kb + h)),
                  pl.BlockSpec((s, wide), lambda h, i: (0, vb + h)),
                  pl.BlockSpec((hps, 2, tk, tq), lambda h, i: (h, 0, 0, 0)),
                  small, small, small, small,
                  pl.BlockSpec((width, 1), lambda h, i: (0, 0))],
        out_specs=pl.BlockSpec((tq, wide), lambda h, i: (i, h)),
        out_shape=jax.ShapeDtypeStruct((s, HB * width), BF16),
        scratch_shapes=[pltpu.VMEM(sh, dt) for sh, dt in scratch],
        compiler_params=_params(("arbitrary", "arbitrary"), _vmem_limit(blocks, padded)),
        name="diff_attention",
    )(proj, proj, proj, bias2, vec(lq1), vec(lk1), vec(lq2), vec(lk2),
      subln_w.astype(F32).reshape(width, 1))


def _cross_block_body(h_ref, wq_ref, kv_ref, wo_ref, g_ref, b_ref, o_ref, ob_ref, wq_bf_ref, wo_bf_ref):
    @pl.when(pl.program_id(0) == 0)
    def _():
        wq_bf_ref[...] = wq_ref[...].astype(BF16)
        wo_bf_ref[...] = wo_ref[...].astype(BF16)

    scale = DH_C ** -0.5
    halves = _row_chunks(h_ref.shape[0], 2)
    hs = [h_ref[rows, :] for rows in halves]
    qs = [jnp.dot(h.astype(BF16), wq_bf_ref[...], preferred_element_type=F32).astype(BF16) for h in hs]
    ss = [[lax.dot_general(q[:, hd * DH_C:(hd + 1) * DH_C], kv_ref[:, hd * DH_C:(hd + 1) * DH_C],
                           (((1,), (1,)), ((), ())), preferred_element_type=F32) * scale
           for hd in range(HC)] for q in qs]
    ocs = []
    for s_heads in ss:
        outs = []
        for hd, s in enumerate(s_heads):
            vh = kv_ref[:, (HC + hd) * DH_C:(HC + hd + 1) * DH_C]
            mx = jnp.max(s, axis=1, keepdims=True)
            p = jnp.exp(s - mx)
            den = jnp.sum(p, axis=1, keepdims=True)
            outs.append((jnp.dot(p.astype(BF16), vh, preferred_element_type=F32) / den).astype(BF16))
        ocs.append(jnp.concatenate(outs, axis=1))
    ys = [ALPHA * h + jnp.dot(oc, wo_bf_ref[...], preferred_element_type=F32) for h, oc in zip(hs, ocs)]
    for rows, y in zip(halves, ys):
        out = _layer_norm_rows(y, g_ref, b_ref)
        o_ref[rows, :] = out
        ob_ref[rows, :] = out.astype(BF16)


def _cross_attention_block(h, kv, w_cq, w_co, gain, bias, *, tm):
    s, d = h.shape
    mlen, w = kv.shape[0], HC * DH_C
    row_block = pl.BlockSpec((tm, d), lambda i: (i, 0))
    whole = lambda shape: pl.BlockSpec(shape, lambda i: (0,) * len(shape))
    blocks = [((tm, d), F32), ((d, w), F32), ((mlen, 2 * w), BF16), ((w, d), F32), ((tm, d), F32), ((tm, d), BF16)]
    scratch = [((d, w), BF16), ((w, d), BF16)]
    return pl.pallas_call(
        _cross_block_body,
        grid=(s // tm,),
        in_specs=[row_block, whole((d, w)), whole((mlen, 2 * w)), whole((w, d)), whole((1, d)), whole((1, d))],
        out_specs=[row_block, row_block],
        out_shape=[jax.ShapeDtypeStruct((s, d), F32), jax.ShapeDtypeStruct((s, d), BF16)],
        scratch_shapes=[pltpu.VMEM(sh, dt) for sh, dt in scratch],
        compiler_params=_params(("arbitrary",), _vmem_limit(blocks, scratch)),
        name="memory_cross_attention_block",
    )(h, w_cq, kv, w_co, gain.reshape(1, d), bias.reshape(1, d))


def kernel(x, mem, rel_bias_table, w_in, sinks, lambda_q1, lambda_k1, lambda_q2, lambda_k2, subln_w,
           w_branch_a, w_branch_b, w_o, ln1_g, ln1_b, w_cq, w_mem_kv, w_co, ln2_g, ln2_b,
           w_gate_up, w_down, ln3_g, ln3_b):
    b, s, d = x.shape
    assert b == 1 and w_in.shape[0] == DEPTH == 1
    (w_in, sinks, lambda_q1, lambda_k1, lambda_q2, lambda_k2, subln_w, w_branch_a, w_branch_b, w_o, ln1_g, ln1_b,
     w_cq, w_mem_kv, w_co, ln2_g, ln2_b, w_gate_up, w_down, ln3_g, ln3_b) = [
        p.reshape(p.shape[1:]) for p in (
            w_in, sinks, lambda_q1, lambda_k1, lambda_q2, lambda_k2, subln_w, w_branch_a, w_branch_b, w_o, ln1_g,
            ln1_b, w_cq, w_mem_kv, w_co, ln2_g, ln2_b, w_gate_up, w_down, ln3_g, ln3_b)]
    qa_w, kva_w, qb_w = HA * DH_A, KV_A * DH_A, HB * 2 * DH_B
    col_qa, col_ka, col_va = 0, qa_w, qa_w + kva_w
    col_qb = col_va + kva_w
    col_kb, col_vb = col_qb + qb_w, col_qb + 2 * qb_w
    col_ga = col_vb + qb_w
    col_gb = col_ga + d
    tq, tk = TILES["diff_attention"]["tq"], TILES["diff_attention"]["tk"]

    h0 = x.reshape(s, d)

    proj = _matmul(h0, w_in, **TILES["in_proj"], out_dtype=BF16, name="in_proj")

    i_a = np.arange(BLK)[:, None]
    j_a = np.arange(2 * BLK)[None, :]
    dist_a = BLK + i_a - j_a
    band_a = (dist_a >= 0) & (dist_a < WINDOW)
    bias_a = _bias_expand(rel_bias_table, [dist_a, dist_a], [band_a & (j_a >= BLK), band_a], head0=0, heads=HA,
                          shift_last_bucket=False, scale=LOG2E)
    o_a = _swa_attention(proj, sinks, bias_a, q_col=col_qa, k_col=col_ka, v_col=col_va)

    bias2 = _band_bias_tiles(rel_bias_table, head0=HA, heads=HB, tile=tq, tile_offsets=(tk, 0), scale=LOG2E)
    o_b = _diff_attention(proj, bias2, lambda_q1, lambda_k1, lambda_q2, lambda_k2, subln_w,
                          q_col=col_qb, k_col=col_kb, v_col=col_vb, tq=tq, tk=tk)

    mix = _gated_branches(o_a, o_b, w_branch_a, w_branch_b, proj, col_ga, col_gb, **TILES["gated_branches"])
    h1 = _matmul_resident_ln(mix, w_o, h0, ln1_g, ln1_b, **TILES["out_proj_ln"])

    kvm = _matmul(mem.reshape(mem.shape[1], d), w_mem_kv, tm=mem.shape[1], tn=HC * DH_C,
                  out_dtype=BF16, name="mem_kv")
    h2, h2b = _cross_attention_block(h1, kvm, w_cq, w_co, ln2_g, ln2_b, **TILES["cross_block"])

    act, w_down_bf = _swiglu_up(h2b, w_gate_up, w_down, **TILES["swiglu_up"])
    h3 = _matmul_residual_ln(act, w_down_bf, h2, ln3_g, ln3_b, **TILES["down_proj_ln"], emit_bf16=False)
    return h3.reshape(b, s, d)
```

```python
import functools
import math

import numpy as np
import jax
import jax.numpy as jnp
from jax import lax
from jax.experimental import pallas as pl
from jax.experimental.pallas import tpu as pltpu

F32 = jnp.float32
BF16 = jnp.bfloat16

BLK = 128
WINDOW = 128
HA, KV_A, DH_A = 16, 2, 64
G_A = HA // KV_A
HB, DH_B = 8, 64
N_BUCKETS, MAX_DISTANCE = 32, 128
HC, DH_C = 4, 128
LN_EPS = 1e-5
DEPTH = 1
ALPHA = (2 * DEPTH) ** 0.25
LAMBDA_INIT = 0.8 - 0.6 * math.exp(-0.3 * 0)
LOG2E = math.log2(math.e)

V7X_LANES = 128
V7X_VMEM_BYTES = 64 * 1024 * 1024
V7X_VMEM_TEMP_BYTES = 12 * 1024 * 1024

EPILOGUE_CHUNKS = 4

TILES = dict(
    in_proj=dict(tm=1024, tn=768),
    diff_attention=dict(tq=512, tk=512),
    gated_branches=dict(tm=2048, tn=256),
    out_proj_ln=dict(tm=512),
    cross_block=dict(tm=512),
    swiglu_up=dict(tm=2048, tn=256),
    down_proj_ln=dict(tm=1024, tk=512),
)

NEG = -1e30


def _nbytes(shape, dtype):
    return int(np.prod(shape)) * jnp.dtype(dtype).itemsize


def _vmem_limit(pipelined, scratch=()):
    need = 2 * sum(_nbytes(s, d) for s, d in pipelined) + sum(_nbytes(s, d) for s, d in scratch)
    need += V7X_VMEM_TEMP_BYTES
    assert need <= V7X_VMEM_BYTES - 4 * 1024 * 1024, need
    return need


def _params(semantics, vmem):
    return pltpu.CompilerParams(dimension_semantics=semantics, vmem_limit_bytes=vmem)


def _act_tile(a_ref, abf_ref):
    if abf_ref is None:
        return a_ref[...]

    @pl.when(pl.program_id(1) == 0)
    def _():
        abf_ref[...] = a_ref[...].astype(BF16)

    return abf_ref[...]


def _mm_body(a_ref, w_ref, o_ref, *scratch):
    a = _act_tile(a_ref, scratch[0] if scratch else None)
    o_ref[...] = jnp.dot(a, w_ref[...].astype(BF16), preferred_element_type=F32).astype(o_ref.dtype)


def _matmul(a, w, *, tm, tn, out_dtype, name):
    m, k = a.shape
    n = w.shape[1]
    cast = a.dtype != BF16
    scratch = [((tm, k), BF16)] if cast else []
    blocks = [((tm, k), a.dtype), ((k, tn), w.dtype), ((tm, tn), out_dtype)]
    return pl.pallas_call(
        _mm_body,
        grid=(m // tm, n // tn),
        in_specs=[pl.BlockSpec((tm, k), lambda i, j: (i, 0)),
                  pl.BlockSpec((k, tn), lambda i, j: (0, j))],
        out_specs=pl.BlockSpec((tm, tn), lambda i, j: (i, j)),
        out_shape=jax.ShapeDtypeStruct((m, n), out_dtype),
        scratch_shapes=[pltpu.VMEM(s, d) for s, d in scratch],
        compiler_params=_params(("arbitrary", "arbitrary"), _vmem_limit(blocks, scratch)),
        name=name,
    )(a, w)


def _row_chunks(rows, chunks):
    size = rows // chunks
    return [slice(c * size, (c + 1) * size) for c in range(chunks)]


def _branch_body(oa_ref, ob_ref, wa_ref, wb_ref, ga_ref, gb_ref, o_ref):
    wa = wa_ref[...].astype(BF16)
    wb = wb_ref[...].astype(BF16)
    for rows in _row_chunks(o_ref.shape[0], EPILOGUE_CHUNKS):
        ya = jnp.dot(oa_ref[rows, :], wa, preferred_element_type=F32)
        yb = jnp.dot(ob_ref[rows, :], wb, preferred_element_type=F32)
        ga = jax.nn.sigmoid(ga_ref[rows, :].astype(F32))
        gb = jax.nn.sigmoid(gb_ref[rows, :].astype(F32))
        o_ref[rows, :] = (ga * ya + gb * yb).astype(o_ref.dtype)


def _gated_branches(o_a, o_b, w_a, w_b, proj, ga_col, gb_col, *, tm, tn):
    m, ka = o_a.shape
    kb = o_b.shape[1]
    n = w_a.shape[1]
    ga_blk, gb_blk = ga_col // tn, gb_col // tn
    blocks = [((tm, ka), BF16), ((tm, kb), BF16), ((ka, tn), F32), ((kb, tn), F32),
              ((tm, tn), proj.dtype), ((tm, tn), proj.dtype), ((tm, tn), BF16)]
    return pl.pallas_call(
        _branch_body,
        grid=(m // tm, n // tn),
        in_specs=[pl.BlockSpec((tm, ka), lambda i, j: (i, 0)),
                  pl.BlockSpec((tm, kb), lambda i, j: (i, 0)),
                  pl.BlockSpec((ka, tn), lambda i, j: (0, j)),
                  pl.BlockSpec((kb, tn), lambda i, j: (0, j)),
                  pl.BlockSpec((tm, tn), lambda i, j: (i, ga_blk + j)),
                  pl.BlockSpec((tm, tn), lambda i, j: (i, gb_blk + j))],
        out_specs=pl.BlockSpec((tm, tn), lambda i, j: (i, j)),
        out_shape=jax.ShapeDtypeStruct((m, n), BF16),
        compiler_params=_params(("arbitrary", "arbitrary"), _vmem_limit(blocks)),
        name="gated_branches",
    )(o_a, o_b, w_a, w_b, proj, proj)


def _swiglu_body(a_ref, wg_ref, wu_ref, wd_ref, o_ref, wd_bf_ref):
    a = a_ref[...]
    wg = wg_ref[...].astype(BF16)
    wu = wu_ref[...].astype(BF16)
    for rows in _row_chunks(o_ref.shape[0], 2):
        g = jnp.dot(a[rows, :], wg, preferred_element_type=F32)
        u = jnp.dot(a[rows, :], wu, preferred_element_type=F32)
        o_ref[rows, :] = (g * jax.nn.sigmoid(g) * u).astype(o_ref.dtype)
    wd_bf_ref[...] = wd_ref[...].astype(BF16)


def _swiglu_up(a, w_gate_up, w_down, *, tm, tn):
    m, k = a.shape
    d_ff = w_gate_up.shape[1] // 2
    row_tiles, col_tiles = m // tm, d_ff // tn
    up_blk = d_ff // tn
    kd, nd = w_down.shape
    slab = kd // (row_tiles * col_tiles)
    assert slab * row_tiles * col_tiles == kd and slab % 16 == 0 and a.dtype == BF16
    slab_block = pl.BlockSpec((slab, nd), lambda i, j: (i * col_tiles + j, 0))
    blocks = [((tm, k), BF16), ((k, tn), F32), ((k, tn), F32), ((tm, tn), BF16), ((slab, nd), F32), ((slab, nd), BF16)]
    return pl.pallas_call(
        _swiglu_body,
        grid=(row_tiles, col_tiles),
        in_specs=[pl.BlockSpec((tm, k), lambda i, j: (i, 0)),
                  pl.BlockSpec((k, tn), lambda i, j: (0, j)),
                  pl.BlockSpec((k, tn), lambda i, j: (0, up_blk + j)),
                  slab_block],
        out_specs=[pl.BlockSpec((tm, tn), lambda i, j: (i, j)), slab_block],
        out_shape=[jax.ShapeDtypeStruct((m, d_ff), BF16), jax.ShapeDtypeStruct((kd, nd), BF16)],
        compiler_params=_params(("arbitrary", "arbitrary"), _vmem_limit(blocks)),
        name="swiglu_up",
    )(a, w_gate_up, w_gate_up, w_down)


def _layer_norm_rows(y, g_ref, b_ref):
    mu = jnp.mean(y, axis=-1, keepdims=True)
    yc = y - mu
    var = jnp.mean(yc * yc, axis=-1, keepdims=True)
    return yc * lax.rsqrt(var + LN_EPS) * g_ref[...] + b_ref[...]


def _resident_ln_body(a_ref, w_ref, r_ref, g_ref, b_ref, o_ref, wbf_ref):
    @pl.when(pl.program_id(0) == 0)
    def _():
        wbf_ref[...] = w_ref[...].astype(BF16)

    halves = _row_chunks(o_ref.shape[0], 2)
    ys = [ALPHA * r_ref[rows, :] + jnp.dot(a_ref[rows, :], wbf_ref[...], preferred_element_type=F32)
          for rows in halves]
    for rows, y in zip(halves, ys):
        o_ref[rows, :] = _layer_norm_rows(y, g_ref, b_ref)


def _matmul_resident_ln(a, w, resid, gain, bias, *, tm):
    m, k = a.shape
    n = w.shape[1]
    row_block = pl.BlockSpec((tm, n), lambda i: (i, 0))
    blocks = [((tm, k), BF16), ((tm, n), F32), ((tm, n), F32)]
    single = [((k, n), F32), ((k, n), BF16)]
    return pl.pallas_call(
        _resident_ln_body,
        grid=(m // tm,),
        in_specs=[pl.BlockSpec((tm, k), lambda i: (i, 0)),
                  pl.BlockSpec((k, n), lambda i: (0, 0), pipeline_mode=pl.Buffered(1)),
                  row_block,
                  pl.BlockSpec((1, n), lambda i: (0, 0)),
                  pl.BlockSpec((1, n), lambda i: (0, 0))],
        out_specs=row_block,
        out_shape=jax.ShapeDtypeStruct((m, n), F32),
        scratch_shapes=[pltpu.VMEM((k, n), BF16)],
        compiler_params=_params(("arbitrary",), _vmem_limit(blocks, single)),
        name="matmul_resident_ln",
    )(a, w, resid, gain.reshape(1, n), bias.reshape(1, n))


def _mm_ln_body(a_ref, w_ref, r_ref, g_ref, b_ref, o_ref, *maybe_bf16_ref, nk):
    kk = pl.program_id(1)
    last = kk == nk - 1

    def partial_product(rows=slice(None)):
        return jnp.dot(a_ref[rows, :], w_ref[...].astype(BF16), preferred_element_type=F32)

    if nk > 1:
        @pl.when(kk == 0)
        def _():
            o_ref[...] = ALPHA * r_ref[...] + partial_product()

        @pl.when(jnp.logical_and(kk > 0, jnp.logical_not(last)))
        def _():
            o_ref[...] += partial_product()

    @pl.when(last)
    def _():
        for rows in _row_chunks(o_ref.shape[0], EPILOGUE_CHUNKS):
            seed = o_ref[rows, :] if nk > 1 else ALPHA * r_ref[rows, :]
            out = _layer_norm_rows(seed + partial_product(rows), g_ref, b_ref)
            o_ref[rows, :] = out
            for ob_ref in maybe_bf16_ref:
                ob_ref[rows, :] = out.astype(BF16)


def _matmul_residual_ln(a, w, resid, gain, bias, *, tm, tk, emit_bf16):
    m, k = a.shape
    n = w.shape[1]
    nk = k // tk
    row_block = pl.BlockSpec((tm, n), lambda i, kk: (i, 0))
    blocks = [((tm, tk), BF16), ((tk, n), w.dtype), ((tm, n), F32), ((tm, n), F32)]
    out_specs, out_shape = [row_block], [jax.ShapeDtypeStruct((m, n), F32)]
    if emit_bf16:
        blocks.append(((tm, n), BF16))
        out_specs.append(row_block)
        out_shape.append(jax.ShapeDtypeStruct((m, n), BF16))
    outs = pl.pallas_call(
        functools.partial(_mm_ln_body, nk=nk),
        grid=(m // tm, nk),
        in_specs=[pl.BlockSpec((tm, tk), lambda i, kk: (i, kk)),
                  pl.BlockSpec((tk, n), lambda i, kk: (kk, 0)),
                  row_block,
                  pl.BlockSpec((1, n), lambda i, kk: (0, 0)),
                  pl.BlockSpec((1, n), lambda i, kk: (0, 0))],
        out_specs=out_specs,
        out_shape=out_shape,
        compiler_params=_params(("arbitrary", "arbitrary"), _vmem_limit(blocks)),
        name="matmul_residual_ln",
    )(a, w, resid, gain.reshape(1, n), bias.reshape(1, n))
    return outs if emit_bf16 else outs[0]


def _rel_bucket_np(dist):
    n = np.maximum(dist, 0)
    exact = N_BUCKETS // 2
    logv = (np.log(np.maximum(n, 1).astype(np.float32) / exact) / math.log(MAX_DISTANCE / exact))
    large = exact + (logv.astype(np.float32) * (N_BUCKETS - exact)).astype(np.int32)
    large = np.minimum(large, N_BUCKETS - 1)
    return np.where(n < exact, n, large).astype(np.int32)


def _bias_expand_body(tab_ref, bucket_ref, o_ref, *, head0, shift_last_bucket, scale):
    h = head0 + pl.program_id(0)
    last = tab_ref[N_BUCKETS - 1, h] if shift_last_bucket else 0.0
    for t in range(bucket_ref.shape[0]):
        bucket = bucket_ref[t]
        acc = jnp.full(bucket.shape, NEG, F32)
        for bkt in range(N_BUCKETS):
            acc = jnp.where(bucket == bkt, (tab_ref[bkt, h] - last) * scale, acc)
        o_ref[t] = acc


def _bias_expand(table, dists, valids, *, head0, heads, shift_last_bucket, scale):
    bucket = np.stack([np.where(v, _rel_bucket_np(d), -1) for d, v in zip(dists, valids)]).astype(np.int32)
    nt, r, c = bucket.shape
    blocks = [((nt, r, c), jnp.int32), ((nt, r, c), F32)]
    return pl.pallas_call(
        functools.partial(_bias_expand_body, head0=head0, shift_last_bucket=shift_last_bucket,
                          scale=scale),
        grid=(heads,),
        in_specs=[pl.BlockSpec(memory_space=pltpu.SMEM),
                  pl.BlockSpec((nt, r, c), lambda h: (0, 0, 0))],
        out_specs=pl.BlockSpec((None, nt, r, c), lambda h: (h, 0, 0, 0)),
        out_shape=jax.ShapeDtypeStruct((heads, nt, r, c), F32),
        compiler_params=_params(("arbitrary",), _vmem_limit(blocks)),
        name="bias_expand",
    )(table.astype(F32), jnp.asarray(bucket))


def _swa_body(sink_ref, q_ref, kp_ref, kc_ref, vp_ref, vc_ref, bias_ref, o_ref):
    n = pl.program_id(0)
    half = DH_A
    pairs = G_A // 2
    rows = pairs * BLK
    lane_k = lax.broadcasted_iota(jnp.int32, (2 * BLK, 2 * half), 1)
    row = lax.broadcasted_iota(jnp.int32, (rows, 1), 0)
    lane_o = lax.broadcasted_iota(jnp.int32, (BLK, 2 * half), 1)
    variant = jnp.minimum(n, 1)

    kf = jnp.concatenate([kp_ref[...], kc_ref[...]], axis=0).astype(F32)
    vf = jnp.concatenate([vp_ref[...], vc_ref[...]], axis=0).astype(F32)
    kr = pltpu.roll(kf, half, 1)
    vr = pltpu.roll(vf, half, 1)

    outs = []
    for g in range(KV_A):
        k_own, k_other = (kf, kr) if g == 0 else (kr, kf)
        k_lo = jnp.where(lane_k < half, k_own, 0.0).astype(BF16)
        k_hi = jnp.where(lane_k >= half, k_other, 0.0).astype(BF16)
        v_dup = (jnp.where(lane_k < half, vf, vr) if g == 0 else jnp.where(lane_k < half, vr, vf)).astype(BF16)
        q_stack = jnp.concatenate(
            [q_ref[:, (g * pairs + t) * 2 * half:(g * pairs + t + 1) * 2 * half] for t in range(pairs)], axis=0)
        q_stack = (q_stack.astype(F32) * (DH_A ** -0.5 * LOG2E)).astype(BF16)
        o_par = []
        for par, k_sel in ((0, k_lo), (1, k_hi)):
            sink = jnp.zeros((rows, 1), F32)
            for t in range(pairs):
                sink = jnp.where(row >= t * BLK, sink_ref[2 * (g * pairs + t) + par] * LOG2E, sink)
            s = lax.dot_general(q_stack, k_sel, (((1,), (1,)), ((), ())),
                                preferred_element_type=F32) + bias_ref[variant, g, par]
            mx = jnp.maximum(jnp.max(s, axis=1, keepdims=True), sink)
            p = jnp.exp2(s - mx)
            den = jnp.sum(p, axis=1, keepdims=True) + jnp.exp2(sink - mx)
            o_par.append(jnp.dot(p.astype(BF16), v_dup, preferred_element_type=F32) / den)
        for t in range(pairs):
            outs.append(jnp.where(lane_o < half, o_par[0][t * BLK:(t + 1) * BLK], o_par[1][t * BLK:(t + 1) * BLK]))
    o_ref[...] = jnp.concatenate(outs, axis=1).astype(o_ref.dtype)


def _swa_attention(proj, sinks, bias, *, q_col, k_col, v_col):
    s = proj.shape[0]
    nb = s // BLK
    qw = HA * DH_A
    kvw = KV_A * DH_A
    assert kvw == V7X_LANES and q_col % qw == 0 and k_col % kvw == 0 and v_col % kvw == 0
    kb, vb = k_col // kvw, v_col // kvw
    pairs = G_A // 2
    bias = bias.reshape(KV_A, pairs, 2, 2, BLK, 2 * BLK).transpose(3, 0, 2, 1, 4, 5)
    bias = bias.reshape(2, KV_A, 2, pairs * BLK, 2 * BLK)
    blocks = [((BLK, qw), BF16)] + [((BLK, kvw), BF16)] * 4 + [(bias.shape, F32), ((BLK, qw), BF16)]
    prev = lambda n: jnp.maximum(n - 1, 0)
    return pl.pallas_call(
        _swa_body,
        grid=(nb,),
        in_specs=[pl.BlockSpec(memory_space=pltpu.SMEM),
                  pl.BlockSpec((BLK, qw), lambda n: (n, q_col // qw)),
                  pl.BlockSpec((BLK, kvw), lambda n: (prev(n), kb)),
                  pl.BlockSpec((BLK, kvw), lambda n: (n, kb)),
                  pl.BlockSpec((BLK, kvw), lambda n: (prev(n), vb)),
                  pl.BlockSpec((BLK, kvw), lambda n: (n, vb)),
                  pl.BlockSpec(bias.shape, lambda n: (0, 0, 0, 0, 0))],
        out_specs=pl.BlockSpec((BLK, qw), lambda n: (n, 0)),
        out_shape=jax.ShapeDtypeStruct((s, qw), BF16),
        compiler_params=_params(("arbitrary",), _vmem_limit(blocks)),
        name="swa_sink_attention",
    )(sinks.astype(F32), proj, proj, proj, proj, proj, bias)


def _band_bias_body(tab_ref, bucket_ref, o_ref, *, head0, scale, tile_offsets):
    h = head0 + pl.program_id(0)
    last = tab_ref[N_BUCKETS - 1, h]
    bands = []
    for t in range(2):
        bucket = bucket_ref[t]
        acc = jnp.full(bucket.shape, NEG, F32)
        for bkt in range(N_BUCKETS):
            acc = jnp.where(bucket == bkt, (tab_ref[bkt, h] - last) * scale, acc)
        bands.append(acc)
    sub = MAX_DISTANCE
    nb = o_ref.shape[1] // sub
    for t, base in enumerate(tile_offsets):
        for rb in range(nb):
            for cb in range(nb):
                off = base + sub * (cb - rb)
                if off < 0:
                    blk = jnp.full((sub, sub), NEG, F32)
                elif off < 2 * sub:
                    blk = bands[off // sub]
                else:
                    blk = jnp.zeros((sub, sub), F32)
                o_ref[t, rb * sub:(rb + 1) * sub, cb * sub:(cb + 1) * sub] = blk


def _band_bias_tiles(table, *, head0, heads, tile, tile_offsets, scale):
    sub = MAX_DISTANCE
    assert tile % sub == 0 and all(o % sub == 0 for o in tile_offsets)
    r = np.arange(sub)[:, None]
    c = np.arange(sub)[None, :]
    bucket = np.stack([np.where(c - r + o >= 0, _rel_bucket_np(c - r + o), -1) for o in (0, sub)]).astype(np.int32)
    nt = len(tile_offsets)
    blocks = [((2, sub, sub), jnp.int32), ((nt, tile, tile), F32)]
    return pl.pallas_call(
        functools.partial(_band_bias_body, head0=head0, scale=scale, tile_offsets=tuple(tile_offsets)),
        grid=(heads,),
        in_specs=[pl.BlockSpec(memory_space=pltpu.SMEM),
                  pl.BlockSpec((2, sub, sub), lambda h: (0, 0, 0))],
        out_specs=pl.BlockSpec((None, nt, tile, tile), lambda h: (h, 0, 0, 0)),
        out_shape=jax.ShapeDtypeStruct((heads, nt, tile, tile), F32),
        compiler_params=_params(("arbitrary",), _vmem_limit(blocks)),
        name="band_bias_tiles",
    )(table.astype(F32), jnp.asarray(bucket))


HEADS_PER_STEP = 2


def _diff_body(q_ref, k_ref, v_ref, b2_ref, lq1_ref, lk1_ref, lq2_ref, lk2_ref, sw_ref,
               o_ref, qz_ref, sa_ref, sb_ref, cma_ref, cmb_ref, acc_ref, m_ref, l_ref, *, tq, tk):
    i = pl.program_id(1)
    width = 2 * DH_B
    heads = range(HEADS_PER_STEP)
    lanes = lambda hh: slice(hh * width, (hh + 1) * width)

    row = lax.broadcasted_iota(jnp.int32, (width, tq), 0)
    for hh in heads:
        qt = (q_ref[:, lanes(hh)].astype(F32) * (DH_B ** -0.5 * LOG2E)).T
        qz_ref[hh, 0] = jnp.where(row < DH_B, qt, 0.0).astype(BF16)
        qz_ref[hh, 1] = jnp.where(row >= DH_B, qt, 0.0).astype(BF16)
    def score_chain(hh, c, tile, bias_tile, s_ref, cm_ref):
        kblk = k_ref[pl.ds(pl.multiple_of(tile * tk, tk), tk), lanes(hh)]
        s = jnp.dot(kblk, qz_ref[hh, c], preferred_element_type=F32)
        if bias_tile is not None:
            s = s + b2_ref[hh, bias_tile]
        s_ref[hh, c] = s
        cm_ref[hh, c] = jnp.max(s, axis=0, keepdims=True)

    def accumulate_chain(hh, c, tile, s_ref, cm_ref, first):
        vblk = v_ref[pl.ds(pl.multiple_of(tile * tk, tk), tk), lanes(hh)]
        m_new = cm_ref[hh, c] if first else jnp.maximum(m_ref[hh, c], cm_ref[hh, c])
        p = jnp.exp2(s_ref[hh, c] - m_new)
        l_new = jnp.sum(p, axis=0, keepdims=True)
        pv = lax.dot_general(vblk, p.astype(BF16), (((0,), (0,)), ((), ())), preferred_element_type=F32)
        if first:
            l_ref[hh, c] = l_new
            acc_ref[hh, c] = pv
        else:
            rescale = jnp.exp2(m_ref[hh, c] - m_new)
            l_ref[hh, c] = rescale * l_ref[hh, c] + l_new
            acc_ref[hh, c] = rescale * acc_ref[hh, c] + pv
        m_ref[hh, c] = m_new

    def stage(score_args=None, acc_args=None, first=False):
        for c in range(2):
            for hh in heads:
                if score_args is not None:
                    score_chain(hh, c, *score_args)
                if acc_args is not None:
                    accumulate_chain(hh, c, *acc_args, first)

    n_items = i + 1
    n_pairs = n_items // 2
    even = n_items % 2 == 0
    slot_a, slot_b = (sa_ref, cma_ref), (sb_ref, cmb_ref)
    stage(score_args=(i, 1, *slot_a))

    def pair(t, prefetch, first=False):
        bias_tile = 0 if first else None
        stage((i - (2 * t + 1), bias_tile, *slot_b), (i - 2 * t, *slot_a), first=first)
        stage((i - (2 * t + 2), None, *slot_a) if prefetch else None, (i - (2 * t + 1), *slot_b))

    @pl.when(n_items == 2)
    def _():
        pair(0, prefetch=False, first=True)

    @pl.when(n_items > 2)
    def _():
        pair(0, prefetch=True, first=True)

    generic = jnp.maximum(n_pairs - 1 - even.astype(jnp.int32), 0)

    def two_pairs(u, carry):
        pair(2 * u + 1, prefetch=True)
        pair(2 * u + 2, prefetch=True)
        return carry

    lax.fori_loop(0, generic // 2, two_pairs, 0)

    @pl.when(generic % 2 == 1)
    def _():
        pair(generic, prefetch=True)

    @pl.when(jnp.logical_and(even, n_pairs >= 2))
    def _():
        pair(n_pairs - 1, prefetch=False)

    @pl.when(n_items == 1)
    def _():
        stage(acc_args=(0, *slot_a), first=True)

    @pl.when(jnp.logical_and(jnp.logical_not(even), n_items > 1))
    def _():
        stage(acc_args=(0, *slot_a))

    lam = (jnp.exp(jnp.sum(lq1_ref[...] * lk1_ref[...], axis=1, keepdims=True))
           - jnp.exp(jnp.sum(lq2_ref[...] * lk2_ref[...], axis=1, keepdims=True)) + LAMBDA_INIT)
    for hh in heads:
        o = acc_ref[hh, 0] / l_ref[hh, 0] - lam * (acc_ref[hh, 1] / l_ref[hh, 1])
        ms = jnp.mean(o * o, axis=0, keepdims=True)
        o = o * lax.rsqrt(ms + LN_EPS) * sw_ref[...] * (1.0 - LAMBDA_INIT)
        o_ref[:, lanes(hh)] = o.T.astype(o_ref.dtype)


def _diff_attention(proj, bias2, lq1, lk1, lq2, lk2, subln_w, *, q_col, k_col, v_col, tq, tk):
    s = proj.shape[0]
    width = 2 * DH_B
    hps = HEADS_PER_STEP
    wide = hps * width
    assert width == V7X_LANES and tq == tk and HB % hps == 0
    assert q_col % wide == 0 and k_col % wide == 0 and v_col % wide == 0
    qb, kb, vb = q_col // wide, k_col // wide, v_col // wide
    blocks = [((tq, wide), BF16), ((s, wide), BF16), ((s, wide), BF16),
              ((hps, 2, tk, tq), F32), ((tq, wide), BF16)]
    scores_buf = ((hps, 2, tk, tq), F32)
    stats_buf = ((hps, 2, 1, tq), F32)
    scratch = [((hps, 2, width, tq), BF16), scores_buf, scores_buf, stats_buf, stats_buf,
               ((hps, 2, width, tq), F32), stats_buf, stats_buf]
    padded = [((hps, 2, 8, tq), F32) if sd == stats_buf else sd for sd in scratch]
    vec = lambda v: v.astype(F32).reshape(1, DH_B)
    small = pl.BlockSpec((1, DH_B), lambda h, i: (0, 0))
    return pl.pallas_call(
        functools.partial(_diff_body, tq=tq, tk=tk),
        grid=(HB // hps, s // tq),
        in_specs=[pl.BlockSpec((tq, wide), lambda h, i: (i, qb + h)),
                  pl.BlockSpec((s, wide), lambda h, i: (0, kb + h)),
                  pl.BlockSpec((s, wide), lambda h, i: (0, vb + h)),
                  pl.BlockSpec((hps, 2, tk, tq), lambda h, i: (h, 0, 0, 0)),
                  small, small, small, small,
                  pl.BlockSpec((width, 1), lambda h, i: (0, 0))],
        out_specs=pl.BlockSpec((tq, wide), lambda h, i: (i, h)),
        out_shape=jax.ShapeDtypeStruct((s, HB * width), BF16),
        scratch_shapes=[pltpu.VMEM(sh, dt) for sh, dt in scratch],
        compiler_params=_params(("arbitrary", "arbitrary"), _vmem_limit(blocks, padded)),
        name="diff_attention",
    )(proj, proj, proj, bias2, vec(lq1), vec(lk1), vec(lq2), vec(lk2),
      subln_w.astype(F32).reshape(width, 1))


def _cross_block_body(h_ref, wq_ref, kv_ref, wo_ref, g_ref, b_ref, o_ref, ob_ref, wq_bf_ref, wo_bf_ref):
    @pl.when(pl.program_id(0) == 0)
    def _():
        wq_bf_ref[...] = wq_ref[...].astype(BF16)
        wo_bf_ref[...] = wo_ref[...].astype(BF16)

    scale = DH_C ** -0.5
    halves = _row_chunks(h_ref.shape[0], 2)
    hs = [h_ref[rows, :] for rows in halves]
    qs = [jnp.dot(h.astype(BF16), wq_bf_ref[...], preferred_element_type=F32).astype(BF16) for h in hs]
    ss = [[lax.dot_general(q[:, hd * DH_C:(hd + 1) * DH_C], kv_ref[:, hd * DH_C:(hd + 1) * DH_C],
                           (((1,), (1,)), ((), ())), preferred_element_type=F32) * scale
           for hd in range(HC)] for q in qs]
    ocs = []
    for s_heads in ss:
        outs = []
        for hd, s in enumerate(s_heads):
            vh = kv_ref[:, (HC + hd) * DH_C:(HC + hd + 1) * DH_C]
            mx = jnp.max(s, axis=1, keepdims=True)
            p = jnp.exp(s - mx)
            den = jnp.sum(p, axis=1, keepdims=True)
            outs.append((jnp.dot(p.astype(BF16), vh, preferred_element_type=F32) / den).astype(BF16))
        ocs.append(jnp.concatenate(outs, axis=1))
    ys = [ALPHA * h + jnp.dot(oc, wo_bf_ref[...], preferred_element_type=F32) for h, oc in zip(hs, ocs)]
    for rows, y in zip(halves, ys):
        out = _layer_norm_rows(y, g_ref, b_ref)
        o_ref[rows, :] = out
        ob_ref[rows, :] = out.astype(BF16)


def _cross_attention_block(h, kv, w_cq, w_co, gain, bias, *, tm):
    s, d = h.shape
    mlen, w = kv.shape[0], HC * DH_C
    row_block = pl.BlockSpec((tm, d), lambda i: (i, 0))
    whole = lambda shape: pl.BlockSpec(shape, lambda i: (0,) * len(shape))
    blocks = [((tm, d), F32), ((d, w), F32), ((mlen, 2 * w), BF16), ((w, d), F32), ((tm, d), F32), ((tm, d), BF16)]
    scratch = [((d, w), BF16), ((w, d), BF16)]
    return pl.pallas_call(
        _cross_block_body,
        grid=(s // tm,),
        in_specs=[row_block, whole((d, w)), whole((mlen, 2 * w)), whole((w, d)), whole((1, d)), whole((1, d))],
        out_specs=[row_block, row_block],
        out_shape=[jax.ShapeDtypeStruct((s, d), F32), jax.ShapeDtypeStruct((s, d), BF16)],
        scratch_shapes=[pltpu.VMEM(sh, dt) for sh, dt in scratch],
        compiler_params=_params(("arbitrary",), _vmem_limit(blocks, scratch)),
        name="memory_cross_attention_block",
    )(h, w_cq, kv, w_co, gain.reshape(1, d), bias.reshape(1, d))


def kernel(x, mem, rel_bias_table, w_in, sinks, lambda_q1, lambda_k1, lambda_q2, lambda_k2, subln_w,
           w_branch_a, w_branch_b, w_o, ln1_g, ln1_b, w_cq, w_mem_kv, w_co, ln2_g, ln2_b,
           w_gate_up, w_down, ln3_g, ln3_b):
    b, s, d = x.shape
    assert b == 1 and w_in.shape[0] == DEPTH == 1
    (w_in, sinks, lambda_q1, lambda_k1, lambda_q2, lambda_k2, subln_w, w_branch_a, w_branch_b, w_o, ln1_g, ln1_b,
     w_cq, w_mem_kv, w_co, ln2_g, ln2_b, w_gate_up, w_down, ln3_g, ln3_b) = [
        p.reshape(p.shape[1:]) for p in (
            w_in, sinks, lambda_q1, lambda_k1, lambda_q2, lambda_k2, subln_w, w_branch_a, w_branch_b, w_o, ln1_g,
            ln1_b, w_cq, w_mem_kv, w_co, ln2_g, ln2_b, w_gate_up, w_down, ln3_g, ln3_b)]
    qa_w, kva_w, qb_w = HA * DH_A, KV_A * DH_A, HB * 2 * DH_B
    col_qa, col_ka, col_va = 0, qa_w, qa_w + kva_w
    col_qb = col_va + kva_w
    col_kb, col_vb = col_qb + qb_w, col_qb + 2 * qb_w
    col_ga = col_vb + qb_w
    col_gb = col_ga + d
    tq, tk = TILES["diff_attention"]["tq"], TILES["diff_attention"]["tk"]

    h0 = x.reshape(s, d)

    proj = _matmul(h0, w_in, **TILES["in_proj"], out_dtype=BF16, name="in_proj")

    i_a = np.arange(BLK)[:, None]
    j_a = np.arange(2 * BLK)[None, :]
    dist_a = BLK + i_a - j_a
    band_a = (dist_a >= 0) & (dist_a < WINDOW)
    bias_a = _bias_expand(rel_bias_table, [dist_a, dist_a], [band_a & (j_a >= BLK), band_a], head0=0, heads=HA,
                          shift_last_bucket=False, scale=LOG2E)
    o_a = _swa_attention(proj, sinks, bias_a, q_col=col_qa, k_col=col_ka, v_col=col_va)

    bias2 = _band_bias_tiles(rel_bias_table, head0=HA, heads=HB, tile=tq, tile_offsets=(tk, 0), scale=LOG2E)
    o_b = _diff_attention(proj, bias2, lambda_q1, lambda_k1, lambda_q2, lambda_k2, subln_w,
                          q_col=col_qb, k_col=col_kb, v_col=col_vb, tq=tq, tk=tk)

    mix = _gated_branches(o_a, o_b, w_branch_a, w_branch_b, proj, col_ga, col_gb, **TILES["gated_branches"])
    h1 = _matmul_resident_ln(mix, w_o, h0, ln1_g, ln1_b, **TILES["out_proj_ln"])

    kvm = _matmul(mem.reshape(mem.shape[1], d), w_mem_kv, tm=mem.shape[1], tn=HC * DH_C,
                  out_dtype=BF16, name="mem_kv")
    h2, h2b = _cross_attention_block(h1, kvm, w_cq, w_co, ln2_g, ln2_b, **TILES["cross_block"])

    act, w_down_bf = _swiglu_up(h2b, w_gate_up, w_down, **TILES["swiglu_up"])
    h3 = _matmul_residual_ln(act, w_down_bf, h2, ln3_g, ln3_b, **TILES["down_proj_ln"], emit_bf16=False)
    return h3.reshape(b, s, d)
```

```python
import functools
import math

import numpy as np
import jax
import jax.numpy as jnp
from jax import lax
from jax.experimental import pallas as pl
from jax.experimental.pallas import tpu as pltpu

F32 = jnp.float32
BF16 = jnp.bfloat16

BLK = 128
WINDOW = 128
HA, KV_A, DH_A = 16, 2, 64
G_A = HA // KV_A
HB, DH_B = 8, 64
N_BUCKETS, MAX_DISTANCE = 32, 128
HC, DH_C = 4, 128
LN_EPS = 1e-5
DEPTH = 1
ALPHA = (2 * DEPTH) ** 0.25
LAMBDA_INIT = 0.8 - 0.6 * math.exp(-0.3 * 0)
LOG2E = math.log2(math.e)

V7X_LANES = 128
V7X_VMEM_BYTES = 64 * 1024 * 1024
V7X_VMEM_TEMP_BYTES = 12 * 1024 * 1024

EPILOGUE_CHUNKS = 4

TILES = dict(
    in_proj=dict(tm=1024, tn=768),
    diff_attention=dict(tq=512, tk=512),
    gated_branches=dict(tm=2048, tn=256),
    out_proj_ln=dict(tm=512),
    cross_block=dict(tm=512),
    swiglu_up=dict(tm=2048, tn=256),
    down_proj_ln=dict(tm=1024, tk=512),
)

NEG = -1e30


def _nbytes(shape, dtype):
    return int(np.prod(shape)) * jnp.dtype(dtype).itemsize


def _vmem_limit(pipelined, scratch=()):
    need = 2 * sum(_nbytes(s, d) for s, d in pipelined) + sum(_nbytes(s, d) for s, d in scratch)
    need += V7X_VMEM_TEMP_BYTES
    assert need <= V7X_VMEM_BYTES - 4 * 1024 * 1024, need
    return need


def _params(semantics, vmem):
    return pltpu.CompilerParams(dimension_semantics=semantics, vmem_limit_bytes=vmem)


def _act_tile(a_ref, abf_ref):
    if abf_ref is None:
        return a_ref[...]

    @pl.when(pl.program_id(1) == 0)
    def _():
        abf_ref[...] = a_ref[...].astype(BF16)

    return abf_ref[...]


def _mm_body(a_ref, w_ref, o_ref, *scratch):
    a = _act_tile(a_ref, scratch[0] if scratch else None)
    o_ref[...] = jnp.dot(a, w_ref[...].astype(BF16), preferred_element_type=F32).astype(o_ref.dtype)


def _matmul(a, w, *, tm, tn, out_dtype, name):
    m, k = a.shape
    n = w.shape[1]
    cast = a.dtype != BF16
    scratch = [((tm, k), BF16)] if cast else []
    blocks = [((tm, k), a.dtype), ((k, tn), w.dtype), ((tm, tn), out_dtype)]
    return pl.pallas_call(
        _mm_body,
        grid=(m // tm, n // tn),
        in_specs=[pl.BlockSpec((tm, k), lambda i, j: (i, 0)),
                  pl.BlockSpec((k, tn), lambda i, j: (0, j))],
        out_specs=pl.BlockSpec((tm, tn), lambda i, j: (i, j)),
        out_shape=jax.ShapeDtypeStruct((m, n), out_dtype),
        scratch_shapes=[pltpu.VMEM(s, d) for s, d in scratch],
        compiler_params=_params(("arbitrary", "arbitrary"), _vmem_limit(blocks, scratch)),
        name=name,
    )(a, w)


def _row_chunks(rows, chunks):
    size = rows // chunks
    return [slice(c * size, (c + 1) * size) for c in range(chunks)]


def _branch_body(oa_ref, ob_ref, wa_ref, wb_ref, ga_ref, gb_ref, o_ref):
    wa = wa_ref[...].astype(BF16)
    wb = wb_ref[...].astype(BF16)
    for rows in _row_chunks(o_ref.shape[0], EPILOGUE_CHUNKS):
        ya = jnp.dot(oa_ref[rows, :], wa, preferred_element_type=F32)
        yb = jnp.dot(ob_ref[rows, :], wb, preferred_element_type=F32)
        ga = jax.nn.sigmoid(ga_ref[rows, :].astype(F32))
        gb = jax.nn.sigmoid(gb_ref[rows, :].astype(F32))
        o_ref[rows, :] = (ga * ya + gb * yb).astype(o_ref.dtype)


def _gated_branches(o_a, o_b, w_a, w_b, proj, ga_col, gb_col, *, tm, tn):
    m, ka = o_a.shape
    kb = o_b.shape[1]
    n = w_a.shape[1]
    ga_blk, gb_blk = ga_col // tn, gb_col // tn
    blocks = [((tm, ka), BF16), ((tm, kb), BF16), ((ka, tn), F32), ((kb, tn), F32),
              ((tm, tn), proj.dtype), ((tm, tn), proj.dtype), ((tm, tn), BF16)]
    return pl.pallas_call(
        _branch_body,
        grid=(m // tm, n // tn),
        in_specs=[pl.BlockSpec((tm, ka), lambda i, j: (i, 0)),
                  pl.BlockSpec((tm, kb), lambda i, j: (i, 0)),
                  pl.BlockSpec((ka, tn), lambda i, j: (0, j)),
                  pl.BlockSpec((kb, tn), lambda i, j: (0, j)),
                  pl.BlockSpec((tm, tn), lambda i, j: (i, ga_blk + j)),
                  pl.BlockSpec((tm, tn), lambda i, j: (i, gb_blk + j))],
        out_specs=pl.BlockSpec((tm, tn), lambda i, j: (i, j)),
        out_shape=jax.ShapeDtypeStruct((m, n), BF16),
        compiler_params=_params(("arbitrary", "arbitrary"), _vmem_limit(blocks)),
        name="gated_branches",
    )(o_a, o_b, w_a, w_b, proj, proj)


def _swiglu_body(a_ref, wg_ref, wu_ref, wd_ref, o_ref, wd_bf_ref):
    a = a_ref[...]
    wg = wg_ref[...].astype(BF16)
    wu = wu_ref[...].astype(BF16)
    for rows in _row_chunks(o_ref.shape[0], 2):
        g = jnp.dot(a[rows, :], wg, preferred_element_type=F32)
        u = jnp.dot(a[rows, :], wu, preferred_element_type=F32)
        o_ref[rows, :] = (g * jax.nn.sigmoid(g) * u).astype(o_ref.dtype)
    wd_bf_ref[...] = wd_ref[...].astype(BF16)


def _swiglu_up(a, w_gate_up, w_down, *, tm, tn):
    m, k = a.shape
    d_ff = w_gate_up.shape[1] // 2
    row_tiles, col_tiles = m // tm, d_ff // tn
    up_blk = d_ff // tn
    kd, nd = w_down.shape
    slab = kd // (row_tiles * col_tiles)
    assert slab * row_tiles * col_tiles == kd and slab % 16 == 0 and a.dtype == BF16
    slab_block = pl.BlockSpec((slab, nd), lambda i, j: (i * col_tiles + j, 0))
    blocks = [((tm, k), BF16), ((k, tn), F32), ((k, tn), F32), ((tm, tn), BF16), ((slab, nd), F32), ((slab, nd), BF16)]
    return pl.pallas_call(
        _swiglu_body,
        grid=(row_tiles, col_tiles),
        in_specs=[pl.BlockSpec((tm, k), lambda i, j: (i, 0)),
                  pl.BlockSpec((k, tn), lambda i, j: (0, j)),
                  pl.BlockSpec((k, tn), lambda i, j: (0, up_blk + j)),
                  slab_block],
        out_specs=[pl.BlockSpec((tm, tn), lambda i, j: (i, j)), slab_block],
        out_shape=[jax.ShapeDtypeStruct((m, d_ff), BF16), jax.ShapeDtypeStruct((kd, nd), BF16)],
        compiler_params=_params(("arbitrary", "arbitrary"), _vmem_limit(blocks)),
        name="swiglu_up",
    )(a, w_gate_up, w_gate_up, w_down)


def _layer_norm_rows(y, g_ref, b_ref):
    mu = jnp.mean(y, axis=-1, keepdims=True)
    yc = y - mu
    var = jnp.mean(yc * yc, axis=-1, keepdims=True)
    return yc * lax.rsqrt(var + LN_EPS) * g_ref[...] + b_ref[...]


def _resident_ln_body(a_ref, w_ref, r_ref, g_ref, b_ref, o_ref, wbf_ref):
    @pl.when(pl.program_id(0) == 0)
    def _():
        wbf_ref[...] = w_ref[...].astype(BF16)

    halves = _row_chunks(o_ref.shape[0], 2)
    ys = [ALPHA * r_ref[rows, :] + jnp.dot(a_ref[rows, :], wbf_ref[...], preferred_element_type=F32)
          for rows in halves]
    for rows, y in zip(halves, ys):
        o_ref[rows, :] = _layer_norm_rows(y, g_ref, b_ref)


def _matmul_resident_ln(a, w, resid, gain, bias, *, tm):
    m, k = a.shape
    n = w.shape[1]
    row_block = pl.BlockSpec((tm, n), lambda i: (i, 0))
    blocks = [((tm, k), BF16), ((tm, n), F32), ((tm, n), F32)]
    single = [((k, n), F32), ((k, n), BF16)]
    return pl.pallas_call(
        _resident_ln_body,
        grid=(m // tm,),
        in_specs=[pl.BlockSpec((tm, k), lambda i: (i, 0)),
                  pl.BlockSpec((k, n), lambda i: (0, 0), pipeline_mode=pl.Buffered(1)),
                  row_block,
                  pl.BlockSpec((1, n), lambda i: (0, 0)),
                  pl.BlockSpec((1, n), lambda i: (0, 0))],
        out_specs=row_block,
        out_shape=jax.ShapeDtypeStruct((m, n), F32),
        scratch_shapes=[pltpu.VMEM((k, n), BF16)],
        compiler_params=_params(("arbitrary",), _vmem_limit(blocks, single)),
        name="matmul_resident_ln",
    )(a, w, resid, gain.reshape(1, n), bias.reshape(1, n))


def _mm_ln_body(a_ref, w_ref, r_ref, g_ref, b_ref, o_ref, *maybe_bf16_ref, nk):
    kk = pl.program_id(1)
    last = kk == nk - 1

    def partial_product(rows=slice(None)):
        return jnp.dot(a_ref[rows, :], w_ref[...].astype(BF16), preferred_element_type=F32)

    if nk > 1:
        @pl.when(kk == 0)
        def _():
            o_ref[...] = ALPHA * r_ref[...] + partial_product()

        @pl.when(jnp.logical_and(kk > 0, jnp.logical_not(last)))
        def _():
            o_ref[...] += partial_product()

    @pl.when(last)
    def _():
        for rows in _row_chunks(o_ref.shape[0], EPILOGUE_CHUNKS):
            seed = o_ref[rows, :] if nk > 1 else ALPHA * r_ref[rows, :]
            out = _layer_norm_rows(seed + partial_product(rows), g_ref, b_ref)
            o_ref[rows, :] = out
            for ob_ref in maybe_bf16_ref:
                ob_ref[rows, :] = out.astype(BF16)


def _matmul_residual_ln(a, w, resid, gain, bias, *, tm, tk, emit_bf16):
    m, k = a.shape
    n = w.shape[1]
    nk = k // tk
    row_block = pl.BlockSpec((tm, n), lambda i, kk: (i, 0))
    blocks = [((tm, tk), BF16), ((tk, n), w.dtype), ((tm, n), F32), ((tm, n), F32)]
    out_specs, out_shape = [row_block], [jax.ShapeDtypeStruct((m, n), F32)]
    if emit_bf16:
        blocks.append(((tm, n), BF16))
        out_specs.append(row_block)
        out_shape.append(jax.ShapeDtypeStruct((m, n), BF16))
    outs = pl.pallas_call(
        functools.partial(_mm_ln_body, nk=nk),
        grid=(m // tm, nk),
        in_specs=[pl.BlockSpec((tm, tk), lambda i, kk: (i, kk)),
                  pl.BlockSpec((tk, n), lambda i, kk: (kk, 0)),
                  row_block,
                  pl.BlockSpec((1, n), lambda i, kk: (0, 0)),
                  pl.BlockSpec((1, n), lambda i, kk: (0, 0))],
        out_specs=out_specs,
        out_shape=out_shape,
        compiler_params=_params(("arbitrary", "arbitrary"), _vmem_limit(blocks)),
        name="matmul_residual_ln",
    )(a, w, resid, gain.reshape(1, n), bias.reshape(1, n))
    return outs if emit_bf16 else outs[0]


def _rel_bucket_np(dist):
    n = np.maximum(dist, 0)
    exact = N_BUCKETS // 2
    logv = (np.log(np.maximum(n, 1).astype(np.float32) / exact) / math.log(MAX_DISTANCE / exact))
    large = exact + (logv.astype(np.float32) * (N_BUCKETS - exact)).astype(np.int32)
    large = np.minimum(large, N_BUCKETS - 1)
    return np.where(n < exact, n, large).astype(np.int32)


def _bias_expand_body(tab_ref, bucket_ref, o_ref, *, head0, shift_last_bucket, scale):
    h = head0 + pl.program_id(0)
    last = tab_ref[N_BUCKETS - 1, h] if shift_last_bucket else 0.0
    for t in range(bucket_ref.shape[0]):
        bucket = bucket_ref[t]
        acc = jnp.full(bucket.shape, NEG, F32)
        for bkt in range(N_BUCKETS):
            acc = jnp.where(bucket == bkt, (tab_ref[bkt, h] - last) * scale, acc)
        o_ref[t] = acc


def _bias_expand(table, dists, valids, *, head0, heads, shift_last_bucket, scale):
    bucket = np.stack([np.where(v, _rel_bucket_np(d), -1) for d, v in zip(dists, valids)]).astype(np.int32)
    nt, r, c = bucket.shape
    blocks = [((nt, r, c), jnp.int32), ((nt, r, c), F32)]
    return pl.pallas_call(
        functools.partial(_bias_expand_body, head0=head0, shift_last_bucket=shift_last_bucket,
                          scale=scale),
        grid=(heads,),
        in_specs=[pl.BlockSpec(memory_space=pltpu.SMEM),
                  pl.BlockSpec((nt, r, c), lambda h: (0, 0, 0))],
        out_specs=pl.BlockSpec((None, nt, r, c), lambda h: (h, 0, 0, 0)),
        out_shape=jax.ShapeDtypeStruct((heads, nt, r, c), F32),
        compiler_params=_params(("arbitrary",), _vmem_limit(blocks)),
        name="bias_expand",
    )(table.astype(F32), jnp.asarray(bucket))


def _swa_body(sink_ref, q_ref, kp_ref, kc_ref, vp_ref, vc_ref, bias_ref, o_ref):
    n = pl.program_id(0)
    half = DH_A
    pairs = G_A // 2
    rows = pairs * BLK
    lane_k = lax.broadcasted_iota(jnp.int32, (2 * BLK, 2 * half), 1)
    row = lax.broadcasted_iota(jnp.int32, (rows, 1), 0)
    lane_o = lax.broadcasted_iota(jnp.int32, (BLK, 2 * half), 1)
    variant = jnp.minimum(n, 1)

    kf = jnp.concatenate([kp_ref[...], kc_ref[...]], axis=0).astype(F32)
    vf = jnp.concatenate([vp_ref[...], vc_ref[...]], axis=0).astype(F32)
    kr = pltpu.roll(kf, half, 1)
    vr = pltpu.roll(vf, half, 1)

    outs = []
    for g in range(KV_A):
        k_own, k_other = (kf, kr) if g == 0 else (kr, kf)
        k_lo = jnp.where(lane_k < half, k_own, 0.0).astype(BF16)
        k_hi = jnp.where(lane_k >= half, k_other, 0.0).astype(BF16)
        v_dup = (jnp.where(lane_k < half, vf, vr) if g == 0 else jnp.where(lane_k < half, vr, vf)).astype(BF16)
        q_stack = jnp.concatenate(
            [q_ref[:, (g * pairs + t) * 2 * half:(g * pairs + t + 1) * 2 * half] for t in range(pairs)], axis=0)
        q_stack = (q_stack.astype(F32) * (DH_A ** -0.5 * LOG2E)).astype(BF16)
        o_par = []
        for par, k_sel in ((0, k_lo), (1, k_hi)):
            sink = jnp.zeros((rows, 1), F32)
            for t in range(pairs):
                sink = jnp.where(row >= t * BLK, sink_ref[2 * (g * pairs + t) + par] * LOG2E, sink)
            s = lax.dot_general(q_stack, k_sel, (((1,), (1,)), ((), ())),
                                preferred_element_type=F32) + bias_ref[variant, g, par]
            mx = jnp.maximum(jnp.max(s, axis=1, keepdims=True), sink)
            p = jnp.exp2(s - mx)
            den = jnp.sum(p, axis=1, keepdims=True) + jnp.exp2(sink - mx)
            o_par.append(jnp.dot(p.astype(BF16), v_dup, preferred_element_type=F32) / den)
        for t in range(pairs):
            outs.append(jnp.where(lane_o < half, o_par[0][t * BLK:(t + 1) * BLK], o_par[1][t * BLK:(t + 1) * BLK]))
    o_ref[...] = jnp.concatenate(outs, axis=1).astype(o_ref.dtype)


def _swa_attention(proj, sinks, bias, *, q_col, k_col, v_col):
    s = proj.shape[0]
    nb = s // BLK
    qw = HA * DH_A
    kvw = KV_A * DH_A
    assert kvw == V7X_LANES and q_col % qw == 0 and k_col % kvw == 0 and v_col % kvw == 0
    kb, vb = k_col // kvw, v_col // kvw
    pairs = G_A // 2
    bias = bias.reshape(KV_A, pairs, 2, 2, BLK, 2 * BLK).transpose(3, 0, 2, 1, 4, 5)
    bias = bias.reshape(2, KV_A, 2, pairs * BLK, 2 * BLK)
    blocks = [((BLK, qw), BF16)] + [((BLK, kvw), BF16)] * 4 + [(bias.shape, F32), ((BLK, qw), BF16)]
    prev = lambda n: jnp.maximum(n - 1, 0)
    return pl.pallas_call(
        _swa_body,
        grid=(nb,),
        in_specs=[pl.BlockSpec(memory_space=pltpu.SMEM),
                  pl.BlockSpec((BLK, qw), lambda n: (n, q_col // qw)),
                  pl.BlockSpec((BLK, kvw), lambda n: (prev(n), kb)),
                  pl.BlockSpec((BLK, kvw), lambda n: (n, kb)),
                  pl.BlockSpec((BLK, kvw), lambda n: (prev(n), vb)),
                  pl.BlockSpec((BLK, kvw), lambda n: (n, vb)),
                  pl.BlockSpec(bias.shape, lambda n: (0, 0, 0, 0, 0))],
        out_specs=pl.BlockSpec((BLK, qw), lambda n: (n, 0)),
        out_shape=jax.ShapeDtypeStruct((s, qw), BF16),
        compiler_params=_params(("arbitrary",), _vmem_limit(blocks)),
        name="swa_sink_attention",
    )(sinks.astype(F32), proj, proj, proj, proj, proj, bias)


def _write_band_bias(tab_ref, bucket_ref, o_ref, h, *, scale, tile_offsets):
    last = tab_ref[N_BUCKETS - 1, h]
    bands = []
    for t in range(2):
        bucket = bucket_ref[t]
        acc = jnp.full(bucket.shape, NEG, F32)
        for bkt in range(N_BUCKETS):
            acc = jnp.where(bucket == bkt, (tab_ref[bkt, h] - last) * scale, acc)
        bands.append(acc)
    sub = MAX_DISTANCE
    nb = o_ref.shape[1] // sub
    for t, base in enumerate(tile_offsets):
        for rb in range(nb):
            for cb in range(nb):
                off = base + sub * (cb - rb)
                if off < 0:
                    blk = jnp.full((sub, sub), NEG, F32)
                elif off < 2 * sub:
                    blk = bands[off // sub]
                else:
                    blk = jnp.zeros((sub, sub), F32)
                o_ref[t, rb * sub:(rb + 1) * sub, cb * sub:(cb + 1) * sub] = blk


def _band_buckets():
    sub = MAX_DISTANCE
    r = np.arange(sub)[:, None]
    c = np.arange(sub)[None, :]
    return np.stack([np.where(c - r + o >= 0, _rel_bucket_np(c - r + o), -1) for o in (0, sub)]).astype(np.int32)


HEADS_PER_STEP = 2


def _diff_body(tab_ref, bucket_ref, q_ref, k_ref, v_ref, lq1_ref, lk1_ref, lq2_ref, lk2_ref, sw_ref,
               o_ref, b2_ref, qz_ref, sa_ref, sb_ref, cma_ref, cmb_ref, acc_ref, m_ref, l_ref, *, tq, tk):
    i = pl.program_id(1)
    width = 2 * DH_B
    heads = range(HEADS_PER_STEP)
    lanes = lambda hh: slice(hh * width, (hh + 1) * width)

    @pl.when(i == 0)
    def _():
        for hh in heads:
            _write_band_bias(tab_ref, bucket_ref, b2_ref.at[hh], HA + pl.program_id(0) * HEADS_PER_STEP + hh,
                             scale=LOG2E, tile_offsets=(tk, 0))

    row = lax.broadcasted_iota(jnp.int32, (width, tq), 0)
    for hh in heads:
        qt = (q_ref[:, lanes(hh)].astype(F32) * (DH_B ** -0.5 * LOG2E)).T
        qz_ref[hh, 0] = jnp.where(row < DH_B, qt, 0.0).astype(BF16)
        qz_ref[hh, 1] = jnp.where(row >= DH_B, qt, 0.0).astype(BF16)
    def score_chain(hh, c, tile, bias_tile, s_ref, cm_ref):
        kblk = k_ref[pl.ds(pl.multiple_of(tile * tk, tk), tk), lanes(hh)]
        s = jnp.dot(kblk, qz_ref[hh, c], preferred_element_type=F32)
        if bias_tile is not None:
            s = s + b2_ref[hh, bias_tile]
        s_ref[hh, c] = s
        cm_ref[hh, c] = jnp.max(s, axis=0, keepdims=True)

    def accumulate_chain(hh, c, tile, s_ref, cm_ref, first):
        vblk = v_ref[pl.ds(pl.multiple_of(tile * tk, tk), tk), lanes(hh)]
        m_new = cm_ref[hh, c] if first else jnp.maximum(m_ref[hh, c], cm_ref[hh, c])
        p = jnp.exp2(s_ref[hh, c] - m_new)
        l_new = jnp.sum(p, axis=0, keepdims=True)
        pv = lax.dot_general(vblk, p.astype(BF16), (((0,), (0,)), ((), ())), preferred_element_type=F32)
        if first:
            l_ref[hh, c] = l_new
            acc_ref[hh, c] = pv
        else:
            rescale = jnp.exp2(m_ref[hh, c] - m_new)
            l_ref[hh, c] = rescale * l_ref[hh, c] + l_new
            acc_ref[hh, c] = rescale * acc_ref[hh, c] + pv
        m_ref[hh, c] = m_new

    def stage(score_args=None, acc_args=None, first=False):
        for c in range(2):
            for hh in heads:
                if score_args is not None:
                    score_chain(hh, c, *score_args)
                if acc_args is not None:
                    accumulate_chain(hh, c, *acc_args, first)

    n_items = i + 1
    tile_of = lambda item: jnp.maximum(i - item, 0)
    slot_a, slot_b = (sa_ref, cma_ref), (sb_ref, cmb_ref)
    stage(score_args=(i, 1, *slot_a))

    @pl.when(n_items >= 2)
    def _():
        stage((i - 1, 0, *slot_b), (i, *slot_a), first=True)
        stage((tile_of(2), None, *slot_a), (i - 1, *slot_b))

    def pair(t):
        stage((i - (2 * t + 1), None, *slot_b), (i - 2 * t, *slot_a))
        stage((tile_of(2 * t + 2), None, *slot_a), (i - (2 * t + 1), *slot_b))

    far_pairs = jnp.maximum(n_items // 2 - 1, 0)

    def two_pairs(u, carry):
        pair(2 * u + 1)
        pair(2 * u + 2)
        return carry

    lax.fori_loop(0, far_pairs // 2, two_pairs, 0)

    @pl.when(far_pairs % 2 == 1)
    def _():
        pair(far_pairs)

    @pl.when(n_items == 1)
    def _():
        stage(acc_args=(0, *slot_a), first=True)

    @pl.when(jnp.logical_and(n_items % 2 == 1, n_items > 1))
    def _():
        stage(acc_args=(0, *slot_a))

    lam = (jnp.exp(jnp.sum(lq1_ref[...] * lk1_ref[...], axis=1, keepdims=True))
           - jnp.exp(jnp.sum(lq2_ref[...] * lk2_ref[...], axis=1, keepdims=True)) + LAMBDA_INIT)
    for hh in heads:
        o = acc_ref[hh, 0] / l_ref[hh, 0] - lam * (acc_ref[hh, 1] / l_ref[hh, 1])
        ms = jnp.mean(o * o, axis=0, keepdims=True)
        o = o * lax.rsqrt(ms + LN_EPS) * sw_ref[...] * (1.0 - LAMBDA_INIT)
        o_ref[:, lanes(hh)] = o.T.astype(o_ref.dtype)


def _diff_attention(proj, table, lq1, lk1, lq2, lk2, subln_w, *, q_col, k_col, v_col, tq, tk):
    s = proj.shape[0]
    width = 2 * DH_B
    hps = HEADS_PER_STEP
    wide = hps * width
    assert width == V7X_LANES and tq == tk and HB % hps == 0
    assert q_col % wide == 0 and k_col % wide == 0 and v_col % wide == 0
    qb, kb, vb = q_col // wide, k_col // wide, v_col // wide
    assert tq == tk and tq % MAX_DISTANCE == 0
    buckets = _band_buckets()
    blocks = [(buckets.shape, jnp.int32), ((tq, wide), BF16), ((s, wide), BF16), ((s, wide), BF16), ((tq, wide), BF16)]
    scores_buf = ((hps, 2, tk, tq), F32)
    stats_buf = ((hps, 2, 1, tq), F32)
    scratch = [((hps, 2, tk, tq), F32), ((hps, 2, width, tq), BF16), scores_buf, scores_buf, stats_buf, stats_buf,
               ((hps, 2, width, tq), F32), stats_buf, stats_buf]
    padded = [((hps, 2, 8, tq), F32) if sd == stats_buf else sd for sd in scratch]
    vec = lambda v: v.astype(F32).reshape(1, DH_B)
    small = pl.BlockSpec((1, DH_B), lambda h, i: (0, 0))
    return pl.pallas_call(
        functools.partial(_diff_body, tq=tq, tk=tk),
        grid=(HB // hps, s // tq),
        in_specs=[pl.BlockSpec(memory_space=pltpu.SMEM),
                  pl.BlockSpec(buckets.shape, lambda h, i: (0, 0, 0)),
                  pl.BlockSpec((tq, wide), lambda h, i: (i, qb + h)),
                  pl.BlockSpec((s, wide), lambda h, i: (0, kb + h)),
                  pl.BlockSpec((s, wide), lambda h, i: (0, vb + h)),
                  small, small, small, small,
                  pl.BlockSpec((width, 1), lambda h, i: (0, 0))],
        out_specs=pl.BlockSpec((tq, wide), lambda h, i: (i, h)),
        out_shape=jax.ShapeDtypeStruct((s, HB * width), BF16),
        scratch_shapes=[pltpu.VMEM(sh, dt) for sh, dt in scratch],
        compiler_params=_params(("arbitrary", "arbitrary"), _vmem_limit(blocks, padded)),
        name="diff_attention",
    )(table.astype(F32), jnp.asarray(buckets), proj, proj, proj, vec(lq1), vec(lk1), vec(lq2), vec(lk2),
      subln_w.astype(F32).reshape(width, 1))


def _cross_block_body(h_ref, wq_ref, kv_ref, wo_ref, g_ref, b_ref, o_ref, ob_ref, wq_bf_ref, wo_bf_ref):
    @pl.when(pl.program_id(0) == 0)
    def _():
        wq_bf_ref[...] = wq_ref[...].astype(BF16)
        wo_bf_ref[...] = wo_ref[...].astype(BF16)

    scale = DH_C ** -0.5
    halves = _row_chunks(h_ref.shape[0], 2)
    hs = [h_ref[rows, :] for rows in halves]
    qs = [jnp.dot(h.astype(BF16), wq_bf_ref[...], preferred_element_type=F32).astype(BF16) for h in hs]
    ss = [[lax.dot_general(q[:, hd * DH_C:(hd + 1) * DH_C], kv_ref[:, hd * DH_C:(hd + 1) * DH_C],
                           (((1,), (1,)), ((), ())), preferred_element_type=F32) * scale
           for hd in range(HC)] for q in qs]
    ocs = []
    for s_heads in ss:
        outs = []
        for hd, s in enumerate(s_heads):
            vh = kv_ref[:, (HC + hd) * DH_C:(HC + hd + 1) * DH_C]
            mx = jnp.max(s, axis=1, keepdims=True)
            p = jnp.exp(s - mx)
            den = jnp.sum(p, axis=1, keepdims=True)
            outs.append((jnp.dot(p.astype(BF16), vh, preferred_element_type=F32) / den).astype(BF16))
        ocs.append(jnp.concatenate(outs, axis=1))
    ys = [ALPHA * h + jnp.dot(oc, wo_bf_ref[...], preferred_element_type=F32) for h, oc in zip(hs, ocs)]
    for rows, y in zip(halves, ys):
        out = _layer_norm_rows(y, g_ref, b_ref)
        o_ref[rows, :] = out
        ob_ref[rows, :] = out.astype(BF16)


def _cross_attention_block(h, kv, w_cq, w_co, gain, bias, *, tm):
    s, d = h.shape
    mlen, w = kv.shape[0], HC * DH_C
    row_block = pl.BlockSpec((tm, d), lambda i: (i, 0))
    whole = lambda shape: pl.BlockSpec(shape, lambda i: (0,) * len(shape))
    blocks = [((tm, d), F32), ((d, w), F32), ((mlen, 2 * w), BF16), ((w, d), F32), ((tm, d), F32), ((tm, d), BF16)]
    scratch = [((d, w), BF16), ((w, d), BF16)]
    return pl.pallas_call(
        _cross_block_body,
        grid=(s // tm,),
        in_specs=[row_block, whole((d, w)), whole((mlen, 2 * w)), whole((w, d)), whole((1, d)), whole((1, d))],
        out_specs=[row_block, row_block],
        out_shape=[jax.ShapeDtypeStruct((s, d), F32), jax.ShapeDtypeStruct((s, d), BF16)],
        scratch_shapes=[pltpu.VMEM(sh, dt) for sh, dt in scratch],
        compiler_params=_params(("arbitrary",), _vmem_limit(blocks, scratch)),
        name="memory_cross_attention_block",
    )(h, w_cq, kv, w_co, gain.reshape(1, d), bias.reshape(1, d))


def kernel(x, mem, rel_bias_table, w_in, sinks, lambda_q1, lambda_k1, lambda_q2, lambda_k2, subln_w,
           w_branch_a, w_branch_b, w_o, ln1_g, ln1_b, w_cq, w_mem_kv, w_co, ln2_g, ln2_b,
           w_gate_up, w_down, ln3_g, ln3_b):
    b, s, d = x.shape
    assert b == 1 and w_in.shape[0] == DEPTH == 1
    (w_in, sinks, lambda_q1, lambda_k1, lambda_q2, lambda_k2, subln_w, w_branch_a, w_branch_b, w_o, ln1_g, ln1_b,
     w_cq, w_mem_kv, w_co, ln2_g, ln2_b, w_gate_up, w_down, ln3_g, ln3_b) = [
        p.reshape(p.shape[1:]) for p in (
            w_in, sinks, lambda_q1, lambda_k1, lambda_q2, lambda_k2, subln_w, w_branch_a, w_branch_b, w_o, ln1_g,
            ln1_b, w_cq, w_mem_kv, w_co, ln2_g, ln2_b, w_gate_up, w_down, ln3_g, ln3_b)]
    qa_w, kva_w, qb_w = HA * DH_A, KV_A * DH_A, HB * 2 * DH_B
    col_qa, col_ka, col_va = 0, qa_w, qa_w + kva_w
    col_qb = col_va + kva_w
    col_kb, col_vb = col_qb + qb_w, col_qb + 2 * qb_w
    col_ga = col_vb + qb_w
    col_gb = col_ga + d
    tq, tk = TILES["diff_attention"]["tq"], TILES["diff_attention"]["tk"]

    h0 = x.reshape(s, d)

    proj = _matmul(h0, w_in, **TILES["in_proj"], out_dtype=BF16, name="in_proj")

    i_a = np.arange(BLK)[:, None]
    j_a = np.arange(2 * BLK)[None, :]
    dist_a = BLK + i_a - j_a
    band_a = (dist_a >= 0) & (dist_a < WINDOW)
    bias_a = _bias_expand(rel_bias_table, [dist_a, dist_a], [band_a & (j_a >= BLK), band_a], head0=0, heads=HA,
                          shift_last_bucket=False, scale=LOG2E)
    o_a = _swa_attention(proj, sinks, bias_a, q_col=col_qa, k_col=col_ka, v_col=col_va)

    o_b = _diff_attention(proj, rel_bias_table, lambda_q1, lambda_k1, lambda_q2, lambda_k2, subln_w,
                          q_col=col_qb, k_col=col_kb, v_col=col_vb, tq=tq, tk=tk)

    mix = _gated_branches(o_a, o_b, w_branch_a, w_branch_b, proj, col_ga, col_gb, **TILES["gated_branches"])
    h1 = _matmul_resident_ln(mix, w_o, h0, ln1_g, ln1_b, **TILES["out_proj_ln"])

    kvm = _matmul(mem.reshape(mem.shape[1], d), w_mem_kv, tm=mem.shape[1], tn=HC * DH_C,
                  out_dtype=BF16, name="mem_kv")
    h2, h2b = _cross_attention_block(h1, kvm, w_cq, w_co, ln2_g, ln2_b, **TILES["cross_block"])

    act, w_down_bf = _swiglu_up(h2b, w_gate_up, w_down, **TILES["swiglu_up"])
    h3 = _matmul_residual_ln(act, w_down_bf, h2, ln3_g, ln3_b, **TILES["down_proj_ln"], emit_bf16=False)
    return h3.reshape(b, s, d)
```

```python
import functools
import math

import numpy as np
import jax
import jax.numpy as jnp
from jax import lax
from jax.experimental import pallas as pl
from jax.experimental.pallas import tpu as pltpu

F32 = jnp.float32
BF16 = jnp.bfloat16

BLK = 128
WINDOW = 128
HA, KV_A, DH_A = 16, 2, 64
G_A = HA // KV_A
HB, DH_B = 8, 64
N_BUCKETS, MAX_DISTANCE = 32, 128
HC, DH_C = 4, 128
LN_EPS = 1e-5
DEPTH = 1
ALPHA = (2 * DEPTH) ** 0.25
LAMBDA_INIT = 0.8 - 0.6 * math.exp(-0.3 * 0)
LOG2E = math.log2(math.e)

V7X_LANES = 128
V7X_VMEM_BYTES = 64 * 1024 * 1024
V7X_VMEM_TEMP_BYTES = 12 * 1024 * 1024

EPILOGUE_CHUNKS = 4

TILES = dict(
    in_proj=dict(tm=1024, tn=768),
    diff_attention=dict(tq=512, tk=512),
    gated_branches=dict(tm=2048, tn=256),
    out_proj_ln=dict(tm=512),
    cross_block=dict(tm=512),
    swiglu_up=dict(tm=2048, tn=256),
    down_proj_ln=dict(tm=1024, tk=512),
)

NEG = -1e30


def _nbytes(shape, dtype):
    return int(np.prod(shape)) * jnp.dtype(dtype).itemsize


def _vmem_limit(pipelined, scratch=()):
    need = 2 * sum(_nbytes(s, d) for s, d in pipelined) + sum(_nbytes(s, d) for s, d in scratch)
    need += V7X_VMEM_TEMP_BYTES
    assert need <= V7X_VMEM_BYTES - 4 * 1024 * 1024, need
    return need


def _params(semantics, vmem):
    return pltpu.CompilerParams(dimension_semantics=semantics, vmem_limit_bytes=vmem)


def _act_tile(a_ref, abf_ref):
    if abf_ref is None:
        return a_ref[...]

    @pl.when(pl.program_id(1) == 0)
    def _():
        abf_ref[...] = a_ref[...].astype(BF16)

    return abf_ref[...]


def _mm_body(a_ref, w_ref, o_ref, *scratch):
    a = _act_tile(a_ref, scratch[0] if scratch else None)
    o_ref[...] = jnp.dot(a, w_ref[...].astype(BF16), preferred_element_type=F32).astype(o_ref.dtype)


def _matmul(a, w, *, tm, tn, out_dtype, name):
    m, k = a.shape
    n = w.shape[1]
    cast = a.dtype != BF16
    scratch = [((tm, k), BF16)] if cast else []
    blocks = [((tm, k), a.dtype), ((k, tn), w.dtype), ((tm, tn), out_dtype)]
    return pl.pallas_call(
        _mm_body,
        grid=(m // tm, n // tn),
        in_specs=[pl.BlockSpec((tm, k), lambda i, j: (i, 0)),
                  pl.BlockSpec((k, tn), lambda i, j: (0, j))],
        out_specs=pl.BlockSpec((tm, tn), lambda i, j: (i, j)),
        out_shape=jax.ShapeDtypeStruct((m, n), out_dtype),
        scratch_shapes=[pltpu.VMEM(s, d) for s, d in scratch],
        compiler_params=_params(("arbitrary", "arbitrary"), _vmem_limit(blocks, scratch)),
        name=name,
    )(a, w)


def _row_chunks(rows, chunks):
    size = rows // chunks
    return [slice(c * size, (c + 1) * size) for c in range(chunks)]


def _branch_body(oa_ref, ob_ref, wa_ref, wb_ref, ga_ref, gb_ref, o_ref):
    wa = wa_ref[...].astype(BF16)
    wb = wb_ref[...].astype(BF16)
    for rows in _row_chunks(o_ref.shape[0], EPILOGUE_CHUNKS):
        ya = jnp.dot(oa_ref[rows, :], wa, preferred_element_type=F32)
        yb = jnp.dot(ob_ref[rows, :], wb, preferred_element_type=F32)
        ga = jax.nn.sigmoid(ga_ref[rows, :].astype(F32))
        gb = jax.nn.sigmoid(gb_ref[rows, :].astype(F32))
        o_ref[rows, :] = (ga * ya + gb * yb).astype(o_ref.dtype)


def _gated_branches(o_a, o_b, w_a, w_b, proj, ga_col, gb_col, *, tm, tn):
    m, ka = o_a.shape
    kb = o_b.shape[1]
    n = w_a.shape[1]
    ga_blk, gb_blk = ga_col // tn, gb_col // tn
    blocks = [((tm, ka), BF16), ((tm, kb), BF16), ((ka, tn), F32), ((kb, tn), F32),
              ((tm, tn), proj.dtype), ((tm, tn), proj.dtype), ((tm, tn), BF16)]
    return pl.pallas_call(
        _branch_body,
        grid=(m // tm, n // tn),
        in_specs=[pl.BlockSpec((tm, ka), lambda i, j: (i, 0)),
                  pl.BlockSpec((tm, kb), lambda i, j: (i, 0)),
                  pl.BlockSpec((ka, tn), lambda i, j: (0, j)),
                  pl.BlockSpec((kb, tn), lambda i, j: (0, j)),
                  pl.BlockSpec((tm, tn), lambda i, j: (i, ga_blk + j)),
                  pl.BlockSpec((tm, tn), lambda i, j: (i, gb_blk + j))],
        out_specs=pl.BlockSpec((tm, tn), lambda i, j: (i, j)),
        out_shape=jax.ShapeDtypeStruct((m, n), BF16),
        compiler_params=_params(("arbitrary", "arbitrary"), _vmem_limit(blocks)),
        name="gated_branches",
    )(o_a, o_b, w_a, w_b, proj, proj)


def _swiglu_body(a_ref, wg_ref, wu_ref, wd_ref, o_ref, wd_bf_ref):
    a = a_ref[...]
    wg = wg_ref[...].astype(BF16)
    wu = wu_ref[...].astype(BF16)
    for rows in _row_chunks(o_ref.shape[0], 2):
        g = jnp.dot(a[rows, :], wg, preferred_element_type=F32)
        u = jnp.dot(a[rows, :], wu, preferred_element_type=F32)
        o_ref[rows, :] = (g * jax.nn.sigmoid(g) * u).astype(o_ref.dtype)
    wd_bf_ref[...] = wd_ref[...].astype(BF16)


def _swiglu_up(a, w_gate_up, w_down, *, tm, tn):
    m, k = a.shape
    d_ff = w_gate_up.shape[1] // 2
    row_tiles, col_tiles = m // tm, d_ff // tn
    up_blk = d_ff // tn
    kd, nd = w_down.shape
    slab = kd // (row_tiles * col_tiles)
    assert slab * row_tiles * col_tiles == kd and slab % 16 == 0 and a.dtype == BF16
    slab_block = pl.BlockSpec((slab, nd), lambda i, j: (i * col_tiles + j, 0))
    blocks = [((tm, k), BF16), ((k, tn), F32), ((k, tn), F32), ((tm, tn), BF16), ((slab, nd), F32), ((slab, nd), BF16)]
    return pl.pallas_call(
        _swiglu_body,
        grid=(row_tiles, col_tiles),
        in_specs=[pl.BlockSpec((tm, k), lambda i, j: (i, 0)),
                  pl.BlockSpec((k, tn), lambda i, j: (0, j)),
                  pl.BlockSpec((k, tn), lambda i, j: (0, up_blk + j)),
                  slab_block],
        out_specs=[pl.BlockSpec((tm, tn), lambda i, j: (i, j)), slab_block],
        out_shape=[jax.ShapeDtypeStruct((m, d_ff), BF16), jax.ShapeDtypeStruct((kd, nd), BF16)],
        compiler_params=_params(("arbitrary", "arbitrary"), _vmem_limit(blocks)),
        name="swiglu_up",
    )(a, w_gate_up, w_gate_up, w_down)


def _layer_norm_rows(y, g_ref, b_ref):
    mu = jnp.mean(y, axis=-1, keepdims=True)
    yc = y - mu
    var = jnp.mean(yc * yc, axis=-1, keepdims=True)
    return yc * lax.rsqrt(var + LN_EPS) * g_ref[...] + b_ref[...]


def _resident_ln_body(a_ref, w_ref, r_ref, g_ref, b_ref, o_ref, wbf_ref):
    @pl.when(pl.program_id(0) == 0)
    def _():
        wbf_ref[...] = w_ref[...].astype(BF16)

    halves = _row_chunks(o_ref.shape[0], 2)
    ys = [ALPHA * r_ref[rows, :] + jnp.dot(a_ref[rows, :], wbf_ref[...], preferred_element_type=F32)
          for rows in halves]
    for rows, y in zip(halves, ys):
        o_ref[rows, :] = _layer_norm_rows(y, g_ref, b_ref)


def _matmul_resident_ln(a, w, resid, gain, bias, *, tm):
    m, k = a.shape
    n = w.shape[1]
    row_block = pl.BlockSpec((tm, n), lambda i: (i, 0))
    blocks = [((tm, k), BF16), ((tm, n), F32), ((tm, n), F32)]
    single = [((k, n), F32), ((k, n), BF16)]
    return pl.pallas_call(
        _resident_ln_body,
        grid=(m // tm,),
        in_specs=[pl.BlockSpec((tm, k), lambda i: (i, 0)),
                  pl.BlockSpec((k, n), lambda i: (0, 0), pipeline_mode=pl.Buffered(1)),
                  row_block,
                  pl.BlockSpec((1, n), lambda i: (0, 0)),
                  pl.BlockSpec((1, n), lambda i: (0, 0))],
        out_specs=row_block,
        out_shape=jax.ShapeDtypeStruct((m, n), F32),
        scratch_shapes=[pltpu.VMEM((k, n), BF16)],
        compiler_params=_params(("arbitrary",), _vmem_limit(blocks, single)),
        name="matmul_resident_ln",
    )(a, w, resid, gain.reshape(1, n), bias.reshape(1, n))


def _mm_ln_body(a_ref, w_ref, r_ref, g_ref, b_ref, o_ref, *maybe_bf16_ref, nk):
    kk = pl.program_id(1)
    last = kk == nk - 1

    def partial_product(rows=slice(None)):
        return jnp.dot(a_ref[rows, :], w_ref[...].astype(BF16), preferred_element_type=F32)

    if nk > 1:
        @pl.when(kk == 0)
        def _():
            o_ref[...] = ALPHA * r_ref[...] + partial_product()

        @pl.when(jnp.logical_and(kk > 0, jnp.logical_not(last)))
        def _():
            o_ref[...] += partial_product()

    @pl.when(last)
    def _():
        for rows in _row_chunks(o_ref.shape[0], EPILOGUE_CHUNKS):
            seed = o_ref[rows, :] if nk > 1 else ALPHA * r_ref[rows, :]
            out = _layer_norm_rows(seed + partial_product(rows), g_ref, b_ref)
            o_ref[rows, :] = out
            for ob_ref in maybe_bf16_ref:
                ob_ref[rows, :] = out.astype(BF16)


def _matmul_residual_ln(a, w, resid, gain, bias, *, tm, tk, emit_bf16):
    m, k = a.shape
    n = w.shape[1]
    nk = k // tk
    row_block = pl.BlockSpec((tm, n), lambda i, kk: (i, 0))
    blocks = [((tm, tk), BF16), ((tk, n), w.dtype), ((tm, n), F32), ((tm, n), F32)]
    out_specs, out_shape = [row_block], [jax.ShapeDtypeStruct((m, n), F32)]
    if emit_bf16:
        blocks.append(((tm, n), BF16))
        out_specs.append(row_block)
        out_shape.append(jax.ShapeDtypeStruct((m, n), BF16))
    outs = pl.pallas_call(
        functools.partial(_mm_ln_body, nk=nk),
        grid=(m // tm, nk),
        in_specs=[pl.BlockSpec((tm, tk), lambda i, kk: (i, kk)),
                  pl.BlockSpec((tk, n), lambda i, kk: (kk, 0)),
                  row_block,
                  pl.BlockSpec((1, n), lambda i, kk: (0, 0)),
                  pl.BlockSpec((1, n), lambda i, kk: (0, 0))],
        out_specs=out_specs,
        out_shape=out_shape,
        compiler_params=_params(("arbitrary", "arbitrary"), _vmem_limit(blocks)),
        name="matmul_residual_ln",
    )(a, w, resid, gain.reshape(1, n), bias.reshape(1, n))
    return outs if emit_bf16 else outs[0]


def _rel_bucket_np(dist):
    n = np.maximum(dist, 0)
    exact = N_BUCKETS // 2
    logv = (np.log(np.maximum(n, 1).astype(np.float32) / exact) / math.log(MAX_DISTANCE / exact))
    large = exact + (logv.astype(np.float32) * (N_BUCKETS - exact)).astype(np.int32)
    large = np.minimum(large, N_BUCKETS - 1)
    return np.where(n < exact, n, large).astype(np.int32)


def _bias_expand_body(tab_ref, bucket_ref, o_ref, *, head0, shift_last_bucket, scale):
    h = head0 + pl.program_id(0)
    last = tab_ref[N_BUCKETS - 1, h] if shift_last_bucket else 0.0
    for t in range(bucket_ref.shape[0]):
        bucket = bucket_ref[t]
        acc = jnp.full(bucket.shape, NEG, F32)
        for bkt in range(N_BUCKETS):
            acc = jnp.where(bucket == bkt, (tab_ref[bkt, h] - last) * scale, acc)
        o_ref[t] = acc


def _bias_expand(table, dists, valids, *, head0, heads, shift_last_bucket, scale):
    bucket = np.stack([np.where(v, _rel_bucket_np(d), -1) for d, v in zip(dists, valids)]).astype(np.int32)
    nt, r, c = bucket.shape
    blocks = [((nt, r, c), jnp.int32), ((nt, r, c), F32)]
    return pl.pallas_call(
        functools.partial(_bias_expand_body, head0=head0, shift_last_bucket=shift_last_bucket,
                          scale=scale),
        grid=(heads,),
        in_specs=[pl.BlockSpec(memory_space=pltpu.SMEM),
                  pl.BlockSpec((nt, r, c), lambda h: (0, 0, 0))],
        out_specs=pl.BlockSpec((None, nt, r, c), lambda h: (h, 0, 0, 0)),
        out_shape=jax.ShapeDtypeStruct((heads, nt, r, c), F32),
        compiler_params=_params(("arbitrary",), _vmem_limit(blocks)),
        name="bias_expand",
    )(table.astype(F32), jnp.asarray(bucket))


def _swa_body(sink_ref, q_ref, kp_ref, kc_ref, vp_ref, vc_ref, bias_ref, o_ref):
    n = pl.program_id(0)
    half = DH_A
    pairs = G_A // 2
    rows = pairs * BLK
    lane_k = lax.broadcasted_iota(jnp.int32, (2 * BLK, 2 * half), 1)
    row = lax.broadcasted_iota(jnp.int32, (rows, 1), 0)
    lane_o = lax.broadcasted_iota(jnp.int32, (BLK, 2 * half), 1)
    variant = jnp.minimum(n, 1)

    kf = jnp.concatenate([kp_ref[...], kc_ref[...]], axis=0).astype(F32)
    vf = jnp.concatenate([vp_ref[...], vc_ref[...]], axis=0).astype(F32)
    kr = pltpu.roll(kf, half, 1)
    vr = pltpu.roll(vf, half, 1)

    outs = []
    for g in range(KV_A):
        k_own, k_other = (kf, kr) if g == 0 else (kr, kf)
        k_lo = jnp.where(lane_k < half, k_own, 0.0).astype(BF16)
        k_hi = jnp.where(lane_k >= half, k_other, 0.0).astype(BF16)
        v_dup = (jnp.where(lane_k < half, vf, vr) if g == 0 else jnp.where(lane_k < half, vr, vf)).astype(BF16)
        q_stack = jnp.concatenate(
            [q_ref[:, (g * pairs + t) * 2 * half:(g * pairs + t + 1) * 2 * half] for t in range(pairs)], axis=0)
        q_stack = (q_stack.astype(F32) * (DH_A ** -0.5 * LOG2E)).astype(BF16)
        o_par = []
        for par, k_sel in ((0, k_lo), (1, k_hi)):
            sink = jnp.zeros((rows, 1), F32)
            for t in range(pairs):
                sink = jnp.where(row >= t * BLK, sink_ref[2 * (g * pairs + t) + par] * LOG2E, sink)
            s = lax.dot_general(q_stack, k_sel, (((1,), (1,)), ((), ())),
                                preferred_element_type=F32) + bias_ref[variant, g, par]
            mx = jnp.maximum(jnp.max(s, axis=1, keepdims=True), sink)
            p = jnp.exp2(s - mx)
            den = jnp.sum(p, axis=1, keepdims=True) + jnp.exp2(sink - mx)
            o_par.append(jnp.dot(p.astype(BF16), v_dup, preferred_element_type=F32) / den)
        for t in range(pairs):
            outs.append(jnp.where(lane_o < half, o_par[0][t * BLK:(t + 1) * BLK], o_par[1][t * BLK:(t + 1) * BLK]))
    o_ref[...] = jnp.concatenate(outs, axis=1).astype(o_ref.dtype)


def _swa_attention(proj, sinks, bias, *, q_col, k_col, v_col):
    s = proj.shape[0]
    nb = s // BLK
    qw = HA * DH_A
    kvw = KV_A * DH_A
    assert kvw == V7X_LANES and q_col % qw == 0 and k_col % kvw == 0 and v_col % kvw == 0
    kb, vb = k_col // kvw, v_col // kvw
    pairs = G_A // 2
    bias = bias.reshape(KV_A, pairs, 2, 2, BLK, 2 * BLK).transpose(3, 0, 2, 1, 4, 5)
    bias = bias.reshape(2, KV_A, 2, pairs * BLK, 2 * BLK)
    blocks = [((BLK, qw), BF16)] + [((BLK, kvw), BF16)] * 4 + [(bias.shape, F32), ((BLK, qw), BF16)]
    prev = lambda n: jnp.maximum(n - 1, 0)
    return pl.pallas_call(
        _swa_body,
        grid=(nb,),
        in_specs=[pl.BlockSpec(memory_space=pltpu.SMEM),
                  pl.BlockSpec((BLK, qw), lambda n: (n, q_col // qw)),
                  pl.BlockSpec((BLK, kvw), lambda n: (prev(n), kb)),
                  pl.BlockSpec((BLK, kvw), lambda n: (n, kb)),
                  pl.BlockSpec((BLK, kvw), lambda n: (prev(n), vb)),
                  pl.BlockSpec((BLK, kvw), lambda n: (n, vb)),
                  pl.BlockSpec(bias.shape, lambda n: (0, 0, 0, 0, 0))],
        out_specs=pl.BlockSpec((BLK, qw), lambda n: (n, 0)),
        out_shape=jax.ShapeDtypeStruct((s, qw), BF16),
        compiler_params=_params(("arbitrary",), _vmem_limit(blocks)),
        name="swa_sink_attention",
    )(sinks.astype(F32), proj, proj, proj, proj, proj, bias)


def _write_band_bias(tab_ref, bucket_ref, o_ref, h, *, scale, tile_offsets):
    last = tab_ref[N_BUCKETS - 1, h]
    bands = []
    for t in range(2):
        bucket = bucket_ref[t]
        acc = jnp.full(bucket.shape, NEG, F32)
        for bkt in range(N_BUCKETS):
            acc = jnp.where(bucket == bkt, (tab_ref[bkt, h] - last) * scale, acc)
        bands.append(acc)
    sub = MAX_DISTANCE
    nb = o_ref.shape[1] // sub
    for t, base in enumerate(tile_offsets):
        for rb in range(nb):
            for cb in range(nb):
                off = base + sub * (cb - rb)
                if off < 0:
                    blk = jnp.full((sub, sub), NEG, F32)
                elif off < 2 * sub:
                    blk = bands[off // sub]
                else:
                    blk = jnp.zeros((sub, sub), F32)
                o_ref[t, rb * sub:(rb + 1) * sub, cb * sub:(cb + 1) * sub] = blk


def _band_buckets():
    sub = MAX_DISTANCE
    r = np.arange(sub)[:, None]
    c = np.arange(sub)[None, :]
    return np.stack([np.where(c - r + o >= 0, _rel_bucket_np(c - r + o), -1) for o in (0, sub)]).astype(np.int32)


HEADS_PER_STEP = 2


def _diff_body(tab_ref, bucket_ref, q_ref, k_ref, v_ref, lq1_ref, lk1_ref, lq2_ref, lk2_ref, sw_ref,
               o_ref, b2_ref, qz_ref, sa_ref, sb_ref, cma_ref, cmb_ref, acc_ref, m_ref, l_ref, *, tq, tk):
    j = pl.program_id(1)
    width = 2 * DH_B
    heads = range(HEADS_PER_STEP)
    lanes = lambda hh: slice(hh * width, (hh + 1) * width)
    q_rows = lambda blk: slice(blk * tq, (blk + 1) * tq)

    @pl.when(j == 0)
    def _():
        for hh in heads:
            _write_band_bias(tab_ref, bucket_ref, b2_ref.at[hh], HA + pl.program_id(0) * HEADS_PER_STEP + hh,
                             scale=LOG2E, tile_offsets=(tk, 0))

    row = lax.broadcasted_iota(jnp.int32, (width, tq), 0)
    for blk in range(2):
        for hh in heads:
            qt = (q_ref[q_rows(blk), lanes(hh)].astype(F32) * (DH_B ** -0.5 * LOG2E)).T
            qz_ref[blk, hh, 0] = jnp.where(row < DH_B, qt, 0.0).astype(BF16)
            qz_ref[blk, hh, 1] = jnp.where(row >= DH_B, qt, 0.0).astype(BF16)

    def score_chain(hh, c, blk, tile, bias_tile, s_ref, cm_ref):
        kblk = k_ref[pl.ds(pl.multiple_of(tile * tk, tk), tk), lanes(hh)]
        s = jnp.dot(kblk, qz_ref[blk, hh, c], preferred_element_type=F32)
        if bias_tile is not None:
            s = s + b2_ref[hh, bias_tile]
        s_ref[hh, c] = s
        cm_ref[hh, c] = jnp.max(s, axis=0, keepdims=True)

    def accumulate_chain(hh, c, tile, s_ref, cm_ref, first):
        vblk = v_ref[pl.ds(pl.multiple_of(tile * tk, tk), tk), lanes(hh)]
        m_new = cm_ref[hh, c] if first else jnp.maximum(m_ref[hh, c], cm_ref[hh, c])
        p = jnp.exp2(s_ref[hh, c] - m_new)
        l_new = jnp.sum(p, axis=0, keepdims=True)
        pv = lax.dot_general(vblk, p.astype(BF16), (((0,), (0,)), ((), ())), preferred_element_type=F32)
        if first:
            l_ref[hh, c] = l_new
            acc_ref[hh, c] = pv
        else:
            rescale = jnp.exp2(m_ref[hh, c] - m_new)
            l_ref[hh, c] = rescale * l_ref[hh, c] + l_new
            acc_ref[hh, c] = rescale * acc_ref[hh, c] + pv
        m_ref[hh, c] = m_new

    def stage(score_args=None, acc_args=None, first=False):
        for c in range(2):
            for hh in heads:
                if score_args is not None:
                    score_chain(hh, c, *score_args)
                if acc_args is not None:
                    accumulate_chain(hh, c, *acc_args, first)

    def finalize(blk):
        lam = (jnp.exp(jnp.sum(lq1_ref[...] * lk1_ref[...], axis=1, keepdims=True))
               - jnp.exp(jnp.sum(lq2_ref[...] * lk2_ref[...], axis=1, keepdims=True)) + LAMBDA_INIT)
        for hh in heads:
            o = acc_ref[hh, 0] / l_ref[hh, 0] - lam * (acc_ref[hh, 1] / l_ref[hh, 1])
            ms = jnp.mean(o * o, axis=0, keepdims=True)
            o = o * lax.rsqrt(ms + LN_EPS) * sw_ref[...] * (1.0 - LAMBDA_INIT)
            o_ref[q_rows(blk), lanes(hh)] = o.T.astype(o_ref.dtype)

    def far_pairs(blk, i, slot_a, slot_b):
        def pair(t):
            stage((blk, i - (2 * t + 1), None, *slot_b), (i - 2 * t, *slot_a))
            stage((blk, jnp.maximum(i - (2 * t + 2), 0), None, *slot_a), (i - (2 * t + 1), *slot_b))

        count = jnp.maximum((i + 1) // 2 - 1, 0)

        def two_pairs(u, carry):
            pair(2 * u + 1)
            pair(2 * u + 2)
            return carry

        lax.fori_loop(0, count // 2, two_pairs, 0)

        @pl.when(count % 2 == 1)
        def _():
            pair(count)

    slot_x, slot_y = (sa_ref, cma_ref), (sb_ref, cmb_ref)

    i0 = 2 * j
    stage(score_args=(0, i0, 1, *slot_x))

    @pl.when(j >= 1)
    def _():
        stage((0, i0 - 1, 0, *slot_y), (i0, *slot_x), first=True)
        stage((0, i0 - 2, None, *slot_x), (i0 - 1, *slot_y))

    far_pairs(0, i0, slot_x, slot_y)

    i1 = i0 + 1

    @pl.when(j == 0)
    def _():
        stage((1, i1, 1, *slot_y), (0, *slot_x), first=True)

    @pl.when(j >= 1)
    def _():
        stage((1, i1, 1, *slot_y), (0, *slot_x))

    finalize(0)

    stage((1, i1 - 1, 0, *slot_x), (i1, *slot_y), first=True)
    stage((1, jnp.maximum(i1 - 2, 0), None, *slot_y), (i1 - 1, *slot_x))
    far_pairs(1, i1, slot_y, slot_x)
    finalize(1)


def _diff_attention(proj, table, lq1, lk1, lq2, lk2, subln_w, *, q_col, k_col, v_col, tq, tk):
    s = proj.shape[0]
    width = 2 * DH_B
    hps = HEADS_PER_STEP
    wide = hps * width
    assert width == V7X_LANES and tq == tk and HB % hps == 0 and s % (2 * tq) == 0
    assert q_col % wide == 0 and k_col % wide == 0 and v_col % wide == 0
    qb, kb, vb = q_col // wide, k_col // wide, v_col // wide
    assert tq % MAX_DISTANCE == 0
    buckets = _band_buckets()
    blocks = [(buckets.shape, jnp.int32), ((2 * tq, wide), BF16), ((s, wide), BF16), ((s, wide), BF16),
              ((2 * tq, wide), BF16)]
    scores_buf = ((hps, 2, tk, tq), F32)
    stats_buf = ((hps, 2, 1, tq), F32)
    scratch = [((hps, 2, tk, tq), F32), ((2, hps, 2, width, tq), BF16), scores_buf, scores_buf, stats_buf, stats_buf,
               ((hps, 2, width, tq), F32), stats_buf, stats_buf]
    padded = [((hps, 2, 8, tq), F32) if sd == stats_buf else sd for sd in scratch]
    vec = lambda v: v.astype(F32).reshape(1, DH_B)
    small = pl.BlockSpec((1, DH_B), lambda h, j: (0, 0))
    return pl.pallas_call(
        functools.partial(_diff_body, tq=tq, tk=tk),
        grid=(HB // hps, s // (2 * tq)),
        in_specs=[pl.BlockSpec(memory_space=pltpu.SMEM),
                  pl.BlockSpec(buckets.shape, lambda h, j: (0, 0, 0)),
                  pl.BlockSpec((2 * tq, wide), lambda h, j: (j, qb + h)),
                  pl.BlockSpec((s, wide), lambda h, j: (0, kb + h)),
                  pl.BlockSpec((s, wide), lambda h, j: (0, vb + h)),
                  small, small, small, small,
                  pl.BlockSpec((width, 1), lambda h, j: (0, 0))],
        out_specs=pl.BlockSpec((2 * tq, wide), lambda h, j: (j, h)),
        out_shape=jax.ShapeDtypeStruct((s, HB * width), BF16),
        scratch_shapes=[pltpu.VMEM(sh, dt) for sh, dt in scratch],
        compiler_params=_params(("arbitrary", "arbitrary"), _vmem_limit(blocks, padded)),
        name="diff_attention",
    )(table.astype(F32), jnp.asarray(buckets), proj, proj, proj, vec(lq1), vec(lk1), vec(lq2), vec(lk2),
      subln_w.astype(F32).reshape(width, 1))


def _cross_block_body(h_ref, wq_ref, kv_ref, wo_ref, g_ref, b_ref, o_ref, ob_ref, wq_bf_ref, wo_bf_ref):
    @pl.when(pl.program_id(0) == 0)
    def _():
        wq_bf_ref[...] = wq_ref[...].astype(BF16)
        wo_bf_ref[...] = wo_ref[...].astype(BF16)

    scale = DH_C ** -0.5
    halves = _row_chunks(h_ref.shape[0], 2)
    hs = [h_ref[rows, :] for rows in halves]
    qs = [jnp.dot(h.astype(BF16), wq_bf_ref[...], preferred_element_type=F32).astype(BF16) for h in hs]
    ss = [[lax.dot_general(q[:, hd * DH_C:(hd + 1) * DH_C], kv_ref[:, hd * DH_C:(hd + 1) * DH_C],
                           (((1,), (1,)), ((), ())), preferred_element_type=F32) * scale
           for hd in range(HC)] for q in qs]
    ocs = []
    for s_heads in ss:
        outs = []
        for hd, s in enumerate(s_heads):
            vh = kv_ref[:, (HC + hd) * DH_C:(HC + hd + 1) * DH_C]
            mx = jnp.max(s, axis=1, keepdims=True)
            p = jnp.exp(s - mx)
            den = jnp.sum(p, axis=1, keepdims=True)
            outs.append((jnp.dot(p.astype(BF16), vh, preferred_element_type=F32) / den).astype(BF16))
        ocs.append(jnp.concatenate(outs, axis=1))
    ys = [ALPHA * h + jnp.dot(oc, wo_bf_ref[...], preferred_element_type=F32) for h, oc in zip(hs, ocs)]
    for rows, y in zip(halves, ys):
        out = _layer_norm_rows(y, g_ref, b_ref)
        o_ref[rows, :] = out
        ob_ref[rows, :] = out.astype(BF16)


def _cross_attention_block(h, kv, w_cq, w_co, gain, bias, *, tm):
    s, d = h.shape
    mlen, w = kv.shape[0], HC * DH_C
    row_block = pl.BlockSpec((tm, d), lambda i: (i, 0))
    whole = lambda shape: pl.BlockSpec(shape, lambda i: (0,) * len(shape))
    blocks = [((tm, d), F32), ((d, w), F32), ((mlen, 2 * w), BF16), ((w, d), F32), ((tm, d), F32), ((tm, d), BF16)]
    scratch = [((d, w), BF16), ((w, d), BF16)]
    return pl.pallas_call(
        _cross_block_body,
        grid=(s // tm,),
        in_specs=[row_block, whole((d, w)), whole((mlen, 2 * w)), whole((w, d)), whole((1, d)), whole((1, d))],
        out_specs=[row_block, row_block],
        out_shape=[jax.ShapeDtypeStruct((s, d), F32), jax.ShapeDtypeStruct((s, d), BF16)],
        scratch_shapes=[pltpu.VMEM(sh, dt) for sh, dt in scratch],
        compiler_params=_params(("arbitrary",), _vmem_limit(blocks, scratch)),
        name="memory_cross_attention_block",
    )(h, w_cq, kv, w_co, gain.reshape(1, d), bias.reshape(1, d))


def kernel(x, mem, rel_bias_table, w_in, sinks, lambda_q1, lambda_k1, lambda_q2, lambda_k2, subln_w,
           w_branch_a, w_branch_b, w_o, ln1_g, ln1_b, w_cq, w_mem_kv, w_co, ln2_g, ln2_b,
           w_gate_up, w_down, ln3_g, ln3_b):
    b, s, d = x.shape
    assert b == 1 and w_in.shape[0] == DEPTH == 1
    (w_in, sinks, lambda_q1, lambda_k1, lambda_q2, lambda_k2, subln_w, w_branch_a, w_branch_b, w_o, ln1_g, ln1_b,
     w_cq, w_mem_kv, w_co, ln2_g, ln2_b, w_gate_up, w_down, ln3_g, ln3_b) = [
        p.reshape(p.shape[1:]) for p in (
            w_in, sinks, lambda_q1, lambda_k1, lambda_q2, lambda_k2, subln_w, w_branch_a, w_branch_b, w_o, ln1_g,
            ln1_b, w_cq, w_mem_kv, w_co, ln2_g, ln2_b, w_gate_up, w_down, ln3_g, ln3_b)]
    qa_w, kva_w, qb_w = HA * DH_A, KV_A * DH_A, HB * 2 * DH_B
    col_qa, col_ka, col_va = 0, qa_w, qa_w + kva_w
    col_qb = col_va + kva_w
    col_kb, col_vb = col_qb + qb_w, col_qb + 2 * qb_w
    col_ga = col_vb + qb_w
    col_gb = col_ga + d
    tq, tk = TILES["diff_attention"]["tq"], TILES["diff_attention"]["tk"]

    h0 = x.reshape(s, d)

    proj = _matmul(h0, w_in, **TILES["in_proj"], out_dtype=BF16, name="in_proj")

    i_a = np.arange(BLK)[:, None]
    j_a = np.arange(2 * BLK)[None, :]
    dist_a = BLK + i_a - j_a
    band_a = (dist_a >= 0) & (dist_a < WINDOW)
    bias_a = _bias_expand(rel_bias_table, [dist_a, dist_a], [band_a & (j_a >= BLK), band_a], head0=0, heads=HA,
                          shift_last_bucket=False, scale=LOG2E)
    o_a = _swa_attention(proj, sinks, bias_a, q_col=col_qa, k_col=col_ka, v_col=col_va)

    o_b = _diff_attention(proj, rel_bias_table, lambda_q1, lambda_k1, lambda_q2, lambda_k2, subln_w,
                          q_col=col_qb, k_col=col_kb, v_col=col_vb, tq=tq, tk=tk)

    mix = _gated_branches(o_a, o_b, w_branch_a, w_branch_b, proj, col_ga, col_gb, **TILES["gated_branches"])
    h1 = _matmul_resident_ln(mix, w_o, h0, ln1_g, ln1_b, **TILES["out_proj_ln"])

    kvm = _matmul(mem.reshape(mem.shape[1], d), w_mem_kv, tm=mem.shape[1], tn=HC * DH_C,
                  out_dtype=BF16, name="mem_kv")
    h2, h2b = _cross_attention_block(h1, kvm, w_cq, w_co, ln2_g, ln2_b, **TILES["cross_block"])

    act, w_down_bf = _swiglu_up(h2b, w_gate_up, w_down, **TILES["swiglu_up"])
    h3 = _matmul_residual_ln(act, w_down_bf, h2, ln3_g, ln3_b, **TILES["down_proj_ln"], emit_bf16=False)
    return h3.reshape(b, s, d)
```

```python
import functools
import math

import numpy as np
import jax
import jax.numpy as jnp
from jax import lax
from jax.experimental import pallas as pl
from jax.experimental.pallas import tpu as pltpu

F32 = jnp.float32
BF16 = jnp.bfloat16

BLK = 128
WINDOW = 128
HA, KV_A, DH_A = 16, 2, 64
G_A = HA // KV_A
HB, DH_B = 8, 64
N_BUCKETS, MAX_DISTANCE = 32, 128
HC, DH_C = 4, 128
LN_EPS = 1e-5
DEPTH = 1
ALPHA = (2 * DEPTH) ** 0.25
LAMBDA_INIT = 0.8 - 0.6 * math.exp(-0.3 * 0)
LOG2E = math.log2(math.e)

V7X_LANES = 128
V7X_VMEM_BYTES = 64 * 1024 * 1024
V7X_VMEM_TEMP_BYTES = 12 * 1024 * 1024

EPILOGUE_CHUNKS = 4

TILES = dict(
    in_proj=dict(tm=1024, tn=768),
    diff_attention=dict(tq=512, tk=512),
    gated_branches=dict(tm=2048, tn=256),
    out_proj_ln=dict(tm=512),
    cross_block=dict(tm=512),
    swiglu_up=dict(tm=2048, tn=256),
    down_proj_ln=dict(tm=1024, tk=512),
)

NEG = -1e30


def _nbytes(shape, dtype):
    return int(np.prod(shape)) * jnp.dtype(dtype).itemsize


def _vmem_limit(pipelined, scratch=()):
    need = 2 * sum(_nbytes(s, d) for s, d in pipelined) + sum(_nbytes(s, d) for s, d in scratch)
    need += V7X_VMEM_TEMP_BYTES
    assert need <= V7X_VMEM_BYTES - 4 * 1024 * 1024, need
    return need


def _params(semantics, vmem):
    return pltpu.CompilerParams(dimension_semantics=semantics, vmem_limit_bytes=vmem)


def _act_tile(a_ref, abf_ref):
    if abf_ref is None:
        return a_ref[...]

    @pl.when(pl.program_id(1) == 0)
    def _():
        abf_ref[...] = a_ref[...].astype(BF16)

    return abf_ref[...]


def _mm_body(a_ref, w_ref, o_ref, *scratch):
    a = _act_tile(a_ref, scratch[0] if scratch else None)
    o_ref[...] = jnp.dot(a, w_ref[...].astype(BF16), preferred_element_type=F32).astype(o_ref.dtype)


def _matmul(a, w, *, tm, tn, out_dtype, name):
    m, k = a.shape
    n = w.shape[1]
    cast = a.dtype != BF16
    scratch = [((tm, k), BF16)] if cast else []
    blocks = [((tm, k), a.dtype), ((k, tn), w.dtype), ((tm, tn), out_dtype)]
    return pl.pallas_call(
        _mm_body,
        grid=(m // tm, n // tn),
        in_specs=[pl.BlockSpec((tm, k), lambda i, j: (i, 0)),
                  pl.BlockSpec((k, tn), lambda i, j: (0, j))],
        out_specs=pl.BlockSpec((tm, tn), lambda i, j: (i, j)),
        out_shape=jax.ShapeDtypeStruct((m, n), out_dtype),
        scratch_shapes=[pltpu.VMEM(s, d) for s, d in scratch],
        compiler_params=_params(("arbitrary", "arbitrary"), _vmem_limit(blocks, scratch)),
        name=name,
    )(a, w)


def _row_chunks(rows, chunks):
    size = rows // chunks
    return [slice(c * size, (c + 1) * size) for c in range(chunks)]


def _branch_body(oa_ref, ob_ref, wa_ref, wb_ref, ga_ref, gb_ref, o_ref):
    wa = wa_ref[...].astype(BF16)
    wb = wb_ref[...].astype(BF16)
    for rows in _row_chunks(o_ref.shape[0], EPILOGUE_CHUNKS):
        ya = jnp.dot(oa_ref[rows, :], wa, preferred_element_type=F32)
        yb = jnp.dot(ob_ref[rows, :], wb, preferred_element_type=F32)
        ga = jax.nn.sigmoid(ga_ref[rows, :].astype(F32))
        gb = jax.nn.sigmoid(gb_ref[rows, :].astype(F32))
        o_ref[rows, :] = (ga * ya + gb * yb).astype(o_ref.dtype)


def _gated_branches(o_a, o_b, w_a, w_b, proj, ga_col, gb_col, *, tm, tn):
    m, ka = o_a.shape
    kb = o_b.shape[1]
    n = w_a.shape[1]
    ga_blk, gb_blk = ga_col // tn, gb_col // tn
    blocks = [((tm, ka), BF16), ((tm, kb), BF16), ((ka, tn), F32), ((kb, tn), F32),
              ((tm, tn), proj.dtype), ((tm, tn), proj.dtype), ((tm, tn), BF16)]
    return pl.pallas_call(
        _branch_body,
        grid=(m // tm, n // tn),
        in_specs=[pl.BlockSpec((tm, ka), lambda i, j: (i, 0)),
                  pl.BlockSpec((tm, kb), lambda i, j: (i, 0)),
                  pl.BlockSpec((ka, tn), lambda i, j: (0, j)),
                  pl.BlockSpec((kb, tn), lambda i, j: (0, j)),
                  pl.BlockSpec((tm, tn), lambda i, j: (i, ga_blk + j)),
                  pl.BlockSpec((tm, tn), lambda i, j: (i, gb_blk + j))],
        out_specs=pl.BlockSpec((tm, tn), lambda i, j: (i, j)),
        out_shape=jax.ShapeDtypeStruct((m, n), BF16),
        compiler_params=_params(("arbitrary", "arbitrary"), _vmem_limit(blocks)),
        name="gated_branches",
    )(o_a, o_b, w_a, w_b, proj, proj)


def _swiglu_body(a_ref, wg_ref, wu_ref, wd_ref, o_ref, wd_bf_ref):
    a = a_ref[...]
    wg = wg_ref[...].astype(BF16)
    wu = wu_ref[...].astype(BF16)
    for rows in _row_chunks(o_ref.shape[0], 2):
        g = jnp.dot(a[rows, :], wg, preferred_element_type=F32)
        u = jnp.dot(a[rows, :], wu, preferred_element_type=F32)
        o_ref[rows, :] = (g * jax.nn.sigmoid(g) * u).astype(o_ref.dtype)
    wd_bf_ref[...] = wd_ref[...].astype(BF16)


def _swiglu_up(a, w_gate_up, w_down, *, tm, tn):
    m, k = a.shape
    d_ff = w_gate_up.shape[1] // 2
    row_tiles, col_tiles = m // tm, d_ff // tn
    up_blk = d_ff // tn
    kd, nd = w_down.shape
    slab = kd // (row_tiles * col_tiles)
    assert slab * row_tiles * col_tiles == kd and slab % 16 == 0 and a.dtype == BF16
    slab_block = pl.BlockSpec((slab, nd), lambda i, j: (i * col_tiles + j, 0))
    blocks = [((tm, k), BF16), ((k, tn), F32), ((k, tn), F32), ((tm, tn), BF16), ((slab, nd), F32), ((slab, nd), BF16)]
    return pl.pallas_call(
        _swiglu_body,
        grid=(row_tiles, col_tiles),
        in_specs=[pl.BlockSpec((tm, k), lambda i, j: (i, 0)),
                  pl.BlockSpec((k, tn), lambda i, j: (0, j)),
                  pl.BlockSpec((k, tn), lambda i, j: (0, up_blk + j)),
                  slab_block],
        out_specs=[pl.BlockSpec((tm, tn), lambda i, j: (i, j)), slab_block],
        out_shape=[jax.ShapeDtypeStruct((m, d_ff), BF16), jax.ShapeDtypeStruct((kd, nd), BF16)],
        compiler_params=_params(("arbitrary", "arbitrary"), _vmem_limit(blocks)),
        name="swiglu_up",
    )(a, w_gate_up, w_gate_up, w_down)


def _layer_norm_rows(y, g_ref, b_ref):
    mu = jnp.mean(y, axis=-1, keepdims=True)
    yc = y - mu
    var = jnp.mean(yc * yc, axis=-1, keepdims=True)
    return yc * lax.rsqrt(var + LN_EPS) * g_ref[...] + b_ref[...]


def _resident_ln_body(a_ref, w_ref, r_ref, g_ref, b_ref, o_ref, wbf_ref):
    @pl.when(pl.program_id(0) == 0)
    def _():
        wbf_ref[...] = w_ref[...].astype(BF16)

    halves = _row_chunks(o_ref.shape[0], 2)
    ys = [ALPHA * r_ref[rows, :] + jnp.dot(a_ref[rows, :], wbf_ref[...], preferred_element_type=F32)
          for rows in halves]
    for rows, y in zip(halves, ys):
        o_ref[rows, :] = _layer_norm_rows(y, g_ref, b_ref)


def _matmul_resident_ln(a, w, resid, gain, bias, *, tm):
    m, k = a.shape
    n = w.shape[1]
    row_block = pl.BlockSpec((tm, n), lambda i: (i, 0))
    blocks = [((tm, k), BF16), ((tm, n), F32), ((tm, n), F32)]
    single = [((k, n), F32), ((k, n), BF16)]
    return pl.pallas_call(
        _resident_ln_body,
        grid=(m // tm,),
        in_specs=[pl.BlockSpec((tm, k), lambda i: (i, 0)),
                  pl.BlockSpec((k, n), lambda i: (0, 0), pipeline_mode=pl.Buffered(1)),
                  row_block,
                  pl.BlockSpec((1, n), lambda i: (0, 0)),
                  pl.BlockSpec((1, n), lambda i: (0, 0))],
        out_specs=row_block,
        out_shape=jax.ShapeDtypeStruct((m, n), F32),
        scratch_shapes=[pltpu.VMEM((k, n), BF16)],
        compiler_params=_params(("arbitrary",), _vmem_limit(blocks, single)),
        name="matmul_resident_ln",
    )(a, w, resid, gain.reshape(1, n), bias.reshape(1, n))


def _mm_ln_body(a_ref, w_ref, r_ref, g_ref, b_ref, o_ref, *maybe_bf16_ref, nk):
    kk = pl.program_id(1)
    last = kk == nk - 1

    def partial_product(rows=slice(None)):
        return jnp.dot(a_ref[rows, :], w_ref[...].astype(BF16), preferred_element_type=F32)

    if nk > 1:
        @pl.when(kk == 0)
        def _():
            o_ref[...] = ALPHA * r_ref[...] + partial_product()

        @pl.when(jnp.logical_and(kk > 0, jnp.logical_not(last)))
        def _():
            o_ref[...] += partial_product()

    @pl.when(last)
    def _():
        for rows in _row_chunks(o_ref.shape[0], EPILOGUE_CHUNKS):
            seed = o_ref[rows, :] if nk > 1 else ALPHA * r_ref[rows, :]
            out = _layer_norm_rows(seed + partial_product(rows), g_ref, b_ref)
            o_ref[rows, :] = out
            for ob_ref in maybe_bf16_ref:
                ob_ref[rows, :] = out.astype(BF16)


def _matmul_residual_ln(a, w, resid, gain, bias, *, tm, tk, emit_bf16):
    m, k = a.shape
    n = w.shape[1]
    nk = k // tk
    row_block = pl.BlockSpec((tm, n), lambda i, kk: (i, 0))
    blocks = [((tm, tk), BF16), ((tk, n), w.dtype), ((tm, n), F32), ((tm, n), F32)]
    out_specs, out_shape = [row_block], [jax.ShapeDtypeStruct((m, n), F32)]
    if emit_bf16:
        blocks.append(((tm, n), BF16))
        out_specs.append(row_block)
        out_shape.append(jax.ShapeDtypeStruct((m, n), BF16))
    outs = pl.pallas_call(
        functools.partial(_mm_ln_body, nk=nk),
        grid=(m // tm, nk),
        in_specs=[pl.BlockSpec((tm, tk), lambda i, kk: (i, kk)),
                  pl.BlockSpec((tk, n), lambda i, kk: (kk, 0)),
                  row_block,
                  pl.BlockSpec((1, n), lambda i, kk: (0, 0)),
                  pl.BlockSpec((1, n), lambda i, kk: (0, 0))],
        out_specs=out_specs,
        out_shape=out_shape,
        compiler_params=_params(("arbitrary", "arbitrary"), _vmem_limit(blocks)),
        name="matmul_residual_ln",
    )(a, w, resid, gain.reshape(1, n), bias.reshape(1, n))
    return outs if emit_bf16 else outs[0]


def _rel_bucket_np(dist):
    n = np.maximum(dist, 0)
    exact = N_BUCKETS // 2
    logv = (np.log(np.maximum(n, 1).astype(np.float32) / exact) / math.log(MAX_DISTANCE / exact))
    large = exact + (logv.astype(np.float32) * (N_BUCKETS - exact)).astype(np.int32)
    large = np.minimum(large, N_BUCKETS - 1)
    return np.where(n < exact, n, large).astype(np.int32)


def _bucket_values(tab_ref, bucket, h, *, offset, scale):
    acc = jnp.full(bucket.shape, NEG, F32)
    for bkt in range(N_BUCKETS):
        acc = jnp.where(bucket == bkt, (tab_ref[bkt, h] - offset) * scale, acc)
    return acc


def _swa_body(sink_ref, tab_ref, bucket_ref, q_ref, kp_ref, kc_ref, vp_ref, vc_ref, o_ref, bias_ref):
    n = pl.program_id(0)
    half = DH_A
    pairs = G_A // 2
    rows = pairs * BLK

    @pl.when(n == 0)
    def _():
        for h in range(HA):
            g, t, par = h // G_A, (h % G_A) // 2, h % 2
            own = _bucket_values(tab_ref, bucket_ref[0], h, offset=0.0, scale=LOG2E)
            prev = _bucket_values(tab_ref, bucket_ref[1], h, offset=0.0, scale=LOG2E)
            q_rows = slice(t * BLK, (t + 1) * BLK)
            for var, prev_half in ((0, jnp.full((BLK, BLK), NEG, F32)), (1, prev)):
                bias_ref[var, g, par, q_rows, 0:BLK] = prev_half
                bias_ref[var, g, par, q_rows, BLK:2 * BLK] = own

    lane_k = lax.broadcasted_iota(jnp.int32, (2 * BLK, 2 * half), 1)
    row = lax.broadcasted_iota(jnp.int32, (rows, 1), 0)
    lane_o = lax.broadcasted_iota(jnp.int32, (BLK, 2 * half), 1)
    variant = jnp.minimum(n, 1)

    kf = jnp.concatenate([kp_ref[...], kc_ref[...]], axis=0).astype(F32)
    vf = jnp.concatenate([vp_ref[...], vc_ref[...]], axis=0).astype(F32)
    kr = pltpu.roll(kf, half, 1)
    vr = pltpu.roll(vf, half, 1)

    outs = []
    for g in range(KV_A):
        k_own, k_other = (kf, kr) if g == 0 else (kr, kf)
        k_lo = jnp.where(lane_k < half, k_own, 0.0).astype(BF16)
        k_hi = jnp.where(lane_k >= half, k_other, 0.0).astype(BF16)
        v_dup = (jnp.where(lane_k < half, vf, vr) if g == 0 else jnp.where(lane_k < half, vr, vf)).astype(BF16)
        q_stack = jnp.concatenate(
            [q_ref[:, (g * pairs + t) * 2 * half:(g * pairs + t + 1) * 2 * half] for t in range(pairs)], axis=0)
        q_stack = (q_stack.astype(F32) * (DH_A ** -0.5 * LOG2E)).astype(BF16)
        o_par = []
        for par, k_sel in ((0, k_lo), (1, k_hi)):
            sink = jnp.zeros((rows, 1), F32)
            for t in range(pairs):
                sink = jnp.where(row >= t * BLK, sink_ref[2 * (g * pairs + t) + par] * LOG2E, sink)
            s = lax.dot_general(q_stack, k_sel, (((1,), (1,)), ((), ())),
                                preferred_element_type=F32) + bias_ref[variant, g, par]
            mx = jnp.maximum(jnp.max(s, axis=1, keepdims=True), sink)
            p = jnp.exp2(s - mx)
            den = jnp.sum(p, axis=1, keepdims=True) + jnp.exp2(sink - mx)
            o_par.append(jnp.dot(p.astype(BF16), v_dup, preferred_element_type=F32) / den)
        for t in range(pairs):
            outs.append(jnp.where(lane_o < half, o_par[0][t * BLK:(t + 1) * BLK], o_par[1][t * BLK:(t + 1) * BLK]))
    o_ref[...] = jnp.concatenate(outs, axis=1).astype(o_ref.dtype)


def _swa_attention(proj, sinks, table, *, q_col, k_col, v_col):
    s = proj.shape[0]
    nb = s // BLK
    qw = HA * DH_A
    kvw = KV_A * DH_A
    assert kvw == V7X_LANES and q_col % qw == 0 and k_col % kvw == 0 and v_col % kvw == 0 and WINDOW == BLK
    kb, vb = k_col // kvw, v_col // kvw
    qi = np.arange(BLK)[:, None]
    kj = np.arange(BLK)[None, :]
    buckets = np.stack([np.where(qi >= kj, _rel_bucket_np(qi - kj), -1),
                        np.where(qi < kj, _rel_bucket_np(BLK + qi - kj), -1)]).astype(np.int32)
    bias_shape = (2, KV_A, 2, (G_A // 2) * BLK, 2 * BLK)
    blocks = [(buckets.shape, jnp.int32), ((BLK, qw), BF16)] + [((BLK, kvw), BF16)] * 4 + [((BLK, qw), BF16)]
    prev = lambda n: jnp.maximum(n - 1, 0)
    return pl.pallas_call(
        _swa_body,
        grid=(nb,),
        in_specs=[pl.BlockSpec(memory_space=pltpu.SMEM),
                  pl.BlockSpec(memory_space=pltpu.SMEM),
                  pl.BlockSpec(buckets.shape, lambda n: (0, 0, 0)),
                  pl.BlockSpec((BLK, qw), lambda n: (n, q_col // qw)),
                  pl.BlockSpec((BLK, kvw), lambda n: (prev(n), kb)),
                  pl.BlockSpec((BLK, kvw), lambda n: (n, kb)),
                  pl.BlockSpec((BLK, kvw), lambda n: (prev(n), vb)),
                  pl.BlockSpec((BLK, kvw), lambda n: (n, vb))],
        out_specs=pl.BlockSpec((BLK, qw), lambda n: (n, 0)),
        out_shape=jax.ShapeDtypeStruct((s, qw), BF16),
        scratch_shapes=[pltpu.VMEM(bias_shape, F32)],
        compiler_params=_params(("arbitrary",), _vmem_limit(blocks, [(bias_shape, F32)])),
        name="swa_sink_attention",
    )(sinks.astype(F32), table.astype(F32), jnp.asarray(buckets), proj, proj, proj, proj, proj)


def _write_band_bias(tab_ref, bucket_ref, o_ref, h, *, scale, tile_offsets):
    last = tab_ref[N_BUCKETS - 1, h]
    bands = [_bucket_values(tab_ref, bucket_ref[t], h, offset=last, scale=scale) for t in range(2)]
    sub = MAX_DISTANCE
    nb = o_ref.shape[1] // sub
    for t, base in enumerate(tile_offsets):
        for rb in range(nb):
            for cb in range(nb):
                off = base + sub * (cb - rb)
                if off < 0:
                    blk = jnp.full((sub, sub), NEG, F32)
                elif off < 2 * sub:
                    blk = bands[off // sub]
                else:
                    blk = jnp.zeros((sub, sub), F32)
                o_ref[t, rb * sub:(rb + 1) * sub, cb * sub:(cb + 1) * sub] = blk


def _band_buckets():
    sub = MAX_DISTANCE
    r = np.arange(sub)[:, None]
    c = np.arange(sub)[None, :]
    return np.stack([np.where(c - r + o >= 0, _rel_bucket_np(c - r + o), -1) for o in (0, sub)]).astype(np.int32)


HEADS_PER_STEP = 2


def _diff_body(tab_ref, bucket_ref, q_ref, k_ref, v_ref, lq1_ref, lk1_ref, lq2_ref, lk2_ref, sw_ref,
               o_ref, b2_ref, qz_ref, sa_ref, sb_ref, cma_ref, cmb_ref, acc_ref, m_ref, l_ref, *, tq, tk):
    j = pl.program_id(1)
    width = 2 * DH_B
    heads = range(HEADS_PER_STEP)
    lanes = lambda hh: slice(hh * width, (hh + 1) * width)
    q_rows = lambda blk: slice(blk * tq, (blk + 1) * tq)

    @pl.when(j == 0)
    def _():
        for hh in heads:
            _write_band_bias(tab_ref, bucket_ref, b2_ref.at[hh], HA + pl.program_id(0) * HEADS_PER_STEP + hh,
                             scale=LOG2E, tile_offsets=(tk, 0))

    row = lax.broadcasted_iota(jnp.int32, (width, tq), 0)
    for blk in range(2):
        for hh in heads:
            qt = (q_ref[q_rows(blk), lanes(hh)].astype(F32) * (DH_B ** -0.5 * LOG2E)).T
            qz_ref[blk, hh, 0] = jnp.where(row < DH_B, qt, 0.0).astype(BF16)
            qz_ref[blk, hh, 1] = jnp.where(row >= DH_B, qt, 0.0).astype(BF16)

    def score_chain(hh, c, blk, tile, bias_tile, s_ref, cm_ref):
        kblk = k_ref[pl.ds(pl.multiple_of(tile * tk, tk), tk), lanes(hh)]
        s = jnp.dot(kblk, qz_ref[blk, hh, c], preferred_element_type=F32)
        if bias_tile is not None:
            s = s + b2_ref[hh, bias_tile]
        s_ref[hh, c] = s
        cm_ref[hh, c] = jnp.max(s, axis=0, keepdims=True)

    def accumulate_chain(hh, c, tile, s_ref, cm_ref, first):
        vblk = v_ref[pl.ds(pl.multiple_of(tile * tk, tk), tk), lanes(hh)]
        m_new = cm_ref[hh, c] if first else jnp.maximum(m_ref[hh, c], cm_ref[hh, c])
        p = jnp.exp2(s_ref[hh, c] - m_new)
        l_new = jnp.sum(p, axis=0, keepdims=True)
        pv = lax.dot_general(vblk, p.astype(BF16), (((0,), (0,)), ((), ())), preferred_element_type=F32)
        if first:
            l_ref[hh, c] = l_new
            acc_ref[hh, c] = pv
        else:
            rescale = jnp.exp2(m_ref[hh, c] - m_new)
            l_ref[hh, c] = rescale * l_ref[hh, c] + l_new
            acc_ref[hh, c] = rescale * acc_ref[hh, c] + pv
        m_ref[hh, c] = m_new

    def stage(score_args=None, acc_args=None, first=False):
        for c in range(2):
            for hh in heads:
                if score_args is not None:
                    score_chain(hh, c, *score_args)
                if acc_args is not None:
                    accumulate_chain(hh, c, *acc_args, first)

    def finalize(blk):
        lam = (jnp.exp(jnp.sum(lq1_ref[...] * lk1_ref[...], axis=1, keepdims=True))
               - jnp.exp(jnp.sum(lq2_ref[...] * lk2_ref[...], axis=1, keepdims=True)) + LAMBDA_INIT)
        for hh in heads:
            o = acc_ref[hh, 0] / l_ref[hh, 0] - lam * (acc_ref[hh, 1] / l_ref[hh, 1])
            ms = jnp.mean(o * o, axis=0, keepdims=True)
            o = o * lax.rsqrt(ms + LN_EPS) * sw_ref[...] * (1.0 - LAMBDA_INIT)
            o_ref[q_rows(blk), lanes(hh)] = o.T.astype(o_ref.dtype)

    def far_pairs(blk, i, slot_a, slot_b):
        def pair(t):
            stage((blk, i - (2 * t + 1), None, *slot_b), (i - 2 * t, *slot_a))
            stage((blk, jnp.maximum(i - (2 * t + 2), 0), None, *slot_a), (i - (2 * t + 1), *slot_b))

        count = jnp.maximum((i + 1) // 2 - 1, 0)

        def two_pairs(u, carry):
            pair(2 * u + 1)
            pair(2 * u + 2)
            return carry

        lax.fori_loop(0, count // 2, two_pairs, 0)

        @pl.when(count % 2 == 1)
        def _():
            pair(count)

    slot_x, slot_y = (sa_ref, cma_ref), (sb_ref, cmb_ref)

    i0 = 2 * j
    stage(score_args=(0, i0, 1, *slot_x))

    @pl.when(j >= 1)
    def _():
        stage((0, i0 - 1, 0, *slot_y), (i0, *slot_x), first=True)
        stage((0, i0 - 2, None, *slot_x), (i0 - 1, *slot_y))

    far_pairs(0, i0, slot_x, slot_y)

    i1 = i0 + 1

    @pl.when(j == 0)
    def _():
        stage((1, i1, 1, *slot_y), (0, *slot_x), first=True)

    @pl.when(j >= 1)
    def _():
        stage((1, i1, 1, *slot_y), (0, *slot_x))

    finalize(0)

    stage((1, i1 - 1, 0, *slot_x), (i1, *slot_y), first=True)
    stage((1, jnp.maximum(i1 - 2, 0), None, *slot_y), (i1 - 1, *slot_x))
    far_pairs(1, i1, slot_y, slot_x)
    finalize(1)


def _diff_attention(proj, table, lq1, lk1, lq2, lk2, subln_w, *, q_col, k_col, v_col, tq, tk):
    s = proj.shape[0]
    width = 2 * DH_B
    hps = HEADS_PER_STEP
    wide = hps * width
    assert width == V7X_LANES and tq == tk and HB % hps == 0 and s % (2 * tq) == 0
    assert q_col % wide == 0 and k_col % wide == 0 and v_col % wide == 0
    qb, kb, vb = q_col // wide, k_col // wide, v_col // wide
    assert tq % MAX_DISTANCE == 0
    buckets = _band_buckets()
    blocks = [(buckets.shape, jnp.int32), ((2 * tq, wide), BF16), ((s, wide), BF16), ((s, wide), BF16),
              ((2 * tq, wide), BF16)]
    scores_buf = ((hps, 2, tk, tq), F32)
    stats_buf = ((hps, 2, 1, tq), F32)
    scratch = [((hps, 2, tk, tq), F32), ((2, hps, 2, width, tq), BF16), scores_buf, scores_buf, stats_buf, stats_buf,
               ((hps, 2, width, tq), F32), stats_buf, stats_buf]
    padded = [((hps, 2, 8, tq), F32) if sd == stats_buf else sd for sd in scratch]
    vec = lambda v: v.astype(F32).reshape(1, DH_B)
    small = pl.BlockSpec((1, DH_B), lambda h, j: (0, 0))
    return pl.pallas_call(
        functools.partial(_diff_body, tq=tq, tk=tk),
        grid=(HB // hps, s // (2 * tq)),
        in_specs=[pl.BlockSpec(memory_space=pltpu.SMEM),
                  pl.BlockSpec(buckets.shape, lambda h, j: (0, 0, 0)),
                  pl.BlockSpec((2 * tq, wide), lambda h, j: (j, qb + h)),
                  pl.BlockSpec((s, wide), lambda h, j: (0, kb + h)),
                  pl.BlockSpec((s, wide), lambda h, j: (0, vb + h)),
                  small, small, small, small,
                  pl.BlockSpec((width, 1), lambda h, j: (0, 0))],
        out_specs=pl.BlockSpec((2 * tq, wide), lambda h, j: (j, h)),
        out_shape=jax.ShapeDtypeStruct((s, HB * width), BF16),
        scratch_shapes=[pltpu.VMEM(sh, dt) for sh, dt in scratch],
        compiler_params=_params(("arbitrary", "arbitrary"), _vmem_limit(blocks, padded)),
        name="diff_attention",
    )(table.astype(F32), jnp.asarray(buckets), proj, proj, proj, vec(lq1), vec(lk1), vec(lq2), vec(lk2),
      subln_w.astype(F32).reshape(width, 1))


def _cross_block_body(h_ref, wq_ref, kv_ref, wo_ref, g_ref, b_ref, o_ref, ob_ref, wq_bf_ref, wo_bf_ref):
    @pl.when(pl.program_id(0) == 0)
    def _():
        wq_bf_ref[...] = wq_ref[...].astype(BF16)
        wo_bf_ref[...] = wo_ref[...].astype(BF16)

    scale = DH_C ** -0.5
    halves = _row_chunks(h_ref.shape[0], 2)
    hs = [h_ref[rows, :] for rows in halves]
    qs = [jnp.dot(h.astype(BF16), wq_bf_ref[...], preferred_element_type=F32).astype(BF16) for h in hs]
    ss = [[lax.dot_general(q[:, hd * DH_C:(hd + 1) * DH_C], kv_ref[:, hd * DH_C:(hd + 1) * DH_C],
                           (((1,), (1,)), ((), ())), preferred_element_type=F32) * scale
           for hd in range(HC)] for q in qs]
    ocs = []
    for s_heads in ss:
        outs = []
        for hd, s in enumerate(s_heads):
            vh = kv_ref[:, (HC + hd) * DH_C:(HC + hd + 1) * DH_C]
            mx = jnp.max(s, axis=1, keepdims=True)
            p = jnp.exp(s - mx)
            den = jnp.sum(p, axis=1, keepdims=True)
            outs.append((jnp.dot(p.astype(BF16), vh, preferred_element_type=F32) / den).astype(BF16))
        ocs.append(jnp.concatenate(outs, axis=1))
    ys = [ALPHA * h + jnp.dot(oc, wo_bf_ref[...], preferred_element_type=F32) for h, oc in zip(hs, ocs)]
    for rows, y in zip(halves, ys):
        out = _layer_norm_rows(y, g_ref, b_ref)
        o_ref[rows, :] = out
        ob_ref[rows, :] = out.astype(BF16)


def _cross_attention_block(h, kv, w_cq, w_co, gain, bias, *, tm):
    s, d = h.shape
    mlen, w = kv.shape[0], HC * DH_C
    row_block = pl.BlockSpec((tm, d), lambda i: (i, 0))
    whole = lambda shape: pl.BlockSpec(shape, lambda i: (0,) * len(shape))
    blocks = [((tm, d), F32), ((d, w), F32), ((mlen, 2 * w), BF16), ((w, d), F32), ((tm, d), F32), ((tm, d), BF16)]
    scratch = [((d, w), BF16), ((w, d), BF16)]
    return pl.pallas_call(
        _cross_block_body,
        grid=(s // tm,),
        in_specs=[row_block, whole((d, w)), whole((mlen, 2 * w)), whole((w, d)), whole((1, d)), whole((1, d))],
        out_specs=[row_block, row_block],
        out_shape=[jax.ShapeDtypeStruct((s, d), F32), jax.ShapeDtypeStruct((s, d), BF16)],
        scratch_shapes=[pltpu.VMEM(sh, dt) for sh, dt in scratch],
        compiler_params=_params(("arbitrary",), _vmem_limit(blocks, scratch)),
        name="memory_cross_attention_block",
    )(h, w_cq, kv, w_co, gain.reshape(1, d), bias.reshape(1, d))


def kernel(x, mem, rel_bias_table, w_in, sinks, lambda_q1, lambda_k1, lambda_q2, lambda_k2, subln_w,
           w_branch_a, w_branch_b, w_o, ln1_g, ln1_b, w_cq, w_mem_kv, w_co, ln2_g, ln2_b,
           w_gate_up, w_down, ln3_g, ln3_b):
    b, s, d = x.shape
    assert b == 1 and w_in.shape[0] == DEPTH == 1
    (w_in, sinks, lambda_q1, lambda_k1, lambda_q2, lambda_k2, subln_w, w_branch_a, w_branch_b, w_o, ln1_g, ln1_b,
     w_cq, w_mem_kv, w_co, ln2_g, ln2_b, w_gate_up, w_down, ln3_g, ln3_b) = [
        p.reshape(p.shape[1:]) for p in (
            w_in, sinks, lambda_q1, lambda_k1, lambda_q2, lambda_k2, subln_w, w_branch_a, w_branch_b, w_o, ln1_g,
            ln1_b, w_cq, w_mem_kv, w_co, ln2_g, ln2_b, w_gate_up, w_down, ln3_g, ln3_b)]
    qa_w, kva_w, qb_w = HA * DH_A, KV_A * DH_A, HB * 2 * DH_B
    col_qa, col_ka, col_va = 0, qa_w, qa_w + kva_w
    col_qb = col_va + kva_w
    col_kb, col_vb = col_qb + qb_w, col_qb + 2 * qb_w
    col_ga = col_vb + qb_w
    col_gb = col_ga + d
    tq, tk = TILES["diff_attention"]["tq"], TILES["diff_attention"]["tk"]

    h0 = x.reshape(s, d)

    proj = _matmul(h0, w_in, **TILES["in_proj"], out_dtype=BF16, name="in_proj")

    o_a = _swa_attention(proj, sinks, rel_bias_table, q_col=col_qa, k_col=col_ka, v_col=col_va)

    o_b = _diff_attention(proj, rel_bias_table, lambda_q1, lambda_k1, lambda_q2, lambda_k2, subln_w,
                          q_col=col_qb, k_col=col_kb, v_col=col_vb, tq=tq, tk=tk)

    mix = _gated_branches(o_a, o_b, w_branch_a, w_branch_b, proj, col_ga, col_gb, **TILES["gated_branches"])
    h1 = _matmul_resident_ln(mix, w_o, h0, ln1_g, ln1_b, **TILES["out_proj_ln"])

    kvm = _matmul(mem.reshape(mem.shape[1], d), w_mem_kv, tm=mem.shape[1], tn=HC * DH_C,
                  out_dtype=BF16, name="mem_kv")
    h2, h2b = _cross_attention_block(h1, kvm, w_cq, w_co, ln2_g, ln2_b, **TILES["cross_block"])

    act, w_down_bf = _swiglu_up(h2b, w_gate_up, w_down, **TILES["swiglu_up"])
    h3 = _matmul_residual_ln(act, w_down_bf, h2, ln3_g, ln3_b, **TILES["down_proj_ln"], emit_bf16=False)
    return h3.reshape(b, s, d)
```

```python
import functools
import math

import numpy as np
import jax
import jax.numpy as jnp
from jax import lax
from jax.experimental import pallas as pl
from jax.experimental.pallas import tpu as pltpu

F32 = jnp.float32
BF16 = jnp.bfloat16

BLK = 128
WINDOW = 128
HA, KV_A, DH_A = 16, 2, 64
G_A = HA // KV_A
HB, DH_B = 8, 64
N_BUCKETS, MAX_DISTANCE = 32, 128
HC, DH_C = 4, 128
LN_EPS = 1e-5
DEPTH = 1
ALPHA = (2 * DEPTH) ** 0.25
LAMBDA_INIT = 0.8 - 0.6 * math.exp(-0.3 * 0)
LOG2E = math.log2(math.e)

V7X_LANES = 128
V7X_VMEM_BYTES = 64 * 1024 * 1024
V7X_VMEM_TEMP_BYTES = 12 * 1024 * 1024

EPILOGUE_CHUNKS = 4

TILES = dict(
    in_proj=dict(tm=1024, tn=768),
    diff_attention=dict(tq=512, tk=512),
    gated_branches=dict(tm=2048, tn=256),
    out_proj_ln=dict(tm=512),
    cross_block=dict(tm=512),
    swiglu_up=dict(tm=2048, tn=256),
    down_proj_ln=dict(tm=1024, tk=512),
)

NEG = -1e30


def _nbytes(shape, dtype):
    return int(np.prod(shape)) * jnp.dtype(dtype).itemsize


def _vmem_limit(pipelined, scratch=()):
    need = 2 * sum(_nbytes(s, d) for s, d in pipelined) + sum(_nbytes(s, d) for s, d in scratch)
    need += V7X_VMEM_TEMP_BYTES
    assert need <= V7X_VMEM_BYTES - 4 * 1024 * 1024, need
    return need


def _params(semantics, vmem):
    return pltpu.CompilerParams(dimension_semantics=semantics, vmem_limit_bytes=vmem)


def _act_tile(a_ref, abf_ref):
    if abf_ref is None:
        return a_ref[...]

    @pl.when(pl.program_id(1) == 0)
    def _():
        abf_ref[...] = a_ref[...].astype(BF16)

    return abf_ref[...]


def _mm_body(a_ref, w_ref, o_ref, *scratch):
    a = _act_tile(a_ref, scratch[0] if scratch else None)
    o_ref[...] = jnp.dot(a, w_ref[...].astype(BF16), preferred_element_type=F32).astype(o_ref.dtype)


def _matmul(a, w, *, tm, tn, out_dtype, name):
    m, k = a.shape
    n = w.shape[1]
    cast = a.dtype != BF16
    scratch = [((tm, k), BF16)] if cast else []
    blocks = [((tm, k), a.dtype), ((k, tn), w.dtype), ((tm, tn), out_dtype)]
    return pl.pallas_call(
        _mm_body,
        grid=(m // tm, n // tn),
        in_specs=[pl.BlockSpec((tm, k), lambda i, j: (i, 0)),
                  pl.BlockSpec((k, tn), lambda i, j: (0, j))],
        out_specs=pl.BlockSpec((tm, tn), lambda i, j: (i, j)),
        out_shape=jax.ShapeDtypeStruct((m, n), out_dtype),
        scratch_shapes=[pltpu.VMEM(s, d) for s, d in scratch],
        compiler_params=_params(("arbitrary", "arbitrary"), _vmem_limit(blocks, scratch)),
        name=name,
    )(a, w)


def _row_chunks(rows, chunks):
    size = rows // chunks
    return [slice(c * size, (c + 1) * size) for c in range(chunks)]


def _branch_body(oa_ref, ob_ref, wa_ref, wb_ref, ga_ref, gb_ref, o_ref):
    wa = wa_ref[...].astype(BF16)
    wb = wb_ref[...].astype(BF16)
    for rows in _row_chunks(o_ref.shape[0], EPILOGUE_CHUNKS):
        ya = jnp.dot(oa_ref[rows, :], wa, preferred_element_type=F32)
        yb = jnp.dot(ob_ref[rows, :], wb, preferred_element_type=F32)
        ga = jax.nn.sigmoid(ga_ref[rows, :].astype(F32))
        gb = jax.nn.sigmoid(gb_ref[rows, :].astype(F32))
        o_ref[rows, :] = (ga * ya + gb * yb).astype(o_ref.dtype)


def _gated_branches(o_a, o_b, w_a, w_b, proj, ga_col, gb_col, *, tm, tn):
    m, ka = o_a.shape
    kb = o_b.shape[1]
    n = w_a.shape[1]
    ga_blk, gb_blk = ga_col // tn, gb_col // tn
    blocks = [((tm, ka), BF16), ((tm, kb), BF16), ((ka, tn), F32), ((kb, tn), F32),
              ((tm, tn), proj.dtype), ((tm, tn), proj.dtype), ((tm, tn), BF16)]
    return pl.pallas_call(
        _branch_body,
        grid=(m // tm, n // tn),
        in_specs=[pl.BlockSpec((tm, ka), lambda i, j: (i, 0)),
                  pl.BlockSpec((tm, kb), lambda i, j: (i, 0)),
                  pl.BlockSpec((ka, tn), lambda i, j: (0, j)),
                  pl.BlockSpec((kb, tn), lambda i, j: (0, j)),
                  pl.BlockSpec((tm, tn), lambda i, j: (i, ga_blk + j)),
                  pl.BlockSpec((tm, tn), lambda i, j: (i, gb_blk + j))],
        out_specs=pl.BlockSpec((tm, tn), lambda i, j: (i, j)),
        out_shape=jax.ShapeDtypeStruct((m, n), BF16),
        compiler_params=_params(("arbitrary", "arbitrary"), _vmem_limit(blocks)),
        name="gated_branches",
    )(o_a, o_b, w_a, w_b, proj, proj)


def _swiglu_body(a_ref, wg_ref, wu_ref, wd_ref, o_ref, wd_bf_ref):
    a = a_ref[...]
    wg = wg_ref[...].astype(BF16)
    wu = wu_ref[...].astype(BF16)
    for rows in _row_chunks(o_ref.shape[0], 2):
        g = jnp.dot(a[rows, :], wg, preferred_element_type=F32)
        u = jnp.dot(a[rows, :], wu, preferred_element_type=F32)
        o_ref[rows, :] = (g * jax.nn.sigmoid(g) * u).astype(o_ref.dtype)
    wd_bf_ref[...] = wd_ref[...].astype(BF16)


def _swiglu_up(a, w_gate_up, w_down, *, tm, tn):
    m, k = a.shape
    d_ff = w_gate_up.shape[1] // 2
    row_tiles, col_tiles = m // tm, d_ff // tn
    up_blk = d_ff // tn
    kd, nd = w_down.shape
    slab = kd // (row_tiles * col_tiles)
    assert slab * row_tiles * col_tiles == kd and slab % 16 == 0 and a.dtype == BF16
    slab_block = pl.BlockSpec((slab, nd), lambda i, j: (i * col_tiles + j, 0))
    blocks = [((tm, k), BF16), ((k, tn), F32), ((k, tn), F32), ((tm, tn), BF16), ((slab, nd), F32), ((slab, nd), BF16)]
    return pl.pallas_call(
        _swiglu_body,
        grid=(row_tiles, col_tiles),
        in_specs=[pl.BlockSpec((tm, k), lambda i, j: (i, 0)),
                  pl.BlockSpec((k, tn), lambda i, j: (0, j)),
                  pl.BlockSpec((k, tn), lambda i, j: (0, up_blk + j)),
                  slab_block],
        out_specs=[pl.BlockSpec((tm, tn), lambda i, j: (i, j)), slab_block],
        out_shape=[jax.ShapeDtypeStruct((m, d_ff), BF16), jax.ShapeDtypeStruct((kd, nd), BF16)],
        compiler_params=_params(("arbitrary", "arbitrary"), _vmem_limit(blocks)),
        name="swiglu_up",
    )(a, w_gate_up, w_gate_up, w_down)


def _layer_norm_rows(y, g_ref, b_ref):
    mu = jnp.mean(y, axis=-1, keepdims=True)
    yc = y - mu
    var = jnp.mean(yc * yc, axis=-1, keepdims=True)
    return yc * lax.rsqrt(var + LN_EPS) * g_ref[...] + b_ref[...]


def _resident_ln_body(a_ref, w_ref, r_ref, g_ref, b_ref, o_ref, wbf_ref):
    @pl.when(pl.program_id(0) == 0)
    def _():
        wbf_ref[...] = w_ref[...].astype(BF16)

    halves = _row_chunks(o_ref.shape[0], 2)
    ys = [ALPHA * r_ref[rows, :] + jnp.dot(a_ref[rows, :], wbf_ref[...], preferred_element_type=F32)
          for rows in halves]
    for rows, y in zip(halves, ys):
        o_ref[rows, :] = _layer_norm_rows(y, g_ref, b_ref)


def _matmul_resident_ln(a, w, resid, gain, bias, *, tm):
    m, k = a.shape
    n = w.shape[1]
    row_block = pl.BlockSpec((tm, n), lambda i: (i, 0))
    blocks = [((tm, k), BF16), ((tm, n), F32), ((tm, n), F32)]
    single = [((k, n), F32), ((k, n), BF16)]
    return pl.pallas_call(
        _resident_ln_body,
        grid=(m // tm,),
        in_specs=[pl.BlockSpec((tm, k), lambda i: (i, 0)),
                  pl.BlockSpec((k, n), lambda i: (0, 0), pipeline_mode=pl.Buffered(1)),
                  row_block,
                  pl.BlockSpec((1, n), lambda i: (0, 0)),
                  pl.BlockSpec((1, n), lambda i: (0, 0))],
        out_specs=row_block,
        out_shape=jax.ShapeDtypeStruct((m, n), F32),
        scratch_shapes=[pltpu.VMEM((k, n), BF16)],
        compiler_params=_params(("arbitrary",), _vmem_limit(blocks, single)),
        name="matmul_resident_ln",
    )(a, w, resid, gain.reshape(1, n), bias.reshape(1, n))


def _mm_ln_body(a_ref, w_ref, r_ref, g_ref, b_ref, o_ref, *maybe_bf16_ref, nk):
    kk = pl.program_id(1)
    last = kk == nk - 1

    def partial_product(rows=slice(None)):
        return jnp.dot(a_ref[rows, :], w_ref[...].astype(BF16), preferred_element_type=F32)

    if nk > 1:
        @pl.when(kk == 0)
        def _():
            o_ref[...] = ALPHA * r_ref[...] + partial_product()

        @pl.when(jnp.logical_and(kk > 0, jnp.logical_not(last)))
        def _():
            o_ref[...] += partial_product()

    @pl.when(last)
    def _():
        for rows in _row_chunks(o_ref.shape[0], EPILOGUE_CHUNKS):
            seed = o_ref[rows, :] if nk > 1 else ALPHA * r_ref[rows, :]
            out = _layer_norm_rows(seed + partial_product(rows), g_ref, b_ref)
            o_ref[rows, :] = out
            for ob_ref in maybe_bf16_ref:
                ob_ref[rows, :] = out.astype(BF16)


def _matmul_residual_ln(a, w, resid, gain, bias, *, tm, tk, emit_bf16):
    m, k = a.shape
    n = w.shape[1]
    nk = k // tk
    row_block = pl.BlockSpec((tm, n), lambda i, kk: (i, 0))
    blocks = [((tm, tk), BF16), ((tk, n), w.dtype), ((tm, n), F32), ((tm, n), F32)]
    out_specs, out_shape = [row_block], [jax.ShapeDtypeStruct((m, n), F32)]
    if emit_bf16:
        blocks.append(((tm, n), BF16))
        out_specs.append(row_block)
        out_shape.append(jax.ShapeDtypeStruct((m, n), BF16))
    outs = pl.pallas_call(
        functools.partial(_mm_ln_body, nk=nk),
        grid=(m // tm, nk),
        in_specs=[pl.BlockSpec((tm, tk), lambda i, kk: (i, kk)),
                  pl.BlockSpec((tk, n), lambda i, kk: (kk, 0)),
                  row_block,
                  pl.BlockSpec((1, n), lambda i, kk: (0, 0)),
                  pl.BlockSpec((1, n), lambda i, kk: (0, 0))],
        out_specs=out_specs,
        out_shape=out_shape,
        compiler_params=_params(("arbitrary", "arbitrary"), _vmem_limit(blocks)),
        name="matmul_residual_ln",
    )(a, w, resid, gain.reshape(1, n), bias.reshape(1, n))
    return outs if emit_bf16 else outs[0]


def _rel_bucket_np(dist):
    n = np.maximum(dist, 0)
    exact = N_BUCKETS // 2
    logv = (np.log(np.maximum(n, 1).astype(np.float32) / exact) / math.log(MAX_DISTANCE / exact))
    large = exact + (logv.astype(np.float32) * (N_BUCKETS - exact)).astype(np.int32)
    large = np.minimum(large, N_BUCKETS - 1)
    return np.where(n < exact, n, large).astype(np.int32)


def _bucket_values(tab_ref, bucket, h, *, offset, scale):
    acc = jnp.full(bucket.shape, NEG, F32)
    for bkt in range(N_BUCKETS):
        acc = jnp.where(bucket == bkt, (tab_ref[bkt, h] - offset) * scale, acc)
    return acc


def _swa_body(sink_ref, tab_ref, bucket_ref, q_ref, kp_ref, kc_ref, vp_ref, vc_ref, o_ref, bias_ref):
    n = pl.program_id(0)
    half = DH_A
    pairs = G_A // 2
    rows = pairs * BLK

    @pl.when(n == 0)
    def _():
        for h in range(HA):
            g, t, par = h // G_A, (h % G_A) // 2, h % 2
            own = _bucket_values(tab_ref, bucket_ref[0], h, offset=0.0, scale=LOG2E)
            prev = _bucket_values(tab_ref, bucket_ref[1], h, offset=0.0, scale=LOG2E)
            q_rows = slice(t * BLK, (t + 1) * BLK)
            for var, prev_half in ((0, jnp.full((BLK, BLK), NEG, F32)), (1, prev)):
                bias_ref[var, g, par, q_rows, 0:BLK] = prev_half
                bias_ref[var, g, par, q_rows, BLK:2 * BLK] = own

    lane_k = lax.broadcasted_iota(jnp.int32, (2 * BLK, 2 * half), 1)
    row = lax.broadcasted_iota(jnp.int32, (rows, 1), 0)
    lane_o = lax.broadcasted_iota(jnp.int32, (BLK, 2 * half), 1)
    variant = jnp.minimum(n, 1)

    kf = jnp.concatenate([kp_ref[...], kc_ref[...]], axis=0).astype(F32)
    vf = jnp.concatenate([vp_ref[...], vc_ref[...]], axis=0).astype(F32)
    kr = pltpu.roll(kf, half, 1)
    vr = pltpu.roll(vf, half, 1)

    outs = []
    for g in range(KV_A):
        k_own, k_other = (kf, kr) if g == 0 else (kr, kf)
        k_lo = jnp.where(lane_k < half, k_own, 0.0).astype(BF16)
        k_hi = jnp.where(lane_k >= half, k_other, 0.0).astype(BF16)
        v_dup = (jnp.where(lane_k < half, vf, vr) if g == 0 else jnp.where(lane_k < half, vr, vf)).astype(BF16)
        q_stack = jnp.concatenate(
            [q_ref[:, (g * pairs + t) * 2 * half:(g * pairs + t + 1) * 2 * half] for t in range(pairs)], axis=0)
        q_stack = (q_stack.astype(F32) * (DH_A ** -0.5 * LOG2E)).astype(BF16)
        o_par = []
        for par, k_sel in ((0, k_lo), (1, k_hi)):
            sink = jnp.zeros((rows, 1), F32)
            for t in range(pairs):
                sink = jnp.where(row >= t * BLK, sink_ref[2 * (g * pairs + t) + par] * LOG2E, sink)
            s = lax.dot_general(q_stack, k_sel, (((1,), (1,)), ((), ())),
                                preferred_element_type=F32) + bias_ref[variant, g, par]
            mx = jnp.maximum(jnp.max(s, axis=1, keepdims=True), sink)
            p = jnp.exp2(s - mx)
            den = jnp.sum(p, axis=1, keepdims=True) + jnp.exp2(sink - mx)
            o_par.append(jnp.dot(p.astype(BF16), v_dup, preferred_element_type=F32) / den)
        for t in range(pairs):
            outs.append(jnp.where(lane_o < half, o_par[0][t * BLK:(t + 1) * BLK], o_par[1][t * BLK:(t + 1) * BLK]))
    o_ref[...] = jnp.concatenate(outs, axis=1).astype(o_ref.dtype)


def _swa_attention(proj, sinks, table, *, q_col, k_col, v_col):
    s = proj.shape[0]
    nb = s // BLK
    qw = HA * DH_A
    kvw = KV_A * DH_A
    assert kvw == V7X_LANES and q_col % qw == 0 and k_col % kvw == 0 and v_col % kvw == 0 and WINDOW == BLK
    kb, vb = k_col // kvw, v_col // kvw
    qi = np.arange(BLK)[:, None]
    kj = np.arange(BLK)[None, :]
    buckets = np.stack([np.where(qi >= kj, _rel_bucket_np(qi - kj), -1),
                        np.where(qi < kj, _rel_bucket_np(BLK + qi - kj), -1)]).astype(np.int32)
    bias_shape = (2, KV_A, 2, (G_A // 2) * BLK, 2 * BLK)
    blocks = [(buckets.shape, jnp.int32), ((BLK, qw), BF16)] + [((BLK, kvw), BF16)] * 4 + [((BLK, qw), BF16)]
    prev = lambda n: jnp.maximum(n - 1, 0)
    return pl.pallas_call(
        _swa_body,
        grid=(nb,),
        in_specs=[pl.BlockSpec(memory_space=pltpu.SMEM),
                  pl.BlockSpec(memory_space=pltpu.SMEM),
                  pl.BlockSpec(buckets.shape, lambda n: (0, 0, 0)),
                  pl.BlockSpec((BLK, qw), lambda n: (n, q_col // qw)),
                  pl.BlockSpec((BLK, kvw), lambda n: (prev(n), kb)),
                  pl.BlockSpec((BLK, kvw), lambda n: (n, kb)),
                  pl.BlockSpec((BLK, kvw), lambda n: (prev(n), vb)),
                  pl.BlockSpec((BLK, kvw), lambda n: (n, vb))],
        out_specs=pl.BlockSpec((BLK, qw), lambda n: (n, 0)),
        out_shape=jax.ShapeDtypeStruct((s, qw), BF16),
        scratch_shapes=[pltpu.VMEM(bias_shape, F32)],
        compiler_params=_params(("arbitrary",), _vmem_limit(blocks, [(bias_shape, F32)])),
        name="swa_sink_attention",
    )(sinks.astype(F32), table.astype(F32), jnp.asarray(buckets), proj, proj, proj, proj, proj)


def _write_band_bias(tab_ref, bucket_ref, o_ref, h, *, scale, tile_offsets):
    last = tab_ref[N_BUCKETS - 1, h]
    bands = [_bucket_values(tab_ref, bucket_ref[t], h, offset=last, scale=scale) for t in range(2)]
    sub = MAX_DISTANCE
    nb = o_ref.shape[1] // sub
    for t, base in enumerate(tile_offsets):
        for rb in range(nb):
            for cb in range(nb):
                off = base + sub * (cb - rb)
                if off < 0:
                    blk = jnp.full((sub, sub), NEG, F32)
                elif off < 2 * sub:
                    blk = bands[off // sub]
                else:
                    blk = jnp.zeros((sub, sub), F32)
                o_ref[t, rb * sub:(rb + 1) * sub, cb * sub:(cb + 1) * sub] = blk


def _band_buckets():
    sub = MAX_DISTANCE
    r = np.arange(sub)[:, None]
    c = np.arange(sub)[None, :]
    return np.stack([np.where(c - r + o >= 0, _rel_bucket_np(c - r + o), -1) for o in (0, sub)]).astype(np.int32)


HEADS_PER_STEP = 2


def _diff_body(tab_ref, bucket_ref, q_ref, k_ref, v_ref, lq1_ref, lk1_ref, lq2_ref, lk2_ref, sw_ref,
               o_ref, b2_ref, qz_ref, sa_ref, sb_ref, cma_ref, cmb_ref, acc_ref, m_ref, l_ref, *, tq, tk):
    j = pl.program_id(1)
    width = 2 * DH_B
    heads = range(HEADS_PER_STEP)
    lanes = lambda hh: slice(hh * width, (hh + 1) * width)
    q_rows = lambda blk: slice(blk * tq, (blk + 1) * tq)

    @pl.when(j == 0)
    def _():
        for hh in heads:
            _write_band_bias(tab_ref, bucket_ref, b2_ref.at[hh], HA + pl.program_id(0) * HEADS_PER_STEP + hh,
                             scale=LOG2E, tile_offsets=(tk, 0))

    row = lax.broadcasted_iota(jnp.int32, (width, tq), 0)
    for blk in range(2):
        for hh in heads:
            qt = (q_ref[q_rows(blk), lanes(hh)].astype(F32) * (DH_B ** -0.5 * LOG2E)).T
            qz_ref[blk, hh, 0] = jnp.where(row < DH_B, qt, 0.0).astype(BF16)
            qz_ref[blk, hh, 1] = jnp.where(row >= DH_B, qt, 0.0).astype(BF16)

    def score_chain(hh, c, blk, tile, bias_tile, s_ref, cm_ref):
        kblk = k_ref[pl.ds(pl.multiple_of(tile * tk, tk), tk), lanes(hh)]
        s = jnp.dot(kblk, qz_ref[blk, hh, c], preferred_element_type=F32)
        if bias_tile is not None:
            s = s + b2_ref[hh, bias_tile]
        s_ref[hh, c] = s
        cm_ref[hh, c] = jnp.max(s, axis=0, keepdims=True)

    def accumulate_chain(hh, c, tile, s_ref, cm_ref, first):
        vblk = v_ref[pl.ds(pl.multiple_of(tile * tk, tk), tk), lanes(hh)]
        m_new = cm_ref[hh, c] if first else jnp.maximum(m_ref[hh, c], cm_ref[hh, c])
        p = jnp.exp2(s_ref[hh, c] - m_new)
        l_new = jnp.sum(p, axis=0, keepdims=True)
        pv = lax.dot_general(vblk, p.astype(BF16), (((0,), (0,)), ((), ())), preferred_element_type=F32)
        if first:
            l_ref[hh, c] = l_new
            acc_ref[hh, c] = pv
        else:
            rescale = jnp.exp2(m_ref[hh, c] - m_new)
            l_ref[hh, c] = rescale * l_ref[hh, c] + l_new
            acc_ref[hh, c] = rescale * acc_ref[hh, c] + pv
        m_ref[hh, c] = m_new

    def stage(score_args=None, acc_args=None, first=False):
        for c in range(2):
            for hh in heads:
                if score_args is not None:
                    score_chain(hh, c, *score_args)
                if acc_args is not None:
                    accumulate_chain(hh, c, *acc_args, first)

    def finalize(blk):
        lam = (jnp.exp(jnp.sum(lq1_ref[...] * lk1_ref[...], axis=1, keepdims=True))
               - jnp.exp(jnp.sum(lq2_ref[...] * lk2_ref[...], axis=1, keepdims=True)) + LAMBDA_INIT)
        for hh in heads:
            o = acc_ref[hh, 0] / l_ref[hh, 0] - lam * (acc_ref[hh, 1] / l_ref[hh, 1])
            ms = jnp.mean(o * o, axis=0, keepdims=True)
            o = o * lax.rsqrt(ms + LN_EPS) * sw_ref[...] * (1.0 - LAMBDA_INIT)
            o_ref[q_rows(blk), lanes(hh)] = o.T.astype(o_ref.dtype)

    def far_pairs(blk, i, slot_a, slot_b):
        def pair(t):
            stage((blk, i - (2 * t + 1), None, *slot_b), (i - 2 * t, *slot_a))
            stage((blk, jnp.maximum(i - (2 * t + 2), 0), None, *slot_a), (i - (2 * t + 1), *slot_b))

        count = jnp.maximum((i + 1) // 2 - 1, 0)

        def two_pairs(u, carry):
            pair(2 * u + 1)
            pair(2 * u + 2)
            return carry

        lax.fori_loop(0, count // 2, two_pairs, 0)

        @pl.when(count % 2 == 1)
        def _():
            pair(count)

    slot_x, slot_y = (sa_ref, cma_ref), (sb_ref, cmb_ref)

    i0 = 2 * j
    stage(score_args=(0, i0, 1, *slot_x))

    @pl.when(j >= 1)
    def _():
        stage((0, i0 - 1, 0, *slot_y), (i0, *slot_x), first=True)
        stage((0, i0 - 2, None, *slot_x), (i0 - 1, *slot_y))

    far_pairs(0, i0, slot_x, slot_y)

    i1 = i0 + 1

    @pl.when(j == 0)
    def _():
        stage((1, i1, 1, *slot_y), (0, *slot_x), first=True)

    @pl.when(j >= 1)
    def _():
        stage((1, i1, 1, *slot_y), (0, *slot_x))

    finalize(0)

    stage((1, i1 - 1, 0, *slot_x), (i1, *slot_y), first=True)
    stage((1, jnp.maximum(i1 - 2, 0), None, *slot_y), (i1 - 1, *slot_x))
    far_pairs(1, i1, slot_y, slot_x)
    finalize(1)


def _diff_attention(proj, table, lq1, lk1, lq2, lk2, subln_w, *, q_col, k_col, v_col, tq, tk):
    s = proj.shape[0]
    width = 2 * DH_B
    hps = HEADS_PER_STEP
    wide = hps * width
    assert width == V7X_LANES and tq == tk and HB % hps == 0 and s % (2 * tq) == 0
    assert q_col % wide == 0 and k_col % wide == 0 and v_col % wide == 0
    qb, kb, vb = q_col // wide, k_col // wide, v_col // wide
    assert tq % MAX_DISTANCE == 0
    buckets = _band_buckets()
    blocks = [(buckets.shape, jnp.int32), ((2 * tq, wide), BF16), ((s, wide), BF16), ((s, wide), BF16),
              ((2 * tq, wide), BF16)]
    scores_buf = ((hps, 2, tk, tq), F32)
    stats_buf = ((hps, 2, 1, tq), F32)
    scratch = [((hps, 2, tk, tq), F32), ((2, hps, 2, width, tq), BF16), scores_buf, scores_buf, stats_buf, stats_buf,
               ((hps, 2, width, tq), F32), stats_buf, stats_buf]
    padded = [((hps, 2, 8, tq), F32) if sd == stats_buf else sd for sd in scratch]
    vec = lambda v: v.astype(F32).reshape(1, DH_B)
    small = pl.BlockSpec((1, DH_B), lambda h, j: (0, 0))
    return pl.pallas_call(
        functools.partial(_diff_body, tq=tq, tk=tk),
        grid=(HB // hps, s // (2 * tq)),
        in_specs=[pl.BlockSpec(memory_space=pltpu.SMEM),
                  pl.BlockSpec(buckets.shape, lambda h, j: (0, 0, 0)),
                  pl.BlockSpec((2 * tq, wide), lambda h, j: (j, qb + h)),
                  pl.BlockSpec((s, wide), lambda h, j: (0, kb + h)),
                  pl.BlockSpec((s, wide), lambda h, j: (0, vb + h)),
                  small, small, small, small,
                  pl.BlockSpec((width, 1), lambda h, j: (0, 0))],
        out_specs=pl.BlockSpec((2 * tq, wide), lambda h, j: (j, h)),
        out_shape=jax.ShapeDtypeStruct((s, HB * width), BF16),
        scratch_shapes=[pltpu.VMEM(sh, dt) for sh, dt in scratch],
        compiler_params=_params(("arbitrary", "arbitrary"), _vmem_limit(blocks, padded)),
        name="diff_attention",
    )(table.astype(F32), jnp.asarray(buckets), proj, proj, proj, vec(lq1), vec(lk1), vec(lq2), vec(lk2),
      subln_w.astype(F32).reshape(width, 1))


def _cross_block_body(h_ref, mem_ref, wkv_ref, wq_ref, wo_ref, g_ref, b_ref, o_ref, ob_ref,
                      wq_bf_ref, wo_bf_ref, kv_ref):
    @pl.when(pl.program_id(0) == 0)
    def _():
        wq_bf_ref[...] = wq_ref[...].astype(BF16)
        wo_bf_ref[...] = wo_ref[...].astype(BF16)
        kv_ref[...] = jnp.dot(mem_ref[...].astype(BF16), wkv_ref[...].astype(BF16),
                              preferred_element_type=F32).astype(BF16)

    scale = DH_C ** -0.5
    halves = _row_chunks(h_ref.shape[0], 2)
    hs = [h_ref[rows, :] for rows in halves]
    qs = [jnp.dot(h.astype(BF16), wq_bf_ref[...], preferred_element_type=F32).astype(BF16) for h in hs]
    ss = [[lax.dot_general(q[:, hd * DH_C:(hd + 1) * DH_C], kv_ref[:, hd * DH_C:(hd + 1) * DH_C],
                           (((1,), (1,)), ((), ())), preferred_element_type=F32) * scale
           for hd in range(HC)] for q in qs]
    ocs = []
    for s_heads in ss:
        outs = []
        for hd, s in enumerate(s_heads):
            vh = kv_ref[:, (HC + hd) * DH_C:(HC + hd + 1) * DH_C]
            mx = jnp.max(s, axis=1, keepdims=True)
            p = jnp.exp(s - mx)
            den = jnp.sum(p, axis=1, keepdims=True)
            outs.append((jnp.dot(p.astype(BF16), vh, preferred_element_type=F32) / den).astype(BF16))
        ocs.append(jnp.concatenate(outs, axis=1))
    ys = [ALPHA * h + jnp.dot(oc, wo_bf_ref[...], preferred_element_type=F32) for h, oc in zip(hs, ocs)]
    for rows, y in zip(halves, ys):
        out = _layer_norm_rows(y, g_ref, b_ref)
        o_ref[rows, :] = out
        ob_ref[rows, :] = out.astype(BF16)


def _cross_attention_block(h, mem, w_mem_kv, w_cq, w_co, gain, bias, *, tm):
    s, d = h.shape
    mlen, w = mem.shape[0], HC * DH_C
    row_block = pl.BlockSpec((tm, d), lambda i: (i, 0))
    whole = lambda shape: pl.BlockSpec(shape, lambda i: (0,) * len(shape), pipeline_mode=pl.Buffered(1))
    blocks = [((tm, d), F32), ((tm, d), F32), ((tm, d), BF16)]
    single = [((mlen, d), F32), ((d, 2 * w), F32), ((d, w), F32), ((w, d), F32)]
    scratch = [((d, w), BF16), ((w, d), BF16), ((mlen, 2 * w), BF16)]
    return pl.pallas_call(
        _cross_block_body,
        grid=(s // tm,),
        in_specs=[row_block, whole((mlen, d)), whole((d, 2 * w)), whole((d, w)), whole((w, d)),
                  whole((1, d)), whole((1, d))],
        out_specs=[row_block, row_block],
        out_shape=[jax.ShapeDtypeStruct((s, d), F32), jax.ShapeDtypeStruct((s, d), BF16)],
        scratch_shapes=[pltpu.VMEM(sh, dt) for sh, dt in scratch],
        compiler_params=_params(("arbitrary",), _vmem_limit(blocks, single + scratch)),
        name="memory_cross_attention_block",
    )(h, mem, w_mem_kv, w_cq, w_co, gain.reshape(1, d), bias.reshape(1, d))


def kernel(x, mem, rel_bias_table, w_in, sinks, lambda_q1, lambda_k1, lambda_q2, lambda_k2, subln_w,
           w_branch_a, w_branch_b, w_o, ln1_g, ln1_b, w_cq, w_mem_kv, w_co, ln2_g, ln2_b,
           w_gate_up, w_down, ln3_g, ln3_b):
    b, s, d = x.shape
    assert b == 1 and w_in.shape[0] == DEPTH == 1
    (w_in, sinks, lambda_q1, lambda_k1, lambda_q2, lambda_k2, subln_w, w_branch_a, w_branch_b, w_o, ln1_g, ln1_b,
     w_cq, w_mem_kv, w_co, ln2_g, ln2_b, w_gate_up, w_down, ln3_g, ln3_b) = [
        p.reshape(p.shape[1:]) for p in (
            w_in, sinks, lambda_q1, lambda_k1, lambda_q2, lambda_k2, subln_w, w_branch_a, w_branch_b, w_o, ln1_g,
            ln1_b, w_cq, w_mem_kv, w_co, ln2_g, ln2_b, w_gate_up, w_down, ln3_g, ln3_b)]
    qa_w, kva_w, qb_w = HA * DH_A, KV_A * DH_A, HB * 2 * DH_B
    col_qa, col_ka, col_va = 0, qa_w, qa_w + kva_w
    col_qb = col_va + kva_w
    col_kb, col_vb = col_qb + qb_w, col_qb + 2 * qb_w
    col_ga = col_vb + qb_w
    col_gb = col_ga + d
    tq, tk = TILES["diff_attention"]["tq"], TILES["diff_attention"]["tk"]

    h0 = x.reshape(s, d)

    proj = _matmul(h0, w_in, **TILES["in_proj"], out_dtype=BF16, name="in_proj")

    o_a = _swa_attention(proj, sinks, rel_bias_table, q_col=col_qa, k_col=col_ka, v_col=col_va)

    o_b = _diff_attention(proj, rel_bias_table, lambda_q1, lambda_k1, lambda_q2, lambda_k2, subln_w,
                          q_col=col_qb, k_col=col_kb, v_col=col_vb, tq=tq, tk=tk)

    mix = _gated_branches(o_a, o_b, w_branch_a, w_branch_b, proj, col_ga, col_gb, **TILES["gated_branches"])
    h1 = _matmul_resident_ln(mix, w_o, h0, ln1_g, ln1_b, **TILES["out_proj_ln"])

    h2, h2b = _cross_attention_block(h1, mem.reshape(mem.shape[1], d), w_mem_kv, w_cq, w_co, ln2_g, ln2_b,
                                     **TILES["cross_block"])

    act, w_down_bf = _swiglu_up(h2b, w_gate_up, w_down, **TILES["swiglu_up"])
    h3 = _matmul_residual_ln(act, w_down_bf, h2, ln3_g, ln3_b, **TILES["down_proj_ln"], emit_bf16=False)
    return h3.reshape(b, s, d)
```

```python
import functools
import math

import numpy as np
import jax
import jax.numpy as jnp
from jax import lax
from jax.experimental import pallas as pl
from jax.experimental.pallas import tpu as pltpu

F32 = jnp.float32
BF16 = jnp.bfloat16

BLK = 128
WINDOW = 128
HA, KV_A, DH_A = 16, 2, 64
G_A = HA // KV_A
HB, DH_B = 8, 64
N_BUCKETS, MAX_DISTANCE = 32, 128
HC, DH_C = 4, 128
LN_EPS = 1e-5
DEPTH = 1
ALPHA = (2 * DEPTH) ** 0.25
LAMBDA_INIT = 0.8 - 0.6 * math.exp(-0.3 * 0)
LOG2E = math.log2(math.e)

V7X_LANES = 128
V7X_VMEM_BYTES = 64 * 1024 * 1024
V7X_VMEM_TEMP_BYTES = 12 * 1024 * 1024

EPILOGUE_CHUNKS = 4

TILES = dict(
    in_proj=dict(tm=1024, tn=768),
    diff_attention=dict(tq=512, tk=512),
    gated_branches=dict(tm=2048, tn=256),
    out_proj_ln=dict(tm=512),
    cross_block=dict(tm=512),
    swiglu_up=dict(tm=2048, tn=256),
    down_proj_ln=dict(tm=1024, tk=512),
)

NEG = -1e30


def _nbytes(shape, dtype):
    return int(np.prod(shape)) * jnp.dtype(dtype).itemsize


def _vmem_limit(pipelined, scratch=()):
    need = 2 * sum(_nbytes(s, d) for s, d in pipelined) + sum(_nbytes(s, d) for s, d in scratch)
    need += V7X_VMEM_TEMP_BYTES
    assert need <= V7X_VMEM_BYTES - 4 * 1024 * 1024, need
    return need


def _params(semantics, vmem):
    return pltpu.CompilerParams(dimension_semantics=semantics, vmem_limit_bytes=vmem)


def _act_tile(a_ref, abf_ref):
    if abf_ref is None:
        return a_ref[...]

    @pl.when(pl.program_id(1) == 0)
    def _():
        abf_ref[...] = a_ref[...].astype(BF16)

    return abf_ref[...]


def _mm_body(a_ref, w_ref, o_ref, *scratch):
    a = _act_tile(a_ref, scratch[0] if scratch else None)
    o_ref[...] = jnp.dot(a, w_ref[...].astype(BF16), preferred_element_type=F32).astype(o_ref.dtype)


def _matmul(a, w, *, tm, tn, out_dtype, name):
    m, k = a.shape
    n = w.shape[1]
    cast = a.dtype != BF16
    scratch = [((tm, k), BF16)] if cast else []
    blocks = [((tm, k), a.dtype), ((k, tn), w.dtype), ((tm, tn), out_dtype)]
    return pl.pallas_call(
        _mm_body,
        grid=(m // tm, n // tn),
        in_specs=[pl.BlockSpec((tm, k), lambda i, j: (i, 0)),
                  pl.BlockSpec((k, tn), lambda i, j: (0, j))],
        out_specs=pl.BlockSpec((tm, tn), lambda i, j: (i, j)),
        out_shape=jax.ShapeDtypeStruct((m, n), out_dtype),
        scratch_shapes=[pltpu.VMEM(s, d) for s, d in scratch],
        compiler_params=_params(("arbitrary", "arbitrary"), _vmem_limit(blocks, scratch)),
        name=name,
    )(a, w)


def _row_chunks(rows, chunks):
    size = rows // chunks
    return [slice(c * size, (c + 1) * size) for c in range(chunks)]


def _branch_body(oa_ref, ob_ref, wa_ref, wb_ref, ga_ref, gb_ref, o_ref):
    wa = wa_ref[...].astype(BF16)
    wb = wb_ref[...].astype(BF16)
    for rows in _row_chunks(o_ref.shape[0], EPILOGUE_CHUNKS):
        ya = jnp.dot(oa_ref[rows, :], wa, preferred_element_type=F32)
        yb = jnp.dot(ob_ref[rows, :], wb, preferred_element_type=F32)
        ga = jax.nn.sigmoid(ga_ref[rows, :].astype(F32))
        gb = jax.nn.sigmoid(gb_ref[rows, :].astype(F32))
        o_ref[rows, :] = (ga * ya + gb * yb).astype(o_ref.dtype)


def _gated_branches(o_a, o_b, w_a, w_b, proj, ga_col, gb_col, *, tm, tn):
    m, ka = o_a.shape
    kb = o_b.shape[1]
    n = w_a.shape[1]
    ga_blk, gb_blk = ga_col // tn, gb_col // tn
    blocks = [((tm, ka), BF16), ((tm, kb), BF16), ((ka, tn), F32), ((kb, tn), F32),
              ((tm, tn), proj.dtype), ((tm, tn), proj.dtype), ((tm, tn), BF16)]
    return pl.pallas_call(
        _branch_body,
        grid=(m // tm, n // tn),
        in_specs=[pl.BlockSpec((tm, ka), lambda i, j: (i, 0)),
                  pl.BlockSpec((tm, kb), lambda i, j: (i, 0)),
                  pl.BlockSpec((ka, tn), lambda i, j: (0, j)),
                  pl.BlockSpec((kb, tn), lambda i, j: (0, j)),
                  pl.BlockSpec((tm, tn), lambda i, j: (i, ga_blk + j)),
                  pl.BlockSpec((tm, tn), lambda i, j: (i, gb_blk + j))],
        out_specs=pl.BlockSpec((tm, tn), lambda i, j: (i, j)),
        out_shape=jax.ShapeDtypeStruct((m, n), BF16),
        compiler_params=_params(("arbitrary", "arbitrary"), _vmem_limit(blocks)),
        name="gated_branches",
    )(o_a, o_b, w_a, w_b, proj, proj)


def _swiglu_body(a_ref, wg_ref, wu_ref, wd_ref, o_ref, wd_bf_ref):
    a = a_ref[...]
    wg = wg_ref[...].astype(BF16)
    wu = wu_ref[...].astype(BF16)
    for rows in _row_chunks(o_ref.shape[0], 2):
        g = jnp.dot(a[rows, :], wg, preferred_element_type=F32)
        u = jnp.dot(a[rows, :], wu, preferred_element_type=F32)
        o_ref[rows, :] = (g * jax.nn.sigmoid(g) * u).astype(o_ref.dtype)
    wd_bf_ref[...] = wd_ref[...].astype(BF16)


def _swiglu_up(a, w_gate_up, w_down, *, tm, tn):
    m, k = a.shape
    d_ff = w_gate_up.shape[1] // 2
    row_tiles, col_tiles = m // tm, d_ff // tn
    up_blk = d_ff // tn
    kd, nd = w_down.shape
    slab = kd // (row_tiles * col_tiles)
    assert slab * row_tiles * col_tiles == kd and slab % 16 == 0 and a.dtype == BF16
    slab_block = pl.BlockSpec((slab, nd), lambda i, j: (i * col_tiles + j, 0))
    blocks = [((tm, k), BF16), ((k, tn), F32), ((k, tn), F32), ((tm, tn), BF16), ((slab, nd), F32), ((slab, nd), BF16)]
    return pl.pallas_call(
        _swiglu_body,
        grid=(row_tiles, col_tiles),
        in_specs=[pl.BlockSpec((tm, k), lambda i, j: (i, 0)),
                  pl.BlockSpec((k, tn), lambda i, j: (0, j)),
                  pl.BlockSpec((k, tn), lambda i, j: (0, up_blk + j)),
                  slab_block],
        out_specs=[pl.BlockSpec((tm, tn), lambda i, j: (i, j)), slab_block],
        out_shape=[jax.ShapeDtypeStruct((m, d_ff), BF16), jax.ShapeDtypeStruct((kd, nd), BF16)],
        compiler_params=_params(("arbitrary", "arbitrary"), _vmem_limit(blocks)),
        name="swiglu_up",
    )(a, w_gate_up, w_gate_up, w_down)


def _layer_norm_rows(y, g_ref, b_ref):
    mu = jnp.mean(y, axis=-1, keepdims=True)
    yc = y - mu
    var = jnp.mean(yc * yc, axis=-1, keepdims=True)
    return yc * lax.rsqrt(var + LN_EPS) * g_ref[...] + b_ref[...]


def _resident_ln_body(a_ref, w_ref, r_ref, g_ref, b_ref, o_ref, wbf_ref):
    @pl.when(pl.program_id(0) == 0)
    def _():
        wbf_ref[...] = w_ref[...].astype(BF16)

    halves = _row_chunks(o_ref.shape[0], 2)
    ys = [ALPHA * r_ref[rows, :] + jnp.dot(a_ref[rows, :], wbf_ref[...], preferred_element_type=F32)
          for rows in halves]
    for rows, y in zip(halves, ys):
        o_ref[rows, :] = _layer_norm_rows(y, g_ref, b_ref)


def _matmul_resident_ln(a, w, resid, gain, bias, *, tm):
    m, k = a.shape
    n = w.shape[1]
    row_block = pl.BlockSpec((tm, n), lambda i: (i, 0))
    blocks = [((tm, k), BF16), ((tm, n), F32), ((tm, n), F32)]
    single = [((k, n), F32), ((k, n), BF16)]
    return pl.pallas_call(
        _resident_ln_body,
        grid=(m // tm,),
        in_specs=[pl.BlockSpec((tm, k), lambda i: (i, 0)),
                  pl.BlockSpec((k, n), lambda i: (0, 0), pipeline_mode=pl.Buffered(1)),
                  row_block,
                  pl.BlockSpec((1, n), lambda i: (0, 0)),
                  pl.BlockSpec((1, n), lambda i: (0, 0))],
        out_specs=row_block,
        out_shape=jax.ShapeDtypeStruct((m, n), F32),
        scratch_shapes=[pltpu.VMEM((k, n), BF16)],
        compiler_params=_params(("arbitrary",), _vmem_limit(blocks, single)),
        name="matmul_resident_ln",
    )(a, w, resid, gain.reshape(1, n), bias.reshape(1, n))


def _mm_ln_body(a_ref, w_ref, r_ref, g_ref, b_ref, o_ref, *, nk):
    kk = pl.program_id(1)
    last = kk == nk - 1

    def partial_product(rows=slice(None)):
        return jnp.dot(a_ref[rows, :], w_ref[...].astype(BF16), preferred_element_type=F32)

    if nk > 1:
        @pl.when(kk == 0)
        def _():
            o_ref[...] = ALPHA * r_ref[...] + partial_product()

        @pl.when(jnp.logical_and(kk > 0, jnp.logical_not(last)))
        def _():
            o_ref[...] += partial_product()

    @pl.when(last)
    def _():
        for rows in _row_chunks(o_ref.shape[0], EPILOGUE_CHUNKS):
            seed = o_ref[rows, :] if nk > 1 else ALPHA * r_ref[rows, :]
            o_ref[rows, :] = _layer_norm_rows(seed + partial_product(rows), g_ref, b_ref)


def _matmul_residual_ln(a, w, resid, gain, bias, *, tm, tk):
    m, k = a.shape
    n = w.shape[1]
    nk = k // tk
    row_block = pl.BlockSpec((tm, n), lambda i, kk: (i, 0))
    blocks = [((tm, tk), BF16), ((tk, n), w.dtype), ((tm, n), F32), ((tm, n), F32)]
    return pl.pallas_call(
        functools.partial(_mm_ln_body, nk=nk),
        grid=(m // tm, nk),
        in_specs=[pl.BlockSpec((tm, tk), lambda i, kk: (i, kk)),
                  pl.BlockSpec((tk, n), lambda i, kk: (kk, 0)),
                  row_block,
                  pl.BlockSpec((1, n), lambda i, kk: (0, 0)),
                  pl.BlockSpec((1, n), lambda i, kk: (0, 0))],
        out_specs=row_block,
        out_shape=jax.ShapeDtypeStruct((m, n), F32),
        compiler_params=_params(("arbitrary", "arbitrary"), _vmem_limit(blocks)),
        name="matmul_residual_ln",
    )(a, w, resid, gain.reshape(1, n), bias.reshape(1, n))


def _rel_bucket_np(dist):
    n = np.maximum(dist, 0)
    exact = N_BUCKETS // 2
    logv = (np.log(np.maximum(n, 1).astype(np.float32) / exact) / math.log(MAX_DISTANCE / exact))
    large = exact + (logv.astype(np.float32) * (N_BUCKETS - exact)).astype(np.int32)
    large = np.minimum(large, N_BUCKETS - 1)
    return np.where(n < exact, n, large).astype(np.int32)


def _bucket_values(tab_ref, bucket, h, *, offset, scale):
    acc = jnp.full(bucket.shape, NEG, F32)
    for bkt in range(N_BUCKETS):
        acc = jnp.where(bucket == bkt, (tab_ref[bkt, h] - offset) * scale, acc)
    return acc


def _swa_body(sink_ref, tab_ref, bucket_ref, q_ref, kp_ref, kc_ref, vp_ref, vc_ref, o_ref, bias_ref):
    n = pl.program_id(0)
    half = DH_A
    pairs = G_A // 2
    rows = pairs * BLK

    @pl.when(n == 0)
    def _():
        for h in range(HA):
            g, t, par = h // G_A, (h % G_A) // 2, h % 2
            own = _bucket_values(tab_ref, bucket_ref[0], h, offset=0.0, scale=LOG2E)
            prev = _bucket_values(tab_ref, bucket_ref[1], h, offset=0.0, scale=LOG2E)
            q_rows = slice(t * BLK, (t + 1) * BLK)
            for var, prev_half in ((0, jnp.full((BLK, BLK), NEG, F32)), (1, prev)):
                bias_ref[var, g, par, q_rows, 0:BLK] = prev_half
                bias_ref[var, g, par, q_rows, BLK:2 * BLK] = own

    lane_k = lax.broadcasted_iota(jnp.int32, (2 * BLK, 2 * half), 1)
    row = lax.broadcasted_iota(jnp.int32, (rows, 1), 0)
    lane_o = lax.broadcasted_iota(jnp.int32, (BLK, 2 * half), 1)
    variant = jnp.minimum(n, 1)

    kf = jnp.concatenate([kp_ref[...], kc_ref[...]], axis=0).astype(F32)
    vf = jnp.concatenate([vp_ref[...], vc_ref[...]], axis=0).astype(F32)
    kr = pltpu.roll(kf, half, 1)
    vr = pltpu.roll(vf, half, 1)

    outs = []
    for g in range(KV_A):
        k_own, k_other = (kf, kr) if g == 0 else (kr, kf)
        k_lo = jnp.where(lane_k < half, k_own, 0.0).astype(BF16)
        k_hi = jnp.where(lane_k >= half, k_other, 0.0).astype(BF16)
        v_dup = (jnp.where(lane_k < half, vf, vr) if g == 0 else jnp.where(lane_k < half, vr, vf)).astype(BF16)
        q_stack = jnp.concatenate(
            [q_ref[:, (g * pairs + t) * 2 * half:(g * pairs + t + 1) * 2 * half] for t in range(pairs)], axis=0)
        q_stack = (q_stack.astype(F32) * (DH_A ** -0.5 * LOG2E)).astype(BF16)
        o_par = []
        for par, k_sel in ((0, k_lo), (1, k_hi)):
            sink = jnp.zeros((rows, 1), F32)
            for t in range(pairs):
                sink = jnp.where(row >= t * BLK, sink_ref[2 * (g * pairs + t) + par] * LOG2E, sink)
            s = lax.dot_general(q_stack, k_sel, (((1,), (1,)), ((), ())),
                                preferred_element_type=F32) + bias_ref[variant, g, par]
            mx = jnp.maximum(jnp.max(s, axis=1, keepdims=True), sink)
            p = jnp.exp2(s - mx)
            den = jnp.sum(p, axis=1, keepdims=True) + jnp.exp2(sink - mx)
            o_par.append(jnp.dot(p.astype(BF16), v_dup, preferred_element_type=F32) / den)
        for t in range(pairs):
            outs.append(jnp.where(lane_o < half, o_par[0][t * BLK:(t + 1) * BLK], o_par[1][t * BLK:(t + 1) * BLK]))
    o_ref[...] = jnp.concatenate(outs, axis=1).astype(o_ref.dtype)


def _swa_attention(proj, sinks, table, *, q_col, k_col, v_col):
    s = proj.shape[0]
    nb = s // BLK
    qw = HA * DH_A
    kvw = KV_A * DH_A
    assert kvw == V7X_LANES and q_col % qw == 0 and k_col % kvw == 0 and v_col % kvw == 0 and WINDOW == BLK
    kb, vb = k_col // kvw, v_col // kvw
    qi = np.arange(BLK)[:, None]
    kj = np.arange(BLK)[None, :]
    buckets = np.stack([np.where(qi >= kj, _rel_bucket_np(qi - kj), -1),
                        np.where(qi < kj, _rel_bucket_np(BLK + qi - kj), -1)]).astype(np.int32)
    bias_shape = (2, KV_A, 2, (G_A // 2) * BLK, 2 * BLK)
    blocks = [(buckets.shape, jnp.int32), ((BLK, qw), BF16)] + [((BLK, kvw), BF16)] * 4 + [((BLK, qw), BF16)]
    prev = lambda n: jnp.maximum(n - 1, 0)
    return pl.pallas_call(
        _swa_body,
        grid=(nb,),
        in_specs=[pl.BlockSpec(memory_space=pltpu.SMEM),
                  pl.BlockSpec(memory_space=pltpu.SMEM),
                  pl.BlockSpec(buckets.shape, lambda n: (0, 0, 0)),
                  pl.BlockSpec((BLK, qw), lambda n: (n, q_col // qw)),
                  pl.BlockSpec((BLK, kvw), lambda n: (prev(n), kb)),
                  pl.BlockSpec((BLK, kvw), lambda n: (n, kb)),
                  pl.BlockSpec((BLK, kvw), lambda n: (prev(n), vb)),
                  pl.BlockSpec((BLK, kvw), lambda n: (n, vb))],
        out_specs=pl.BlockSpec((BLK, qw), lambda n: (n, 0)),
        out_shape=jax.ShapeDtypeStruct((s, qw), BF16),
        scratch_shapes=[pltpu.VMEM(bias_shape, F32)],
        compiler_params=_params(("arbitrary",), _vmem_limit(blocks, [(bias_shape, F32)])),
        name="swa_sink_attention",
    )(sinks.astype(F32), table.astype(F32), jnp.asarray(buckets), proj, proj, proj, proj, proj)


def _write_band_bias(tab_ref, bucket_ref, o_ref, h, *, scale, tile_offsets):
    last = tab_ref[N_BUCKETS - 1, h]
    bands = [_bucket_values(tab_ref, bucket_ref[t], h, offset=last, scale=scale) for t in range(2)]
    sub = MAX_DISTANCE
    nb = o_ref.shape[1] // sub
    for t, base in enumerate(tile_offsets):
        for rb in range(nb):
            for cb in range(nb):
                off = base + sub * (cb - rb)
                if off < 0:
                    blk = jnp.full((sub, sub), NEG, F32)
                elif off < 2 * sub:
                    blk = bands[off // sub]
                else:
                    blk = jnp.zeros((sub, sub), F32)
                o_ref[t, rb * sub:(rb + 1) * sub, cb * sub:(cb + 1) * sub] = blk


def _band_buckets():
    sub = MAX_DISTANCE
    r = np.arange(sub)[:, None]
    c = np.arange(sub)[None, :]
    return np.stack([np.where(c - r + o >= 0, _rel_bucket_np(c - r + o), -1) for o in (0, sub)]).astype(np.int32)


HEADS_PER_STEP = 2


def _diff_body(tab_ref, bucket_ref, q_ref, k_ref, v_ref, lq1_ref, lk1_ref, lq2_ref, lk2_ref, sw_ref,
               o_ref, b2_ref, qz_ref, sa_ref, sb_ref, cma_ref, cmb_ref, acc_ref, m_ref, l_ref, *, tq, tk):
    j = pl.program_id(1)
    width = 2 * DH_B
    heads = range(HEADS_PER_STEP)
    lanes = lambda hh: slice(hh * width, (hh + 1) * width)
    q_rows = lambda blk: slice(blk * tq, (blk + 1) * tq)

    @pl.when(j == 0)
    def _():
        for hh in heads:
            _write_band_bias(tab_ref, bucket_ref, b2_ref.at[hh], HA + pl.program_id(0) * HEADS_PER_STEP + hh,
                             scale=LOG2E, tile_offsets=(tk, 0))

    row = lax.broadcasted_iota(jnp.int32, (width, tq), 0)
    for blk in range(2):
        for hh in heads:
            qt = (q_ref[q_rows(blk), lanes(hh)].astype(F32) * (DH_B ** -0.5 * LOG2E)).T
            qz_ref[blk, hh, 0] = jnp.where(row < DH_B, qt, 0.0).astype(BF16)
            qz_ref[blk, hh, 1] = jnp.where(row >= DH_B, qt, 0.0).astype(BF16)

    def score_chain(hh, c, blk, tile, bias_tile, s_ref, cm_ref):
        kblk = k_ref[pl.ds(pl.multiple_of(tile * tk, tk), tk), lanes(hh)]
        s = jnp.dot(kblk, qz_ref[blk, hh, c], preferred_element_type=F32)
        if bias_tile is not None:
            s = s + b2_ref[hh, bias_tile]
        s_ref[hh, c] = s
        cm_ref[hh, c] = jnp.max(s, axis=0, keepdims=True)

    def accumulate_chain(hh, c, tile, s_ref, cm_ref, first):
        vblk = v_ref[pl.ds(pl.multiple_of(tile * tk, tk), tk), lanes(hh)]
        m_new = cm_ref[hh, c] if first else jnp.maximum(m_ref[hh, c], cm_ref[hh, c])
        p = jnp.exp2(s_ref[hh, c] - m_new)
        l_new = jnp.sum(p, axis=0, keepdims=True)
        pv = lax.dot_general(vblk, p.astype(BF16), (((0,), (0,)), ((), ())), preferred_element_type=F32)
        if first:
            l_ref[hh, c] = l_new
            acc_ref[hh, c] = pv
        else:
            rescale = jnp.exp2(m_ref[hh, c] - m_new)
            l_ref[hh, c] = rescale * l_ref[hh, c] + l_new
            acc_ref[hh, c] = rescale * acc_ref[hh, c] + pv
        m_ref[hh, c] = m_new

    def stage(score_args=None, acc_args=None, first=False):
        for c in range(2):
            for hh in heads:
                if score_args is not None:
                    score_chain(hh, c, *score_args)
                if acc_args is not None:
                    accumulate_chain(hh, c, *acc_args, first)

    def finalize(blk):
        lam = (jnp.exp(jnp.sum(lq1_ref[...] * lk1_ref[...], axis=1, keepdims=True))
               - jnp.exp(jnp.sum(lq2_ref[...] * lk2_ref[...], axis=1, keepdims=True)) + LAMBDA_INIT)
        for hh in heads:
            o = acc_ref[hh, 0] / l_ref[hh, 0] - lam * (acc_ref[hh, 1] / l_ref[hh, 1])
            ms = jnp.mean(o * o, axis=0, keepdims=True)
            o = o * lax.rsqrt(ms + LN_EPS) * sw_ref[...] * (1.0 - LAMBDA_INIT)
            o_ref[q_rows(blk), lanes(hh)] = o.T.astype(o_ref.dtype)

    def far_pairs(blk, i, slot_a, slot_b):
        def pair(t):
            stage((blk, i - (2 * t + 1), None, *slot_b), (i - 2 * t, *slot_a))
            stage((blk, jnp.maximum(i - (2 * t + 2), 0), None, *slot_a), (i - (2 * t + 1), *slot_b))

        count = jnp.maximum((i + 1) // 2 - 1, 0)

        def two_pairs(u, carry):
            pair(2 * u + 1)
            pair(2 * u + 2)
            return carry

        lax.fori_loop(0, count // 2, two_pairs, 0)

        @pl.when(count % 2 == 1)
        def _():
            pair(count)

    slot_x, slot_y = (sa_ref, cma_ref), (sb_ref, cmb_ref)

    i0 = 2 * j
    stage(score_args=(0, i0, 1, *slot_x))

    @pl.when(j >= 1)
    def _():
        stage((0, i0 - 1, 0, *slot_y), (i0, *slot_x), first=True)
        stage((0, i0 - 2, None, *slot_x), (i0 - 1, *slot_y))

    far_pairs(0, i0, slot_x, slot_y)

    i1 = i0 + 1

    @pl.when(j == 0)
    def _():
        stage((1, i1, 1, *slot_y), (0, *slot_x), first=True)

    @pl.when(j >= 1)
    def _():
        stage((1, i1, 1, *slot_y), (0, *slot_x))

    finalize(0)

    stage((1, i1 - 1, 0, *slot_x), (i1, *slot_y), first=True)
    stage((1, jnp.maximum(i1 - 2, 0), None, *slot_y), (i1 - 1, *slot_x))
    far_pairs(1, i1, slot_y, slot_x)
    finalize(1)


def _diff_attention(proj, table, lq1, lk1, lq2, lk2, subln_w, *, q_col, k_col, v_col, tq, tk):
    s = proj.shape[0]
    width = 2 * DH_B
    hps = HEADS_PER_STEP
    wide = hps * width
    assert width == V7X_LANES and tq == tk and HB % hps == 0 and s % (2 * tq) == 0
    assert q_col % wide == 0 and k_col % wide == 0 and v_col % wide == 0
    qb, kb, vb = q_col // wide, k_col // wide, v_col // wide
    assert tq % MAX_DISTANCE == 0
    buckets = _band_buckets()
    blocks = [(buckets.shape, jnp.int32), ((2 * tq, wide), BF16), ((s, wide), BF16), ((s, wide), BF16),
              ((2 * tq, wide), BF16)]
    scores_buf = ((hps, 2, tk, tq), F32)
    stats_buf = ((hps, 2, 1, tq), F32)
    scratch = [((hps, 2, tk, tq), F32), ((2, hps, 2, width, tq), BF16), scores_buf, scores_buf, stats_buf, stats_buf,
               ((hps, 2, width, tq), F32), stats_buf, stats_buf]
    padded = [((hps, 2, 8, tq), F32) if sd == stats_buf else sd for sd in scratch]
    vec = lambda v: v.astype(F32).reshape(1, DH_B)
    small = pl.BlockSpec((1, DH_B), lambda h, j: (0, 0))
    return pl.pallas_call(
        functools.partial(_diff_body, tq=tq, tk=tk),
        grid=(HB // hps, s // (2 * tq)),
        in_specs=[pl.BlockSpec(memory_space=pltpu.SMEM),
                  pl.BlockSpec(buckets.shape, lambda h, j: (0, 0, 0)),
                  pl.BlockSpec((2 * tq, wide), lambda h, j: (j, qb + h)),
                  pl.BlockSpec((s, wide), lambda h, j: (0, kb + h)),
                  pl.BlockSpec((s, wide), lambda h, j: (0, vb + h)),
                  small, small, small, small,
                  pl.BlockSpec((width, 1), lambda h, j: (0, 0))],
        out_specs=pl.BlockSpec((2 * tq, wide), lambda h, j: (j, h)),
        out_shape=jax.ShapeDtypeStruct((s, HB * width), BF16),
        scratch_shapes=[pltpu.VMEM(sh, dt) for sh, dt in scratch],
        compiler_params=_params(("arbitrary", "arbitrary"), _vmem_limit(blocks, padded)),
        name="diff_attention",
    )(table.astype(F32), jnp.asarray(buckets), proj, proj, proj, vec(lq1), vec(lk1), vec(lq2), vec(lk2),
      subln_w.astype(F32).reshape(width, 1))


def _cross_block_body(h_ref, mem_ref, wkv_ref, wq_ref, wo_ref, g_ref, b_ref, o_ref, ob_ref,
                      wq_bf_ref, wo_bf_ref, kv_ref):
    @pl.when(pl.program_id(0) == 0)
    def _():
        wq_bf_ref[...] = wq_ref[...].astype(BF16)
        wo_bf_ref[...] = wo_ref[...].astype(BF16)
        kv_ref[...] = jnp.dot(mem_ref[...].astype(BF16), wkv_ref[...].astype(BF16),
                              preferred_element_type=F32).astype(BF16)

    scale = DH_C ** -0.5
    halves = _row_chunks(h_ref.shape[0], 2)
    hs = [h_ref[rows, :] for rows in halves]
    qs = [jnp.dot(h.astype(BF16), wq_bf_ref[...], preferred_element_type=F32).astype(BF16) for h in hs]
    ss = [[lax.dot_general(q[:, hd * DH_C:(hd + 1) * DH_C], kv_ref[:, hd * DH_C:(hd + 1) * DH_C],
                           (((1,), (1,)), ((), ())), preferred_element_type=F32) * scale
           for hd in range(HC)] for q in qs]
    ocs = []
    for s_heads in ss:
        outs = []
        for hd, s in enumerate(s_heads):
            vh = kv_ref[:, (HC + hd) * DH_C:(HC + hd + 1) * DH_C]
            mx = jnp.max(s, axis=1, keepdims=True)
            p = jnp.exp(s - mx)
            den = jnp.sum(p, axis=1, keepdims=True)
            outs.append((jnp.dot(p.astype(BF16), vh, preferred_element_type=F32) / den).astype(BF16))
        ocs.append(jnp.concatenate(outs, axis=1))
    ys = [ALPHA * h + jnp.dot(oc, wo_bf_ref[...], preferred_element_type=F32) for h, oc in zip(hs, ocs)]
    for rows, y in zip(halves, ys):
        out = _layer_norm_rows(y, g_ref, b_ref)
        o_ref[rows, :] = out
        ob_ref[rows, :] = out.astype(BF16)


def _cross_attention_block(h, mem, w_mem_kv, w_cq, w_co, gain, bias, *, tm):
    s, d = h.shape
    mlen, w = mem.shape[0], HC * DH_C
    row_block = pl.BlockSpec((tm, d), lambda i: (i, 0))
    whole = lambda shape: pl.BlockSpec(shape, lambda i: (0,) * len(shape), pipeline_mode=pl.Buffered(1))
    blocks = [((tm, d), F32), ((tm, d), F32), ((tm, d), BF16)]
    single = [((mlen, d), F32), ((d, 2 * w), F32), ((d, w), F32), ((w, d), F32)]
    scratch = [((d, w), BF16), ((w, d), BF16), ((mlen, 2 * w), BF16)]
    return pl.pallas_call(
        _cross_block_body,
        grid=(s // tm,),
        in_specs=[row_block, whole((mlen, d)), whole((d, 2 * w)), whole((d, w)), whole((w, d)),
                  whole((1, d)), whole((1, d))],
        out_specs=[row_block, row_block],
        out_shape=[jax.ShapeDtypeStruct((s, d), F32), jax.ShapeDtypeStruct((s, d), BF16)],
        scratch_shapes=[pltpu.VMEM(sh, dt) for sh, dt in scratch],
        compiler_params=_params(("arbitrary",), _vmem_limit(blocks, single + scratch)),
        name="memory_cross_attention_block",
    )(h, mem, w_mem_kv, w_cq, w_co, gain.reshape(1, d), bias.reshape(1, d))


def kernel(x, mem, rel_bias_table, w_in, sinks, lambda_q1, lambda_k1, lambda_q2, lambda_k2, subln_w,
           w_branch_a, w_branch_b, w_o, ln1_g, ln1_b, w_cq, w_mem_kv, w_co, ln2_g, ln2_b,
           w_gate_up, w_down, ln3_g, ln3_b):
    b, s, d = x.shape
    assert b == 1 and w_in.shape[0] == DEPTH == 1
    (w_in, sinks, lambda_q1, lambda_k1, lambda_q2, lambda_k2, subln_w, w_branch_a, w_branch_b, w_o, ln1_g, ln1_b,
     w_cq, w_mem_kv, w_co, ln2_g, ln2_b, w_gate_up, w_down, ln3_g, ln3_b) = [
        p.reshape(p.shape[1:]) for p in (
            w_in, sinks, lambda_q1, lambda_k1, lambda_q2, lambda_k2, subln_w, w_branch_a, w_branch_b, w_o, ln1_g,
            ln1_b, w_cq, w_mem_kv, w_co, ln2_g, ln2_b, w_gate_up, w_down, ln3_g, ln3_b)]
    qa_w, kva_w, qb_w = HA * DH_A, KV_A * DH_A, HB * 2 * DH_B
    col_qa, col_ka, col_va = 0, qa_w, qa_w + kva_w
    col_qb = col_va + kva_w
    col_kb, col_vb = col_qb + qb_w, col_qb + 2 * qb_w
    col_ga = col_vb + qb_w
    col_gb = col_ga + d
    tq, tk = TILES["diff_attention"]["tq"], TILES["diff_attention"]["tk"]

    h0 = x.reshape(s, d)

    proj = _matmul(h0, w_in, **TILES["in_proj"], out_dtype=BF16, name="in_proj")

    o_a = _swa_attention(proj, sinks, rel_bias_table, q_col=col_qa, k_col=col_ka, v_col=col_va)

    o_b = _diff_attention(proj, rel_bias_table, lambda_q1, lambda_k1, lambda_q2, lambda_k2, subln_w,
                          q_col=col_qb, k_col=col_kb, v_col=col_vb, tq=tq, tk=tk)

    mix = _gated_branches(o_a, o_b, w_branch_a, w_branch_b, proj, col_ga, col_gb, **TILES["gated_branches"])
    h1 = _matmul_resident_ln(mix, w_o, h0, ln1_g, ln1_b, **TILES["out_proj_ln"])

    h2, h2b = _cross_attention_block(h1, mem.reshape(mem.shape[1], d), w_mem_kv, w_cq, w_co, ln2_g, ln2_b,
                                     **TILES["cross_block"])

    act, w_down_bf = _swiglu_up(h2b, w_gate_up, w_down, **TILES["swiglu_up"])
    h3 = _matmul_residual_ln(act, w_down_bf, h2, ln3_g, ln3_b, **TILES["down_proj_ln"])
    return h3.reshape(b, s, d)
```

```python
import functools
import math

import numpy as np
import jax
import jax.numpy as jnp
from jax import lax
from jax.experimental import pallas as pl
from jax.experimental.pallas import tpu as pltpu

F32 = jnp.float32
BF16 = jnp.bfloat16

BLK = 128
WINDOW = 128
HA, KV_A, DH_A = 16, 2, 64
G_A = HA // KV_A
HB, DH_B = 8, 64
N_BUCKETS, MAX_DISTANCE = 32, 128
HC, DH_C = 4, 128
LN_EPS = 1e-5
DEPTH = 1
ALPHA = (2 * DEPTH) ** 0.25
LAMBDA_INIT = 0.8 - 0.6 * math.exp(-0.3 * 0)
LOG2E = math.log2(math.e)

V7X_LANES = 128
V7X_VMEM_BYTES = 64 * 1024 * 1024
V7X_VMEM_TEMP_BYTES = 12 * 1024 * 1024
VMEM_REQUEST_CAP = V7X_VMEM_BYTES - 4 * 1024 * 1024

EPILOGUE_CHUNKS = 4

TILES = dict(
    in_proj=dict(tm=1024, tn=768),
    diff_attention=dict(tq=512, tk=512),
    gated_branches=dict(tm=2048, tn=256),
    out_proj_ln=dict(tm=512),
    cross_block=dict(tm=512),
    swiglu_up=dict(tm=2048, tn=256),
    down_proj_ln=dict(tm=1024, tk=512),
)

NEG = -1e30


def _nbytes(shape, dtype):
    return int(np.prod(shape)) * jnp.dtype(dtype).itemsize


def _vmem_limit(pipelined, scratch=()):
    need = 2 * sum(_nbytes(s, d) for s, d in pipelined) + sum(_nbytes(s, d) for s, d in scratch)
    need += V7X_VMEM_TEMP_BYTES
    assert need <= VMEM_REQUEST_CAP, need
    return need


def _params(semantics, vmem):
    return pltpu.CompilerParams(dimension_semantics=semantics, vmem_limit_bytes=vmem)


def _act_tile(a_ref, abf_ref):
    if abf_ref is None:
        return a_ref[...]

    @pl.when(pl.program_id(1) == 0)
    def _():
        abf_ref[...] = a_ref[...].astype(BF16)

    return abf_ref[...]


def _mm_body(a_ref, w_ref, o_ref, *scratch):
    a = _act_tile(a_ref, scratch[0] if scratch else None)
    o_ref[...] = jnp.dot(a, w_ref[...].astype(BF16), preferred_element_type=F32).astype(o_ref.dtype)


def _matmul(a, w, *, tm, tn, out_dtype, name):
    m, k = a.shape
    n = w.shape[1]
    cast = a.dtype != BF16
    scratch = [((tm, k), BF16)] if cast else []
    blocks = [((tm, k), a.dtype), ((k, tn), w.dtype), ((tm, tn), out_dtype)]
    return pl.pallas_call(
        _mm_body,
        grid=(m // tm, n // tn),
        in_specs=[pl.BlockSpec((tm, k), lambda i, j: (i, 0)),
                  pl.BlockSpec((k, tn), lambda i, j: (0, j))],
        out_specs=pl.BlockSpec((tm, tn), lambda i, j: (i, j)),
        out_shape=jax.ShapeDtypeStruct((m, n), out_dtype),
        scratch_shapes=[pltpu.VMEM(s, d) for s, d in scratch],
        compiler_params=_params(("arbitrary", "arbitrary"), _vmem_limit(blocks, scratch)),
        name=name,
    )(a, w)


def _row_chunks(rows, chunks):
    size = rows // chunks
    return [slice(c * size, (c + 1) * size) for c in range(chunks)]


def _branch_body(oa_ref, ob_ref, wa_ref, wb_ref, ga_ref, gb_ref, o_ref):
    wa = wa_ref[...].astype(BF16)
    wb = wb_ref[...].astype(BF16)
    for rows in _row_chunks(o_ref.shape[0], EPILOGUE_CHUNKS):
        ya = jnp.dot(oa_ref[rows, :], wa, preferred_element_type=F32)
        yb = jnp.dot(ob_ref[rows, :], wb, preferred_element_type=F32)
        ga = jax.nn.sigmoid(ga_ref[rows, :].astype(F32))
        gb = jax.nn.sigmoid(gb_ref[rows, :].astype(F32))
        o_ref[rows, :] = (ga * ya + gb * yb).astype(o_ref.dtype)


def _gated_branches(o_a, o_b, w_a, w_b, proj, ga_col, gb_col, *, tm, tn):
    m, ka = o_a.shape
    kb = o_b.shape[1]
    n = w_a.shape[1]
    ga_blk, gb_blk = ga_col // tn, gb_col // tn
    blocks = [((tm, ka), BF16), ((tm, kb), BF16), ((ka, tn), F32), ((kb, tn), F32),
              ((tm, tn), proj.dtype), ((tm, tn), proj.dtype), ((tm, tn), BF16)]
    return pl.pallas_call(
        _branch_body,
        grid=(m // tm, n // tn),
        in_specs=[pl.BlockSpec((tm, ka), lambda i, j: (i, 0)),
                  pl.BlockSpec((tm, kb), lambda i, j: (i, 0)),
                  pl.BlockSpec((ka, tn), lambda i, j: (0, j)),
                  pl.BlockSpec((kb, tn), lambda i, j: (0, j)),
                  pl.BlockSpec((tm, tn), lambda i, j: (i, ga_blk + j)),
                  pl.BlockSpec((tm, tn), lambda i, j: (i, gb_blk + j))],
        out_specs=pl.BlockSpec((tm, tn), lambda i, j: (i, j)),
        out_shape=jax.ShapeDtypeStruct((m, n), BF16),
        compiler_params=_params(("arbitrary", "arbitrary"), _vmem_limit(blocks)),
        name="gated_branches",
    )(o_a, o_b, w_a, w_b, proj, proj)


def _swiglu_body(a_ref, wg_ref, wu_ref, wd_ref, o_ref, wd_bf_ref):
    a = a_ref[...]
    wg = wg_ref[...].astype(BF16)
    wu = wu_ref[...].astype(BF16)
    for rows in _row_chunks(o_ref.shape[0], 2):
        g = jnp.dot(a[rows, :], wg, preferred_element_type=F32)
        u = jnp.dot(a[rows, :], wu, preferred_element_type=F32)
        o_ref[rows, :] = (g * jax.nn.sigmoid(g) * u).astype(o_ref.dtype)
    wd_bf_ref[...] = wd_ref[...].astype(BF16)


def _swiglu_up(a, w_gate_up, w_down, *, tm, tn):
    m, k = a.shape
    d_ff = w_gate_up.shape[1] // 2
    row_tiles, col_tiles = m // tm, d_ff // tn
    up_blk = d_ff // tn
    kd, nd = w_down.shape
    slab = kd // (row_tiles * col_tiles)
    assert slab * row_tiles * col_tiles == kd and slab % 16 == 0 and a.dtype == BF16
    slab_block = pl.BlockSpec((slab, nd), lambda i, j: (i * col_tiles + j, 0))
    blocks = [((tm, k), BF16), ((k, tn), F32), ((k, tn), F32), ((tm, tn), BF16), ((slab, nd), F32), ((slab, nd), BF16)]
    return pl.pallas_call(
        _swiglu_body,
        grid=(row_tiles, col_tiles),
        in_specs=[pl.BlockSpec((tm, k), lambda i, j: (i, 0)),
                  pl.BlockSpec((k, tn), lambda i, j: (0, j)),
                  pl.BlockSpec((k, tn), lambda i, j: (0, up_blk + j)),
                  slab_block],
        out_specs=[pl.BlockSpec((tm, tn), lambda i, j: (i, j)), slab_block],
        out_shape=[jax.ShapeDtypeStruct((m, d_ff), BF16), jax.ShapeDtypeStruct((kd, nd), BF16)],
        compiler_params=_params(("arbitrary", "arbitrary"), _vmem_limit(blocks)),
        name="swiglu_up",
    )(a, w_gate_up, w_gate_up, w_down)


def _layer_norm_rows(y, g_ref, b_ref):
    mu = jnp.mean(y, axis=-1, keepdims=True)
    yc = y - mu
    var = jnp.mean(yc * yc, axis=-1, keepdims=True)
    return yc * lax.rsqrt(var + LN_EPS) * g_ref[...] + b_ref[...]


def _resident_ln_body(a_ref, w_ref, r_ref, g_ref, b_ref, o_ref, wbf_ref):
    @pl.when(pl.program_id(0) == 0)
    def _():
        wbf_ref[...] = w_ref[...].astype(BF16)

    halves = _row_chunks(o_ref.shape[0], 2)
    ys = [ALPHA * r_ref[rows, :] + jnp.dot(a_ref[rows, :], wbf_ref[...], preferred_element_type=F32)
          for rows in halves]
    for rows, y in zip(halves, ys):
        o_ref[rows, :] = _layer_norm_rows(y, g_ref, b_ref)


def _matmul_resident_ln(a, w, resid, gain, bias, *, tm):
    m, k = a.shape
    n = w.shape[1]
    row_block = pl.BlockSpec((tm, n), lambda i: (i, 0))
    blocks = [((tm, k), BF16), ((tm, n), F32), ((tm, n), F32)]
    single = [((k, n), F32), ((k, n), BF16)]
    return pl.pallas_call(
        _resident_ln_body,
        grid=(m // tm,),
        in_specs=[pl.BlockSpec((tm, k), lambda i: (i, 0)),
                  pl.BlockSpec((k, n), lambda i: (0, 0), pipeline_mode=pl.Buffered(1)),
                  row_block,
                  pl.BlockSpec((1, n), lambda i: (0, 0)),
                  pl.BlockSpec((1, n), lambda i: (0, 0))],
        out_specs=row_block,
        out_shape=jax.ShapeDtypeStruct((m, n), F32),
        scratch_shapes=[pltpu.VMEM((k, n), BF16)],
        compiler_params=_params(("arbitrary",), _vmem_limit(blocks, single)),
        name="matmul_resident_ln",
    )(a, w, resid, gain.reshape(1, n), bias.reshape(1, n))


def _mm_ln_body(a_ref, w_ref, r_ref, g_ref, b_ref, o_ref, *, nk):
    kk = pl.program_id(1)
    last = kk == nk - 1

    def partial_product(rows=slice(None)):
        return jnp.dot(a_ref[rows, :], w_ref[...].astype(BF16), preferred_element_type=F32)

    if nk > 1:
        @pl.when(kk == 0)
        def _():
            o_ref[...] = ALPHA * r_ref[...] + partial_product()

        @pl.when(jnp.logical_and(kk > 0, jnp.logical_not(last)))
        def _():
            o_ref[...] += partial_product()

    @pl.when(last)
    def _():
        for rows in _row_chunks(o_ref.shape[0], EPILOGUE_CHUNKS):
            seed = o_ref[rows, :] if nk > 1 else ALPHA * r_ref[rows, :]
            o_ref[rows, :] = _layer_norm_rows(seed + partial_product(rows), g_ref, b_ref)


def _matmul_residual_ln(a, w, resid, gain, bias, *, tm, tk):
    m, k = a.shape
    n = w.shape[1]
    nk = k // tk
    row_block = pl.BlockSpec((tm, n), lambda i, kk: (i, 0))
    blocks = [((tm, tk), BF16), ((tk, n), w.dtype), ((tm, n), F32), ((tm, n), F32)]
    return pl.pallas_call(
        functools.partial(_mm_ln_body, nk=nk),
        grid=(m // tm, nk),
        in_specs=[pl.BlockSpec((tm, tk), lambda i, kk: (i, kk)),
                  pl.BlockSpec((tk, n), lambda i, kk: (kk, 0)),
                  row_block,
                  pl.BlockSpec((1, n), lambda i, kk: (0, 0)),
                  pl.BlockSpec((1, n), lambda i, kk: (0, 0))],
        out_specs=row_block,
        out_shape=jax.ShapeDtypeStruct((m, n), F32),
        compiler_params=_params(("arbitrary", "arbitrary"), _vmem_limit(blocks)),
        name="matmul_residual_ln",
    )(a, w, resid, gain.reshape(1, n), bias.reshape(1, n))


def _rel_bucket_np(dist):
    n = np.maximum(dist, 0)
    exact = N_BUCKETS // 2
    logv = (np.log(np.maximum(n, 1).astype(np.float32) / exact) / math.log(MAX_DISTANCE / exact))
    large = exact + (logv.astype(np.float32) * (N_BUCKETS - exact)).astype(np.int32)
    large = np.minimum(large, N_BUCKETS - 1)
    return np.where(n < exact, n, large).astype(np.int32)


def _bucket_values(tab_ref, bucket, h, *, offset, scale):
    acc = jnp.full(bucket.shape, NEG, F32)
    for bkt in range(N_BUCKETS):
        acc = jnp.where(bucket == bkt, (tab_ref[bkt, h] - offset) * scale, acc)
    return acc


def _swa_body(sink_ref, tab_ref, bucket_ref, q_ref, kp_ref, kc_ref, vp_ref, vc_ref, o_ref, bias_ref):
    n = pl.program_id(0)
    half = DH_A
    pairs = G_A // 2
    rows = pairs * BLK

    @pl.when(n == 0)
    def _():
        for h in range(HA):
            g, t, par = h // G_A, (h % G_A) // 2, h % 2
            own = _bucket_values(tab_ref, bucket_ref[0], h, offset=0.0, scale=LOG2E)
            prev = _bucket_values(tab_ref, bucket_ref[1], h, offset=0.0, scale=LOG2E)
            q_rows = slice(t * BLK, (t + 1) * BLK)
            for var, prev_half in ((0, jnp.full((BLK, BLK), NEG, F32)), (1, prev)):
                bias_ref[var, g, par, q_rows, 0:BLK] = prev_half
                bias_ref[var, g, par, q_rows, BLK:2 * BLK] = own

    lane_k = lax.broadcasted_iota(jnp.int32, (2 * BLK, 2 * half), 1)
    row = lax.broadcasted_iota(jnp.int32, (rows, 1), 0)
    lane_o = lax.broadcasted_iota(jnp.int32, (BLK, 2 * half), 1)
    variant = jnp.minimum(n, 1)

    kf = jnp.concatenate([kp_ref[...], kc_ref[...]], axis=0).astype(F32)
    vf = jnp.concatenate([vp_ref[...], vc_ref[...]], axis=0).astype(F32)
    kr = pltpu.roll(kf, half, 1)
    vr = pltpu.roll(vf, half, 1)

    outs = []
    for g in range(KV_A):
        k_own, k_other = (kf, kr) if g == 0 else (kr, kf)
        k_lo = jnp.where(lane_k < half, k_own, 0.0).astype(BF16)
        k_hi = jnp.where(lane_k >= half, k_other, 0.0).astype(BF16)
        v_dup = (jnp.where(lane_k < half, vf, vr) if g == 0 else jnp.where(lane_k < half, vr, vf)).astype(BF16)
        q_stack = jnp.concatenate(
            [q_ref[:, (g * pairs + t) * 2 * half:(g * pairs + t + 1) * 2 * half] for t in range(pairs)], axis=0)
        q_stack = (q_stack.astype(F32) * (DH_A ** -0.5 * LOG2E)).astype(BF16)
        o_par = []
        for par, k_sel in ((0, k_lo), (1, k_hi)):
            sink = jnp.zeros((rows, 1), F32)
            for t in range(pairs):
                sink = jnp.where(row >= t * BLK, sink_ref[2 * (g * pairs + t) + par] * LOG2E, sink)
            s = lax.dot_general(q_stack, k_sel, (((1,), (1,)), ((), ())),
                                preferred_element_type=F32) + bias_ref[variant, g, par]
            mx = jnp.maximum(jnp.max(s, axis=1, keepdims=True), sink)
            p = jnp.exp2(s - mx)
            den = jnp.sum(p, axis=1, keepdims=True) + jnp.exp2(sink - mx)
            o_par.append(jnp.dot(p.astype(BF16), v_dup, preferred_element_type=F32) / den)
        for t in range(pairs):
            outs.append(jnp.where(lane_o < half, o_par[0][t * BLK:(t + 1) * BLK], o_par[1][t * BLK:(t + 1) * BLK]))
    o_ref[...] = jnp.concatenate(outs, axis=1).astype(o_ref.dtype)


def _swa_attention(proj, sinks, table, *, q_col, k_col, v_col):
    s = proj.shape[0]
    nb = s // BLK
    qw = HA * DH_A
    kvw = KV_A * DH_A
    assert kvw == V7X_LANES and q_col % qw == 0 and k_col % kvw == 0 and v_col % kvw == 0 and WINDOW == BLK
    kb, vb = k_col // kvw, v_col // kvw
    qi = np.arange(BLK)[:, None]
    kj = np.arange(BLK)[None, :]
    buckets = np.stack([np.where(qi >= kj, _rel_bucket_np(qi - kj), -1),
                        np.where(qi < kj, _rel_bucket_np(BLK + qi - kj), -1)]).astype(np.int32)
    bias_shape = (2, KV_A, 2, (G_A // 2) * BLK, 2 * BLK)
    blocks = [(buckets.shape, jnp.int32), ((BLK, qw), BF16)] + [((BLK, kvw), BF16)] * 4 + [((BLK, qw), BF16)]
    prev = lambda n: jnp.maximum(n - 1, 0)
    return pl.pallas_call(
        _swa_body,
        grid=(nb,),
        in_specs=[pl.BlockSpec(memory_space=pltpu.SMEM),
                  pl.BlockSpec(memory_space=pltpu.SMEM),
                  pl.BlockSpec(buckets.shape, lambda n: (0, 0, 0)),
                  pl.BlockSpec((BLK, qw), lambda n: (n, q_col // qw)),
                  pl.BlockSpec((BLK, kvw), lambda n: (prev(n), kb)),
                  pl.BlockSpec((BLK, kvw), lambda n: (n, kb)),
                  pl.BlockSpec((BLK, kvw), lambda n: (prev(n), vb)),
                  pl.BlockSpec((BLK, kvw), lambda n: (n, vb))],
        out_specs=pl.BlockSpec((BLK, qw), lambda n: (n, 0)),
        out_shape=jax.ShapeDtypeStruct((s, qw), BF16),
        scratch_shapes=[pltpu.VMEM(bias_shape, F32)],
        compiler_params=_params(("arbitrary",), max(_vmem_limit(blocks, [(bias_shape, F32)]), VMEM_REQUEST_CAP)),
        name="swa_sink_attention",
    )(sinks.astype(F32), table.astype(F32), jnp.asarray(buckets), proj, proj, proj, proj, proj)


def _write_band_bias(tab_ref, bucket_ref, o_ref, h, *, scale, tile_offsets):
    last = tab_ref[N_BUCKETS - 1, h]
    bands = [_bucket_values(tab_ref, bucket_ref[t], h, offset=last, scale=scale) for t in range(2)]
    sub = MAX_DISTANCE
    nb = o_ref.shape[1] // sub
    for t, base in enumerate(tile_offsets):
        for rb in range(nb):
            for cb in range(nb):
                off = base + sub * (cb - rb)
                if off < 0:
                    blk = jnp.full((sub, sub), NEG, F32)
                elif off < 2 * sub:
                    blk = bands[off // sub]
                else:
                    blk = jnp.zeros((sub, sub), F32)
                o_ref[t, rb * sub:(rb + 1) * sub, cb * sub:(cb + 1) * sub] = blk


def _band_buckets():
    sub = MAX_DISTANCE
    r = np.arange(sub)[:, None]
    c = np.arange(sub)[None, :]
    return np.stack([np.where(c - r + o >= 0, _rel_bucket_np(c - r + o), -1) for o in (0, sub)]).astype(np.int32)


HEADS_PER_STEP = 2


def _diff_body(tab_ref, bucket_ref, q_ref, k_ref, v_ref, lq1_ref, lk1_ref, lq2_ref, lk2_ref, sw_ref,
               o_ref, b2_ref, qz_ref, sa_ref, sb_ref, cma_ref, cmb_ref, acc_ref, m_ref, l_ref, *, tq, tk):
    j = pl.program_id(1)
    width = 2 * DH_B
    heads = range(HEADS_PER_STEP)
    lanes = lambda hh: slice(hh * width, (hh + 1) * width)
    q_rows = lambda blk: slice(blk * tq, (blk + 1) * tq)

    @pl.when(j == 0)
    def _():
        for hh in heads:
            _write_band_bias(tab_ref, bucket_ref, b2_ref.at[hh], HA + pl.program_id(0) * HEADS_PER_STEP + hh,
                             scale=LOG2E, tile_offsets=(tk, 0))

    row = lax.broadcasted_iota(jnp.int32, (width, tq), 0)
    for blk in range(2):
        for hh in heads:
            qt = (q_ref[q_rows(blk), lanes(hh)].astype(F32) * (DH_B ** -0.5 * LOG2E)).T
            qz_ref[blk, hh, 0] = jnp.where(row < DH_B, qt, 0.0).astype(BF16)
            qz_ref[blk, hh, 1] = jnp.where(row >= DH_B, qt, 0.0).astype(BF16)

    def score_chain(hh, c, blk, tile, bias_tile, s_ref, cm_ref):
        kblk = k_ref[pl.ds(pl.multiple_of(tile * tk, tk), tk), lanes(hh)]
        s = jnp.dot(kblk, qz_ref[blk, hh, c], preferred_element_type=F32)
        if bias_tile is not None:
            s = s + b2_ref[hh, bias_tile]
        s_ref[hh, c] = s
        cm_ref[hh, c] = jnp.max(s, axis=0, keepdims=True)

    def accumulate_chain(hh, c, tile, s_ref, cm_ref, first):
        vblk = v_ref[pl.ds(pl.multiple_of(tile * tk, tk), tk), lanes(hh)]
        m_new = cm_ref[hh, c] if first else jnp.maximum(m_ref[hh, c], cm_ref[hh, c])
        p = jnp.exp2(s_ref[hh, c] - m_new)
        l_new = jnp.sum(p, axis=0, keepdims=True)
        pv = lax.dot_general(vblk, p.astype(BF16), (((0,), (0,)), ((), ())), preferred_element_type=F32)
        if first:
            l_ref[hh, c] = l_new
            acc_ref[hh, c] = pv
        else:
            rescale = jnp.exp2(m_ref[hh, c] - m_new)
            l_ref[hh, c] = rescale * l_ref[hh, c] + l_new
            acc_ref[hh, c] = rescale * acc_ref[hh, c] + pv
        m_ref[hh, c] = m_new

    def stage(score_args=None, acc_args=None, first=False):
        for c in range(2):
            for hh in heads:
                if score_args is not None:
                    score_chain(hh, c, *score_args)
                if acc_args is not None:
                    accumulate_chain(hh, c, *acc_args, first)

    def finalize(blk):
        lam = (jnp.exp(jnp.sum(lq1_ref[...] * lk1_ref[...], axis=1, keepdims=True))
               - jnp.exp(jnp.sum(lq2_ref[...] * lk2_ref[...], axis=1, keepdims=True)) + LAMBDA_INIT)
        for hh in heads:
            o = acc_ref[hh, 0] / l_ref[hh, 0] - lam * (acc_ref[hh, 1] / l_ref[hh, 1])
            ms = jnp.mean(o * o, axis=0, keepdims=True)
            o = o * lax.rsqrt(ms + LN_EPS) * sw_ref[...] * (1.0 - LAMBDA_INIT)
            o_ref[q_rows(blk), lanes(hh)] = o.T.astype(o_ref.dtype)

    def far_pairs(blk, i, slot_a, slot_b):
        def pair(t):
            stage((blk, i - (2 * t + 1), None, *slot_b), (i - 2 * t, *slot_a))
            stage((blk, jnp.maximum(i - (2 * t + 2), 0), None, *slot_a), (i - (2 * t + 1), *slot_b))

        count = jnp.maximum((i + 1) // 2 - 1, 0)

        def two_pairs(u, carry):
            pair(2 * u + 1)
            pair(2 * u + 2)
            return carry

        lax.fori_loop(0, count // 2, two_pairs, 0)

        @pl.when(count % 2 == 1)
        def _():
            pair(count)

    slot_x, slot_y = (sa_ref, cma_ref), (sb_ref, cmb_ref)

    i0 = 2 * j
    stage(score_args=(0, i0, 1, *slot_x))

    @pl.when(j >= 1)
    def _():
        stage((0, i0 - 1, 0, *slot_y), (i0, *slot_x), first=True)
        stage((0, i0 - 2, None, *slot_x), (i0 - 1, *slot_y))

    far_pairs(0, i0, slot_x, slot_y)

    i1 = i0 + 1

    @pl.when(j == 0)
    def _():
        stage((1, i1, 1, *slot_y), (0, *slot_x), first=True)

    @pl.when(j >= 1)
    def _():
        stage((1, i1, 1, *slot_y), (0, *slot_x))

    finalize(0)

    stage((1, i1 - 1, 0, *slot_x), (i1, *slot_y), first=True)
    stage((1, jnp.maximum(i1 - 2, 0), None, *slot_y), (i1 - 1, *slot_x))
    far_pairs(1, i1, slot_y, slot_x)
    finalize(1)


def _diff_attention(proj, table, lq1, lk1, lq2, lk2, subln_w, *, q_col, k_col, v_col, tq, tk):
    s = proj.shape[0]
    width = 2 * DH_B
    hps = HEADS_PER_STEP
    wide = hps * width
    assert width == V7X_LANES and tq == tk and HB % hps == 0 and s % (2 * tq) == 0
    assert q_col % wide == 0 and k_col % wide == 0 and v_col % wide == 0
    qb, kb, vb = q_col // wide, k_col // wide, v_col // wide
    assert tq % MAX_DISTANCE == 0
    buckets = _band_buckets()
    blocks = [(buckets.shape, jnp.int32), ((2 * tq, wide), BF16), ((s, wide), BF16), ((s, wide), BF16),
              ((2 * tq, wide), BF16)]
    scores_buf = ((hps, 2, tk, tq), F32)
    stats_buf = ((hps, 2, 1, tq), F32)
    scratch = [((hps, 2, tk, tq), F32), ((2, hps, 2, width, tq), BF16), scores_buf, scores_buf, stats_buf, stats_buf,
               ((hps, 2, width, tq), F32), stats_buf, stats_buf]
    padded = [((hps, 2, 8, tq), F32) if sd == stats_buf else sd for sd in scratch]
    vec = lambda v: v.astype(F32).reshape(1, DH_B)
    small = pl.BlockSpec((1, DH_B), lambda h, j: (0, 0))
    return pl.pallas_call(
        functools.partial(_diff_body, tq=tq, tk=tk),
        grid=(HB // hps, s // (2 * tq)),
        in_specs=[pl.BlockSpec(memory_space=pltpu.SMEM),
                  pl.BlockSpec(buckets.shape, lambda h, j: (0, 0, 0)),
                  pl.BlockSpec((2 * tq, wide), lambda h, j: (j, qb + h)),
                  pl.BlockSpec((s, wide), lambda h, j: (0, kb + h)),
                  pl.BlockSpec((s, wide), lambda h, j: (0, vb + h)),
                  small, small, small, small,
                  pl.BlockSpec((width, 1), lambda h, j: (0, 0))],
        out_specs=pl.BlockSpec((2 * tq, wide), lambda h, j: (j, h)),
        out_shape=jax.ShapeDtypeStruct((s, HB * width), BF16),
        scratch_shapes=[pltpu.VMEM(sh, dt) for sh, dt in scratch],
        compiler_params=_params(("arbitrary", "arbitrary"), _vmem_limit(blocks, padded)),
        name="diff_attention",
    )(table.astype(F32), jnp.asarray(buckets), proj, proj, proj, vec(lq1), vec(lk1), vec(lq2), vec(lk2),
      subln_w.astype(F32).reshape(width, 1))


def _cross_block_body(h_ref, mem_ref, wkv_ref, wq_ref, wo_ref, g_ref, b_ref, o_ref, ob_ref,
                      wq_bf_ref, wo_bf_ref, kv_ref):
    @pl.when(pl.program_id(0) == 0)
    def _():
        wq_bf_ref[...] = wq_ref[...].astype(BF16)
        wo_bf_ref[...] = wo_ref[...].astype(BF16)
        kv_ref[...] = jnp.dot(mem_ref[...].astype(BF16), wkv_ref[...].astype(BF16),
                              preferred_element_type=F32).astype(BF16)

    scale = DH_C ** -0.5
    halves = _row_chunks(h_ref.shape[0], 2)
    hs = [h_ref[rows, :] for rows in halves]
    qs = [jnp.dot(h.astype(BF16), wq_bf_ref[...], preferred_element_type=F32).astype(BF16) for h in hs]
    ss = [[lax.dot_general(q[:, hd * DH_C:(hd + 1) * DH_C], kv_ref[:, hd * DH_C:(hd + 1) * DH_C],
                           (((1,), (1,)), ((), ())), preferred_element_type=F32) * scale
           for hd in range(HC)] for q in qs]
    ocs = []
    for s_heads in ss:
        outs = []
        for hd, s in enumerate(s_heads):
            vh = kv_ref[:, (HC + hd) * DH_C:(HC + hd + 1) * DH_C]
            mx = jnp.max(s, axis=1, keepdims=True)
            p = jnp.exp(s - mx)
            den = jnp.sum(p, axis=1, keepdims=True)
            outs.append((jnp.dot(p.astype(BF16), vh, preferred_element_type=F32) / den).astype(BF16))
        ocs.append(jnp.concatenate(outs, axis=1))
    ys = [ALPHA * h + jnp.dot(oc, wo_bf_ref[...], preferred_element_type=F32) for h, oc in zip(hs, ocs)]
    for rows, y in zip(halves, ys):
        out = _layer_norm_rows(y, g_ref, b_ref)
        o_ref[rows, :] = out
        ob_ref[rows, :] = out.astype(BF16)


def _cross_attention_block(h, mem, w_mem_kv, w_cq, w_co, gain, bias, *, tm):
    s, d = h.shape
    mlen, w = mem.shape[0], HC * DH_C
    row_block = pl.BlockSpec((tm, d), lambda i: (i, 0))
    whole = lambda shape: pl.BlockSpec(shape, lambda i: (0,) * len(shape), pipeline_mode=pl.Buffered(1))
    blocks = [((tm, d), F32), ((tm, d), F32), ((tm, d), BF16)]
    single = [((mlen, d), F32), ((d, 2 * w), F32), ((d, w), F32), ((w, d), F32)]
    scratch = [((d, w), BF16), ((w, d), BF16), ((mlen, 2 * w), BF16)]
    return pl.pallas_call(
        _cross_block_body,
        grid=(s // tm,),
        in_specs=[row_block, whole((mlen, d)), whole((d, 2 * w)), whole((d, w)), whole((w, d)),
                  whole((1, d)), whole((1, d))],
        out_specs=[row_block, row_block],
        out_shape=[jax.ShapeDtypeStruct((s, d), F32), jax.ShapeDtypeStruct((s, d), BF16)],
        scratch_shapes=[pltpu.VMEM(sh, dt) for sh, dt in scratch],
        compiler_params=_params(("arbitrary",), _vmem_limit(blocks, single + scratch)),
        name="memory_cross_attention_block",
    )(h, mem, w_mem_kv, w_cq, w_co, gain.reshape(1, d), bias.reshape(1, d))


def kernel(x, mem, rel_bias_table, w_in, sinks, lambda_q1, lambda_k1, lambda_q2, lambda_k2, subln_w,
           w_branch_a, w_branch_b, w_o, ln1_g, ln1_b, w_cq, w_mem_kv, w_co, ln2_g, ln2_b,
           w_gate_up, w_down, ln3_g, ln3_b):
    b, s, d = x.shape
    assert b == 1 and w_in.shape[0] == DEPTH == 1
    (w_in, sinks, lambda_q1, lambda_k1, lambda_q2, lambda_k2, subln_w, w_branch_a, w_branch_b, w_o, ln1_g, ln1_b,
     w_cq, w_mem_kv, w_co, ln2_g, ln2_b, w_gate_up, w_down, ln3_g, ln3_b) = [
        p.reshape(p.shape[1:]) for p in (
            w_in, sinks, lambda_q1, lambda_k1, lambda_q2, lambda_k2, subln_w, w_branch_a, w_branch_b, w_o, ln1_g,
            ln1_b, w_cq, w_mem_kv, w_co, ln2_g, ln2_b, w_gate_up, w_down, ln3_g, ln3_b)]
    qa_w, kva_w, qb_w = HA * DH_A, KV_A * DH_A, HB * 2 * DH_B
    col_qa, col_ka, col_va = 0, qa_w, qa_w + kva_w
    col_qb = col_va + kva_w
    col_kb, col_vb = col_qb + qb_w, col_qb + 2 * qb_w
    col_ga = col_vb + qb_w
    col_gb = col_ga + d
    tq, tk = TILES["diff_attention"]["tq"], TILES["diff_attention"]["tk"]

    h0 = x.reshape(s, d)

    proj = _matmul(h0, w_in, **TILES["in_proj"], out_dtype=BF16, name="in_proj")

    o_a = _swa_attention(proj, sinks, rel_bias_table, q_col=col_qa, k_col=col_ka, v_col=col_va)

    o_b = _diff_attention(proj, rel_bias_table, lambda_q1, lambda_k1, lambda_q2, lambda_k2, subln_w,
                          q_col=col_qb, k_col=col_kb, v_col=col_vb, tq=tq, tk=tk)

    mix = _gated_branches(o_a, o_b, w_branch_a, w_branch_b, proj, col_ga, col_gb, **TILES["gated_branches"])
    h1 = _matmul_resident_ln(mix, w_o, h0, ln1_g, ln1_b, **TILES["out_proj_ln"])

    h2, h2b = _cross_attention_block(h1, mem.reshape(mem.shape[1], d), w_mem_kv, w_cq, w_co, ln2_g, ln2_b,
                                     **TILES["cross_block"])

    act, w_down_bf = _swiglu_up(h2b, w_gate_up, w_down, **TILES["swiglu_up"])
    h3 = _matmul_residual_ln(act, w_down_bf, h2, ln3_g, ln3_b, **TILES["down_proj_ln"])
    return h3.reshape(b, s, d)
```
